```python
import jax, jax.numpy as jnp
from jax import lax
import numpy as np

D_MODEL = 1024
BATCH = 1
SEQ = 16384
DEPTH = 2
DEC_BATCH = 32
DEC_SEQ = 8
PAST_LEN = 16384
PAGE_SIZE = 128

N_BRANCH = 4
BR_W = D_MODEL // N_BRANCH
HEAD_DIM = 64
HEADS = BR_W // HEAD_DIM
CHUNK = 64
Q_BLOCK = 128
ROPE_BASE = 10000.0
RMS_EPS = 1e-6
FORGET_BIAS = 3.0
N_EXPERTS = 64
TOP_K = 6
N_GROUPS = 8
TOPK_GROUPS = 4
D_EXPERT = D_MODEL // 4
D_SHARED = D_MODEL // 4
ROUTED_SCALE = 2.5
MOE_BLOCK = 64

OFF_ML = 0
ML_COLS = 4 * BR_W + 2 * HEADS
OFF_HG = OFF_ML + ML_COLS
HG_COLS = 4 * BR_W
OFF_RT = OFF_HG + HG_COLS
RT_COLS = 4 * BR_W
OFF_FX = OFF_RT + RT_COLS
FX_COLS = 3 * BR_W + HEADS
OFF_GT = OFF_FX + FX_COLS
GT_COLS = N_BRANCH * D_MODEL
N_IN = OFF_GT + GT_COLS

kernel_name = "hybrid_mlstm_hgrn2_retnet_fox_moe_decoder_step"


def rmsnorm(x, w):
    xf = x.astype(jnp.float32)
    return xf * lax.rsqrt(jnp.mean(xf * xf, axis=-1, keepdims=True) + RMS_EPS) * w


def head_norm(o, w):
    B, L = o.shape[:2]
    return rmsnorm(o, w.reshape(HEADS, HEAD_DIM)).reshape(B, L, BR_W)


def rope(x, pos):
    half = HEAD_DIM // 2
    inv = ROPE_BASE ** (-jnp.arange(half, dtype=jnp.float32) / half)
    ang = pos.astype(jnp.float32)[:, None] * inv[None, :]
    cos = jnp.cos(ang)[None, :, None, :]
    sin = jnp.sin(ang)[None, :, None, :]
    x1, x2 = x[..., :half], x[..., half:]
    return jnp.concatenate([x1 * cos - x2 * sin, x1 * sin + x2 * cos], axis=-1)


def causal_mask(L):
    return jnp.tril(jnp.ones((L, L), dtype=bool))


def run_chunks(step, carry, xs, chunk):
    B, L = xs[0].shape[:2]
    n = L // chunk

    def split(a):
        return jnp.moveaxis(a.reshape((B, n, chunk) + a.shape[2:]), 1, 0)

    carry, ys = lax.scan(step, carry, tuple(split(a) for a in xs))
    return carry, jnp.moveaxis(ys, 0, 1).reshape((B, L) + ys.shape[3:])


def mlstm_chunk(carry, inp):
    C, n, m = carry
    q, k, v, ig, lf = inp
    L = q.shape[1]
    b = jnp.cumsum(lf, axis=1)
    D = b[:, :, None, :] - b[:, None, :, :] + ig[:, None, :, :]
    D = jnp.where(causal_mask(L)[None, :, :, None], D, -jnp.inf)
    m_inter = b + m[:, None, :]
    m_t = jnp.maximum(m_inter, jnp.max(D, axis=2))
    w_intra = jnp.exp(D - m_t[:, :, None, :])
    w_inter = jnp.exp(m_inter - m_t)
    A = jnp.einsum('bthd,bshd->btsh', q, k) * w_intra
    num = jnp.einsum('btsh,bshv->bthv', A, v) + w_inter[..., None] * jnp.einsum('bhvk,bthk->bthv', C, q)
    den = jnp.sum(A, axis=2) + w_inter * jnp.einsum('bhk,bthk->bth', n, q)
    h = num / jnp.maximum(jnp.abs(den), jnp.exp(-m_t))[..., None]
    m_new = m_t[:, -1]
    w_s = jnp.exp(b[:, -1:, :] - b + ig - m_new[:, None, :])
    decay = jnp.exp(b[:, -1] + m - m_new)
    C_new = decay[..., None, None] * C + jnp.einsum('bsh,bshv,bshk->bhvk', w_s, v, k)
    n_new = decay[..., None] * n + jnp.einsum('bsh,bshk->bhk', w_s, k)
    return (C_new, n_new, m_new), h


def hgrn_chunk(S, inp):
    q, k, v, lf = inp
    L = q.shape[1]
    b = jnp.cumsum(lf, axis=1)
    dec = jnp.exp(jnp.where(causal_mask(L)[None, :, :, None, None],
                            b[:, :, None] - b[:, None, :], -jnp.inf))
    A = jnp.einsum('bthk,bshk,btshk->btsh', q, k, dec)
    o = jnp.einsum('btsh,bshv->bthv', A, v) + jnp.einsum('bthk,bhkv->bthv', q * jnp.exp(b), S)
    w_s = jnp.exp(b[:, -1:] - b)
    S_new = jnp.exp(b[:, -1])[..., None] * S + jnp.einsum('bshk,bshv->bhkv', k * w_s, v)
    return S_new, o


def retention_log_gamma():
    return jnp.log1p(-jnp.exp2(-5.0 - jnp.arange(HEADS, dtype=jnp.float32)))


def retention_chunk(S, inp):
    q, k, v = inp
    L = q.shape[1]
    lg = retention_log_gamma()
    idx = jnp.arange(L, dtype=jnp.float32)
    diff = idx[:, None] - idx[None, :]
    dec = jnp.exp(jnp.where(causal_mask(L)[:, :, None], diff[:, :, None] * lg, -jnp.inf))
    A = jnp.einsum('bthd,bshd->btsh', q, k) * dec
    o = (jnp.einsum('btsh,bshv->bthv', A, v)
         + jnp.exp((idx[:, None] + 1.0) * lg)[None, :, :, None] * jnp.einsum('bthk,bhkv->bthv', q, S))
    w_s = jnp.exp((L - 1.0 - idx)[:, None] * lg)
    S_new = jnp.exp(L * lg)[None, :, None, None] * S + jnp.einsum('bshk,sh,bshv->bhkv', k, w_s, v)
    return S_new, o


def fox_prompt(q, k, v, lf):
    B, S = q.shape[:2]
    nb = S // Q_BLOCK
    scale = HEAD_DIM ** -0.5
    cum = jnp.cumsum(lf, axis=1).transpose(0, 2, 1)
    kpos = jnp.arange(S)
    qb = jnp.moveaxis(q.reshape(B, nb, Q_BLOCK, HEADS, HEAD_DIM), 1, 0)
    cb = jnp.moveaxis(cum.reshape(B, HEADS, nb, Q_BLOCK), 2, 0)

    def block(args):
        qi, ci, i = args
        qpos = i * Q_BLOCK + jnp.arange(Q_BLOCK)
        s = jnp.einsum('bqhd,bkhd->bhqk', qi, k) * scale + (ci[..., None] - cum[:, :, None, :])
        s = jnp.where(qpos[:, None] >= kpos[None, :], s, -jnp.inf)
        p = jax.nn.softmax(s.astype(jnp.float32), axis=-1)
        return jnp.einsum('bhqk,bkhd->bqhd', p, v)

    o = lax.map(block, (qb, cb, jnp.arange(nb)))
    return jnp.moveaxis(o, 0, 1).reshape(B, S, HEADS, HEAD_DIM)


def fox_decode(q, k, v, lf, kp, vp, lfp):
    scale = HEAD_DIM ** -0.5
    T = q.shape[1]
    P = kp.shape[1]
    suffix = (lax.cumsum(lfp, axis=1, reverse=True) - lfp).transpose(0, 2, 1)
    cum = jnp.cumsum(lf, axis=1).transpose(0, 2, 1)
    s_past = jnp.einsum('bqhd,bkhd->bhqk', q, kp) * scale + (cum[..., None] + suffix[:, :, None, :])
    s_new = jnp.einsum('bqhd,bkhd->bhqk', q, k) * scale + (cum[..., None] - cum[:, :, None, :])
    s_new = jnp.where(causal_mask(T)[None, None], s_new, -jnp.inf)
    p = jax.nn.softmax(jnp.concatenate([s_past, s_new], axis=-1).astype(jnp.float32), axis=-1)
    return jnp.einsum('bhqk,bkhd->bqhd', p[..., :P], vp) + jnp.einsum('bhqk,bkhd->bqhd', p[..., P:], v)


def hgrn_lower_bound(lb_logits, l):
    p = jax.nn.softmax(lb_logits.astype(jnp.float32), axis=0)
    return jnp.cumsum(p, axis=0)[l] - p[0]


def routed_experts(xf, eidx, ew, w_gu, w_down):
    T, D = xf.shape
    A = T * TOP_K
    flat_e = eidx.reshape(A)
    order = jnp.argsort(flat_e)
    e_sorted = flat_e[order]
    tok_sorted = (order // TOP_K).astype(jnp.int32)
    w_sorted = ew.reshape(A)[order]
    counts = jnp.bincount(flat_e, length=N_EXPERTS)
    padded = (counts + MOE_BLOCK - 1) // MOE_BLOCK * MOE_BLOCK
    pad_end = jnp.cumsum(padded)
    pad_start = pad_end - padded
    start = jnp.cumsum(counts) - counts
    dest = pad_start[e_sorted] + jnp.arange(A) - start[e_sorted]
    n_blocks = (A + MOE_BLOCK - 1) // MOE_BLOCK + N_EXPERTS
    n_rows = n_blocks * MOE_BLOCK
    buf_tok = jnp.zeros((n_rows,), jnp.int32).at[dest].set(tok_sorted)
    buf_w = jnp.zeros((n_rows,), w_sorted.dtype).at[dest].set(w_sorted)
    blk_e = jnp.minimum(jnp.searchsorted(pad_end, jnp.arange(n_blocks) * MOE_BLOCK, side='right'), N_EXPERTS - 1)

    def block(args):
        tok, wt, e = args
        g, u = jnp.split(xf[tok] @ w_gu[e], 2, axis=-1)
        return ((jax.nn.silu(g) * u) @ w_down[e]) * wt[:, None]

    yb = lax.map(block, (buf_tok.reshape(n_blocks, MOE_BLOCK), buf_w.reshape(n_blocks, MOE_BLOCK), blk_e))
    return jax.ops.segment_sum(yb.reshape(n_rows, D), buf_tok, num_segments=T)


def moe(h, w_router, router_bias, w_gu, w_down, ws_gu, ws_down):
    B, L, D = h.shape
    xf = h.reshape(B * L, D)
    T = xf.shape[0]
    s = jax.nn.sigmoid((xf @ w_router).astype(jnp.float32))
    sel = s + router_bias
    gsc = jnp.sum(lax.top_k(sel.reshape(T, N_GROUPS, N_EXPERTS // N_GROUPS), 2)[0], axis=-1)
    _, gidx = lax.top_k(gsc, TOPK_GROUPS)
    gmask = jnp.any(gidx[:, :, None] == jnp.arange(N_GROUPS)[None, None, :], axis=1)
    sel = jnp.where(jnp.repeat(gmask, N_EXPERTS // N_GROUPS, axis=1), sel, -jnp.inf)
    _, eidx = lax.top_k(sel, TOP_K)
    ew = jnp.take_along_axis(s, eidx, axis=-1)
    ew = ew / jnp.sum(ew, axis=-1, keepdims=True) * ROUTED_SCALE
    routed = routed_experts(xf, eidx, ew, w_gu, w_down)
    g, u = jnp.split(xf @ ws_gu, 2, axis=-1)
    shared = (jax.nn.silu(g) * u) @ ws_down
    return (routed + shared).reshape(B, L, D)


def run_trunk(x, c, pos, init, fox_attend, chunk, P):
    B, L, _ = x.shape
    f32 = jnp.float32
    C_in, n_in, m_in, Sh_in, Sr_in = init
    rows_k, rows_v, rows_lf, Cs, ns, ms, Shs, Srs = ([] for _ in range(8))

    def heads(a):
        return a.reshape(B, L, HEADS, HEAD_DIM)

    for l in range(DEPTH):
        mod = jax.nn.silu(c.astype(f32)) @ P["w_ada"][l] + P["b_ada"][l]
        sh1, sc1, g1, sh2, sc2, g2 = jnp.split(mod[:, None, :], 6, axis=-1)
        h = rmsnorm(x, P["norm_mix_w"][l]) * (1.0 + sc1) + sh1
        proj = (h @ P["w_in"][l] + P["b_in"][l]).astype(f32)

        ml = proj[..., OFF_ML:OFF_ML + ML_COLS]
        mq, mk, mv, mo = (ml[..., i * BR_W:(i + 1) * BR_W] for i in range(4))
        m_ig = ml[..., 4 * BR_W:4 * BR_W + HEADS]
        m_lf = jax.nn.log_sigmoid(ml[..., 4 * BR_W + HEADS:])
        (C_new, n_new, m_new), mh = run_chunks(
            mlstm_chunk, (C_in[l], n_in[l], m_in[l]),
            (heads(mq), heads(mk) * HEAD_DIM ** -0.5, heads(mv), m_ig, m_lf), chunk)
        out_ml = head_norm(mh, P["mlstm_norm_w"][l]) * jax.nn.sigmoid(mo)

        hg = proj[..., OFF_HG:OFF_HG + HG_COLS]
        hq, hf, hi, hgt = (hg[..., i * BR_W:(i + 1) * BR_W] for i in range(4))
        lb = hgrn_lower_bound(P["hgrn_lb_logits"], l)
        f = lb + (1.0 - lb) * jax.nn.sigmoid(hf)
        Sh_new, ho = run_chunks(
            hgrn_chunk, Sh_in[l],
            (heads(jax.nn.silu(hq)), heads(1.0 - f), heads(hi), heads(jnp.log(f))), chunk)
        out_hg = head_norm(ho, P["hgrn_norm_w"][l]) * jax.nn.silu(hgt)

        rt = proj[..., OFF_RT:OFF_RT + RT_COLS]
        rq, rk, rv, rg = (rt[..., i * BR_W:(i + 1) * BR_W] for i in range(4))
        Sr_new, ro = run_chunks(
            retention_chunk, Sr_in[l],
            (rope(heads(rq), pos), rope(heads(rk), pos) * HEAD_DIM ** -0.5, heads(rv)), chunk)
        out_rt = head_norm(ro, P["ret_norm_w"][l]) * jax.nn.silu(rg)

        fx = proj[..., OFF_FX:OFF_FX + FX_COLS]
        fq, fk, fv = (heads(fx[..., i * BR_W:(i + 1) * BR_W]) for i in range(3))
        f_lf = jax.nn.log_sigmoid(fx[..., 3 * BR_W:])
        out_fx = fox_attend(l, fq, fk, fv, f_lf).reshape(B, L, BR_W)

        branches = (out_ml, out_hg, out_rt, out_fx)
        merged = jax.nn.sigmoid(proj[..., OFF_GT:OFF_GT + D_MODEL]) * (branches[0] @ P["w_branch"][l, 0])
        for bi in range(1, N_BRANCH):
            gate = jax.nn.sigmoid(proj[..., OFF_GT + bi * D_MODEL:OFF_GT + (bi + 1) * D_MODEL])
            merged = merged + gate * (branches[bi] @ P["w_branch"][l, bi])
        x = x + g1 * (merged @ P["w_out"][l])

        h2 = rmsnorm(x, P["norm_ffn_w"][l]) * (1.0 + sc2) + sh2
        x = x + g2 * moe(h2, P["w_router"][l], P["router_bias"][l], P["w_exp_gu"][l],
                         P["w_exp_down"][l], P["w_shared_gu"][l], P["w_shared_down"][l])

        rows_k.append(fk); rows_v.append(fv); rows_lf.append(f_lf)
        Cs.append(C_new); ns.append(n_new); ms.append(m_new); Shs.append(Sh_new); Srs.append(Sr_new)

    y = rmsnorm(x, P["norm_f_w"])
    return (y, jnp.stack(rows_k), jnp.stack(rows_v), jnp.stack(rows_lf), jnp.stack(Cs),
            jnp.stack(ns), jnp.stack(ms), jnp.stack(Shs), jnp.stack(Srs))


def setup_inputs(seed: int = 0) -> dict:
    key = jax.random.key(seed)
    ks = jax.random.split(key, 40)
    f32 = jnp.float32

    def nrm(k, shape, s=1.0):
        return s * jax.random.normal(k, shape, f32)

    n_pages = PAST_LEN // PAGE_SIZE
    n_used = DEC_BATCH * n_pages
    n_pool = n_used + (n_used + 3) // 4
    page_table = jax.random.permutation(ks[0], n_pool)[:n_used].reshape(DEC_BATCH, n_pages).astype(jnp.int32)

    b_in = nrm(ks[1], (DEPTH, N_IN), 0.02)
    f_ml = OFF_ML + 4 * BR_W + HEADS
    f_fx = OFF_FX + 3 * BR_W
    b_in = b_in.at[:, f_ml:f_ml + HEADS].add(FORGET_BIAS).at[:, f_fx:f_fx + HEADS].add(FORGET_BIAS)

    return {
        "x_prompt": nrm(ks[2], (BATCH, SEQ, D_MODEL)),
        "x_sample": nrm(ks[3], (DEC_BATCH, DEC_SEQ, D_MODEL)),
        "cache_fox_k": nrm(ks[4], (DEPTH, n_pool, PAGE_SIZE, HEADS, HEAD_DIM)),
        "cache_fox_v": nrm(ks[5], (DEPTH, n_pool, PAGE_SIZE, HEADS, HEAD_DIM)),
        "cache_fox_logf": jax.nn.log_sigmoid(FORGET_BIAS + nrm(ks[6], (DEPTH, n_pool, PAGE_SIZE, HEADS))),
        "state_mlstm_C": nrm(ks[7], (DEPTH, DEC_BATCH, HEADS, HEAD_DIM, HEAD_DIM), 0.1),
        "state_mlstm_n": nrm(ks[8], (DEPTH, DEC_BATCH, HEADS, HEAD_DIM), 0.1),
        "state_mlstm_m": nrm(ks[9], (DEPTH, DEC_BATCH, HEADS)),
        "state_hgrn_S": nrm(ks[10], (DEPTH, DEC_BATCH, HEADS, HEAD_DIM, HEAD_DIM)),
        "state_ret_S": nrm(ks[11], (DEPTH, DEC_BATCH, HEADS, HEAD_DIM, HEAD_DIM)),
        "page_table": page_table,
        "c_prompt": nrm(ks[12], (BATCH, D_MODEL)),
        "c_sample": nrm(ks[13], (DEC_BATCH, D_MODEL)),
        "w_ada": nrm(ks[14], (DEPTH, D_MODEL, 6 * D_MODEL), 0.5 * D_MODEL ** -0.5),
        "b_ada": nrm(ks[15], (DEPTH, 6 * D_MODEL), 0.02),
        "norm_mix_w": 1.0 + nrm(ks[16], (DEPTH, D_MODEL), 0.02),
        "norm_ffn_w": 1.0 + nrm(ks[17], (DEPTH, D_MODEL), 0.02),
        "w_in": nrm(ks[18], (DEPTH, D_MODEL, N_IN), D_MODEL ** -0.5),
        "b_in": b_in,
        "hgrn_lb_logits": nrm(ks[19], (DEPTH, BR_W)),
        "mlstm_norm_w": 1.0 + nrm(ks[20], (DEPTH, BR_W), 0.02),
        "hgrn_norm_w": 1.0 + nrm(ks[21], (DEPTH, BR_W), 0.02),
        "ret_norm_w": 1.0 + nrm(ks[22], (DEPTH, BR_W), 0.02),
        "w_branch": nrm(ks[23], (DEPTH, N_BRANCH, BR_W, D_MODEL), BR_W ** -0.5),
        "w_out": nrm(ks[24], (DEPTH, D_MODEL, D_MODEL), D_MODEL ** -0.5),
        "w_router": nrm(ks[25], (DEPTH, D_MODEL, N_EXPERTS), D_MODEL ** -0.5),
        "router_bias": nrm(ks[26], (DEPTH, N_EXPERTS), 0.01),
        "w_exp_gu": nrm(ks[27], (DEPTH, N_EXPERTS, D_MODEL, 2 * D_EXPERT), D_MODEL ** -0.5),
        "w_exp_down": nrm(ks[28], (DEPTH, N_EXPERTS, D_EXPERT, D_MODEL), D_EXPERT ** -0.5),
        "w_shared_gu": nrm(ks[29], (DEPTH, D_MODEL, 2 * D_SHARED), D_MODEL ** -0.5),
        "w_shared_down": nrm(ks[30], (DEPTH, D_SHARED, D_MODEL), D_SHARED ** -0.5),
        "norm_f_w": 1.0 + nrm(ks[31], (D_MODEL,), 0.02),
    }


def reference(x_prompt, x_sample, cache_fox_k, cache_fox_v, cache_fox_logf, state_mlstm_C, state_mlstm_n,
              state_mlstm_m, state_hgrn_S, state_ret_S, page_table, c_prompt, c_sample, w_ada, b_ada,
              norm_mix_w, norm_ffn_w, w_in, b_in, hgrn_lb_logits, mlstm_norm_w, hgrn_norm_w, ret_norm_w,
              w_branch, w_out, w_router, router_bias, w_exp_gu, w_exp_down, w_shared_gu, w_shared_down,
              norm_f_w):
    f32 = jnp.float32
    P = dict(w_ada=w_ada, b_ada=b_ada, norm_mix_w=norm_mix_w, norm_ffn_w=norm_ffn_w, w_in=w_in, b_in=b_in,
             hgrn_lb_logits=hgrn_lb_logits, mlstm_norm_w=mlstm_norm_w, hgrn_norm_w=hgrn_norm_w,
             ret_norm_w=ret_norm_w, w_branch=w_branch, w_out=w_out, w_router=w_router,
             router_bias=router_bias, w_exp_gu=w_exp_gu, w_exp_down=w_exp_down,
             w_shared_gu=w_shared_gu, w_shared_down=w_shared_down, norm_f_w=norm_f_w)
    B, S, _ = x_prompt.shape
    DB, T, _ = x_sample.shape
    n_pages = page_table.shape[1]
    past_len = n_pages * PAGE_SIZE

    prompt_init = (jnp.zeros((DEPTH, B, HEADS, HEAD_DIM, HEAD_DIM), f32),
                   jnp.zeros((DEPTH, B, HEADS, HEAD_DIM), f32),
                   jnp.zeros((DEPTH, B, HEADS), f32),
                   jnp.zeros((DEPTH, B, HEADS, HEAD_DIM, HEAD_DIM), f32),
                   jnp.zeros((DEPTH, B, HEADS, HEAD_DIM, HEAD_DIM), f32))

    def fox_attend_prompt(l, q, k, v, lf):
        return fox_prompt(q, k, v, lf)

    (y_prompt, p_fox_k, p_fox_v, p_fox_logf, p_mlstm_C, p_mlstm_n, p_mlstm_m, p_hgrn_S, p_ret_S) = run_trunk(
        x_prompt, c_prompt, jnp.arange(S), prompt_init, fox_attend_prompt, CHUNK, P)

    sample_init = (state_mlstm_C.astype(f32), state_mlstm_n.astype(f32), state_mlstm_m.astype(f32),
                   state_hgrn_S.astype(f32), state_ret_S.astype(f32))

    def fox_attend_sample(l, q, k, v, lf):
        kp = cache_fox_k[l][page_table].reshape(DB, past_len, HEADS, HEAD_DIM)
        vp = cache_fox_v[l][page_table].reshape(DB, past_len, HEADS, HEAD_DIM)
        lfp = cache_fox_logf[l][page_table].reshape(DB, past_len, HEADS).astype(f32)
        return fox_decode(q, k, v, lf, kp, vp, lfp)

    (y_sample, s_fox_k, s_fox_v, s_fox_logf, s_mlstm_C, s_mlstm_n, s_mlstm_m, s_hgrn_S, s_ret_S) = run_trunk(
        x_sample, c_sample, past_len + jnp.arange(T), sample_init, fox_attend_sample, T, P)

    return (y_prompt, y_sample, p_fox_k, p_fox_v, p_fox_logf, p_mlstm_C, p_mlstm_n, p_mlstm_m, p_hgrn_S,
            p_ret_S, s_fox_k, s_fox_v, s_fox_logf, s_mlstm_C, s_mlstm_n, s_mlstm_m, s_hgrn_S, s_ret_S)
```

```python
import functools
import math

import numpy as np
import jax
import jax.numpy as jnp
from jax import lax
from jax.experimental import pallas as pl
from jax.experimental.pallas import tpu as pltpu

F32 = jnp.float32
BF16 = jnp.bfloat16
HIGHEST = lax.Precision.HIGHEST

D_MODEL = 1024
N_BRANCH = 4
BR_W = 256
HEAD_DIM = 64
HEADS = 4
ROPE_BASE = 10000.0
RMS_EPS = 1e-6
N_EXPERTS = 64
TOP_K = 6
N_GROUPS = 8
GROUP_SIZE = N_EXPERTS // N_GROUPS
TOPK_GROUPS = 4
D_EXPERT = 256
ROUTED_SCALE = 2.5
PAGE_SIZE = 128
QK_SCALE = HEAD_DIM ** -0.5

ML_COLS = 4 * BR_W + 2 * HEADS
OFF_HG = ML_COLS
OFF_RT = OFF_HG + 4 * BR_W
OFF_FX = OFF_RT + 4 * BR_W
OFF_GT = OFF_FX + 3 * BR_W + HEADS
N_IN = OFF_GT + N_BRANCH * D_MODEL

P_ML, P_HG, P_RT, P_FX, P_SM, P_GT, P_TOT = 0, 1024, 2048, 3072, 3840, 4096, 8192
SM_IG, SM_MLF, SM_FLF = 0, 4, 8

NEG = -1e30
VMEM_LIMIT = 56 * 1024 * 1024


def _cparams(sem):
    return pltpu.CompilerParams(dimension_semantics=sem, vmem_limit_bytes=VMEM_LIMIT)


def _dot(a, b):
    return jnp.dot(a.astype(BF16), b.astype(BF16), preferred_element_type=F32)


def _dot_nt(a, b):
    return lax.dot_general(a.astype(BF16), b.astype(BF16), (((1,), (1,)), ((), ())), preferred_element_type=F32)


def _dot_tn(a, b):
    return lax.dot_general(a.astype(BF16), b.astype(BF16), (((0,), (0,)), ((), ())), preferred_element_type=F32)


def _dot_hi(a, b):
    return jnp.dot(a, b, preferred_element_type=F32, precision=HIGHEST)


def _dot_nt_hi(a, b):
    return lax.dot_general(a, b, (((1,), (1,)), ((), ())), preferred_element_type=F32, precision=HIGHEST)


def _iota(shape, dim):
    return lax.broadcasted_iota(jnp.int32, shape, dim)


def _eye(n):
    return (_iota((n, n), 0) == _iota((n, n), 1)).astype(F32)


def _transpose(x):
    return _dot_nt_hi(_eye(x.shape[1]), x)


def _sigmoid(x):
    return jax.nn.sigmoid(x)


def _silu(x):
    return x * jax.nn.sigmoid(x)


def _log_sigmoid(x):
    return jnp.minimum(x, 0.0) - jnp.log1p(jnp.exp(-jnp.abs(x)))


def _head_masks(n=BR_W):
    lane = _iota((1, n), 1) >> 6
    return [lane == h for h in range(HEADS)]


def _block_diag_mask():
    return (_iota((BR_W, BR_W), 0) >> 6) == (_iota((BR_W, BR_W), 1) >> 6)


def _per_head_lanes(vals, masks):
    out = jnp.where(masks[0], vals[0], 0.0)
    for h in range(1, HEADS):
        out = jnp.where(masks[h], vals[h], out)
    return out


def _rms(x, eps=RMS_EPS):
    return x * lax.rsqrt(jnp.mean(x * x, axis=-1, keepdims=True) + eps)


def _head_norm(o, gain, bdf):
    ms = _dot_hi(o * o, bdf) * (1.0 / HEAD_DIM)
    return o * lax.rsqrt(ms + RMS_EPS) * gain


def _ada_kernel(c_ref, w_ref, b_ref, o_ref):
    o_ref[...] = _dot(_silu(c_ref[...]), w_ref[...]) + b_ref[...]


def _ada(c_all, w_ada, b_ada):
    depth = w_ada.shape[0]
    n_c = c_all.shape[0]
    tn = 1536
    return pl.pallas_call(
        _ada_kernel,
        grid=(depth, 6 * D_MODEL // tn),
        in_specs=[pl.BlockSpec((n_c, D_MODEL), lambda l, j: (0, 0)),
                  pl.BlockSpec((None, D_MODEL, tn), lambda l, j: (l, 0, j)),
                  pl.BlockSpec((None, 1, tn), lambda l, j: (l, 0, j))],
        out_specs=pl.BlockSpec((None, n_c, tn), lambda l, j: (l, 0, j)),
        out_shape=jax.ShapeDtypeStruct((depth, n_c, 6 * D_MODEL), F32),
        compiler_params=_cparams(("arbitrary", "arbitrary")),
        name="ada",
    )(c_all, w_ada, b_ada.reshape(depth, 1, 6 * D_MODEL))


def _inproj_kernel(x_ref, mod_ref, nw_ref, w_ref, b_ref, o_ref, h_scr):
    @pl.when(pl.program_id(1) == 0)
    def _():
        h = _rms(x_ref[...]) * nw_ref[...]
        h = h * (1.0 + mod_ref[:, D_MODEL:2 * D_MODEL]) + mod_ref[:, 0:D_MODEL]
        h_scr[...] = h.astype(BF16)

    o_ref[...] = jnp.dot(h_scr[...], w_ref[...], preferred_element_type=F32) + b_ref[...]


def _inproj(x, mod, norm_w, w_p, b_p):
    t = x.shape[0]
    tm = min(t, 1024)
    tn = 1024
    per_row = mod.shape[0] != 1
    mod_spec = (pl.BlockSpec((tm, 6 * D_MODEL), lambda i, j: (i, 0)) if per_row
                else pl.BlockSpec((1, 6 * D_MODEL), lambda i, j: (0, 0)))
    return pl.pallas_call(
        _inproj_kernel,
        grid=(t // tm, P_TOT // tn),
        in_specs=[pl.BlockSpec((tm, D_MODEL), lambda i, j: (i, 0)),
                  mod_spec,
                  pl.BlockSpec((1, D_MODEL), lambda i, j: (0, 0)),
                  pl.BlockSpec((D_MODEL, tn), lambda i, j: (0, j)),
                  pl.BlockSpec((1, tn), lambda i, j: (0, j))],
        out_specs=pl.BlockSpec((tm, tn), lambda i, j: (i, j)),
        out_shape=jax.ShapeDtypeStruct((t, P_TOT), F32),
        scratch_shapes=[pltpu.VMEM((tm, D_MODEL), BF16)],
        compiler_params=_cparams(("arbitrary", "arbitrary")),
        name="inproj",
    )(x, mod, norm_w, w_p, b_p)


def _mixers_kernel(layer, lc, sc,
                   ml_ref, hg_ref, rt_ref, sm_ref, cos_ref, sin_ref,
                   c0_ref, n0_ref, m0_ref, sh0_ref, sr0_ref, lbl_ref, gml_ref, ghg_ref, grt_ref,
                   br_ref, lf_ref, cumt_ref, cout_ref, nout_ref, mout_ref, shout_ref, srout_ref,
                   cbd, sht, srbd, n_s, m_s, carry_s, ohg_s):
    c = pl.program_id(1)
    n_c = pl.num_programs(1)
    masks = _head_masks()
    bd = _block_diag_mask()
    bdf = bd.astype(F32)
    tile = ((_iota((HEAD_DIM, BR_W), 1) & (HEAD_DIM - 1)) == _iota((HEAD_DIM, BR_W), 0)).astype(F32)

    @pl.when(c == 0)
    def _init():
        def expand(ref):
            return jnp.where(bd, _dot_hi(ref[...], tile), 0.0)
        cbd[...] = expand(c0_ref)
        sht[...] = _transpose(expand(sh0_ref))
        srbd[...] = expand(sr0_ref)
        n_s[...] = n0_ref[...]
        m_s[...] = m0_ref[...]
        carry_s[...] = jnp.zeros_like(carry_s)

    row = _iota((lc, lc), 0)
    col = _iota((lc, lc), 1)
    causal = row >= col
    tril = causal.astype(F32)

    sm = sm_ref[:, 0:128]
    lane128 = _iota((1, 128), 1)
    sm2 = jnp.where((lane128 >= SM_MLF) & (lane128 < SM_FLF + HEADS), _log_sigmoid(sm), sm)
    cum = _dot_hi(tril, sm2)
    sm2_t = _transpose(sm2)
    cum_t = _transpose(cum)
    lf_ref[...] = sm2
    cum_tg = cum_t + carry_s[...]
    carry_s[...] = cum_tg[:, lc - 1:lc]
    cumt_ref[...] = cum_tg[SM_FLF:SM_FLF + 8, :]

    q = ml_ref[:, 0:BR_W]
    k = ml_ref[:, BR_W:2 * BR_W] * QK_SCALE
    v = ml_ref[:, 2 * BR_W:3 * BR_W]
    og = ml_ref[:, 3 * BR_W:4 * BR_W]
    n_row = n_s[...]
    m_row = m_s[...]
    q_c = _dot_nt(q, cbd[...])
    h_all = jnp.zeros((lc, BR_W), F32)
    w_lanes = jnp.zeros((lc, BR_W), F32)
    decay_lanes = jnp.zeros((1, BR_W), F32)
    m_new_row = jnp.zeros((1, 128), F32)
    for h in range(HEADS):
        ig_c = sm2[:, SM_IG + h:SM_IG + h + 1]
        b_c = cum[:, SM_MLF + h:SM_MLF + h + 1]
        ig_r = sm2_t[SM_IG + h:SM_IG + h + 1, :]
        b_r = cum_t[SM_MLF + h:SM_MLF + h + 1, :]
        m_prev = m_row[:, h:h + 1]
        dmat = jnp.where(causal, b_c - b_r + ig_r, NEG)
        m_inter = b_c + m_prev
        m_t = jnp.maximum(m_inter, jnp.max(dmat, axis=1, keepdims=True))
        w_intra = jnp.exp(dmat - m_t)
        w_inter = jnp.exp(m_inter - m_t)
        qh = jnp.where(masks[h], q, 0.0)
        a = _dot_nt(qh, k) * w_intra
        num = _dot(a, v) + w_inter * q_c
        den = jnp.sum(a, axis=1, keepdims=True) + w_inter * jnp.sum(qh * n_row, axis=1, keepdims=True)
        hh = num / jnp.maximum(jnp.abs(den), jnp.exp(-m_t))
        h_all = jnp.where(masks[h], hh, h_all)
        m_new = m_t[lc - 1:lc, :]
        b_last = b_c[lc - 1:lc, :]
        w_s = jnp.exp(b_last - b_c + ig_c - m_new)
        decay = jnp.exp(b_last + m_prev - m_new)
        w_lanes = jnp.where(masks[h], w_s, w_lanes)
        decay_lanes = jnp.where(masks[h], decay, decay_lanes)
        m_new_row = jnp.where(lane128 == h, m_new, m_new_row)
    kw = k * w_lanes
    cbd[...] = cbd[...] * decay_lanes + jnp.where(bd, _dot_tn(v * w_lanes, k), 0.0)
    n_s[...] = n_row * decay_lanes + jnp.sum(kw, axis=0, keepdims=True)
    m_s[...] = m_new_row
    out_ml = _head_norm(h_all, gml_ref[...], bdf) * _sigmoid(og)

    lbl = lbl_ref[...]
    pr = jnp.exp(lbl - jnp.max(lbl, axis=0, keepdims=True))
    pr = pr / jnp.sum(pr, axis=0, keepdims=True)
    lb = jnp.zeros((1, BR_W), F32)
    for i in range(1, layer + 1):
        lb = lb + pr[i:i + 1, :]
    tril_sc = (_iota((sc, sc), 0) >= _iota((sc, sc), 1)).astype(F32)
    s_idx = _iota((sc, BR_W), 0)

    def hg_body(i, carry):
        r0 = pl.multiple_of(i * sc, sc)
        hq = hg_ref[pl.ds(r0, sc), 0:BR_W]
        hf = hg_ref[pl.ds(r0, sc), BR_W:2 * BR_W]
        vi = hg_ref[pl.ds(r0, sc), 2 * BR_W:3 * BR_W]
        qi = _silu(hq)
        f = lb + (1.0 - lb) * _sigmoid(hf)
        ki = 1.0 - f
        bi = _dot_hi(tril_sc, jnp.log(f))
        st = sht[...]
        o_inter = _dot_nt(qi * jnp.exp(bi), st)
        rows = []
        for t in range(sc):
            e_t = jnp.exp(jnp.where(s_idx <= t, bi[t:t + 1, :] - bi, NEG)) * qi[t:t + 1, :] * ki
            rows.append(e_t)
        e_all = jnp.concatenate(rows, axis=0)
        r_all = _dot(e_all, bdf)
        o_diag = jnp.sum(r_all.reshape(sc, sc, BR_W) * vi[None, :, :], axis=1)
        ohg_s[pl.ds(r0, sc), :] = o_inter + o_diag
        b_last = bi[sc - 1:sc, :]
        sht[...] = st * jnp.exp(b_last) + jnp.where(bd, _dot_tn(vi, ki * jnp.exp(b_last - bi)), 0.0)
        return carry

    lax.fori_loop(0, lc // sc, hg_body, 0)
    out_hg = _head_norm(ohg_s[...], ghg_ref[...], bdf) * _silu(hg_ref[:, 3 * BR_W:4 * BR_W])

    cosv = cos_ref[...]
    sinv = sin_ref[...]
    lane = _iota((1, BR_W), 1)
    first_half = (lane & (HEAD_DIM - 1)) < (HEAD_DIM // 2)

    def rope(x):
        partner = jnp.where(first_half, pltpu.roll(x, BR_W - HEAD_DIM // 2, 1), pltpu.roll(x, HEAD_DIM // 2, 1))
        return x * cosv + partner * sinv

    rq = rope(rt_ref[:, 0:BR_W])
    rk = rope(rt_ref[:, BR_W:2 * BR_W]) * QK_SCALE
    rv = rt_ref[:, 2 * BR_W:3 * BR_W]
    lg = [math.log1p(-(2.0 ** (-5.0 - h))) for h in range(HEADS)]
    lg_lanes = _per_head_lanes([jnp.full((1, 1), g, F32) for g in lg], masks)
    diff = (row - col).astype(F32)
    o_rt = jnp.zeros((lc, BR_W), F32)
    for h in range(HEADS):
        dec = jnp.exp(jnp.where(causal, diff * lg[h], NEG))
        a = _dot_nt(jnp.where(masks[h], rq, 0.0), rk) * dec
        o_rt = jnp.where(masks[h], _dot(a, rv), o_rt)
    t_idx = _iota((lc, BR_W), 0).astype(F32)
    o_rt = o_rt + _dot(rq * jnp.exp((t_idx + 1.0) * lg_lanes), srbd[...])
    w_ret = jnp.exp((lc - 1.0 - t_idx) * lg_lanes)
    srbd[...] = srbd[...] * jnp.exp(lc * lg_lanes) + jnp.where(bd, _dot_tn(rk * w_ret, rv), 0.0)
    out_rt = _head_norm(o_rt, grt_ref[...], bdf) * _silu(rt_ref[:, 3 * BR_W:4 * BR_W])

    br_ref[:, 0:BR_W] = out_ml
    br_ref[:, BR_W:2 * BR_W] = out_hg
    br_ref[:, 2 * BR_W:3 * BR_W] = out_rt

    @pl.when(c == n_c - 1)
    def _fin():
        def compact(x):
            return _dot_nt_hi(x, tile)
        cout_ref[...] = compact(cbd[...])
        shout_ref[...] = compact(_transpose(sht[...]))
        srout_ref[...] = compact(srbd[...])
        nout_ref[...] = n_s[...]
        mout_ref[...] = m_s[...]


def _mixers(layer, proj, cos_t, sin_t, c0, n0, m0, sh0, sr0, lb_logits, g_ml, g_hg, g_rt, lc):
    b, l, _ = proj.shape
    sc = min(16, lc)
    n_c = l // lc
    depth = lb_logits.shape[0]
    cb = lambda blk: pl.BlockSpec((None, lc, 1024), lambda bi, ci, blk=blk: (bi, ci, blk))
    st_spec = pl.BlockSpec((None, BR_W, HEAD_DIM), lambda bi, ci: (bi, 0, 0))
    row_spec = lambda n: pl.BlockSpec((None, 1, n), lambda bi, ci: (bi, 0, 0))
    full = lambda r, cc: pl.BlockSpec((r, cc), lambda bi, ci: (0, 0))
    outs = pl.pallas_call(
        functools.partial(_mixers_kernel, layer, lc, sc),
        grid=(b, n_c),
        in_specs=[cb(0), cb(1), cb(2),
                  pl.BlockSpec((None, lc, 256), lambda bi, ci: (bi, ci, P_SM // 256)),
                  pl.BlockSpec((lc, BR_W), lambda bi, ci: (ci, 0)),
                  pl.BlockSpec((lc, BR_W), lambda bi, ci: (ci, 0)),
                  st_spec, row_spec(BR_W), row_spec(128), st_spec, st_spec,
                  full(depth, BR_W), full(1, BR_W), full(1, BR_W), full(1, BR_W)],
        out_specs=[pl.BlockSpec((None, lc, 3 * BR_W), lambda bi, ci: (bi, ci, 0)),
                   pl.BlockSpec((None, lc, 128), lambda bi, ci: (bi, ci, 0)),
                   pl.BlockSpec((None, 8, lc), lambda bi, ci: (bi, 0, ci)),
                   st_spec, row_spec(BR_W), row_spec(128), st_spec, st_spec],
        out_shape=[jax.ShapeDtypeStruct((b, l, 3 * BR_W), F32),
                   jax.ShapeDtypeStruct((b, l, 128), F32),
                   jax.ShapeDtypeStruct((b, 8, l), F32),
                   jax.ShapeDtypeStruct((b, BR_W, HEAD_DIM), F32),
                   jax.ShapeDtypeStruct((b, 1, BR_W), F32),
                   jax.ShapeDtypeStruct((b, 1, 128), F32),
                   jax.ShapeDtypeStruct((b, BR_W, HEAD_DIM), F32),
                   jax.ShapeDtypeStruct((b, BR_W, HEAD_DIM), F32)],
        scratch_shapes=[pltpu.VMEM((BR_W, BR_W), F32), pltpu.VMEM((BR_W, BR_W), F32), pltpu.VMEM((BR_W, BR_W), F32),
                        pltpu.VMEM((1, BR_W), F32), pltpu.VMEM((1, 128), F32), pltpu.VMEM((128, 1), F32),
                        pltpu.VMEM((lc, BR_W), F32)],
        compiler_params=_cparams(("arbitrary", "arbitrary")),
        name="mixers",
    )(proj, proj, proj, proj, cos_t, sin_t, c0, n0, m0, sh0, sr0, lb_logits, g_ml, g_hg, g_rt)
    return outs


def _fox_prompt_kernel(tq, tk, qi_ref, kj_ref, q_ref, k_ref, v_ref, ck_ref, o_ref, m_s, l_s, acc_s):
    s = pl.program_id(0)
    i = qi_ref[s]
    j = kj_ref[s]
    masks = _head_masks()

    @pl.when(j == 0)
    def _init():
        m_s[...] = jnp.full_like(m_s, NEG)
        l_s[...] = jnp.zeros_like(l_s)
        acc_s[...] = jnp.zeros_like(acc_s)

    q = q_ref[...] * QK_SCALE
    kb = k_ref[...].astype(BF16)
    vb = v_ref[...].astype(BF16)
    nck = -ck_ref[...]
    visible = (i * tq + _iota((tq, tk), 0)) >= (j * tk + _iota((tq, tk), 1))
    for h in range(HEADS):
        qh = jnp.where(masks[h], q, 0.0).astype(BF16)
        sc = lax.dot_general(qh, kb, (((1,), (1,)), ((), ())), preferred_element_type=F32) + nck[h:h + 1, :]
        sc = jnp.where(visible, sc, NEG)
        m_old = m_s[h]
        m_new = jnp.maximum(m_old, jnp.max(sc, axis=1, keepdims=True))
        alpha = jnp.exp(m_old - m_new)
        p = jnp.exp(sc - m_new)
        l_s[h] = alpha * l_s[h] + jnp.sum(p, axis=1, keepdims=True)
        acc_s[h] = alpha * acc_s[h] + jnp.dot(p.astype(BF16), vb, preferred_element_type=F32)
        m_s[h] = m_new

    @pl.when(j == i)
    def _fin():
        out = jnp.zeros((tq, BR_W), F32)
        for h in range(HEADS):
            out = jnp.where(masks[h], acc_s[h] / l_s[h], out)
        o_ref[...] = out


def _fox_prompt(proj, cum_t, tq):
    t = proj.shape[0]
    tk = tq
    nq = t // tq
    qi = np.concatenate([np.full(i + 1, i, np.int32) for i in range(nq)])
    kj = np.concatenate([np.arange(i + 1, dtype=np.int32) for i in range(nq)])
    cq = P_FX // BR_W
    grid_spec = pltpu.PrefetchScalarGridSpec(
        num_scalar_prefetch=2,
        grid=(len(qi),),
        in_specs=[pl.BlockSpec((tq, BR_W), lambda s, qi, kj: (qi[s], cq)),
                  pl.BlockSpec((tk, BR_W), lambda s, qi, kj: (kj[s], cq + 1)),
                  pl.BlockSpec((tk, BR_W), lambda s, qi, kj: (kj[s], cq + 2)),
                  pl.BlockSpec((8, tk), lambda s, qi, kj: (0, kj[s]))],
        out_specs=pl.BlockSpec((tq, BR_W), lambda s, qi, kj: (qi[s], 0)),
        scratch_shapes=[pltpu.VMEM((HEADS, tq, 1), F32), pltpu.VMEM((HEADS, tq, 1), F32),
                        pltpu.VMEM((HEADS, tq, BR_W), F32)],
    )
    return pl.pallas_call(
        functools.partial(_fox_prompt_kernel, tq, tk),
        grid_spec=grid_spec,
        out_shape=jax.ShapeDtypeStruct((t, BR_W), F32),
        compiler_params=_cparams(("arbitrary",)),
        name="fox_prompt",
    )(jnp.asarray(qi), jnp.asarray(kj), proj, proj, proj, cum_t)


def _fox_decode_kernel(t_new, n_slots, pt_ref, q_ref, k_ref, v_ref, sm_ref, *rest):
    k_pages = rest[0:n_slots]
    v_pages = rest[n_slots:2 * n_slots]
    lf_pages = rest[2 * n_slots:3 * n_slots]
    o_ref = rest[3 * n_slots]
    m_s, l_s, acc_s, carry_s = rest[3 * n_slots + 1:]
    g = pl.program_id(1)
    n_g = pl.num_programs(1)
    masks = _head_masks()
    rows = HEADS * t_new

    q = q_ref[...] * QK_SCALE
    qbd = jnp.concatenate([jnp.where(masks[h], q, 0.0) for h in range(HEADS)], axis=0).astype(BF16)

    def update(sc, vb):
        m_old = m_s[...]
        m_new = jnp.maximum(m_old, jnp.max(sc, axis=1, keepdims=True))
        alpha = jnp.exp(m_old - m_new)
        p = jnp.exp(sc - m_new)
        l_s[...] = alpha * l_s[...] + jnp.sum(p, axis=1, keepdims=True)
        acc_s[...] = alpha * acc_s[...] + jnp.dot(p.astype(BF16), vb, preferred_element_type=F32)
        m_s[...] = m_new

    @pl.when(g == 0)
    def _new_rows():
        pad = jnp.zeros((PAGE_SIZE - t_new, BR_W), F32)
        kn = jnp.concatenate([k_ref[...], pad], axis=0).astype(BF16)
        vn = jnp.concatenate([v_ref[...], pad], axis=0).astype(BF16)
        smp = jnp.concatenate([sm_ref[:, 0:128], jnp.zeros((PAGE_SIZE - t_new, 128), F32)], axis=0)
        lane128 = _iota((1, 128), 1)
        lf = jnp.where((lane128 >= SM_FLF) & (lane128 < SM_FLF + HEADS), _log_sigmoid(smp), 0.0)
        tril = (_iota((PAGE_SIZE, PAGE_SIZE), 0) >= _iota((PAGE_SIZE, PAGE_SIZE), 1)).astype(F32)
        cum_t = _transpose(_dot_hi(tril, lf))
        t_of_row = _iota((rows, PAGE_SIZE), 0) & (t_new - 1)
        s_of_col = _iota((rows, PAGE_SIZE), 1)
        bias = jnp.concatenate([jnp.broadcast_to(-cum_t[SM_FLF + h:SM_FLF + h + 1, :], (t_new, PAGE_SIZE))
                                for h in range(HEADS)], axis=0)
        sc = lax.dot_general(qbd, kn, (((1,), (1,)), ((), ())), preferred_element_type=F32) + bias
        sc = jnp.where(s_of_col <= t_of_row, sc, NEG)
        m_s[...] = jnp.full_like(m_s, NEG)
        l_s[...] = jnp.zeros_like(l_s)
        acc_s[...] = jnp.zeros_like(acc_s)
        carry_s[...] = jnp.zeros_like(carry_s)
        update(sc, vn)

    strict = (_iota((PAGE_SIZE, PAGE_SIZE), 0) > _iota((PAGE_SIZE, PAGE_SIZE), 1)).astype(F32)
    for slot in reversed(range(n_slots)):
        lf = lf_pages[slot][...]
        carry = carry_s[...]
        suf = _dot_hi(lf, strict) + carry
        carry_s[...] = carry + jnp.sum(lf, axis=1, keepdims=True)
        bias = jnp.concatenate([jnp.broadcast_to(suf[h:h + 1, :], (t_new, PAGE_SIZE)) for h in range(HEADS)], axis=0)
        kb = k_pages[slot][...].astype(BF16)
        sc = lax.dot_general(qbd, kb, (((1,), (1,)), ((), ())), preferred_element_type=F32) + bias
        update(sc, v_pages[slot][...].astype(BF16))

    @pl.when(g == n_g - 1)
    def _fin():
        res = acc_s[...] / l_s[...]
        out = jnp.zeros((t_new, BR_W), F32)
        for h in range(HEADS):
            out = jnp.where(masks[h], res[h * t_new:(h + 1) * t_new, :], out)
        o_ref[...] = out


def _fox_decode(layer, proj, cache_k, cache_v, cache_lf_t, page_table, n_slots):
    db, t_new, _ = proj.shape
    n_pages = page_table.shape[1]
    n_g = n_pages // n_slots
    cq = P_FX // BR_W

    def page_idx(slot):
        return lambda b, g, pt: (layer, pt[b * n_pages + (n_g - 1 - g) * n_slots + slot], 0, 0)

    kv_specs = [pl.BlockSpec((None, None, PAGE_SIZE, BR_W), page_idx(s)) for s in range(n_slots)]
    lf_specs = [pl.BlockSpec((None, None, HEADS, PAGE_SIZE), page_idx(s)) for s in range(n_slots)]
    rows = HEADS * t_new
    grid_spec = pltpu.PrefetchScalarGridSpec(
        num_scalar_prefetch=1,
        grid=(db, n_g),
        in_specs=[pl.BlockSpec((None, t_new, BR_W), lambda b, g, pt: (b, 0, cq)),
                  pl.BlockSpec((None, t_new, BR_W), lambda b, g, pt: (b, 0, cq + 1)),
                  pl.BlockSpec((None, t_new, BR_W), lambda b, g, pt: (b, 0, cq + 2)),
                  pl.BlockSpec((None, t_new, BR_W), lambda b, g, pt: (b, 0, P_SM // BR_W))]
                 + kv_specs + kv_specs + lf_specs,
        out_specs=pl.BlockSpec((None, t_new, BR_W), lambda b, g, pt: (b, 0, 0)),
        scratch_shapes=[pltpu.VMEM((rows, 1), F32), pltpu.VMEM((rows, 1), F32), pltpu.VMEM((rows, BR_W), F32),
                        pltpu.VMEM((HEADS, 1), F32)],
    )
    return pl.pallas_call(
        functools.partial(_fox_decode_kernel, t_new, n_slots),
        grid_spec=grid_spec,
        out_shape=jax.ShapeDtypeStruct((db, t_new, BR_W), F32),
        compiler_params=_cparams(("arbitrary", "arbitrary")),
        name="fox_decode",
    )(page_table.reshape(-1), proj, proj, proj, proj,
      *([cache_k] * n_slots), *([cache_v] * n_slots), *([cache_lf_t] * n_slots))


def _merge_kernel(br_ref, fx_ref, g0_ref, g1_ref, g2_ref, g3_ref, x_ref, mod_ref, wb_ref, wo_ref, nw_ref,
                  x1_ref, h2_ref):
    merged = _sigmoid(g0_ref[...]) * _dot(br_ref[:, 0:BR_W], wb_ref[0])
    merged += _sigmoid(g1_ref[...]) * _dot(br_ref[:, BR_W:2 * BR_W], wb_ref[1])
    merged += _sigmoid(g2_ref[...]) * _dot(br_ref[:, 2 * BR_W:3 * BR_W], wb_ref[2])
    merged += _sigmoid(g3_ref[...]) * _dot(fx_ref[...], wb_ref[3])
    x1 = x_ref[...] + mod_ref[:, 2 * D_MODEL:3 * D_MODEL] * _dot(merged, wo_ref[...])
    x1_ref[...] = x1
    h2 = _rms(x1) * nw_ref[...]
    h2_ref[...] = h2 * (1.0 + mod_ref[:, 4 * D_MODEL:5 * D_MODEL]) + mod_ref[:, 3 * D_MODEL:4 * D_MODEL]


def _merge(br3, fox, proj, x, mod, w_branch_b, w_out_b, norm_w):
    t = x.shape[0]
    tm = min(t, 256)
    per_row = mod.shape[0] != 1
    mod_spec = (pl.BlockSpec((tm, 6 * D_MODEL), lambda i: (i, 0)) if per_row
                else pl.BlockSpec((1, 6 * D_MODEL), lambda i: (0, 0)))
    gate = lambda b: pl.BlockSpec((tm, D_MODEL), lambda i, b=b: (i, P_GT // D_MODEL + b))
    return pl.pallas_call(
        _merge_kernel,
        grid=(t // tm,),
        in_specs=[pl.BlockSpec((tm, 3 * BR_W), lambda i: (i, 0)),
                  pl.BlockSpec((tm, BR_W), lambda i: (i, 0)),
                  gate(0), gate(1), gate(2), gate(3),
                  pl.BlockSpec((tm, D_MODEL), lambda i: (i, 0)),
                  mod_spec,
                  pl.BlockSpec((N_BRANCH, BR_W, D_MODEL), lambda i: (0, 0, 0)),
                  pl.BlockSpec((D_MODEL, D_MODEL), lambda i: (0, 0)),
                  pl.BlockSpec((1, D_MODEL), lambda i: (0, 0))],
        out_specs=[pl.BlockSpec((tm, D_MODEL), lambda i: (i, 0)),
                   pl.BlockSpec((tm, D_MODEL), lambda i: (i, 0))],
        out_shape=[jax.ShapeDtypeStruct((t, D_MODEL), F32), jax.ShapeDtypeStruct((t, D_MODEL), F32)],
        compiler_params=_cparams(("arbitrary",)),
        name="merge",
    )(br3, fox, proj, proj, proj, proj, x, mod, w_branch_b, w_out_b, norm_w)


def _router_kernel(h_ref, wr_ref, rb_ref, wt_ref):
    tm = h_ref.shape[0]
    logits = _dot_nt(wr_ref[...], h_ref[...])
    s = _sigmoid(logits)
    sel = s + rb_ref[...]
    ninf = -jnp.inf
    sub = _iota((GROUP_SIZE, tm), 0)
    gsc = []
    for g in range(N_GROUPS):
        blk = sel[g * GROUP_SIZE:(g + 1) * GROUP_SIZE, :]
        m1 = jnp.max(blk, axis=0, keepdims=True)
        first = jnp.min(jnp.where(blk == m1, sub, GROUP_SIZE), axis=0, keepdims=True)
        m2 = jnp.max(jnp.where(sub == first, ninf, blk), axis=0, keepdims=True)
        gsc.append(m1 + m2)
    chosen = [jnp.zeros((1, tm), jnp.bool_) for _ in range(N_GROUPS)]
    for _ in range(TOPK_GROUPS):
        mx = gsc[0]
        for g in range(1, N_GROUPS):
            mx = jnp.maximum(mx, gsc[g])
        taken = jnp.zeros((1, tm), jnp.bool_)
        for g in range(N_GROUPS):
            pick = (gsc[g] == mx) & jnp.logical_not(taken)
            taken = taken | pick
            chosen[g] = chosen[g] | pick
            gsc[g] = jnp.where(pick, ninf, gsc[g])
    selm = jnp.concatenate(
        [jnp.where(chosen[g], sel[g * GROUP_SIZE:(g + 1) * GROUP_SIZE, :], ninf) for g in range(N_GROUPS)], axis=0)
    eidx = _iota((N_EXPERTS, tm), 0)
    picked = jnp.zeros((N_EXPERTS, tm), jnp.bool_)
    for _ in range(TOP_K):
        mx = jnp.max(selm, axis=0, keepdims=True)
        first = jnp.min(jnp.where(selm == mx, eidx, N_EXPERTS), axis=0, keepdims=True)
        pick = eidx == first
        picked = picked | pick
        selm = jnp.where(pick, ninf, selm)
    w = jnp.where(picked, s, 0.0)
    wt_ref[...] = w / jnp.sum(w, axis=0, keepdims=True) * ROUTED_SCALE


def _router(h2, w_router_t, router_bias_col):
    t = h2.shape[0]
    tm = min(t, 512)
    return pl.pallas_call(
        _router_kernel,
        grid=(t // tm,),
        in_specs=[pl.BlockSpec((tm, D_MODEL), lambda i: (i, 0)),
                  pl.BlockSpec((N_EXPERTS, D_MODEL), lambda i: (0, 0)),
                  pl.BlockSpec((N_EXPERTS, 1), lambda i: (0, 0))],
        out_specs=pl.BlockSpec((N_EXPERTS, tm), lambda i: (0, i)),
        out_shape=jax.ShapeDtypeStruct((N_EXPERTS, t), F32),
        compiler_params=_cparams(("arbitrary",)),
        name="router",
    )(h2, w_router_t, router_bias_col)


def _moe_kernel(final, h_ref, wt_ref, wgu_ref, wd_ref, wsgu_ref, wsd_ref, x1_ref, mod_ref, nf_ref, o_ref,
                acc_s, hb_s, wtok_s):
    e = pl.program_id(1)
    n_e = pl.num_programs(1)

    def swiglu(gu):
        return _silu(gu[:, 0:D_EXPERT]) * gu[:, D_EXPERT:2 * D_EXPERT]

    @pl.when(e == 0)
    def _init():
        hb = h_ref[...].astype(BF16)
        hb_s[...] = hb
        wtok_s[...] = _transpose(wt_ref[...])
        acc_s[...] = _dot(swiglu(jnp.dot(hb, wsgu_ref[...], preferred_element_type=F32)), wsd_ref[...])

    wcol = jnp.sum(jnp.where(_iota((1, N_EXPERTS), 1) == e, wtok_s[...], 0.0), axis=1, keepdims=True)
    a = swiglu(jnp.dot(hb_s[...], wgu_ref[...], preferred_element_type=F32))
    acc_s[...] += _dot(a, wd_ref[...]) * wcol

    @pl.when(e == n_e - 1)
    def _fin():
        x2 = x1_ref[...] + mod_ref[:, 5 * D_MODEL:6 * D_MODEL] * acc_s[...]
        if final:
            x2 = _rms(x2) * nf_ref[...]
        o_ref[...] = x2


def _moe(h2, wt, wgu_b, wd_b, wsgu_b, wsd_b, x1, mod, norm_f, final):
    t = h2.shape[0]
    tm = min(t, 1024)
    per_row = mod.shape[0] != 1
    mod_spec = (pl.BlockSpec((tm, 6 * D_MODEL), lambda i, e: (i, 0)) if per_row
                else pl.BlockSpec((1, 6 * D_MODEL), lambda i, e: (0, 0)))
    return pl.pallas_call(
        functools.partial(_moe_kernel, final),
        grid=(t // tm, N_EXPERTS),
        in_specs=[pl.BlockSpec((tm, D_MODEL), lambda i, e: (i, 0)),
                  pl.BlockSpec((N_EXPERTS, tm), lambda i, e: (0, i)),
                  pl.BlockSpec((None, D_MODEL, 2 * D_EXPERT), lambda i, e: (e, 0, 0)),
                  pl.BlockSpec((None, D_EXPERT, D_MODEL), lambda i, e: (e, 0, 0)),
                  pl.BlockSpec((D_MODEL, 2 * D_EXPERT), lambda i, e: (0, 0)),
                  pl.BlockSpec((D_EXPERT, D_MODEL), lambda i, e: (0, 0)),
                  pl.BlockSpec((tm, D_MODEL), lambda i, e: (i, 0)),
                  mod_spec,
                  pl.BlockSpec((1, D_MODEL), lambda i, e: (0, 0))],
        out_specs=pl.BlockSpec((tm, D_MODEL), lambda i, e: (i, 0)),
        out_shape=jax.ShapeDtypeStruct((t, D_MODEL), F32),
        scratch_shapes=[pltpu.VMEM((tm, D_MODEL), F32), pltpu.VMEM((tm, D_MODEL), BF16),
                        pltpu.VMEM((tm, N_EXPERTS), F32)],
        compiler_params=_cparams(("arbitrary", "arbitrary")),
        name="moe",
    )(h2, wt, wgu_b, wd_b, wsgu_b, wsd_b, x1, mod, norm_f)


def _permute_in_cols(w):
    ml = w[..., 0:ML_COLS]
    fx = w[..., OFF_FX:OFF_GT]
    small = jnp.concatenate([ml[..., 4 * BR_W:], fx[..., 3 * BR_W:]], axis=-1)
    pad = jnp.zeros(w.shape[:-1] + (P_GT - P_SM - small.shape[-1],), w.dtype)
    return jnp.concatenate([ml[..., :4 * BR_W], w[..., OFF_HG:OFF_RT], w[..., OFF_RT:OFF_FX], fx[..., :3 * BR_W],
                            small, pad, w[..., OFF_GT:]], axis=-1)


def _rope_tables(pos):
    half = HEAD_DIM // 2
    inv = ROPE_BASE ** (-jnp.arange(half, dtype=F32) / half)
    ang = pos.astype(F32)[:, None] * inv[None, :]
    cos = jnp.cos(ang)
    sin = jnp.sin(ang)
    cos_h = jnp.concatenate([cos, cos], axis=-1)
    sin_h = jnp.concatenate([-sin, sin], axis=-1)
    return jnp.tile(cos_h, (1, HEADS)), jnp.tile(sin_h, (1, HEADS))


def kernel(x_prompt, x_sample, cache_fox_k, cache_fox_v, cache_fox_logf, state_mlstm_C, state_mlstm_n, state_mlstm_m, state_hgrn_S, state_ret_S, page_table, c_prompt, c_sample, w_ada, b_ada, norm_mix_w, norm_ffn_w, w_in, b_in, hgrn_lb_logits, mlstm_norm_w, hgrn_norm_w, ret_norm_w, w_branch, w_out, w_router, router_bias, w_exp_gu, w_exp_down, w_shared_gu, w_shared_down, norm_f_w):
    depth = w_in.shape[0]
    bp, seq, _ = x_prompt.shape
    db, t_new, _ = x_sample.shape
    n_pool = cache_fox_k.shape[1]
    n_pages = page_table.shape[1]
    past_len = n_pages * PAGE_SIZE
    assert bp == 1 and seq % 128 == 0 and t_new == 8

    w_in_p = _permute_in_cols(w_in).astype(BF16)
    b_in_p = _permute_in_cols(b_in).reshape(depth, 1, P_TOT)
    w_branch_b = w_branch.astype(BF16)
    w_out_b = w_out.astype(BF16)
    wgu_b = w_exp_gu.astype(BF16)
    wd_b = w_exp_down.astype(BF16)
    wsgu_b = w_shared_gu.astype(BF16)
    wsd_b = w_shared_down.astype(BF16)
    w_router_t = jnp.swapaxes(w_router, 1, 2)
    cache_k = cache_fox_k.reshape(depth, n_pool, PAGE_SIZE, BR_W)
    cache_v = cache_fox_v.reshape(depth, n_pool, PAGE_SIZE, BR_W)
    cache_lf_t = jnp.swapaxes(cache_fox_logf, 2, 3)

    mods = _ada(jnp.concatenate([c_prompt, c_sample], axis=0), w_ada, b_ada)

    cos_p, sin_p = _rope_tables(jnp.arange(seq))
    cos_s, sin_s = _rope_tables(past_len + jnp.arange(t_new))

    def trunk(x, mod_of_layer, cos_t, sin_t, init, fox_fn, lc):
        b, l, _ = x.shape
        xt = x.reshape(b * l, D_MODEL)
        c_in, n_in, m_in, sh_in, sr_in = init
        per_layer = []
        for layer in range(depth):
            mod = mod_of_layer(layer)
            proj = _inproj(xt, mod, norm_mix_w[layer][None], w_in_p[layer], b_in_p[layer])
            proj3 = proj.reshape(b, l, P_TOT)
            br3, lf_rows, cum_t, c_new, n_new, m_new, sh_new, sr_new = _mixers(
                layer, proj3, cos_t, sin_t, c_in[layer], n_in[layer], m_in[layer], sh_in[layer], sr_in[layer],
                hgrn_lb_logits, mlstm_norm_w[layer][None], hgrn_norm_w[layer][None], ret_norm_w[layer][None], lc)
            fox = fox_fn(layer, proj3, cum_t)
            x1, h2 = _merge(br3.reshape(b * l, 3 * BR_W), fox.reshape(b * l, BR_W), proj, xt, mod,
                            w_branch_b[layer], w_out_b[layer], norm_ffn_w[layer][None])
            wt = _router(h2, w_router_t[layer], router_bias[layer][:, None])
            xt = _moe(h2, wt, wgu_b[layer], wd_b[layer], wsgu_b[layer], wsd_b[layer], x1, mod, norm_f_w[None],
                      layer == depth - 1)
            per_layer.append((
                proj3[..., P_FX + BR_W:P_FX + 2 * BR_W].reshape(b, l, HEADS, HEAD_DIM),
                proj3[..., P_FX + 2 * BR_W:P_FX + 3 * BR_W].reshape(b, l, HEADS, HEAD_DIM),
                lf_rows[..., SM_FLF:SM_FLF + HEADS],
                c_new.reshape(b, HEADS, HEAD_DIM, HEAD_DIM),
                n_new.reshape(b, HEADS, HEAD_DIM),
                m_new[:, 0, :HEADS],
                sh_new.reshape(b, HEADS, HEAD_DIM, HEAD_DIM),
                sr_new.reshape(b, HEADS, HEAD_DIM, HEAD_DIM)))
        stacked = tuple(jnp.stack([p[i] for p in per_layer]) for i in range(8))
        return (xt.reshape(b, l, D_MODEL),) + stacked

    def head_major(s, b):
        return s.astype(F32).reshape(depth, b, BR_W, HEAD_DIM)

    zero_state = jnp.zeros((depth, bp, BR_W, HEAD_DIM), F32)
    prompt_init = (zero_state, jnp.zeros((depth, bp, 1, BR_W), F32), jnp.zeros((depth, bp, 1, 128), F32),
                   zero_state, zero_state)
    prompt_out = trunk(x_prompt, lambda layer: mods[layer, 0:1], cos_p, sin_p, prompt_init,
                       lambda layer, proj3, cum_t: _fox_prompt(proj3[0], cum_t[0], min(seq, 512)),
                       128)

    m_pad = jnp.pad(state_mlstm_m.astype(F32), ((0, 0), (0, 0), (0, 128 - HEADS))).reshape(depth, db, 1, 128)
    sample_init = (head_major(state_mlstm_C, db), state_mlstm_n.astype(F32).reshape(depth, db, 1, BR_W), m_pad,
                   head_major(state_hgrn_S, db), head_major(state_ret_S, db))
    n_slots = 16 if n_pages % 16 == 0 else 1
    sample_out = trunk(x_sample, lambda layer: jnp.repeat(mods[layer, 1:], t_new, axis=0), cos_s, sin_s, sample_init,
                       lambda layer, proj3, cum_t: _fox_decode(layer, proj3, cache_k, cache_v, cache_lf_t,
                                                               page_table, n_slots),
                       t_new)

    return (prompt_out[0], sample_out[0]) + prompt_out[1:] + sample_out[1:]
```

```python
import functools
import math

import numpy as np
import jax
import jax.numpy as jnp
from jax import lax
from jax.experimental import pallas as pl
from jax.experimental.pallas import tpu as pltpu

F32 = jnp.float32
BF16 = jnp.bfloat16
HIGHEST = lax.Precision.HIGHEST

D_MODEL = 1024
N_BRANCH = 4
BR_W = 256
HEAD_DIM = 64
HEADS = 4
ROPE_BASE = 10000.0
RMS_EPS = 1e-6
N_EXPERTS = 64
TOP_K = 6
N_GROUPS = 8
GROUP_SIZE = N_EXPERTS // N_GROUPS
TOPK_GROUPS = 4
D_EXPERT = 256
ROUTED_SCALE = 2.5
PAGE_SIZE = 128
QK_SCALE = HEAD_DIM ** -0.5
LOG2E = math.log2(math.e)

ML_COLS = 4 * BR_W + 2 * HEADS
OFF_HG = ML_COLS
OFF_RT = OFF_HG + 4 * BR_W
OFF_FX = OFF_RT + 4 * BR_W
OFF_GT = OFF_FX + 3 * BR_W + HEADS
N_IN = OFF_GT + N_BRANCH * D_MODEL

P_ML, P_HG, P_RT, P_FX, P_SM, P_GT, P_TOT = 0, 1024, 2048, 3072, 3840, 4096, 8192
SM_IG, SM_MLF, SM_FLF = 0, 4, 8

NEG = -1e30
VMEM_LIMIT = 56 * 1024 * 1024


def _cparams(sem):
    return pltpu.CompilerParams(dimension_semantics=sem, vmem_limit_bytes=VMEM_LIMIT)


def _dot(a, b):
    return jnp.dot(a.astype(BF16), b.astype(BF16), preferred_element_type=F32)


def _dot_nt(a, b):
    return lax.dot_general(a.astype(BF16), b.astype(BF16), (((1,), (1,)), ((), ())), preferred_element_type=F32)


def _dot_tn(a, b):
    return lax.dot_general(a.astype(BF16), b.astype(BF16), (((0,), (0,)), ((), ())), preferred_element_type=F32)


def _dot_hi(a, b):
    return jnp.dot(a, b, preferred_element_type=F32, precision=HIGHEST)


def _dot_nt_hi(a, b):
    return lax.dot_general(a, b, (((1,), (1,)), ((), ())), preferred_element_type=F32, precision=HIGHEST)


def _iota(shape, dim):
    return lax.broadcasted_iota(jnp.int32, shape, dim)


def _eye(n):
    return (_iota((n, n), 0) == _iota((n, n), 1)).astype(F32)


def _transpose(x):
    return _dot_nt_hi(_eye(x.shape[1]), x)


def _sigmoid(x):
    return jax.nn.sigmoid(x)


def _silu(x):
    return x * jax.nn.sigmoid(x)


def _log_sigmoid(x):
    return jnp.minimum(x, 0.0) - jnp.log1p(jnp.exp(-jnp.abs(x)))


def _head_masks(n=BR_W):
    lane = _iota((1, n), 1) >> 6
    return [lane == h for h in range(HEADS)]


def _block_diag_mask():
    return (_iota((BR_W, BR_W), 0) >> 6) == (_iota((BR_W, BR_W), 1) >> 6)


def _per_head_lanes(vals, masks):
    out = jnp.where(masks[0], vals[0], 0.0)
    for h in range(1, HEADS):
        out = jnp.where(masks[h], vals[h], out)
    return out


def _rms(x, eps=RMS_EPS):
    return x * lax.rsqrt(jnp.mean(x * x, axis=-1, keepdims=True) + eps)


def _head_norm(o, gain, bdf):
    ms = _dot_hi(o * o, bdf) * (1.0 / HEAD_DIM)
    return o * lax.rsqrt(ms + RMS_EPS) * gain


def _ada_kernel(c_ref, w_ref, b_ref, o_ref):
    o_ref[...] = _dot(_silu(c_ref[...]), w_ref[...]) + b_ref[...]


def _ada(c_all, w_ada, b_ada):
    depth = w_ada.shape[0]
    n_c = c_all.shape[0]
    tn = 1536
    return pl.pallas_call(
        _ada_kernel,
        grid=(depth, 6 * D_MODEL // tn),
        in_specs=[pl.BlockSpec((n_c, D_MODEL), lambda l, j: (0, 0)),
                  pl.BlockSpec((None, D_MODEL, tn), lambda l, j: (l, 0, j)),
                  pl.BlockSpec((None, 1, tn), lambda l, j: (l, 0, j))],
        out_specs=pl.BlockSpec((None, n_c, tn), lambda l, j: (l, 0, j)),
        out_shape=jax.ShapeDtypeStruct((depth, n_c, 6 * D_MODEL), F32),
        compiler_params=_cparams(("arbitrary", "arbitrary")),
        name="ada",
    )(c_all, w_ada, b_ada.reshape(depth, 1, 6 * D_MODEL))


def _inproj_kernel(x_ref, mod_ref, nw_ref, w_ref, b_ref, o_ref, h_scr):
    @pl.when(pl.program_id(1) == 0)
    def _():
        h = _rms(x_ref[...]) * nw_ref[...]
        h = h * (1.0 + mod_ref[:, D_MODEL:2 * D_MODEL]) + mod_ref[:, 0:D_MODEL]
        h_scr[...] = h.astype(BF16)

    o_ref[...] = jnp.dot(h_scr[...], w_ref[...], preferred_element_type=F32) + b_ref[...]


def _inproj(x, mod, norm_w, w_p, b_p):
    t = x.shape[0]
    tm = min(t, 1024)
    tn = 1024
    per_row = mod.shape[0] != 1
    mod_spec = (pl.BlockSpec((tm, 6 * D_MODEL), lambda i, j: (i, 0)) if per_row
                else pl.BlockSpec((1, 6 * D_MODEL), lambda i, j: (0, 0)))
    return pl.pallas_call(
        _inproj_kernel,
        grid=(t // tm, P_TOT // tn),
        in_specs=[pl.BlockSpec((tm, D_MODEL), lambda i, j: (i, 0)),
                  mod_spec,
                  pl.BlockSpec((1, D_MODEL), lambda i, j: (0, 0)),
                  pl.BlockSpec((D_MODEL, tn), lambda i, j: (0, j)),
                  pl.BlockSpec((1, tn), lambda i, j: (0, j))],
        out_specs=pl.BlockSpec((tm, tn), lambda i, j: (i, j)),
        out_shape=jax.ShapeDtypeStruct((t, P_TOT), F32),
        scratch_shapes=[pltpu.VMEM((tm, D_MODEL), BF16)],
        compiler_params=_cparams(("arbitrary", "arbitrary")),
        name="inproj",
    )(x, mod, norm_w, w_p, b_p)


def _mixers_kernel(layer, lc, sc,
                   ml_ref, hg_ref, rt_ref, sm_ref, cos_ref, sin_ref,
                   c0_ref, n0_ref, m0_ref, sh0_ref, sr0_ref, lbl_ref, gml_ref, ghg_ref, grt_ref,
                   br_ref, lf_ref, cumt_ref, cout_ref, nout_ref, mout_ref, shout_ref, srout_ref,
                   cbd, sht, srbd, n_s, m_s, carry_s, ohg_s):
    c = pl.program_id(1)
    n_c = pl.num_programs(1)
    masks = _head_masks()
    bd = _block_diag_mask()
    bdf = bd.astype(F32)
    tile = ((_iota((HEAD_DIM, BR_W), 1) & (HEAD_DIM - 1)) == _iota((HEAD_DIM, BR_W), 0)).astype(F32)

    @pl.when(c == 0)
    def _init():
        def expand(ref):
            return jnp.where(bd, _dot_hi(ref[...], tile), 0.0)
        cbd[...] = expand(c0_ref)
        sht[...] = _transpose(expand(sh0_ref))
        srbd[...] = expand(sr0_ref)
        n_s[...] = n0_ref[...]
        m_s[...] = m0_ref[...]
        carry_s[...] = jnp.zeros_like(carry_s)

    row = _iota((lc, lc), 0)
    col = _iota((lc, lc), 1)
    causal = row >= col
    tril = causal.astype(F32)

    sm = sm_ref[:, 0:128]
    lane128 = _iota((1, 128), 1)
    sm2 = jnp.where((lane128 >= SM_MLF) & (lane128 < SM_FLF + HEADS), _log_sigmoid(sm), sm)
    cum = _dot_hi(tril, sm2)
    sm2_t = _transpose(sm2)
    cum_t = _transpose(cum)
    lf_ref[...] = sm2
    cum_tg = cum_t + carry_s[...]
    carry_s[...] = cum_tg[:, lc - 1:lc]
    cumt_ref[...] = cum_tg[SM_FLF:SM_FLF + 8, :]

    q = ml_ref[:, 0:BR_W]
    k = ml_ref[:, BR_W:2 * BR_W] * QK_SCALE
    v = ml_ref[:, 2 * BR_W:3 * BR_W]
    og = ml_ref[:, 3 * BR_W:4 * BR_W]
    n_row = n_s[...]
    m_row = m_s[...]
    q_c = _dot_nt(q, cbd[...])
    h_all = jnp.zeros((lc, BR_W), F32)
    w_lanes = jnp.zeros((lc, BR_W), F32)
    decay_lanes = jnp.zeros((1, BR_W), F32)
    m_new_row = jnp.zeros((1, 128), F32)
    for h in range(HEADS):
        ig_c = sm2[:, SM_IG + h:SM_IG + h + 1]
        b_c = cum[:, SM_MLF + h:SM_MLF + h + 1]
        ig_r = sm2_t[SM_IG + h:SM_IG + h + 1, :]
        b_r = cum_t[SM_MLF + h:SM_MLF + h + 1, :]
        m_prev = m_row[:, h:h + 1]
        dmat = jnp.where(causal, b_c - b_r + ig_r, NEG)
        m_inter = b_c + m_prev
        m_t = jnp.maximum(m_inter, jnp.max(dmat, axis=1, keepdims=True))
        w_intra = jnp.exp(dmat - m_t)
        w_inter = jnp.exp(m_inter - m_t)
        qh = jnp.where(masks[h], q, 0.0)
        a = _dot_nt(qh, k) * w_intra
        num = _dot(a, v) + w_inter * q_c
        den = jnp.sum(a, axis=1, keepdims=True) + w_inter * jnp.sum(qh * n_row, axis=1, keepdims=True)
        hh = num / jnp.maximum(jnp.abs(den), jnp.exp(-m_t))
        h_all = jnp.where(masks[h], hh, h_all)
        m_new = m_t[lc - 1:lc, :]
        b_last = b_c[lc - 1:lc, :]
        w_s = jnp.exp(b_last - b_c + ig_c - m_new)
        decay = jnp.exp(b_last + m_prev - m_new)
        w_lanes = jnp.where(masks[h], w_s, w_lanes)
        decay_lanes = jnp.where(masks[h], decay, decay_lanes)
        m_new_row = jnp.where(lane128 == h, m_new, m_new_row)
    kw = k * w_lanes
    cbd[...] = cbd[...] * decay_lanes + jnp.where(bd, _dot_tn(v * w_lanes, k), 0.0)
    n_s[...] = n_row * decay_lanes + jnp.sum(kw, axis=0, keepdims=True)
    m_s[...] = m_new_row
    out_ml = _head_norm(h_all, gml_ref[...], bdf) * _sigmoid(og)

    lbl = lbl_ref[...]
    pr = jnp.exp(lbl - jnp.max(lbl, axis=0, keepdims=True))
    pr = pr / jnp.sum(pr, axis=0, keepdims=True)
    lb = jnp.zeros((1, BR_W), F32)
    for i in range(1, layer + 1):
        lb = lb + pr[i:i + 1, :]
    tril_sc = (_iota((sc, sc), 0) >= _iota((sc, sc), 1)).astype(F32)
    s_idx = _iota((sc, BR_W), 0)

    def hg_body(i, carry):
        r0 = pl.multiple_of(i * sc, sc)
        hq = hg_ref[pl.ds(r0, sc), 0:BR_W]
        hf = hg_ref[pl.ds(r0, sc), BR_W:2 * BR_W]
        vi = hg_ref[pl.ds(r0, sc), 2 * BR_W:3 * BR_W]
        qi = _silu(hq)
        f = lb + (1.0 - lb) * _sigmoid(hf)
        ki = 1.0 - f
        bi = _dot_hi(tril_sc, jnp.log(f))
        st = sht[...]
        o_inter = _dot_nt(qi * jnp.exp(bi), st)
        rows = []
        for t in range(sc):
            e_t = jnp.exp(jnp.where(s_idx <= t, bi[t:t + 1, :] - bi, NEG)) * qi[t:t + 1, :] * ki
            rows.append(e_t)
        e_all = jnp.concatenate(rows, axis=0)
        r_all = _dot(e_all, bdf)
        o_diag = jnp.sum(r_all.reshape(sc, sc, BR_W) * vi[None, :, :], axis=1)
        ohg_s[pl.ds(r0, sc), :] = o_inter + o_diag
        b_last = bi[sc - 1:sc, :]
        sht[...] = st * jnp.exp(b_last) + jnp.where(bd, _dot_tn(vi, ki * jnp.exp(b_last - bi)), 0.0)
        return carry

    lax.fori_loop(0, lc // sc, hg_body, 0)
    out_hg = _head_norm(ohg_s[...], ghg_ref[...], bdf) * _silu(hg_ref[:, 3 * BR_W:4 * BR_W])

    cosv = cos_ref[...]
    sinv = sin_ref[...]
    lane = _iota((1, BR_W), 1)
    first_half = (lane & (HEAD_DIM - 1)) < (HEAD_DIM // 2)

    def rope(x):
        partner = jnp.where(first_half, pltpu.roll(x, BR_W - HEAD_DIM // 2, 1), pltpu.roll(x, HEAD_DIM // 2, 1))
        return x * cosv + partner * sinv

    rq = rope(rt_ref[:, 0:BR_W])
    rk = rope(rt_ref[:, BR_W:2 * BR_W]) * QK_SCALE
    rv = rt_ref[:, 2 * BR_W:3 * BR_W]
    lg = [math.log1p(-(2.0 ** (-5.0 - h))) for h in range(HEADS)]
    lg_lanes = _per_head_lanes([jnp.full((1, 1), g, F32) for g in lg], masks)
    diff = (row - col).astype(F32)
    o_rt = jnp.zeros((lc, BR_W), F32)
    for h in range(HEADS):
        dec = jnp.exp(jnp.where(causal, diff * lg[h], NEG))
        a = _dot_nt(jnp.where(masks[h], rq, 0.0), rk) * dec
        o_rt = jnp.where(masks[h], _dot(a, rv), o_rt)
    t_idx = _iota((lc, BR_W), 0).astype(F32)
    o_rt = o_rt + _dot(rq * jnp.exp((t_idx + 1.0) * lg_lanes), srbd[...])
    w_ret = jnp.exp((lc - 1.0 - t_idx) * lg_lanes)
    srbd[...] = srbd[...] * jnp.exp(lc * lg_lanes) + jnp.where(bd, _dot_tn(rk * w_ret, rv), 0.0)
    out_rt = _head_norm(o_rt, grt_ref[...], bdf) * _silu(rt_ref[:, 3 * BR_W:4 * BR_W])

    br_ref[:, 0:BR_W] = out_ml
    br_ref[:, BR_W:2 * BR_W] = out_hg
    br_ref[:, 2 * BR_W:3 * BR_W] = out_rt

    @pl.when(c == n_c - 1)
    def _fin():
        def compact(x):
            return _dot_nt_hi(x, tile)
        cout_ref[...] = compact(cbd[...])
        shout_ref[...] = compact(_transpose(sht[...]))
        srout_ref[...] = compact(srbd[...])
        nout_ref[...] = n_s[...]
        mout_ref[...] = m_s[...]


def _mixers(layer, proj, cos_t, sin_t, c0, n0, m0, sh0, sr0, lb_logits, g_ml, g_hg, g_rt, lc):
    b, l, _ = proj.shape
    sc = min(16, lc)
    n_c = l // lc
    depth = lb_logits.shape[0]
    cb = lambda blk: pl.BlockSpec((None, lc, 1024), lambda bi, ci, blk=blk: (bi, ci, blk))
    st_spec = pl.BlockSpec((None, BR_W, HEAD_DIM), lambda bi, ci: (bi, 0, 0))
    row_spec = lambda n: pl.BlockSpec((None, 1, n), lambda bi, ci: (bi, 0, 0))
    full = lambda r, cc: pl.BlockSpec((r, cc), lambda bi, ci: (0, 0))
    outs = pl.pallas_call(
        functools.partial(_mixers_kernel, layer, lc, sc),
        grid=(b, n_c),
        in_specs=[cb(0), cb(1), cb(2),
                  pl.BlockSpec((None, lc, 256), lambda bi, ci: (bi, ci, P_SM // 256)),
                  pl.BlockSpec((lc, BR_W), lambda bi, ci: (ci, 0)),
                  pl.BlockSpec((lc, BR_W), lambda bi, ci: (ci, 0)),
                  st_spec, row_spec(BR_W), row_spec(128), st_spec, st_spec,
                  full(depth, BR_W), full(1, BR_W), full(1, BR_W), full(1, BR_W)],
        out_specs=[pl.BlockSpec((None, lc, 3 * BR_W), lambda bi, ci: (bi, ci, 0)),
                   pl.BlockSpec((None, lc, 128), lambda bi, ci: (bi, ci, 0)),
                   pl.BlockSpec((None, 8, lc), lambda bi, ci: (bi, 0, ci)),
                   st_spec, row_spec(BR_W), row_spec(128), st_spec, st_spec],
        out_shape=[jax.ShapeDtypeStruct((b, l, 3 * BR_W), F32),
                   jax.ShapeDtypeStruct((b, l, 128), F32),
                   jax.ShapeDtypeStruct((b, 8, l), F32),
                   jax.ShapeDtypeStruct((b, BR_W, HEAD_DIM), F32),
                   jax.ShapeDtypeStruct((b, 1, BR_W), F32),
                   jax.ShapeDtypeStruct((b, 1, 128), F32),
                   jax.ShapeDtypeStruct((b, BR_W, HEAD_DIM), F32),
                   jax.ShapeDtypeStruct((b, BR_W, HEAD_DIM), F32)],
        scratch_shapes=[pltpu.VMEM((BR_W, BR_W), F32), pltpu.VMEM((BR_W, BR_W), F32), pltpu.VMEM((BR_W, BR_W), F32),
                        pltpu.VMEM((1, BR_W), F32), pltpu.VMEM((1, 128), F32), pltpu.VMEM((128, 1), F32),
                        pltpu.VMEM((lc, BR_W), F32)],
        compiler_params=_cparams(("arbitrary", "arbitrary")),
        name="mixers",
    )(proj, proj, proj, proj, cos_t, sin_t, c0, n0, m0, sh0, sr0, lb_logits, g_ml, g_hg, g_rt)
    return outs


def _fox_prompt_kernel(tq, tk, strip, qi_ref, kj_ref, q_ref, k_ref, v_ref, ck_ref, o_ref,
                       m_s, l_s, alpha_s, acc_s, s_scr, p_scr):
    step = pl.program_id(0)
    i = qi_ref[step]
    j = kj_ref[step]
    masks = _head_masks()

    @pl.when(j == 0)
    def _init():
        m_s[...] = jnp.full_like(m_s, NEG)
        l_s[...] = jnp.zeros_like(l_s)
        acc_s[...] = jnp.zeros_like(acc_s)

    q = q_ref[...] * (QK_SCALE * LOG2E)
    kb = k_ref[...].astype(BF16)
    vb = v_ref[...].astype(BF16)
    nck = ck_ref[...] * (-LOG2E)
    for h in range(HEADS):
        s_scr[h] = _dot_nt(jnp.where(masks[h], q, 0.0), kb) + nck[h:h + 1, :]

    n_rep = tk // 128

    def softmax_strips(diagonal):
        col = _iota((strip, tk), 1)
        row = _iota((strip, tk), 0)
        for h in range(HEADS):
            parts = []
            for r0 in range(0, tq, strip):
                sc = s_scr[h, pl.ds(r0, strip), :]
                if diagonal:
                    sc = jnp.where(row + r0 >= col, sc, NEG)
                    s_scr[h, pl.ds(r0, strip), :] = sc
                parts.append(jnp.max(sc, axis=1, keepdims=True))
            m_old = m_s[h]
            m_new = jnp.maximum(m_old, jnp.broadcast_to(jnp.concatenate(parts, axis=0), (tq, 128)))
            alpha = jnp.exp2(m_old - m_new)
            m_s[h] = m_new
            alpha_s[h] = alpha
        for h in range(HEADS):
            for r0 in range(0, tq, strip):
                rows = pl.ds(r0, strip)
                m_rep = jnp.concatenate([m_s[h, rows, :]] * n_rep, axis=1)
                p = jnp.exp2(s_scr[h, rows, :] - m_rep)
                p_scr[h, rows, :] = p.astype(BF16)
                psum = p[:, 0:128]
                for c in range(1, n_rep):
                    psum = psum + p[:, c * 128:(c + 1) * 128]
                l_s[h, rows, :] = alpha_s[h, rows, :] * l_s[h, rows, :] + psum

    @pl.when(j < i)
    def _off_diagonal():
        softmax_strips(False)

    @pl.when(j == i)
    def _diagonal():
        softmax_strips(True)

    for h in range(HEADS):
        alpha = alpha_s[h]
        acc_s[h] = jnp.concatenate([alpha, alpha], axis=1) * acc_s[h] + jnp.dot(p_scr[h], vb,
                                                                               preferred_element_type=F32)

    @pl.when(j == i)
    def _fin():
        out = jnp.zeros((tq, BR_W), F32)
        for h in range(HEADS):
            out = jnp.where(masks[h], acc_s[h] / jnp.sum(l_s[h], axis=1, keepdims=True), out)
        o_ref[...] = out


def _fox_prompt(proj, cum_t, tq):
    t = proj.shape[0]
    tk = tq
    nq = t // tq
    qi = np.concatenate([np.full(i + 1, i, np.int32) for i in range(nq)])
    kj = np.concatenate([np.arange(i + 1, dtype=np.int32) for i in range(nq)])
    cq = P_FX // BR_W
    grid_spec = pltpu.PrefetchScalarGridSpec(
        num_scalar_prefetch=2,
        grid=(len(qi),),
        in_specs=[pl.BlockSpec((tq, BR_W), lambda s, qi, kj: (qi[s], cq)),
                  pl.BlockSpec((tk, BR_W), lambda s, qi, kj: (kj[s], cq + 1)),
                  pl.BlockSpec((tk, BR_W), lambda s, qi, kj: (kj[s], cq + 2)),
                  pl.BlockSpec((8, tk), lambda s, qi, kj: (0, kj[s]))],
        out_specs=pl.BlockSpec((tq, BR_W), lambda s, qi, kj: (qi[s], 0)),
        scratch_shapes=[pltpu.VMEM((HEADS, tq, 128), F32), pltpu.VMEM((HEADS, tq, 128), F32),
                        pltpu.VMEM((HEADS, tq, 128), F32), pltpu.VMEM((HEADS, tq, BR_W), F32),
                        pltpu.VMEM((HEADS, tq, tk), F32), pltpu.VMEM((HEADS, tq, tk), BF16)],
    )
    return pl.pallas_call(
        functools.partial(_fox_prompt_kernel, tq, tk, min(32, tq)),
        grid_spec=grid_spec,
        out_shape=jax.ShapeDtypeStruct((t, BR_W), F32),
        compiler_params=_cparams(("arbitrary",)),
        name="fox_prompt",
    )(jnp.asarray(qi), jnp.asarray(kj), proj, proj, proj, cum_t)


def _fox_decode_kernel(t_new, n_slots, pt_ref, q_ref, k_ref, v_ref, sm_ref, *rest):
    kt_pages = rest[0:n_slots]
    vt_pages = rest[n_slots:2 * n_slots]
    lf_pages = rest[2 * n_slots:3 * n_slots]
    o_ref = rest[3 * n_slots]
    m_s, l_s, acc_s, carry_s = rest[3 * n_slots + 1:]
    g = pl.program_id(1)
    n_g = pl.num_programs(1)
    masks = _head_masks()
    rows = HEADS * t_new

    q = q_ref[...] * QK_SCALE
    qbd = jnp.concatenate([jnp.where(masks[h], q, 0.0) for h in range(HEADS)], axis=0).astype(BF16)

    def per_head_rows(x):
        return jnp.concatenate([jnp.broadcast_to(x[h:h + 1, :], (t_new, x.shape[1])) for h in range(HEADS)], axis=0)

    def softmax_step(sc):
        m_old = m_s[...]
        m_new = jnp.maximum(m_old, jnp.max(sc, axis=1, keepdims=True))
        alpha = jnp.exp(m_old - m_new)
        p = jnp.exp(sc - m_new)
        l_s[...] = alpha * l_s[...] + jnp.sum(p, axis=1, keepdims=True)
        m_s[...] = m_new
        return alpha, p.astype(BF16)

    @pl.when(g == 0)
    def _new_rows():
        pad = jnp.zeros((PAGE_SIZE - t_new, BR_W), F32)
        kn = jnp.concatenate([k_ref[...], pad], axis=0).astype(BF16)
        vn = jnp.concatenate([v_ref[...], pad], axis=0).astype(BF16)
        smp = jnp.concatenate([sm_ref[:, 0:128], jnp.zeros((PAGE_SIZE - t_new, 128), F32)], axis=0)
        lane128 = _iota((1, 128), 1)
        lf = jnp.where((lane128 >= SM_FLF) & (lane128 < SM_FLF + HEADS), _log_sigmoid(smp), 0.0)
        tril = (_iota((PAGE_SIZE, PAGE_SIZE), 0) >= _iota((PAGE_SIZE, PAGE_SIZE), 1)).astype(F32)
        cum_t = _transpose(_dot_hi(tril, lf))
        t_of_row = _iota((rows, PAGE_SIZE), 0) & (t_new - 1)
        s_of_col = _iota((rows, PAGE_SIZE), 1)
        sc = _dot_nt(qbd, kn) - per_head_rows(cum_t[SM_FLF:SM_FLF + HEADS, :])
        sc = jnp.where(s_of_col <= t_of_row, sc, NEG)
        m_s[...] = jnp.full_like(m_s, NEG)
        l_s[...] = jnp.zeros_like(l_s)
        carry_s[...] = jnp.zeros_like(carry_s)
        _, p = softmax_step(sc)
        acc_s[...] = jnp.dot(p, vn, preferred_element_type=F32)

    strict = (_iota((PAGE_SIZE, PAGE_SIZE), 0) > _iota((PAGE_SIZE, PAGE_SIZE), 1)).astype(F32)
    ones = jnp.ones((PAGE_SIZE, PAGE_SIZE), F32)
    later = carry_s[...]
    sufs = [None] * n_slots
    for slot in reversed(range(n_slots)):
        lf = lf_pages[slot][...]
        sufs[slot] = _dot_hi(lf, strict) + later
        later = later + _dot_hi(lf, ones)
    carry_s[...] = later
    bias = per_head_rows(jnp.concatenate(sufs, axis=1))
    sc = jnp.concatenate([jnp.dot(qbd, kt_pages[slot][...].astype(BF16), preferred_element_type=F32)
                          for slot in range(n_slots)], axis=1) + bias
    alpha, p = softmax_step(sc)
    pv = _dot_nt(p[:, 0:PAGE_SIZE], vt_pages[0][...])
    for slot in range(1, n_slots):
        pv += _dot_nt(p[:, slot * PAGE_SIZE:(slot + 1) * PAGE_SIZE], vt_pages[slot][...])
    acc_s[...] = alpha * acc_s[...] + pv

    @pl.when(g == n_g - 1)
    def _fin():
        res = acc_s[...] / l_s[...]
        out = jnp.zeros((t_new, BR_W), F32)
        for h in range(HEADS):
            out = jnp.where(masks[h], res[h * t_new:(h + 1) * t_new, :], out)
        o_ref[...] = out


def _fox_decode(layer, proj, cache_kt, cache_vt, cache_lf_t, page_table, n_slots):
    db, t_new, _ = proj.shape
    n_pages = page_table.shape[1]
    n_g = n_pages // n_slots
    cq = P_FX // BR_W

    def page_idx(slot):
        return lambda b, g, pt: (layer, pt[b * n_pages + (n_g - 1 - g) * n_slots + slot], 0, 0)

    kv_specs = [pl.BlockSpec((None, None, BR_W, PAGE_SIZE), page_idx(s)) for s in range(n_slots)]
    lf_specs = [pl.BlockSpec((None, None, HEADS, PAGE_SIZE), page_idx(s)) for s in range(n_slots)]
    rows = HEADS * t_new
    grid_spec = pltpu.PrefetchScalarGridSpec(
        num_scalar_prefetch=1,
        grid=(db, n_g),
        in_specs=[pl.BlockSpec((None, t_new, BR_W), lambda b, g, pt: (b, 0, cq)),
                  pl.BlockSpec((None, t_new, BR_W), lambda b, g, pt: (b, 0, cq + 1)),
                  pl.BlockSpec((None, t_new, BR_W), lambda b, g, pt: (b, 0, cq + 2)),
                  pl.BlockSpec((None, t_new, BR_W), lambda b, g, pt: (b, 0, P_SM // BR_W))]
                 + kv_specs + kv_specs + lf_specs,
        out_specs=pl.BlockSpec((None, t_new, BR_W), lambda b, g, pt: (b, 0, 0)),
        scratch_shapes=[pltpu.VMEM((rows, 1), F32), pltpu.VMEM((rows, 1), F32), pltpu.VMEM((rows, BR_W), F32),
                        pltpu.VMEM((HEADS, PAGE_SIZE), F32)],
    )
    return pl.pallas_call(
        functools.partial(_fox_decode_kernel, t_new, n_slots),
        grid_spec=grid_spec,
        out_shape=jax.ShapeDtypeStruct((db, t_new, BR_W), F32),
        compiler_params=_cparams(("arbitrary", "arbitrary")),
        name="fox_decode",
    )(page_table.reshape(-1), proj, proj, proj, proj,
      *([cache_kt] * n_slots), *([cache_vt] * n_slots), *([cache_lf_t] * n_slots))


def _merge_kernel(br_ref, fx_ref, g0_ref, g1_ref, g2_ref, g3_ref, x_ref, mod_ref, wb_ref, wo_ref, nw_ref,
                  x1_ref, h2_ref):
    merged = _sigmoid(g0_ref[...]) * _dot(br_ref[:, 0:BR_W], wb_ref[0])
    merged += _sigmoid(g1_ref[...]) * _dot(br_ref[:, BR_W:2 * BR_W], wb_ref[1])
    merged += _sigmoid(g2_ref[...]) * _dot(br_ref[:, 2 * BR_W:3 * BR_W], wb_ref[2])
    merged += _sigmoid(g3_ref[...]) * _dot(fx_ref[...], wb_ref[3])
    x1 = x_ref[...] + mod_ref[:, 2 * D_MODEL:3 * D_MODEL] * _dot(merged, wo_ref[...])
    x1_ref[...] = x1
    h2 = _rms(x1) * nw_ref[...]
    h2_ref[...] = h2 * (1.0 + mod_ref[:, 4 * D_MODEL:5 * D_MODEL]) + mod_ref[:, 3 * D_MODEL:4 * D_MODEL]


def _merge(br3, fox, proj, x, mod, w_branch_b, w_out_b, norm_w):
    t = x.shape[0]
    tm = min(t, 256)
    per_row = mod.shape[0] != 1
    mod_spec = (pl.BlockSpec((tm, 6 * D_MODEL), lambda i: (i, 0)) if per_row
                else pl.BlockSpec((1, 6 * D_MODEL), lambda i: (0, 0)))
    gate = lambda b: pl.BlockSpec((tm, D_MODEL), lambda i, b=b: (i, P_GT // D_MODEL + b))
    return pl.pallas_call(
        _merge_kernel,
        grid=(t // tm,),
        in_specs=[pl.BlockSpec((tm, 3 * BR_W), lambda i: (i, 0)),
                  pl.BlockSpec((tm, BR_W), lambda i: (i, 0)),
                  gate(0), gate(1), gate(2), gate(3),
                  pl.BlockSpec((tm, D_MODEL), lambda i: (i, 0)),
                  mod_spec,
                  pl.BlockSpec((N_BRANCH, BR_W, D_MODEL), lambda i: (0, 0, 0)),
                  pl.BlockSpec((D_MODEL, D_MODEL), lambda i: (0, 0)),
                  pl.BlockSpec((1, D_MODEL), lambda i: (0, 0))],
        out_specs=[pl.BlockSpec((tm, D_MODEL), lambda i: (i, 0)),
                   pl.BlockSpec((tm, D_MODEL), lambda i: (i, 0))],
        out_shape=[jax.ShapeDtypeStruct((t, D_MODEL), F32), jax.ShapeDtypeStruct((t, D_MODEL), F32)],
        compiler_params=_cparams(("arbitrary",)),
        name="merge",
    )(br3, fox, proj, proj, proj, proj, x, mod, w_branch_b, w_out_b, norm_w)


def _router_kernel(h_ref, wr_ref, rb_ref, wt_ref):
    tm = h_ref.shape[0]
    logits = _dot_nt(wr_ref[...], h_ref[...])
    s = _sigmoid(logits)
    sel = s + rb_ref[...]
    ninf = -jnp.inf
    sub = _iota((GROUP_SIZE, tm), 0)
    gsc = []
    for g in range(N_GROUPS):
        blk = sel[g * GROUP_SIZE:(g + 1) * GROUP_SIZE, :]
        m1 = jnp.max(blk, axis=0, keepdims=True)
        first = jnp.min(jnp.where(blk == m1, sub, GROUP_SIZE), axis=0, keepdims=True)
        m2 = jnp.max(jnp.where(sub == first, ninf, blk), axis=0, keepdims=True)
        gsc.append(m1 + m2)
    chosen = [jnp.zeros((1, tm), jnp.bool_) for _ in range(N_GROUPS)]
    for _ in range(TOPK_GROUPS):
        mx = gsc[0]
        for g in range(1, N_GROUPS):
            mx = jnp.maximum(mx, gsc[g])
        taken = jnp.zeros((1, tm), jnp.bool_)
        for g in range(N_GROUPS):
            pick = (gsc[g] == mx) & jnp.logical_not(taken)
            taken = taken | pick
            chosen[g] = chosen[g] | pick
            gsc[g] = jnp.where(pick, ninf, gsc[g])
    selm = jnp.concatenate(
        [jnp.where(chosen[g], sel[g * GROUP_SIZE:(g + 1) * GROUP_SIZE, :], ninf) for g in range(N_GROUPS)], axis=0)
    eidx = _iota((N_EXPERTS, tm), 0)
    picked = jnp.zeros((N_EXPERTS, tm), jnp.bool_)
    for _ in range(TOP_K):
        mx = jnp.max(selm, axis=0, keepdims=True)
        first = jnp.min(jnp.where(selm == mx, eidx, N_EXPERTS), axis=0, keepdims=True)
        pick = eidx == first
        picked = picked | pick
        selm = jnp.where(pick, ninf, selm)
    w = jnp.where(picked, s, 0.0)
    wt_ref[...] = w / jnp.sum(w, axis=0, keepdims=True) * ROUTED_SCALE


def _router(h2, w_router_t, router_bias_col):
    t = h2.shape[0]
    tm = min(t, 512)
    return pl.pallas_call(
        _router_kernel,
        grid=(t // tm,),
        in_specs=[pl.BlockSpec((tm, D_MODEL), lambda i: (i, 0)),
                  pl.BlockSpec((N_EXPERTS, D_MODEL), lambda i: (0, 0)),
                  pl.BlockSpec((N_EXPERTS, 1), lambda i: (0, 0))],
        out_specs=pl.BlockSpec((N_EXPERTS, tm), lambda i: (0, i)),
        out_shape=jax.ShapeDtypeStruct((N_EXPERTS, t), F32),
        compiler_params=_cparams(("arbitrary",)),
        name="router",
    )(h2, w_router_t, router_bias_col)


def _moe_kernel(final, h_ref, wt_ref, wgu_ref, wd_ref, wsgu_ref, wsd_ref, x1_ref, mod_ref, nf_ref, o_ref,
                acc_s, hb_s, wtok_s):
    e = pl.program_id(1)
    n_e = pl.num_programs(1)

    def swiglu(gu):
        return _silu(gu[:, 0:D_EXPERT]) * gu[:, D_EXPERT:2 * D_EXPERT]

    @pl.when(e == 0)
    def _init():
        hb = h_ref[...].astype(BF16)
        hb_s[...] = hb
        wtok_s[...] = _transpose(wt_ref[...])
        acc_s[...] = _dot(swiglu(jnp.dot(hb, wsgu_ref[...], preferred_element_type=F32)), wsd_ref[...])

    wcol = jnp.sum(jnp.where(_iota((1, N_EXPERTS), 1) == e, wtok_s[...], 0.0), axis=1, keepdims=True)
    a = swiglu(jnp.dot(hb_s[...], wgu_ref[...], preferred_element_type=F32))
    acc_s[...] += _dot(a, wd_ref[...]) * wcol

    @pl.when(e == n_e - 1)
    def _fin():
        x2 = x1_ref[...] + mod_ref[:, 5 * D_MODEL:6 * D_MODEL] * acc_s[...]
        if final:
            x2 = _rms(x2) * nf_ref[...]
        o_ref[...] = x2


def _moe(h2, wt, wgu_b, wd_b, wsgu_b, wsd_b, x1, mod, norm_f, final):
    t = h2.shape[0]
    tm = min(t, 1024)
    per_row = mod.shape[0] != 1
    mod_spec = (pl.BlockSpec((tm, 6 * D_MODEL), lambda i, e: (i, 0)) if per_row
                else pl.BlockSpec((1, 6 * D_MODEL), lambda i, e: (0, 0)))
    return pl.pallas_call(
        functools.partial(_moe_kernel, final),
        grid=(t // tm, N_EXPERTS),
        in_specs=[pl.BlockSpec((tm, D_MODEL), lambda i, e: (i, 0)),
                  pl.BlockSpec((N_EXPERTS, tm), lambda i, e: (0, i)),
                  pl.BlockSpec((None, D_MODEL, 2 * D_EXPERT), lambda i, e: (e, 0, 0)),
                  pl.BlockSpec((None, D_EXPERT, D_MODEL), lambda i, e: (e, 0, 0)),
                  pl.BlockSpec((D_MODEL, 2 * D_EXPERT), lambda i, e: (0, 0)),
                  pl.BlockSpec((D_EXPERT, D_MODEL), lambda i, e: (0, 0)),
                  pl.BlockSpec((tm, D_MODEL), lambda i, e: (i, 0)),
                  mod_spec,
                  pl.BlockSpec((1, D_MODEL), lambda i, e: (0, 0))],
        out_specs=pl.BlockSpec((tm, D_MODEL), lambda i, e: (i, 0)),
        out_shape=jax.ShapeDtypeStruct((t, D_MODEL), F32),
        scratch_shapes=[pltpu.VMEM((tm, D_MODEL), F32), pltpu.VMEM((tm, D_MODEL), BF16),
                        pltpu.VMEM((tm, N_EXPERTS), F32)],
        compiler_params=_cparams(("arbitrary", "arbitrary")),
        name="moe",
    )(h2, wt, wgu_b, wd_b, wsgu_b, wsd_b, x1, mod, norm_f)


def _permute_in_cols(w):
    ml = w[..., 0:ML_COLS]
    fx = w[..., OFF_FX:OFF_GT]
    small = jnp.concatenate([ml[..., 4 * BR_W:], fx[..., 3 * BR_W:]], axis=-1)
    pad = jnp.zeros(w.shape[:-1] + (P_GT - P_SM - small.shape[-1],), w.dtype)
    return jnp.concatenate([ml[..., :4 * BR_W], w[..., OFF_HG:OFF_RT], w[..., OFF_RT:OFF_FX], fx[..., :3 * BR_W],
                            small, pad, w[..., OFF_GT:]], axis=-1)


def _rope_tables(pos):
    half = HEAD_DIM // 2
    inv = ROPE_BASE ** (-jnp.arange(half, dtype=F32) / half)
    ang = pos.astype(F32)[:, None] * inv[None, :]
    cos = jnp.cos(ang)
    sin = jnp.sin(ang)
    cos_h = jnp.concatenate([cos, cos], axis=-1)
    sin_h = jnp.concatenate([-sin, sin], axis=-1)
    return jnp.tile(cos_h, (1, HEADS)), jnp.tile(sin_h, (1, HEADS))


def kernel(x_prompt, x_sample, cache_fox_k, cache_fox_v, cache_fox_logf, state_mlstm_C, state_mlstm_n, state_mlstm_m, state_hgrn_S, state_ret_S, page_table, c_prompt, c_sample, w_ada, b_ada, norm_mix_w, norm_ffn_w, w_in, b_in, hgrn_lb_logits, mlstm_norm_w, hgrn_norm_w, ret_norm_w, w_branch, w_out, w_router, router_bias, w_exp_gu, w_exp_down, w_shared_gu, w_shared_down, norm_f_w):
    depth = w_in.shape[0]
    bp, seq, _ = x_prompt.shape
    db, t_new, _ = x_sample.shape
    n_pool = cache_fox_k.shape[1]
    n_pages = page_table.shape[1]
    past_len = n_pages * PAGE_SIZE
    assert bp == 1 and seq % 128 == 0 and t_new == 8

    w_in_p = _permute_in_cols(w_in).astype(BF16)
    b_in_p = _permute_in_cols(b_in).reshape(depth, 1, P_TOT)
    w_branch_b = w_branch.astype(BF16)
    w_out_b = w_out.astype(BF16)
    wgu_b = w_exp_gu.astype(BF16)
    wd_b = w_exp_down.astype(BF16)
    wsgu_b = w_shared_gu.astype(BF16)
    wsd_b = w_shared_down.astype(BF16)
    w_router_t = jnp.swapaxes(w_router, 1, 2)
    cache_k = jnp.transpose(cache_fox_k, (0, 1, 3, 4, 2)).reshape(depth, n_pool, BR_W, PAGE_SIZE)
    cache_v = jnp.transpose(cache_fox_v, (0, 1, 3, 4, 2)).reshape(depth, n_pool, BR_W, PAGE_SIZE)
    cache_lf_t = jnp.swapaxes(cache_fox_logf, 2, 3)

    mods = _ada(jnp.concatenate([c_prompt, c_sample], axis=0), w_ada, b_ada)

    cos_p, sin_p = _rope_tables(jnp.arange(seq))
    cos_s, sin_s = _rope_tables(past_len + jnp.arange(t_new))

    def trunk(x, mod_of_layer, cos_t, sin_t, init, fox_fn, lc):
        b, l, _ = x.shape
        xt = x.reshape(b * l, D_MODEL)
        c_in, n_in, m_in, sh_in, sr_in = init
        per_layer = []
        for layer in range(depth):
            mod = mod_of_layer(layer)
            proj = _inproj(xt, mod, norm_mix_w[layer][None], w_in_p[layer], b_in_p[layer])
            proj3 = proj.reshape(b, l, P_TOT)
            br3, lf_rows, cum_t, c_new, n_new, m_new, sh_new, sr_new = _mixers(
                layer, proj3, cos_t, sin_t, c_in[layer], n_in[layer], m_in[layer], sh_in[layer], sr_in[layer],
                hgrn_lb_logits, mlstm_norm_w[layer][None], hgrn_norm_w[layer][None], ret_norm_w[layer][None], lc)
            fox = fox_fn(layer, proj3, cum_t)
            x1, h2 = _merge(br3.reshape(b * l, 3 * BR_W), fox.reshape(b * l, BR_W), proj, xt, mod,
                            w_branch_b[layer], w_out_b[layer], norm_ffn_w[layer][None])
            wt = _router(h2, w_router_t[layer], router_bias[layer][:, None])
            xt = _moe(h2, wt, wgu_b[layer], wd_b[layer], wsgu_b[layer], wsd_b[layer], x1, mod, norm_f_w[None],
                      layer == depth - 1)
            per_layer.append((
                proj3[..., P_FX + BR_W:P_FX + 2 * BR_W].reshape(b, l, HEADS, HEAD_DIM),
                proj3[..., P_FX + 2 * BR_W:P_FX + 3 * BR_W].reshape(b, l, HEADS, HEAD_DIM),
                lf_rows[..., SM_FLF:SM_FLF + HEADS],
                c_new.reshape(b, HEADS, HEAD_DIM, HEAD_DIM),
                n_new.reshape(b, HEADS, HEAD_DIM),
                m_new[:, 0, :HEADS],
                sh_new.reshape(b, HEADS, HEAD_DIM, HEAD_DIM),
                sr_new.reshape(b, HEADS, HEAD_DIM, HEAD_DIM)))
        stacked = tuple(jnp.stack([p[i] for p in per_layer]) for i in range(8))
        return (xt.reshape(b, l, D_MODEL),) + stacked

    def head_major(s, b):
        return s.astype(F32).reshape(depth, b, BR_W, HEAD_DIM)

    zero_state = jnp.zeros((depth, bp, BR_W, HEAD_DIM), F32)
    prompt_init = (zero_state, jnp.zeros((depth, bp, 1, BR_W), F32), jnp.zeros((depth, bp, 1, 128), F32),
                   zero_state, zero_state)
    prompt_out = trunk(x_prompt, lambda layer: mods[layer, 0:1], cos_p, sin_p, prompt_init,
                       lambda layer, proj3, cum_t: _fox_prompt(proj3[0], cum_t[0], min(seq, 512)),
                       128)

    m_pad = jnp.pad(state_mlstm_m.astype(F32), ((0, 0), (0, 0), (0, 128 - HEADS))).reshape(depth, db, 1, 128)
    sample_init = (head_major(state_mlstm_C, db), state_mlstm_n.astype(F32).reshape(depth, db, 1, BR_W), m_pad,
                   head_major(state_hgrn_S, db), head_major(state_ret_S, db))
    n_slots = 16 if n_pages % 16 == 0 else 1
    sample_out = trunk(x_sample, lambda layer: jnp.repeat(mods[layer, 1:], t_new, axis=0), cos_s, sin_s, sample_init,
                       lambda layer, proj3, cum_t: _fox_decode(layer, proj3, cache_k, cache_v, cache_lf_t,
                                                               page_table, n_slots),
                       t_new)

    return (prompt_out[0], sample_out[0]) + prompt_out[1:] + sample_out[1:]
```

```python
import functools
import math

import numpy as np
import jax
import jax.numpy as jnp
from jax import lax
from jax.experimental import pallas as pl
from jax.experimental.pallas import tpu as pltpu

F32 = jnp.float32
BF16 = jnp.bfloat16
HIGHEST = lax.Precision.HIGHEST

D_MODEL = 1024
N_BRANCH = 4
BR_W = 256
HEAD_DIM = 64
HEADS = 4
ROPE_BASE = 10000.0
RMS_EPS = 1e-6
N_EXPERTS = 64
TOP_K = 6
N_GROUPS = 8
GROUP_SIZE = N_EXPERTS // N_GROUPS
TOPK_GROUPS = 4
D_EXPERT = 256
ROUTED_SCALE = 2.5
PAGE_SIZE = 128
QK_SCALE = HEAD_DIM ** -0.5
LOG2E = math.log2(math.e)

ML_COLS = 4 * BR_W + 2 * HEADS
OFF_HG = ML_COLS
OFF_RT = OFF_HG + 4 * BR_W
OFF_FX = OFF_RT + 4 * BR_W
OFF_GT = OFF_FX + 3 * BR_W + HEADS
N_IN = OFF_GT + N_BRANCH * D_MODEL

P_ML, P_HG, P_RT, P_FX, P_SM, P_GT, P_TOT = 0, 1024, 2048, 3072, 3840, 4096, 8192
SM_IG, SM_MLF, SM_FLF = 0, 4, 8

NEG = -1e30
VMEM_LIMIT = 56 * 1024 * 1024


def _cparams(sem):
    return pltpu.CompilerParams(dimension_semantics=sem, vmem_limit_bytes=VMEM_LIMIT)


def _dot(a, b):
    return jnp.dot(a.astype(BF16), b.astype(BF16), preferred_element_type=F32)


def _dot_nt(a, b):
    return lax.dot_general(a.astype(BF16), b.astype(BF16), (((1,), (1,)), ((), ())), preferred_element_type=F32)


def _dot_tn(a, b):
    return lax.dot_general(a.astype(BF16), b.astype(BF16), (((0,), (0,)), ((), ())), preferred_element_type=F32)


def _split3(x):
    x1 = x.astype(BF16)
    r1 = x - x1.astype(F32)
    x2 = r1.astype(BF16)
    x3 = (r1 - x2.astype(F32)).astype(BF16)
    return x1, x2, x3


def _sel(x, m01, dims):
    m = m01.astype(BF16)
    x1, x2, x3 = _split3(x)
    if dims == "mx":
        f = lambda xi: jnp.dot(m, xi, preferred_element_type=F32)
    elif dims == "xm":
        f = lambda xi: jnp.dot(xi, m, preferred_element_type=F32)
    elif dims == "xmT":
        f = lambda xi: lax.dot_general(xi, m, (((1,), (1,)), ((), ())), preferred_element_type=F32)
    else:
        f = lambda xi: lax.dot_general(m, xi, (((1,), (1,)), ((), ())), preferred_element_type=F32)
    return (f(x1) + f(x2)) + f(x3)


def _iota(shape, dim):
    return lax.broadcasted_iota(jnp.int32, shape, dim)


def _eye(n):
    return (_iota((n, n), 0) == _iota((n, n), 1)).astype(F32)


def _transpose(x):
    if x.shape[0] % 128 == 0 and x.shape[1] % 128 == 0:
        return x.T
    return _sel(x, _eye(x.shape[1]), "mxT")


def _sigmoid(x):
    return jax.nn.sigmoid(x)


def _silu(x):
    return x * jax.nn.sigmoid(x)


def _log_sigmoid(x):
    return jnp.minimum(x, 0.0) - jnp.log1p(jnp.exp(-jnp.abs(x)))


def _head_masks(n=BR_W):
    lane = _iota((1, n), 1) >> 6
    return [lane == h for h in range(HEADS)]


def _block_diag_mask():
    return (_iota((BR_W, BR_W), 0) >> 6) == (_iota((BR_W, BR_W), 1) >> 6)


def _per_head_lanes(vals, masks):
    out = jnp.where(masks[0], vals[0], 0.0)
    for h in range(1, HEADS):
        out = jnp.where(masks[h], vals[h], out)
    return out


def _rms(x, eps=RMS_EPS):
    return x * lax.rsqrt(jnp.mean(x * x, axis=-1, keepdims=True) + eps)


def _head_norm(o, gain, bdf):
    ms = _sel(o * o, bdf, "xm") * (1.0 / HEAD_DIM)
    return o * lax.rsqrt(ms + RMS_EPS) * gain


def _ada_kernel(c_ref, w_ref, b_ref, o_ref):
    o_ref[...] = _dot(_silu(c_ref[...]), w_ref[...]) + b_ref[...]


def _ada(c_all, w_ada, b_ada):
    depth = w_ada.shape[0]
    n_c = c_all.shape[0]
    tn = 1536
    return pl.pallas_call(
        _ada_kernel,
        grid=(depth, 6 * D_MODEL // tn),
        in_specs=[pl.BlockSpec((n_c, D_MODEL), lambda l, j: (0, 0)),
                  pl.BlockSpec((None, D_MODEL, tn), lambda l, j: (l, 0, j)),
                  pl.BlockSpec((None, 1, tn), lambda l, j: (l, 0, j))],
        out_specs=pl.BlockSpec((None, n_c, tn), lambda l, j: (l, 0, j)),
        out_shape=jax.ShapeDtypeStruct((depth, n_c, 6 * D_MODEL), F32),
        compiler_params=_cparams(("arbitrary", "arbitrary")),
        name="ada",
    )(c_all, w_ada, b_ada.reshape(depth, 1, 6 * D_MODEL))


def _inproj_kernel(x_ref, mod_ref, nw_ref, w_ref, b_ref, o_ref, h_scr):
    @pl.when(pl.program_id(1) == 0)
    def _():
        h = _rms(x_ref[...]) * nw_ref[...]
        h = h * (1.0 + mod_ref[:, D_MODEL:2 * D_MODEL]) + mod_ref[:, 0:D_MODEL]
        h_scr[...] = h.astype(BF16)

    o_ref[...] = jnp.dot(h_scr[...], w_ref[...], preferred_element_type=F32) + b_ref[...]


def _inproj(x, mod, norm_w, w_p, b_p):
    t = x.shape[0]
    tm = min(t, 1024)
    tn = 1024
    per_row = mod.shape[0] != 1
    mod_spec = (pl.BlockSpec((tm, 6 * D_MODEL), lambda i, j: (i, 0)) if per_row
                else pl.BlockSpec((1, 6 * D_MODEL), lambda i, j: (0, 0)))
    return pl.pallas_call(
        _inproj_kernel,
        grid=(t // tm, P_TOT // tn),
        in_specs=[pl.BlockSpec((tm, D_MODEL), lambda i, j: (i, 0)),
                  mod_spec,
                  pl.BlockSpec((1, D_MODEL), lambda i, j: (0, 0)),
                  pl.BlockSpec((D_MODEL, tn), lambda i, j: (0, j)),
                  pl.BlockSpec((1, tn), lambda i, j: (0, j))],
        out_specs=pl.BlockSpec((tm, tn), lambda i, j: (i, j)),
        out_shape=jax.ShapeDtypeStruct((t, P_TOT), F32),
        scratch_shapes=[pltpu.VMEM((tm, D_MODEL), BF16)],
        compiler_params=_cparams(("arbitrary", "arbitrary")),
        name="inproj",
    )(x, mod, norm_w, w_p, b_p)


def _mixers_kernel(layer, lc, sc,
                   ml_ref, hg_ref, rt_ref, sm_ref, cos_ref, sin_ref,
                   c0_ref, n0_ref, m0_ref, sh0_ref, sr0_ref, lbl_ref, gml_ref, ghg_ref, grt_ref,
                   br_ref, lf_ref, cumt_ref, cout_ref, nout_ref, mout_ref, shout_ref, srout_ref,
                   cbd, sht, srbd, n_s, m_s, carry_s, ohg_s):
    c = pl.program_id(1)
    n_c = pl.num_programs(1)
    masks = _head_masks()
    bd = _block_diag_mask()
    bdf = bd.astype(F32)
    tile = ((_iota((HEAD_DIM, BR_W), 1) & (HEAD_DIM - 1)) == _iota((HEAD_DIM, BR_W), 0)).astype(F32)

    @pl.when(c == 0)
    def _init():
        def expand(ref):
            return jnp.where(bd, _sel(ref[...], tile, "xm"), 0.0)
        cbd[...] = expand(c0_ref)
        sht[...] = _transpose(expand(sh0_ref))
        srbd[...] = expand(sr0_ref)
        n_s[...] = n0_ref[...]
        m_s[...] = m0_ref[...]
        carry_s[...] = jnp.zeros_like(carry_s)

    row = _iota((lc, lc), 0)
    col = _iota((lc, lc), 1)
    causal = row >= col
    tril = causal.astype(F32)

    sm = sm_ref[:, 0:128]
    lane128 = _iota((1, 128), 1)
    sm2 = jnp.where((lane128 >= SM_MLF) & (lane128 < SM_FLF + HEADS), _log_sigmoid(sm), sm)
    cum = _sel(sm2, tril, "mx")
    sm2_t = _transpose(sm2)
    cum_t = _transpose(cum)
    lf_ref[...] = sm2
    cum_tg = cum_t + carry_s[...]
    carry_s[...] = cum_tg[:, lc - 1:lc]
    cumt_ref[...] = cum_tg[SM_FLF:SM_FLF + 8, :]

    q = ml_ref[:, 0:BR_W]
    k = ml_ref[:, BR_W:2 * BR_W] * QK_SCALE
    v = ml_ref[:, 2 * BR_W:3 * BR_W]
    og = ml_ref[:, 3 * BR_W:4 * BR_W]
    n_row = n_s[...]
    m_row = m_s[...]
    q_c = _dot_nt(q, cbd[...])
    h_all = jnp.zeros((lc, BR_W), F32)
    w_lanes = jnp.zeros((lc, BR_W), F32)
    decay_lanes = jnp.zeros((1, BR_W), F32)
    m_new_row = jnp.zeros((1, 128), F32)
    for h in range(HEADS):
        ig_c = sm2[:, SM_IG + h:SM_IG + h + 1]
        b_c = cum[:, SM_MLF + h:SM_MLF + h + 1]
        ig_r = sm2_t[SM_IG + h:SM_IG + h + 1, :]
        b_r = cum_t[SM_MLF + h:SM_MLF + h + 1, :]
        m_prev = m_row[:, h:h + 1]
        dmat = jnp.where(causal, b_c - b_r + ig_r, NEG)
        m_inter = b_c + m_prev
        m_t = jnp.maximum(m_inter, jnp.max(dmat, axis=1, keepdims=True))
        w_intra = jnp.exp(dmat - m_t)
        w_inter = jnp.exp(m_inter - m_t)
        qh = jnp.where(masks[h], q, 0.0)
        a = _dot_nt(qh, k) * w_intra
        num = _dot(a, v) + w_inter * q_c
        den = jnp.sum(a, axis=1, keepdims=True) + w_inter * jnp.sum(qh * n_row, axis=1, keepdims=True)
        hh = num / jnp.maximum(jnp.abs(den), jnp.exp(-m_t))
        h_all = jnp.where(masks[h], hh, h_all)
        m_new = m_t[lc - 1:lc, :]
        b_last = b_c[lc - 1:lc, :]
        w_s = jnp.exp(b_last - b_c + ig_c - m_new)
        decay = jnp.exp(b_last + m_prev - m_new)
        w_lanes = jnp.where(masks[h], w_s, w_lanes)
        decay_lanes = jnp.where(masks[h], decay, decay_lanes)
        m_new_row = jnp.where(lane128 == h, m_new, m_new_row)
    kw = k * w_lanes
    cbd[...] = cbd[...] * decay_lanes + jnp.where(bd, _dot_tn(v * w_lanes, k), 0.0)
    n_s[...] = n_row * decay_lanes + jnp.sum(kw, axis=0, keepdims=True)
    m_s[...] = m_new_row
    out_ml = _head_norm(h_all, gml_ref[...], bdf) * _sigmoid(og)

    lbl = lbl_ref[...]
    pr = jnp.exp(lbl - jnp.max(lbl, axis=0, keepdims=True))
    pr = pr / jnp.sum(pr, axis=0, keepdims=True)
    lb = jnp.zeros((1, BR_W), F32)
    for i in range(1, layer + 1):
        lb = lb + pr[i:i + 1, :]
    tril_sc = (_iota((sc, sc), 0) >= _iota((sc, sc), 1)).astype(F32)
    s_idx = _iota((sc, BR_W), 0)

    def hg_body(i, carry):
        r0 = pl.multiple_of(i * sc, sc)
        hq = hg_ref[pl.ds(r0, sc), 0:BR_W]
        hf = hg_ref[pl.ds(r0, sc), BR_W:2 * BR_W]
        vi = hg_ref[pl.ds(r0, sc), 2 * BR_W:3 * BR_W]
        qi = _silu(hq)
        f = lb + (1.0 - lb) * _sigmoid(hf)
        ki = 1.0 - f
        bi = _sel(jnp.log(f), tril_sc, "mx")
        st = sht[...]
        o_inter = _dot_nt(qi * jnp.exp(bi), st)
        rows = []
        for t in range(sc):
            e_t = jnp.exp(jnp.where(s_idx <= t, bi[t:t + 1, :] - bi, NEG)) * qi[t:t + 1, :] * ki
            rows.append(e_t)
        e_all = jnp.concatenate(rows, axis=0)
        r_all = _dot(e_all, bdf)
        o_diag = jnp.sum(r_all.reshape(sc, sc, BR_W) * vi[None, :, :], axis=1)
        ohg_s[pl.ds(r0, sc), :] = o_inter + o_diag
        b_last = bi[sc - 1:sc, :]
        sht[...] = st * jnp.exp(b_last) + jnp.where(bd, _dot_tn(vi, ki * jnp.exp(b_last - bi)), 0.0)
        return carry

    lax.fori_loop(0, lc // sc, hg_body, 0, unroll=True)
    out_hg = _head_norm(ohg_s[...], ghg_ref[...], bdf) * _silu(hg_ref[:, 3 * BR_W:4 * BR_W])

    cosv = cos_ref[...]
    sinv = sin_ref[...]
    lane = _iota((1, BR_W), 1)
    first_half = (lane & (HEAD_DIM - 1)) < (HEAD_DIM // 2)

    def rope(x):
        partner = jnp.where(first_half, pltpu.roll(x, BR_W - HEAD_DIM // 2, 1), pltpu.roll(x, HEAD_DIM // 2, 1))
        return x * cosv + partner * sinv

    rq = rope(rt_ref[:, 0:BR_W])
    rk = rope(rt_ref[:, BR_W:2 * BR_W]) * QK_SCALE
    rv = rt_ref[:, 2 * BR_W:3 * BR_W]
    lg = [math.log1p(-(2.0 ** (-5.0 - h))) for h in range(HEADS)]
    lg_lanes = _per_head_lanes([jnp.full((1, 1), g, F32) for g in lg], masks)
    diff = (row - col).astype(F32)
    o_rt = jnp.zeros((lc, BR_W), F32)
    for h in range(HEADS):
        dec = jnp.exp(jnp.where(causal, diff * lg[h], NEG))
        a = _dot_nt(jnp.where(masks[h], rq, 0.0), rk) * dec
        o_rt = jnp.where(masks[h], _dot(a, rv), o_rt)
    t_idx = _iota((lc, BR_W), 0).astype(F32)
    o_rt = o_rt + _dot(rq * jnp.exp((t_idx + 1.0) * lg_lanes), srbd[...])
    w_ret = jnp.exp((lc - 1.0 - t_idx) * lg_lanes)
    srbd[...] = srbd[...] * jnp.exp(lc * lg_lanes) + jnp.where(bd, _dot_tn(rk * w_ret, rv), 0.0)
    out_rt = _head_norm(o_rt, grt_ref[...], bdf) * _silu(rt_ref[:, 3 * BR_W:4 * BR_W])

    br_ref[:, 0:BR_W] = out_ml
    br_ref[:, BR_W:2 * BR_W] = out_hg
    br_ref[:, 2 * BR_W:3 * BR_W] = out_rt

    @pl.when(c == n_c - 1)
    def _fin():
        def compact(x):
            return _sel(x, tile, "xmT")
        cout_ref[...] = compact(cbd[...])
        shout_ref[...] = compact(_transpose(sht[...]))
        srout_ref[...] = compact(srbd[...])
        nout_ref[...] = n_s[...]
        mout_ref[...] = m_s[...]


def _mixers(layer, proj, cos_t, sin_t, c0, n0, m0, sh0, sr0, lb_logits, g_ml, g_hg, g_rt, lc):
    b, l, _ = proj.shape
    sc = min(16, lc)
    n_c = l // lc
    depth = lb_logits.shape[0]
    cb = lambda blk: pl.BlockSpec((None, lc, 1024), lambda bi, ci, blk=blk: (bi, ci, blk))
    st_spec = pl.BlockSpec((None, BR_W, HEAD_DIM), lambda bi, ci: (bi, 0, 0))
    row_spec = lambda n: pl.BlockSpec((None, 1, n), lambda bi, ci: (bi, 0, 0))
    full = lambda r, cc: pl.BlockSpec((r, cc), lambda bi, ci: (0, 0))
    outs = pl.pallas_call(
        functools.partial(_mixers_kernel, layer, lc, sc),
        grid=(b, n_c),
        in_specs=[cb(0), cb(1), cb(2),
                  pl.BlockSpec((None, lc, 256), lambda bi, ci: (bi, ci, P_SM // 256)),
                  pl.BlockSpec((lc, BR_W), lambda bi, ci: (ci, 0)),
                  pl.BlockSpec((lc, BR_W), lambda bi, ci: (ci, 0)),
                  st_spec, row_spec(BR_W), row_spec(128), st_spec, st_spec,
                  full(depth, BR_W), full(1, BR_W), full(1, BR_W), full(1, BR_W)],
        out_specs=[pl.BlockSpec((None, lc, 3 * BR_W), lambda bi, ci: (bi, ci, 0)),
                   pl.BlockSpec((None, lc, 128), lambda bi, ci: (bi, ci, 0)),
                   pl.BlockSpec((None, 8, lc), lambda bi, ci: (bi, 0, ci)),
                   st_spec, row_spec(BR_W), row_spec(128), st_spec, st_spec],
        out_shape=[jax.ShapeDtypeStruct((b, l, 3 * BR_W), F32),
                   jax.ShapeDtypeStruct((b, l, 128), F32),
                   jax.ShapeDtypeStruct((b, 8, l), F32),
                   jax.ShapeDtypeStruct((b, BR_W, HEAD_DIM), F32),
                   jax.ShapeDtypeStruct((b, 1, BR_W), F32),
                   jax.ShapeDtypeStruct((b, 1, 128), F32),
                   jax.ShapeDtypeStruct((b, BR_W, HEAD_DIM), F32),
                   jax.ShapeDtypeStruct((b, BR_W, HEAD_DIM), F32)],
        scratch_shapes=[pltpu.VMEM((BR_W, BR_W), F32), pltpu.VMEM((BR_W, BR_W), F32), pltpu.VMEM((BR_W, BR_W), F32),
                        pltpu.VMEM((1, BR_W), F32), pltpu.VMEM((1, 128), F32), pltpu.VMEM((128, 1), F32),
                        pltpu.VMEM((lc, BR_W), F32)],
        compiler_params=_cparams(("arbitrary", "arbitrary")),
        name="mixers",
    )(proj, proj, proj, proj, cos_t, sin_t, c0, n0, m0, sh0, sr0, lb_logits, g_ml, g_hg, g_rt)
    return outs


def _fox_prompt_kernel(tq, tk, strip, qi_ref, kj_ref, q_ref, k_ref, v_ref, ck_ref, o_ref,
                       m_s, l_s, alpha_s, acc_s, s_scr, p_scr):
    step = pl.program_id(0)
    i = qi_ref[step]
    j = kj_ref[step]
    masks = _head_masks()

    @pl.when(j == 0)
    def _init():
        m_s[...] = jnp.full_like(m_s, NEG)
        l_s[...] = jnp.zeros_like(l_s)
        acc_s[...] = jnp.zeros_like(acc_s)

    q = q_ref[...] * (QK_SCALE * LOG2E)
    kb = k_ref[...].astype(BF16)
    vb = v_ref[...].astype(BF16)
    nck = ck_ref[...] * (-LOG2E)
    n_rep = tk // 128

    def attend(diagonal):
        col = _iota((strip, tk), 1)
        row = _iota((strip, tk), 0)
        for h in range(HEADS):
            s_scr[h] = _dot_nt(jnp.where(masks[h], q, 0.0), kb) + nck[h:h + 1, :]
            parts = []
            for r0 in range(0, tq, strip):
                sc = s_scr[h, pl.ds(r0, strip), :]
                if diagonal:
                    sc = jnp.where(row + r0 >= col, sc, NEG)
                    s_scr[h, pl.ds(r0, strip), :] = sc
                parts.append(jnp.max(sc, axis=1, keepdims=True))
            m_old = m_s[h]
            m_new = jnp.maximum(m_old, jnp.broadcast_to(jnp.concatenate(parts, axis=0), (tq, 128)))
            m_s[h] = m_new
            alpha_s[h] = jnp.exp2(m_old - m_new)
            for r0 in range(0, tq, strip):
                rows = pl.ds(r0, strip)
                m_rep = jnp.concatenate([m_s[h, rows, :]] * n_rep, axis=1)
                p = jnp.exp2(s_scr[h, rows, :] - m_rep)
                p_scr[h, rows, :] = p.astype(BF16)
                psum = p[:, 0:128]
                for c in range(1, n_rep):
                    psum = psum + p[:, c * 128:(c + 1) * 128]
                l_s[h, rows, :] = alpha_s[h, rows, :] * l_s[h, rows, :] + psum
            alpha = alpha_s[h]
            acc_s[h] = jnp.concatenate([alpha, alpha], axis=1) * acc_s[h] + jnp.dot(p_scr[h], vb,
                                                                                   preferred_element_type=F32)

    @pl.when(j < i)
    def _off_diagonal():
        attend(False)

    @pl.when(j == i)
    def _diagonal():
        attend(True)
        out = jnp.zeros((tq, BR_W), F32)
        for h in range(HEADS):
            out = jnp.where(masks[h], acc_s[h] / jnp.sum(l_s[h], axis=1, keepdims=True), out)
        o_ref[...] = out


def _fox_prompt(proj, cum_t, tq):
    t = proj.shape[0]
    tk = tq
    nq = t // tq
    qi = np.concatenate([np.full(i + 1, i, np.int32) for i in range(nq)])
    kj = np.concatenate([np.arange(i + 1, dtype=np.int32) for i in range(nq)])
    cq = P_FX // BR_W
    grid_spec = pltpu.PrefetchScalarGridSpec(
        num_scalar_prefetch=2,
        grid=(len(qi),),
        in_specs=[pl.BlockSpec((tq, BR_W), lambda s, qi, kj: (qi[s], cq)),
                  pl.BlockSpec((tk, BR_W), lambda s, qi, kj: (kj[s], cq + 1)),
                  pl.BlockSpec((tk, BR_W), lambda s, qi, kj: (kj[s], cq + 2)),
                  pl.BlockSpec((8, tk), lambda s, qi, kj: (0, kj[s]))],
        out_specs=pl.BlockSpec((tq, BR_W), lambda s, qi, kj: (qi[s], 0)),
        scratch_shapes=[pltpu.VMEM((HEADS, tq, 128), F32), pltpu.VMEM((HEADS, tq, 128), F32),
                        pltpu.VMEM((HEADS, tq, 128), F32), pltpu.VMEM((HEADS, tq, BR_W), F32),
                        pltpu.VMEM((HEADS, tq, tk), F32), pltpu.VMEM((HEADS, tq, tk), BF16)],
    )
    return pl.pallas_call(
        functools.partial(_fox_prompt_kernel, tq, tk, min(32, tq)),
        grid_spec=grid_spec,
        out_shape=jax.ShapeDtypeStruct((t, BR_W), F32),
        compiler_params=_cparams(("arbitrary",)),
        name="fox_prompt",
    )(jnp.asarray(qi), jnp.asarray(kj), proj, proj, proj, cum_t)


def _fox_decode_kernel(t_new, n_slots, pt_ref, q_ref, k_ref, v_ref, sm_ref, *rest):
    kt_pages = rest[0:n_slots]
    vt_pages = rest[n_slots:2 * n_slots]
    lf_pages = rest[2 * n_slots:3 * n_slots]
    o_ref = rest[3 * n_slots]
    m_s, l_s, acc_s, carry_s = rest[3 * n_slots + 1:]
    g = pl.program_id(1)
    n_g = pl.num_programs(1)
    masks = _head_masks()
    rows = HEADS * t_new

    q = q_ref[...] * QK_SCALE
    qbd = jnp.concatenate([jnp.where(masks[h], q, 0.0) for h in range(HEADS)], axis=0).astype(BF16)

    def per_head_rows(x):
        return jnp.concatenate([jnp.broadcast_to(x[h:h + 1, :], (t_new, x.shape[1])) for h in range(HEADS)], axis=0)

    def softmax_step(sc):
        m_old = m_s[...]
        m_new = jnp.maximum(m_old, jnp.max(sc, axis=1, keepdims=True))
        alpha = jnp.exp(m_old - m_new)
        p = jnp.exp(sc - m_new)
        l_s[...] = alpha * l_s[...] + jnp.sum(p, axis=1, keepdims=True)
        m_s[...] = m_new
        return alpha, p.astype(BF16)

    @pl.when(g == 0)
    def _new_rows():
        pad = jnp.zeros((PAGE_SIZE - t_new, BR_W), F32)
        kn = jnp.concatenate([k_ref[...], pad], axis=0).astype(BF16)
        vn = jnp.concatenate([v_ref[...], pad], axis=0).astype(BF16)
        smp = jnp.concatenate([sm_ref[:, 0:128], jnp.zeros((PAGE_SIZE - t_new, 128), F32)], axis=0)
        lane128 = _iota((1, 128), 1)
        lf = jnp.where((lane128 >= SM_FLF) & (lane128 < SM_FLF + HEADS), _log_sigmoid(smp), 0.0)
        tril = (_iota((PAGE_SIZE, PAGE_SIZE), 0) >= _iota((PAGE_SIZE, PAGE_SIZE), 1)).astype(F32)
        cum_t = _transpose(_sel(lf, tril, "mx"))
        t_of_row = _iota((rows, PAGE_SIZE), 0) & (t_new - 1)
        s_of_col = _iota((rows, PAGE_SIZE), 1)
        sc = _dot_nt(qbd, kn) - per_head_rows(cum_t[SM_FLF:SM_FLF + HEADS, :])
        sc = jnp.where(s_of_col <= t_of_row, sc, NEG)
        m_s[...] = jnp.full_like(m_s, NEG)
        l_s[...] = jnp.zeros_like(l_s)
        carry_s[...] = jnp.zeros_like(carry_s)
        _, p = softmax_step(sc)
        acc_s[...] = jnp.dot(p, vn, preferred_element_type=F32)

    strict = (_iota((PAGE_SIZE, PAGE_SIZE), 0) > _iota((PAGE_SIZE, PAGE_SIZE), 1)).astype(F32)
    strict_ones = jnp.concatenate([strict, jnp.ones((PAGE_SIZE, PAGE_SIZE), F32)], axis=1)
    later = carry_s[...]
    sufs = [None] * n_slots
    for slot in reversed(range(n_slots)):
        r = _sel(lf_pages[slot][...], strict_ones, "xm")
        sufs[slot] = r[:, 0:PAGE_SIZE] + later
        later = later + r[:, PAGE_SIZE:2 * PAGE_SIZE]
    carry_s[...] = later
    bias = per_head_rows(jnp.concatenate(sufs, axis=1))
    sc = jnp.concatenate([jnp.dot(qbd, kt_pages[slot][...].astype(BF16), preferred_element_type=F32)
                          for slot in range(n_slots)], axis=1) + bias
    alpha, p = softmax_step(sc)
    pv = _dot_nt(p[:, 0:PAGE_SIZE], vt_pages[0][...])
    for slot in range(1, n_slots):
        pv += _dot_nt(p[:, slot * PAGE_SIZE:(slot + 1) * PAGE_SIZE], vt_pages[slot][...])
    acc_s[...] = alpha * acc_s[...] + pv

    @pl.when(g == n_g - 1)
    def _fin():
        res = acc_s[...] / l_s[...]
        out = jnp.zeros((t_new, BR_W), F32)
        for h in range(HEADS):
            out = jnp.where(masks[h], res[h * t_new:(h + 1) * t_new, :], out)
        o_ref[...] = out


def _fox_decode(layer, proj, cache_kt, cache_vt, cache_lf_t, page_table, n_slots):
    db, t_new, _ = proj.shape
    n_pages = page_table.shape[1]
    n_g = n_pages // n_slots
    cq = P_FX // BR_W

    def page_idx(slot):
        return lambda b, g, pt: (layer, pt[b * n_pages + (n_g - 1 - g) * n_slots + slot], 0, 0)

    kv_specs = [pl.BlockSpec((None, None, BR_W, PAGE_SIZE), page_idx(s)) for s in range(n_slots)]
    lf_specs = [pl.BlockSpec((None, None, HEADS, PAGE_SIZE), page_idx(s)) for s in range(n_slots)]
    rows = HEADS * t_new
    grid_spec = pltpu.PrefetchScalarGridSpec(
        num_scalar_prefetch=1,
        grid=(db, n_g),
        in_specs=[pl.BlockSpec((None, t_new, BR_W), lambda b, g, pt: (b, 0, cq)),
                  pl.BlockSpec((None, t_new, BR_W), lambda b, g, pt: (b, 0, cq + 1)),
                  pl.BlockSpec((None, t_new, BR_W), lambda b, g, pt: (b, 0, cq + 2)),
                  pl.BlockSpec((None, t_new, BR_W), lambda b, g, pt: (b, 0, P_SM // BR_W))]
                 + kv_specs + kv_specs + lf_specs,
        out_specs=pl.BlockSpec((None, t_new, BR_W), lambda b, g, pt: (b, 0, 0)),
        scratch_shapes=[pltpu.VMEM((rows, 1), F32), pltpu.VMEM((rows, 1), F32), pltpu.VMEM((rows, BR_W), F32),
                        pltpu.VMEM((HEADS, PAGE_SIZE), F32)],
    )
    return pl.pallas_call(
        functools.partial(_fox_decode_kernel, t_new, n_slots),
        grid_spec=grid_spec,
        out_shape=jax.ShapeDtypeStruct((db, t_new, BR_W), F32),
        compiler_params=_cparams(("arbitrary", "arbitrary")),
        name="fox_decode",
    )(page_table.reshape(-1), proj, proj, proj, proj,
      *([cache_kt] * n_slots), *([cache_vt] * n_slots), *([cache_lf_t] * n_slots))


def _merge_kernel(br_ref, fx_ref, g0_ref, g1_ref, g2_ref, g3_ref, x_ref, mod_ref, wb_ref, wo_ref, nw_ref,
                  x1_ref, h2_ref):
    merged = _sigmoid(g0_ref[...]) * _dot(br_ref[:, 0:BR_W], wb_ref[0])
    merged += _sigmoid(g1_ref[...]) * _dot(br_ref[:, BR_W:2 * BR_W], wb_ref[1])
    merged += _sigmoid(g2_ref[...]) * _dot(br_ref[:, 2 * BR_W:3 * BR_W], wb_ref[2])
    merged += _sigmoid(g3_ref[...]) * _dot(fx_ref[...], wb_ref[3])
    x1 = x_ref[...] + mod_ref[:, 2 * D_MODEL:3 * D_MODEL] * _dot(merged, wo_ref[...])
    x1_ref[...] = x1
    h2 = _rms(x1) * nw_ref[...]
    h2_ref[...] = h2 * (1.0 + mod_ref[:, 4 * D_MODEL:5 * D_MODEL]) + mod_ref[:, 3 * D_MODEL:4 * D_MODEL]


def _merge(br3, fox, proj, x, mod, w_branch_b, w_out_b, norm_w):
    t = x.shape[0]
    tm = min(t, 256)
    per_row = mod.shape[0] != 1
    mod_spec = (pl.BlockSpec((tm, 6 * D_MODEL), lambda i: (i, 0)) if per_row
                else pl.BlockSpec((1, 6 * D_MODEL), lambda i: (0, 0)))
    gate = lambda b: pl.BlockSpec((tm, D_MODEL), lambda i, b=b: (i, P_GT // D_MODEL + b))
    return pl.pallas_call(
        _merge_kernel,
        grid=(t // tm,),
        in_specs=[pl.BlockSpec((tm, 3 * BR_W), lambda i: (i, 0)),
                  pl.BlockSpec((tm, BR_W), lambda i: (i, 0)),
                  gate(0), gate(1), gate(2), gate(3),
                  pl.BlockSpec((tm, D_MODEL), lambda i: (i, 0)),
                  mod_spec,
                  pl.BlockSpec((N_BRANCH, BR_W, D_MODEL), lambda i: (0, 0, 0)),
                  pl.BlockSpec((D_MODEL, D_MODEL), lambda i: (0, 0)),
                  pl.BlockSpec((1, D_MODEL), lambda i: (0, 0))],
        out_specs=[pl.BlockSpec((tm, D_MODEL), lambda i: (i, 0)),
                   pl.BlockSpec((tm, D_MODEL), lambda i: (i, 0))],
        out_shape=[jax.ShapeDtypeStruct((t, D_MODEL), F32), jax.ShapeDtypeStruct((t, D_MODEL), F32)],
        compiler_params=_cparams(("arbitrary",)),
        name="merge",
    )(br3, fox, proj, proj, proj, proj, x, mod, w_branch_b, w_out_b, norm_w)


def _router_kernel(h_ref, wr_ref, rb_ref, wt_ref):
    tm = h_ref.shape[0]
    logits = _dot_nt(wr_ref[...], h_ref[...])
    s = _sigmoid(logits)
    sel = s + rb_ref[...]
    ninf = -jnp.inf
    sub = _iota((GROUP_SIZE, tm), 0)
    gsc = []
    for g in range(N_GROUPS):
        blk = sel[g * GROUP_SIZE:(g + 1) * GROUP_SIZE, :]
        m1 = jnp.max(blk, axis=0, keepdims=True)
        first = jnp.min(jnp.where(blk == m1, sub, GROUP_SIZE), axis=0, keepdims=True)
        m2 = jnp.max(jnp.where(sub == first, ninf, blk), axis=0, keepdims=True)
        gsc.append(m1 + m2)
    chosen = [jnp.zeros((1, tm), jnp.bool_) for _ in range(N_GROUPS)]
    for _ in range(TOPK_GROUPS):
        mx = gsc[0]
        for g in range(1, N_GROUPS):
            mx = jnp.maximum(mx, gsc[g])
        taken = jnp.zeros((1, tm), jnp.bool_)
        for g in range(N_GROUPS):
            pick = (gsc[g] == mx) & jnp.logical_not(taken)
            taken = taken | pick
            chosen[g] = chosen[g] | pick
            gsc[g] = jnp.where(pick, ninf, gsc[g])
    selm = jnp.concatenate(
        [jnp.where(chosen[g], sel[g * GROUP_SIZE:(g + 1) * GROUP_SIZE, :], ninf) for g in range(N_GROUPS)], axis=0)
    eidx = _iota((N_EXPERTS, tm), 0)
    picked = jnp.zeros((N_EXPERTS, tm), jnp.bool_)
    for _ in range(TOP_K):
        mx = jnp.max(selm, axis=0, keepdims=True)
        first = jnp.min(jnp.where(selm == mx, eidx, N_EXPERTS), axis=0, keepdims=True)
        pick = eidx == first
        picked = picked | pick
        selm = jnp.where(pick, ninf, selm)
    w = jnp.where(picked, s, 0.0)
    wt_ref[...] = w / jnp.sum(w, axis=0, keepdims=True) * ROUTED_SCALE


def _router(h2, w_router_t, router_bias_col):
    t = h2.shape[0]
    tm = min(t, 512)
    return pl.pallas_call(
        _router_kernel,
        grid=(t // tm,),
        in_specs=[pl.BlockSpec((tm, D_MODEL), lambda i: (i, 0)),
                  pl.BlockSpec((N_EXPERTS, D_MODEL), lambda i: (0, 0)),
                  pl.BlockSpec((N_EXPERTS, 1), lambda i: (0, 0))],
        out_specs=pl.BlockSpec((N_EXPERTS, tm), lambda i: (0, i)),
        out_shape=jax.ShapeDtypeStruct((N_EXPERTS, t), F32),
        compiler_params=_cparams(("arbitrary",)),
        name="router",
    )(h2, w_router_t, router_bias_col)


def _moe_kernel(final, h_ref, wt_ref, wgu_ref, wd_ref, wsgu_ref, wsd_ref, x1_ref, mod_ref, nf_ref, o_ref,
                acc_s, hb_s, wtok_s):
    e = pl.program_id(1)
    n_e = pl.num_programs(1)

    def swiglu(gu):
        return _silu(gu[:, 0:D_EXPERT]) * gu[:, D_EXPERT:2 * D_EXPERT]

    @pl.when(e == 0)
    def _init():
        hb = h_ref[...].astype(BF16)
        hb_s[...] = hb
        wtok_s[...] = _transpose(wt_ref[...])
        acc_s[...] = _dot(swiglu(jnp.dot(hb, wsgu_ref[...], preferred_element_type=F32)), wsd_ref[...])

    wcol = jnp.sum(jnp.where(_iota((1, N_EXPERTS), 1) == e, wtok_s[...], 0.0), axis=1, keepdims=True)
    a = swiglu(jnp.dot(hb_s[...], wgu_ref[...], preferred_element_type=F32))
    acc_s[...] += _dot(a, wd_ref[...]) * wcol

    @pl.when(e == n_e - 1)
    def _fin():
        x2 = x1_ref[...] + mod_ref[:, 5 * D_MODEL:6 * D_MODEL] * acc_s[...]
        if final:
            x2 = _rms(x2) * nf_ref[...]
        o_ref[...] = x2


def _moe(h2, wt, wgu_b, wd_b, wsgu_b, wsd_b, x1, mod, norm_f, final):
    t = h2.shape[0]
    tm = min(t, 1024)
    per_row = mod.shape[0] != 1
    mod_spec = (pl.BlockSpec((tm, 6 * D_MODEL), lambda i, e: (i, 0)) if per_row
                else pl.BlockSpec((1, 6 * D_MODEL), lambda i, e: (0, 0)))
    return pl.pallas_call(
        functools.partial(_moe_kernel, final),
        grid=(t // tm, N_EXPERTS),
        in_specs=[pl.BlockSpec((tm, D_MODEL), lambda i, e: (i, 0)),
                  pl.BlockSpec((N_EXPERTS, tm), lambda i, e: (0, i)),
                  pl.BlockSpec((None, D_MODEL, 2 * D_EXPERT), lambda i, e: (e, 0, 0)),
                  pl.BlockSpec((None, D_EXPERT, D_MODEL), lambda i, e: (e, 0, 0)),
                  pl.BlockSpec((D_MODEL, 2 * D_EXPERT), lambda i, e: (0, 0)),
                  pl.BlockSpec((D_EXPERT, D_MODEL), lambda i, e: (0, 0)),
                  pl.BlockSpec((tm, D_MODEL), lambda i, e: (i, 0)),
                  mod_spec,
                  pl.BlockSpec((1, D_MODEL), lambda i, e: (0, 0))],
        out_specs=pl.BlockSpec((tm, D_MODEL), lambda i, e: (i, 0)),
        out_shape=jax.ShapeDtypeStruct((t, D_MODEL), F32),
        scratch_shapes=[pltpu.VMEM((tm, D_MODEL), F32), pltpu.VMEM((tm, D_MODEL), BF16),
                        pltpu.VMEM((tm, N_EXPERTS), F32)],
        compiler_params=_cparams(("arbitrary", "arbitrary")),
        name="moe",
    )(h2, wt, wgu_b, wd_b, wsgu_b, wsd_b, x1, mod, norm_f)


def _permute_in_cols(w):
    ml = w[..., 0:ML_COLS]
    fx = w[..., OFF_FX:OFF_GT]
    small = jnp.concatenate([ml[..., 4 * BR_W:], fx[..., 3 * BR_W:]], axis=-1)
    pad = jnp.zeros(w.shape[:-1] + (P_GT - P_SM - small.shape[-1],), w.dtype)
    return jnp.concatenate([ml[..., :4 * BR_W], w[..., OFF_HG:OFF_RT], w[..., OFF_RT:OFF_FX], fx[..., :3 * BR_W],
                            small, pad, w[..., OFF_GT:]], axis=-1)


def _rope_tables(pos):
    half = HEAD_DIM // 2
    inv = ROPE_BASE ** (-jnp.arange(half, dtype=F32) / half)
    ang = pos.astype(F32)[:, None] * inv[None, :]
    cos = jnp.cos(ang)
    sin = jnp.sin(ang)
    cos_h = jnp.concatenate([cos, cos], axis=-1)
    sin_h = jnp.concatenate([-sin, sin], axis=-1)
    return jnp.tile(cos_h, (1, HEADS)), jnp.tile(sin_h, (1, HEADS))


def kernel(x_prompt, x_sample, cache_fox_k, cache_fox_v, cache_fox_logf, state_mlstm_C, state_mlstm_n, state_mlstm_m, state_hgrn_S, state_ret_S, page_table, c_prompt, c_sample, w_ada, b_ada, norm_mix_w, norm_ffn_w, w_in, b_in, hgrn_lb_logits, mlstm_norm_w, hgrn_norm_w, ret_norm_w, w_branch, w_out, w_router, router_bias, w_exp_gu, w_exp_down, w_shared_gu, w_shared_down, norm_f_w):
    depth = w_in.shape[0]
    bp, seq, _ = x_prompt.shape
    db, t_new, _ = x_sample.shape
    n_pool = cache_fox_k.shape[1]
    n_pages = page_table.shape[1]
    past_len = n_pages * PAGE_SIZE
    assert bp == 1 and seq % 128 == 0 and t_new == 8

    w_in_p = _permute_in_cols(w_in).astype(BF16)
    b_in_p = _permute_in_cols(b_in).reshape(depth, 1, P_TOT)
    w_branch_b = w_branch.astype(BF16)
    w_out_b = w_out.astype(BF16)
    wgu_b = w_exp_gu.astype(BF16)
    wd_b = w_exp_down.astype(BF16)
    wsgu_b = w_shared_gu.astype(BF16)
    wsd_b = w_shared_down.astype(BF16)
    w_router_t = jnp.swapaxes(w_router, 1, 2)
    cache_k = jnp.transpose(cache_fox_k, (0, 1, 3, 4, 2)).reshape(depth, n_pool, BR_W, PAGE_SIZE)
    cache_v = jnp.transpose(cache_fox_v, (0, 1, 3, 4, 2)).reshape(depth, n_pool, BR_W, PAGE_SIZE)
    cache_lf_t = jnp.swapaxes(cache_fox_logf, 2, 3)

    mods = _ada(jnp.concatenate([c_prompt, c_sample], axis=0), w_ada, b_ada)

    cos_p, sin_p = _rope_tables(jnp.arange(seq))
    cos_s, sin_s = _rope_tables(past_len + jnp.arange(t_new))

    def trunk(x, mod_of_layer, cos_t, sin_t, init, fox_fn, lc):
        b, l, _ = x.shape
        xt = x.reshape(b * l, D_MODEL)
        c_in, n_in, m_in, sh_in, sr_in = init
        per_layer = []
        for layer in range(depth):
            mod = mod_of_layer(layer)
            proj = _inproj(xt, mod, norm_mix_w[layer][None], w_in_p[layer], b_in_p[layer])
            proj3 = proj.reshape(b, l, P_TOT)
            br3, lf_rows, cum_t, c_new, n_new, m_new, sh_new, sr_new = _mixers(
                layer, proj3, cos_t, sin_t, c_in[layer], n_in[layer], m_in[layer], sh_in[layer], sr_in[layer],
                hgrn_lb_logits, mlstm_norm_w[layer][None], hgrn_norm_w[layer][None], ret_norm_w[layer][None], lc)
            fox = fox_fn(layer, proj3, cum_t)
            x1, h2 = _merge(br3.reshape(b * l, 3 * BR_W), fox.reshape(b * l, BR_W), proj, xt, mod,
                            w_branch_b[layer], w_out_b[layer], norm_ffn_w[layer][None])
            wt = _router(h2, w_router_t[layer], router_bias[layer][:, None])
            xt = _moe(h2, wt, wgu_b[layer], wd_b[layer], wsgu_b[layer], wsd_b[layer], x1, mod, norm_f_w[None],
                      layer == depth - 1)
            per_layer.append((
                proj3[..., P_FX + BR_W:P_FX + 2 * BR_W].reshape(b, l, HEADS, HEAD_DIM),
                proj3[..., P_FX + 2 * BR_W:P_FX + 3 * BR_W].reshape(b, l, HEADS, HEAD_DIM),
                lf_rows[..., SM_FLF:SM_FLF + HEADS],
                c_new.reshape(b, HEADS, HEAD_DIM, HEAD_DIM),
                n_new.reshape(b, HEADS, HEAD_DIM),
                m_new[:, 0, :HEADS],
                sh_new.reshape(b, HEADS, HEAD_DIM, HEAD_DIM),
                sr_new.reshape(b, HEADS, HEAD_DIM, HEAD_DIM)))
        stacked = tuple(jnp.stack([p[i] for p in per_layer]) for i in range(8))
        return (xt.reshape(b, l, D_MODEL),) + stacked

    def head_major(s, b):
        return s.astype(F32).reshape(depth, b, BR_W, HEAD_DIM)

    zero_state = jnp.zeros((depth, bp, BR_W, HEAD_DIM), F32)
    prompt_init = (zero_state, jnp.zeros((depth, bp, 1, BR_W), F32), jnp.zeros((depth, bp, 1, 128), F32),
                   zero_state, zero_state)
    prompt_out = trunk(x_prompt, lambda layer: mods[layer, 0:1], cos_p, sin_p, prompt_init,
                       lambda layer, proj3, cum_t: _fox_prompt(proj3[0], cum_t[0], min(seq, 512)),
                       128)

    m_pad = jnp.pad(state_mlstm_m.astype(F32), ((0, 0), (0, 0), (0, 128 - HEADS))).reshape(depth, db, 1, 128)
    sample_init = (head_major(state_mlstm_C, db), state_mlstm_n.astype(F32).reshape(depth, db, 1, BR_W), m_pad,
                   head_major(state_hgrn_S, db), head_major(state_ret_S, db))
    n_slots = 16 if n_pages % 16 == 0 else 1
    sample_out = trunk(x_sample, lambda layer: jnp.repeat(mods[layer, 1:], t_new, axis=0), cos_s, sin_s, sample_init,
                       lambda layer, proj3, cum_t: _fox_decode(layer, proj3, cache_k, cache_v, cache_lf_t,
                                                               page_table, n_slots),
                       t_new)

    return (prompt_out[0], sample_out[0]) + prompt_out[1:] + sample_out[1:]
```

```python
import functools
import math

import numpy as np
import jax
import jax.numpy as jnp
from jax import lax
from jax.experimental import pallas as pl
from jax.experimental.pallas import tpu as pltpu
from jax.experimental.pallas import tpu_sc as plsc

F32 = jnp.float32
BF16 = jnp.bfloat16
HIGHEST = lax.Precision.HIGHEST

D_MODEL = 1024
N_BRANCH = 4
BR_W = 256
HEAD_DIM = 64
HEADS = 4
ROPE_BASE = 10000.0
RMS_EPS = 1e-6
N_EXPERTS = 64
TOP_K = 6
N_GROUPS = 8
GROUP_SIZE = N_EXPERTS // N_GROUPS
TOPK_GROUPS = 4
D_EXPERT = 256
ROUTED_SCALE = 2.5
PAGE_SIZE = 128
QK_SCALE = HEAD_DIM ** -0.5
LOG2E = math.log2(math.e)

ML_COLS = 4 * BR_W + 2 * HEADS
OFF_HG = ML_COLS
OFF_RT = OFF_HG + 4 * BR_W
OFF_FX = OFF_RT + 4 * BR_W
OFF_GT = OFF_FX + 3 * BR_W + HEADS
N_IN = OFF_GT + N_BRANCH * D_MODEL

P_ML, P_HG, P_RT, P_FX, P_SM, P_GT, P_TOT = 0, 1024, 2048, 3072, 3840, 4096, 8192
SM_IG, SM_MLF, SM_FLF = 0, 4, 8

NEG = -1e30
VMEM_LIMIT = 56 * 1024 * 1024


def _cparams(sem):
    return pltpu.CompilerParams(dimension_semantics=sem, vmem_limit_bytes=VMEM_LIMIT)


def _dot(a, b):
    return jnp.dot(a.astype(BF16), b.astype(BF16), preferred_element_type=F32)


def _dot_nt(a, b):
    return lax.dot_general(a.astype(BF16), b.astype(BF16), (((1,), (1,)), ((), ())), preferred_element_type=F32)


def _dot_tn(a, b):
    return lax.dot_general(a.astype(BF16), b.astype(BF16), (((0,), (0,)), ((), ())), preferred_element_type=F32)


def _split3(x):
    x1 = x.astype(BF16)
    r1 = x - x1.astype(F32)
    x2 = r1.astype(BF16)
    x3 = (r1 - x2.astype(F32)).astype(BF16)
    return x1, x2, x3


def _sel(x, m01, dims):
    m = m01.astype(BF16)
    x1, x2, x3 = _split3(x)
    if dims == "mx":
        f = lambda xi: jnp.dot(m, xi, preferred_element_type=F32)
    elif dims == "xm":
        f = lambda xi: jnp.dot(xi, m, preferred_element_type=F32)
    elif dims == "xmT":
        f = lambda xi: lax.dot_general(xi, m, (((1,), (1,)), ((), ())), preferred_element_type=F32)
    else:
        f = lambda xi: lax.dot_general(m, xi, (((1,), (1,)), ((), ())), preferred_element_type=F32)
    return (f(x1) + f(x2)) + f(x3)


def _iota(shape, dim):
    return lax.broadcasted_iota(jnp.int32, shape, dim)


def _eye(n):
    return (_iota((n, n), 0) == _iota((n, n), 1)).astype(F32)


def _transpose(x):
    if x.shape[0] % 128 == 0 and x.shape[1] % 128 == 0:
        return x.T
    return _sel(x, _eye(x.shape[1]), "mxT")


def _sigmoid(x):
    return jax.nn.sigmoid(x)


def _silu(x):
    return x * jax.nn.sigmoid(x)


def _log_sigmoid(x):
    return jnp.minimum(x, 0.0) - jnp.log1p(jnp.exp(-jnp.abs(x)))


def _head_masks(n=BR_W):
    lane = _iota((1, n), 1) >> 6
    return [lane == h for h in range(HEADS)]


def _block_diag_mask():
    return (_iota((BR_W, BR_W), 0) >> 6) == (_iota((BR_W, BR_W), 1) >> 6)


def _per_head_lanes(vals, masks):
    out = jnp.where(masks[0], vals[0], 0.0)
    for h in range(1, HEADS):
        out = jnp.where(masks[h], vals[h], out)
    return out


def _rms(x, eps=RMS_EPS):
    return x * lax.rsqrt(jnp.mean(x * x, axis=-1, keepdims=True) + eps)


def _head_norm(o, gain, bdf):
    ms = _sel(o * o, bdf, "xm") * (1.0 / HEAD_DIM)
    return o * lax.rsqrt(ms + RMS_EPS) * gain


def _ada_kernel(c_ref, w_ref, b_ref, o_ref):
    o_ref[...] = _dot(_silu(c_ref[...]), w_ref[...]) + b_ref[...]


def _ada(c_all, w_ada, b_ada):
    depth = w_ada.shape[0]
    n_c = c_all.shape[0]
    tn = 1536
    return pl.pallas_call(
        _ada_kernel,
        grid=(depth, 6 * D_MODEL // tn),
        in_specs=[pl.BlockSpec((n_c, D_MODEL), lambda l, j: (0, 0)),
                  pl.BlockSpec((None, D_MODEL, tn), lambda l, j: (l, 0, j)),
                  pl.BlockSpec((None, 1, tn), lambda l, j: (l, 0, j))],
        out_specs=pl.BlockSpec((None, n_c, tn), lambda l, j: (l, 0, j)),
        out_shape=jax.ShapeDtypeStruct((depth, n_c, 6 * D_MODEL), F32),
        compiler_params=_cparams(("arbitrary", "arbitrary")),
        name="ada",
    )(c_all, w_ada, b_ada.reshape(depth, 1, 6 * D_MODEL))


def _inproj_kernel(x_ref, mod_ref, nw_ref, w_ref, b_ref, o_ref, h_scr):
    @pl.when(pl.program_id(1) == 0)
    def _():
        h = _rms(x_ref[...]) * nw_ref[...]
        h = h * (1.0 + mod_ref[:, D_MODEL:2 * D_MODEL]) + mod_ref[:, 0:D_MODEL]
        h_scr[...] = h.astype(BF16)

    o_ref[...] = jnp.dot(h_scr[...], w_ref[...], preferred_element_type=F32) + b_ref[...]


def _inproj(x, mod, norm_w, w_p, b_p):
    t = x.shape[0]
    tm = min(t, 1024)
    tn = 1024
    per_row = mod.shape[0] != 1
    mod_spec = (pl.BlockSpec((tm, 6 * D_MODEL), lambda i, j: (i, 0)) if per_row
                else pl.BlockSpec((1, 6 * D_MODEL), lambda i, j: (0, 0)))
    return pl.pallas_call(
        _inproj_kernel,
        grid=(t // tm, P_TOT // tn),
        in_specs=[pl.BlockSpec((tm, D_MODEL), lambda i, j: (i, 0)),
                  mod_spec,
                  pl.BlockSpec((1, D_MODEL), lambda i, j: (0, 0)),
                  pl.BlockSpec((D_MODEL, tn), lambda i, j: (0, j)),
                  pl.BlockSpec((1, tn), lambda i, j: (0, j))],
        out_specs=pl.BlockSpec((tm, tn), lambda i, j: (i, j)),
        out_shape=jax.ShapeDtypeStruct((t, P_TOT), F32),
        scratch_shapes=[pltpu.VMEM((tm, D_MODEL), BF16)],
        compiler_params=_cparams(("arbitrary", "arbitrary")),
        name="inproj",
    )(x, mod, norm_w, w_p, b_p)


def _mixers_kernel(layer, lc, sc,
                   ml_ref, hg_ref, rt_ref, sm_ref, cos_ref, sin_ref,
                   c0_ref, n0_ref, m0_ref, sh0_ref, sr0_ref, lbl_ref, gml_ref, ghg_ref, grt_ref,
                   br_ref, lf_ref, cumt_ref, cout_ref, nout_ref, mout_ref, shout_ref, srout_ref,
                   cbd, sht, srbd, n_s, m_s, carry_s, ohg_s):
    c = pl.program_id(1)
    n_c = pl.num_programs(1)
    masks = _head_masks()
    bd = _block_diag_mask()
    bdf = bd.astype(F32)
    tile = ((_iota((HEAD_DIM, BR_W), 1) & (HEAD_DIM - 1)) == _iota((HEAD_DIM, BR_W), 0)).astype(F32)

    @pl.when(c == 0)
    def _init():
        def expand(ref):
            return jnp.where(bd, _sel(ref[...], tile, "xm"), 0.0)
        cbd[...] = expand(c0_ref)
        sht[...] = _transpose(expand(sh0_ref))
        srbd[...] = expand(sr0_ref)
        n_s[...] = n0_ref[...]
        m_s[...] = m0_ref[...]
        carry_s[...] = jnp.zeros_like(carry_s)

    row = _iota((lc, lc), 0)
    col = _iota((lc, lc), 1)
    causal = row >= col
    tril = causal.astype(F32)

    sm = sm_ref[:, 0:128]
    lane128 = _iota((1, 128), 1)
    sm2 = jnp.where((lane128 >= SM_MLF) & (lane128 < SM_FLF + HEADS), _log_sigmoid(sm), sm)
    cum = _sel(sm2, tril, "mx")
    sm2_t = _transpose(sm2)
    cum_t = _transpose(cum)
    lf_ref[...] = sm2
    cum_tg = cum_t + carry_s[...]
    carry_s[...] = cum_tg[:, lc - 1:lc]
    cumt_ref[...] = cum_tg[SM_FLF:SM_FLF + 8, :]

    q = ml_ref[:, 0:BR_W]
    k = ml_ref[:, BR_W:2 * BR_W] * QK_SCALE
    v = ml_ref[:, 2 * BR_W:3 * BR_W]
    og = ml_ref[:, 3 * BR_W:4 * BR_W]
    n_row = n_s[...]
    m_row = m_s[...]
    q_c = _dot_nt(q, cbd[...])
    h_all = jnp.zeros((lc, BR_W), F32)
    w_lanes = jnp.zeros((lc, BR_W), F32)
    decay_lanes = jnp.zeros((1, BR_W), F32)
    m_new_row = jnp.zeros((1, 128), F32)
    for h in range(HEADS):
        ig_c = sm2[:, SM_IG + h:SM_IG + h + 1]
        b_c = cum[:, SM_MLF + h:SM_MLF + h + 1]
        ig_r = sm2_t[SM_IG + h:SM_IG + h + 1, :]
        b_r = cum_t[SM_MLF + h:SM_MLF + h + 1, :]
        m_prev = m_row[:, h:h + 1]
        dmat = jnp.where(causal, b_c - b_r + ig_r, NEG)
        m_inter = b_c + m_prev
        m_t = jnp.maximum(m_inter, jnp.max(dmat, axis=1, keepdims=True))
        w_intra = jnp.exp(dmat - m_t)
        w_inter = jnp.exp(m_inter - m_t)
        qh = jnp.where(masks[h], q, 0.0)
        a = _dot_nt(qh, k) * w_intra
        num = _dot(a, v) + w_inter * q_c
        den = jnp.sum(a, axis=1, keepdims=True) + w_inter * jnp.sum(qh * n_row, axis=1, keepdims=True)
        hh = num / jnp.maximum(jnp.abs(den), jnp.exp(-m_t))
        h_all = jnp.where(masks[h], hh, h_all)
        m_new = m_t[lc - 1:lc, :]
        b_last = b_c[lc - 1:lc, :]
        w_s = jnp.exp(b_last - b_c + ig_c - m_new)
        decay = jnp.exp(b_last + m_prev - m_new)
        w_lanes = jnp.where(masks[h], w_s, w_lanes)
        decay_lanes = jnp.where(masks[h], decay, decay_lanes)
        m_new_row = jnp.where(lane128 == h, m_new, m_new_row)
    kw = k * w_lanes
    cbd[...] = cbd[...] * decay_lanes + jnp.where(bd, _dot_tn(v * w_lanes, k), 0.0)
    n_s[...] = n_row * decay_lanes + jnp.sum(kw, axis=0, keepdims=True)
    m_s[...] = m_new_row
    out_ml = _head_norm(h_all, gml_ref[...], bdf) * _sigmoid(og)

    lbl = lbl_ref[...]
    pr = jnp.exp(lbl - jnp.max(lbl, axis=0, keepdims=True))
    pr = pr / jnp.sum(pr, axis=0, keepdims=True)
    lb = jnp.zeros((1, BR_W), F32)
    for i in range(1, layer + 1):
        lb = lb + pr[i:i + 1, :]
    tril_sc = (_iota((sc, sc), 0) >= _iota((sc, sc), 1)).astype(F32)
    s_idx = _iota((sc, BR_W), 0)

    def hg_body(i, carry):
        r0 = pl.multiple_of(i * sc, sc)
        hq = hg_ref[pl.ds(r0, sc), 0:BR_W]
        hf = hg_ref[pl.ds(r0, sc), BR_W:2 * BR_W]
        vi = hg_ref[pl.ds(r0, sc), 2 * BR_W:3 * BR_W]
        qi = _silu(hq)
        f = lb + (1.0 - lb) * _sigmoid(hf)
        ki = 1.0 - f
        bi = _sel(jnp.log(f), tril_sc, "mx")
        st = sht[...]
        o_inter = _dot_nt(qi * jnp.exp(bi), st)
        rows = []
        for t in range(sc):
            e_t = jnp.exp(jnp.where(s_idx <= t, bi[t:t + 1, :] - bi, NEG)) * qi[t:t + 1, :] * ki
            rows.append(e_t)
        e_all = jnp.concatenate(rows, axis=0)
        r_all = _dot(e_all, bdf)
        o_diag = jnp.sum(r_all.reshape(sc, sc, BR_W) * vi[None, :, :], axis=1)
        ohg_s[pl.ds(r0, sc), :] = o_inter + o_diag
        b_last = bi[sc - 1:sc, :]
        sht[...] = st * jnp.exp(b_last) + jnp.where(bd, _dot_tn(vi, ki * jnp.exp(b_last - bi)), 0.0)
        return carry

    lax.fori_loop(0, lc // sc, hg_body, 0, unroll=True)
    out_hg = _head_norm(ohg_s[...], ghg_ref[...], bdf) * _silu(hg_ref[:, 3 * BR_W:4 * BR_W])

    cosv = cos_ref[...]
    sinv = sin_ref[...]
    lane = _iota((1, BR_W), 1)
    first_half = (lane & (HEAD_DIM - 1)) < (HEAD_DIM // 2)

    def rope(x):
        partner = jnp.where(first_half, pltpu.roll(x, BR_W - HEAD_DIM // 2, 1), pltpu.roll(x, HEAD_DIM // 2, 1))
        return x * cosv + partner * sinv

    rq = rope(rt_ref[:, 0:BR_W])
    rk = rope(rt_ref[:, BR_W:2 * BR_W]) * QK_SCALE
    rv = rt_ref[:, 2 * BR_W:3 * BR_W]
    lg = [math.log1p(-(2.0 ** (-5.0 - h))) for h in range(HEADS)]
    lg_lanes = _per_head_lanes([jnp.full((1, 1), g, F32) for g in lg], masks)
    diff = (row - col).astype(F32)
    o_rt = jnp.zeros((lc, BR_W), F32)
    for h in range(HEADS):
        dec = jnp.exp(jnp.where(causal, diff * lg[h], NEG))
        a = _dot_nt(jnp.where(masks[h], rq, 0.0), rk) * dec
        o_rt = jnp.where(masks[h], _dot(a, rv), o_rt)
    t_idx = _iota((lc, BR_W), 0).astype(F32)
    o_rt = o_rt + _dot(rq * jnp.exp((t_idx + 1.0) * lg_lanes), srbd[...])
    w_ret = jnp.exp((lc - 1.0 - t_idx) * lg_lanes)
    srbd[...] = srbd[...] * jnp.exp(lc * lg_lanes) + jnp.where(bd, _dot_tn(rk * w_ret, rv), 0.0)
    out_rt = _head_norm(o_rt, grt_ref[...], bdf) * _silu(rt_ref[:, 3 * BR_W:4 * BR_W])

    br_ref[:, 0:BR_W] = out_ml
    br_ref[:, BR_W:2 * BR_W] = out_hg
    br_ref[:, 2 * BR_W:3 * BR_W] = out_rt

    @pl.when(c == n_c - 1)
    def _fin():
        def compact(x):
            return _sel(x, tile, "xmT")
        cout_ref[...] = compact(cbd[...])
        shout_ref[...] = compact(_transpose(sht[...]))
        srout_ref[...] = compact(srbd[...])
        nout_ref[...] = n_s[...]
        mout_ref[...] = m_s[...]


def _mixers(layer, proj, cos_t, sin_t, c0, n0, m0, sh0, sr0, lb_logits, g_ml, g_hg, g_rt, lc):
    b, l, _ = proj.shape
    sc = min(16, lc)
    n_c = l // lc
    depth = lb_logits.shape[0]
    cb = lambda blk: pl.BlockSpec((None, lc, 1024), lambda bi, ci, blk=blk: (bi, ci, blk))
    st_spec = pl.BlockSpec((None, BR_W, HEAD_DIM), lambda bi, ci: (bi, 0, 0))
    row_spec = lambda n: pl.BlockSpec((None, 1, n), lambda bi, ci: (bi, 0, 0))
    full = lambda r, cc: pl.BlockSpec((r, cc), lambda bi, ci: (0, 0))
    outs = pl.pallas_call(
        functools.partial(_mixers_kernel, layer, lc, sc),
        grid=(b, n_c),
        in_specs=[cb(0), cb(1), cb(2),
                  pl.BlockSpec((None, lc, 256), lambda bi, ci: (bi, ci, P_SM // 256)),
                  pl.BlockSpec((lc, BR_W), lambda bi, ci: (ci, 0)),
                  pl.BlockSpec((lc, BR_W), lambda bi, ci: (ci, 0)),
                  st_spec, row_spec(BR_W), row_spec(128), st_spec, st_spec,
                  full(depth, BR_W), full(1, BR_W), full(1, BR_W), full(1, BR_W)],
        out_specs=[pl.BlockSpec((None, lc, 3 * BR_W), lambda bi, ci: (bi, ci, 0)),
                   pl.BlockSpec((None, lc, 128), lambda bi, ci: (bi, ci, 0)),
                   pl.BlockSpec((None, 8, lc), lambda bi, ci: (bi, 0, ci)),
                   st_spec, row_spec(BR_W), row_spec(128), st_spec, st_spec],
        out_shape=[jax.ShapeDtypeStruct((b, l, 3 * BR_W), F32),
                   jax.ShapeDtypeStruct((b, l, 128), F32),
                   jax.ShapeDtypeStruct((b, 8, l), F32),
                   jax.ShapeDtypeStruct((b, BR_W, HEAD_DIM), F32),
                   jax.ShapeDtypeStruct((b, 1, BR_W), F32),
                   jax.ShapeDtypeStruct((b, 1, 128), F32),
                   jax.ShapeDtypeStruct((b, BR_W, HEAD_DIM), F32),
                   jax.ShapeDtypeStruct((b, BR_W, HEAD_DIM), F32)],
        scratch_shapes=[pltpu.VMEM((BR_W, BR_W), F32), pltpu.VMEM((BR_W, BR_W), F32), pltpu.VMEM((BR_W, BR_W), F32),
                        pltpu.VMEM((1, BR_W), F32), pltpu.VMEM((1, 128), F32), pltpu.VMEM((128, 1), F32),
                        pltpu.VMEM((lc, BR_W), F32)],
        compiler_params=_cparams(("arbitrary", "arbitrary")),
        name="mixers",
    )(proj, proj, proj, proj, cos_t, sin_t, c0, n0, m0, sh0, sr0, lb_logits, g_ml, g_hg, g_rt)
    return outs


def _fox_prompt_kernel(tq, tk, strip, qi_ref, kj_ref, q_ref, k_ref, v_ref, ck_ref, o_ref,
                       m_s, l_s, alpha_s, acc_s, s_scr, p_scr):
    step = pl.program_id(0)
    i = qi_ref[step]
    j = kj_ref[step]
    masks = _head_masks()

    @pl.when(j == 0)
    def _init():
        m_s[...] = jnp.full_like(m_s, NEG)
        l_s[...] = jnp.zeros_like(l_s)
        acc_s[...] = jnp.zeros_like(acc_s)

    q = q_ref[...] * (QK_SCALE * LOG2E)
    kb = k_ref[...].astype(BF16)
    vb = v_ref[...].astype(BF16)
    nck = ck_ref[...] * (-LOG2E)
    n_rep = tk // 128

    def attend(diagonal):
        col = _iota((strip, tk), 1)
        row = _iota((strip, tk), 0)
        for h in range(HEADS):
            s_scr[h] = _dot_nt(jnp.where(masks[h], q, 0.0), kb) + nck[h:h + 1, :]
            parts = []
            for r0 in range(0, tq, strip):
                sc = s_scr[h, pl.ds(r0, strip), :]
                if diagonal:
                    sc = jnp.where(row + r0 >= col, sc, NEG)
                    s_scr[h, pl.ds(r0, strip), :] = sc
                parts.append(jnp.max(sc, axis=1, keepdims=True))
            m_old = m_s[h]
            m_new = jnp.maximum(m_old, jnp.broadcast_to(jnp.concatenate(parts, axis=0), (tq, 128)))
            m_s[h] = m_new
            alpha_s[h] = jnp.exp2(m_old - m_new)
            for r0 in range(0, tq, strip):
                rows = pl.ds(r0, strip)
                m_rep = jnp.concatenate([m_s[h, rows, :]] * n_rep, axis=1)
                p = jnp.exp2(s_scr[h, rows, :] - m_rep)
                p_scr[h, rows, :] = p.astype(BF16)
                psum = p[:, 0:128]
                for c in range(1, n_rep):
                    psum = psum + p[:, c * 128:(c + 1) * 128]
                l_s[h, rows, :] = alpha_s[h, rows, :] * l_s[h, rows, :] + psum
            alpha = alpha_s[h]
            acc_s[h] = jnp.concatenate([alpha, alpha], axis=1) * acc_s[h] + jnp.dot(p_scr[h], vb,
                                                                                   preferred_element_type=F32)

    @pl.when(j < i)
    def _off_diagonal():
        attend(False)

    @pl.when(j == i)
    def _diagonal():
        attend(True)
        out = jnp.zeros((tq, BR_W), F32)
        for h in range(HEADS):
            out = jnp.where(masks[h], acc_s[h] / jnp.sum(l_s[h], axis=1, keepdims=True), out)
        o_ref[...] = out


def _fox_prompt(proj, cum_t, tq):
    t = proj.shape[0]
    tk = tq
    nq = t // tq
    qi = np.concatenate([np.full(i + 1, i, np.int32) for i in range(nq)])
    kj = np.concatenate([np.arange(i + 1, dtype=np.int32) for i in range(nq)])
    cq = P_FX // BR_W
    grid_spec = pltpu.PrefetchScalarGridSpec(
        num_scalar_prefetch=2,
        grid=(len(qi),),
        in_specs=[pl.BlockSpec((tq, BR_W), lambda s, qi, kj: (qi[s], cq)),
                  pl.BlockSpec((tk, BR_W), lambda s, qi, kj: (kj[s], cq + 1)),
                  pl.BlockSpec((tk, BR_W), lambda s, qi, kj: (kj[s], cq + 2)),
                  pl.BlockSpec((8, tk), lambda s, qi, kj: (0, kj[s]))],
        out_specs=pl.BlockSpec((tq, BR_W), lambda s, qi, kj: (qi[s], 0)),
        scratch_shapes=[pltpu.VMEM((HEADS, tq, 128), F32), pltpu.VMEM((HEADS, tq, 128), F32),
                        pltpu.VMEM((HEADS, tq, 128), F32), pltpu.VMEM((HEADS, tq, BR_W), F32),
                        pltpu.VMEM((HEADS, tq, tk), F32), pltpu.VMEM((HEADS, tq, tk), BF16)],
    )
    return pl.pallas_call(
        functools.partial(_fox_prompt_kernel, tq, tk, min(32, tq)),
        grid_spec=grid_spec,
        out_shape=jax.ShapeDtypeStruct((t, BR_W), F32),
        compiler_params=_cparams(("arbitrary",)),
        name="fox_prompt",
    )(jnp.asarray(qi), jnp.asarray(kj), proj, proj, proj, cum_t)


def _fox_decode_kernel(t_new, n_slots, pt_ref, q_ref, k_ref, v_ref, sm_ref, *rest):
    kt_pages = rest[0:n_slots]
    vt_pages = rest[n_slots:2 * n_slots]
    lf_pages = rest[2 * n_slots:3 * n_slots]
    o_ref = rest[3 * n_slots]
    m_s, l_s, acc_s, carry_s = rest[3 * n_slots + 1:]
    g = pl.program_id(1)
    n_g = pl.num_programs(1)
    masks = _head_masks()
    rows = HEADS * t_new

    q = q_ref[...] * QK_SCALE
    qbd = jnp.concatenate([jnp.where(masks[h], q, 0.0) for h in range(HEADS)], axis=0).astype(BF16)

    def per_head_rows(x):
        return jnp.concatenate([jnp.broadcast_to(x[h:h + 1, :], (t_new, x.shape[1])) for h in range(HEADS)], axis=0)

    def softmax_step(sc):
        m_old = m_s[...]
        m_new = jnp.maximum(m_old, jnp.max(sc, axis=1, keepdims=True))
        alpha = jnp.exp(m_old - m_new)
        p = jnp.exp(sc - m_new)
        l_s[...] = alpha * l_s[...] + jnp.sum(p, axis=1, keepdims=True)
        m_s[...] = m_new
        return alpha, p.astype(BF16)

    @pl.when(g == 0)
    def _new_rows():
        pad = jnp.zeros((PAGE_SIZE - t_new, BR_W), F32)
        kn = jnp.concatenate([k_ref[...], pad], axis=0).astype(BF16)
        vn = jnp.concatenate([v_ref[...], pad], axis=0).astype(BF16)
        smp = jnp.concatenate([sm_ref[:, 0:128], jnp.zeros((PAGE_SIZE - t_new, 128), F32)], axis=0)
        lane128 = _iota((1, 128), 1)
        lf = jnp.where((lane128 >= SM_FLF) & (lane128 < SM_FLF + HEADS), _log_sigmoid(smp), 0.0)
        tril = (_iota((PAGE_SIZE, PAGE_SIZE), 0) >= _iota((PAGE_SIZE, PAGE_SIZE), 1)).astype(F32)
        cum_t = _transpose(_sel(lf, tril, "mx"))
        t_of_row = _iota((rows, PAGE_SIZE), 0) & (t_new - 1)
        s_of_col = _iota((rows, PAGE_SIZE), 1)
        sc = _dot_nt(qbd, kn) - per_head_rows(cum_t[SM_FLF:SM_FLF + HEADS, :])
        sc = jnp.where(s_of_col <= t_of_row, sc, NEG)
        m_s[...] = jnp.full_like(m_s, NEG)
        l_s[...] = jnp.zeros_like(l_s)
        carry_s[...] = jnp.zeros_like(carry_s)
        _, p = softmax_step(sc)
        acc_s[...] = jnp.dot(p, vn, preferred_element_type=F32)

    strict = (_iota((PAGE_SIZE, PAGE_SIZE), 0) > _iota((PAGE_SIZE, PAGE_SIZE), 1)).astype(F32)
    strict_ones = jnp.concatenate([strict, jnp.ones((PAGE_SIZE, PAGE_SIZE), F32)], axis=1)
    later = carry_s[...]
    sufs = [None] * n_slots
    for slot in reversed(range(n_slots)):
        r = _sel(lf_pages[slot][...], strict_ones, "xm")
        sufs[slot] = r[:, 0:PAGE_SIZE] + later
        later = later + r[:, PAGE_SIZE:2 * PAGE_SIZE]
    carry_s[...] = later
    bias = per_head_rows(jnp.concatenate(sufs, axis=1))
    sc = jnp.concatenate([jnp.dot(qbd, kt_pages[slot][...].astype(BF16), preferred_element_type=F32)
                          for slot in range(n_slots)], axis=1) + bias
    alpha, p = softmax_step(sc)
    pv = _dot_nt(p[:, 0:PAGE_SIZE], vt_pages[0][...])
    for slot in range(1, n_slots):
        pv += _dot_nt(p[:, slot * PAGE_SIZE:(slot + 1) * PAGE_SIZE], vt_pages[slot][...])
    acc_s[...] = alpha * acc_s[...] + pv

    @pl.when(g == n_g - 1)
    def _fin():
        res = acc_s[...] / l_s[...]
        out = jnp.zeros((t_new, BR_W), F32)
        for h in range(HEADS):
            out = jnp.where(masks[h], res[h * t_new:(h + 1) * t_new, :], out)
        o_ref[...] = out


def _fox_decode(layer, proj, cache_kt, cache_vt, cache_lf_t, page_table, n_slots):
    db, t_new, _ = proj.shape
    n_pages = page_table.shape[1]
    n_g = n_pages // n_slots
    cq = P_FX // BR_W

    def page_idx(slot):
        return lambda b, g, pt: (layer, pt[b * n_pages + (n_g - 1 - g) * n_slots + slot], 0, 0)

    kv_specs = [pl.BlockSpec((None, None, BR_W, PAGE_SIZE), page_idx(s)) for s in range(n_slots)]
    lf_specs = [pl.BlockSpec((None, None, HEADS, PAGE_SIZE), page_idx(s)) for s in range(n_slots)]
    rows = HEADS * t_new
    grid_spec = pltpu.PrefetchScalarGridSpec(
        num_scalar_prefetch=1,
        grid=(db, n_g),
        in_specs=[pl.BlockSpec((None, t_new, BR_W), lambda b, g, pt: (b, 0, cq)),
                  pl.BlockSpec((None, t_new, BR_W), lambda b, g, pt: (b, 0, cq + 1)),
                  pl.BlockSpec((None, t_new, BR_W), lambda b, g, pt: (b, 0, cq + 2)),
                  pl.BlockSpec((None, t_new, BR_W), lambda b, g, pt: (b, 0, P_SM // BR_W))]
                 + kv_specs + kv_specs + lf_specs,
        out_specs=pl.BlockSpec((None, t_new, BR_W), lambda b, g, pt: (b, 0, 0)),
        scratch_shapes=[pltpu.VMEM((rows, 1), F32), pltpu.VMEM((rows, 1), F32), pltpu.VMEM((rows, BR_W), F32),
                        pltpu.VMEM((HEADS, PAGE_SIZE), F32)],
    )
    return pl.pallas_call(
        functools.partial(_fox_decode_kernel, t_new, n_slots),
        grid_spec=grid_spec,
        out_shape=jax.ShapeDtypeStruct((db, t_new, BR_W), F32),
        compiler_params=_cparams(("arbitrary", "arbitrary")),
        name="fox_decode",
    )(page_table.reshape(-1), proj, proj, proj, proj,
      *([cache_kt] * n_slots), *([cache_vt] * n_slots), *([cache_lf_t] * n_slots))


def _merge_kernel(br_ref, fx_ref, g0_ref, g1_ref, g2_ref, g3_ref, x_ref, mod_ref, wb_ref, wo_ref, nw_ref,
                  x1_ref, h2_ref):
    merged = _sigmoid(g0_ref[...]) * _dot(br_ref[:, 0:BR_W], wb_ref[0])
    merged += _sigmoid(g1_ref[...]) * _dot(br_ref[:, BR_W:2 * BR_W], wb_ref[1])
    merged += _sigmoid(g2_ref[...]) * _dot(br_ref[:, 2 * BR_W:3 * BR_W], wb_ref[2])
    merged += _sigmoid(g3_ref[...]) * _dot(fx_ref[...], wb_ref[3])
    x1 = x_ref[...] + mod_ref[:, 2 * D_MODEL:3 * D_MODEL] * _dot(merged, wo_ref[...])
    x1_ref[...] = x1
    h2 = _rms(x1) * nw_ref[...]
    h2_ref[...] = h2 * (1.0 + mod_ref[:, 4 * D_MODEL:5 * D_MODEL]) + mod_ref[:, 3 * D_MODEL:4 * D_MODEL]


def _merge(br3, fox, proj, x, mod, w_branch_b, w_out_b, norm_w):
    t = x.shape[0]
    tm = min(t, 256)
    per_row = mod.shape[0] != 1
    mod_spec = (pl.BlockSpec((tm, 6 * D_MODEL), lambda i: (i, 0)) if per_row
                else pl.BlockSpec((1, 6 * D_MODEL), lambda i: (0, 0)))
    gate = lambda b: pl.BlockSpec((tm, D_MODEL), lambda i, b=b: (i, P_GT // D_MODEL + b))
    return pl.pallas_call(
        _merge_kernel,
        grid=(t // tm,),
        in_specs=[pl.BlockSpec((tm, 3 * BR_W), lambda i: (i, 0)),
                  pl.BlockSpec((tm, BR_W), lambda i: (i, 0)),
                  gate(0), gate(1), gate(2), gate(3),
                  pl.BlockSpec((tm, D_MODEL), lambda i: (i, 0)),
                  mod_spec,
                  pl.BlockSpec((N_BRANCH, BR_W, D_MODEL), lambda i: (0, 0, 0)),
                  pl.BlockSpec((D_MODEL, D_MODEL), lambda i: (0, 0)),
                  pl.BlockSpec((1, D_MODEL), lambda i: (0, 0))],
        out_specs=[pl.BlockSpec((tm, D_MODEL), lambda i: (i, 0)),
                   pl.BlockSpec((tm, D_MODEL), lambda i: (i, 0))],
        out_shape=[jax.ShapeDtypeStruct((t, D_MODEL), F32), jax.ShapeDtypeStruct((t, D_MODEL), F32)],
        compiler_params=_cparams(("arbitrary",)),
        name="merge",
    )(br3, fox, proj, proj, proj, proj, x, mod, w_branch_b, w_out_b, norm_w)


def _select_experts(h, wr_t, rb_col):
    tm = h.shape[0]
    logits = _dot_nt(wr_t, h)
    s = _sigmoid(logits)
    sel = s + rb_col
    ninf = -jnp.inf
    sub = _iota((GROUP_SIZE, tm), 0)
    gsc = []
    for g in range(N_GROUPS):
        blk = sel[g * GROUP_SIZE:(g + 1) * GROUP_SIZE, :]
        m1 = jnp.max(blk, axis=0, keepdims=True)
        first = jnp.min(jnp.where(blk == m1, sub, GROUP_SIZE), axis=0, keepdims=True)
        m2 = jnp.max(jnp.where(sub == first, ninf, blk), axis=0, keepdims=True)
        gsc.append(m1 + m2)
    chosen = [jnp.zeros((1, tm), jnp.bool_) for _ in range(N_GROUPS)]
    for _ in range(TOPK_GROUPS):
        mx = gsc[0]
        for g in range(1, N_GROUPS):
            mx = jnp.maximum(mx, gsc[g])
        taken = jnp.zeros((1, tm), jnp.bool_)
        for g in range(N_GROUPS):
            pick = (gsc[g] == mx) & jnp.logical_not(taken)
            taken = taken | pick
            chosen[g] = chosen[g] | pick
            gsc[g] = jnp.where(pick, ninf, gsc[g])
    selm = jnp.concatenate(
        [jnp.where(chosen[g], sel[g * GROUP_SIZE:(g + 1) * GROUP_SIZE, :], ninf) for g in range(N_GROUPS)], axis=0)
    eidx = _iota((N_EXPERTS, tm), 0)
    firsts, picks = [], []
    for _ in range(TOP_K):
        mx = jnp.max(selm, axis=0, keepdims=True)
        first = jnp.min(jnp.where(selm == mx, eidx, N_EXPERTS), axis=0, keepdims=True)
        pick = eidx == first
        selm = jnp.where(pick, ninf, selm)
        firsts.append(first)
        picks.append(pick)
    return s, firsts, picks


def _router_kernel(h_ref, wr_ref, rb_ref, wt_ref):
    s, _, picks = _select_experts(h_ref[...], wr_ref[...], rb_ref[...])
    picked = picks[0]
    for pick in picks[1:]:
        picked = picked | pick
    w = jnp.where(picked, s, 0.0)
    wt_ref[...] = w / jnp.sum(w, axis=0, keepdims=True) * ROUTED_SCALE


def _router(h2, w_router_t, router_bias_col):
    t = h2.shape[0]
    tm = min(t, 512)
    return pl.pallas_call(
        _router_kernel,
        grid=(t // tm,),
        in_specs=[pl.BlockSpec((tm, D_MODEL), lambda i: (i, 0)),
                  pl.BlockSpec((N_EXPERTS, D_MODEL), lambda i: (0, 0)),
                  pl.BlockSpec((N_EXPERTS, 1), lambda i: (0, 0))],
        out_specs=pl.BlockSpec((N_EXPERTS, tm), lambda i: (0, i)),
        out_shape=jax.ShapeDtypeStruct((N_EXPERTS, t), F32),
        compiler_params=_cparams(("arbitrary",)),
        name="router",
    )(h2, w_router_t, router_bias_col)


def _moe_kernel(final, h_ref, wt_ref, wgu_ref, wd_ref, wsgu_ref, wsd_ref, x1_ref, mod_ref, nf_ref, o_ref,
                acc_s, hb_s, wtok_s):
    e = pl.program_id(1)
    n_e = pl.num_programs(1)

    def swiglu(gu):
        return _silu(gu[:, 0:D_EXPERT]) * gu[:, D_EXPERT:2 * D_EXPERT]

    @pl.when(e == 0)
    def _init():
        hb = h_ref[...].astype(BF16)
        hb_s[...] = hb
        wtok_s[...] = _transpose(wt_ref[...])
        acc_s[...] = _dot(swiglu(jnp.dot(hb, wsgu_ref[...], preferred_element_type=F32)), wsd_ref[...])

    wcol = jnp.sum(jnp.where(_iota((1, N_EXPERTS), 1) == e, wtok_s[...], 0.0), axis=1, keepdims=True)
    a = swiglu(jnp.dot(hb_s[...], wgu_ref[...], preferred_element_type=F32))
    acc_s[...] += _dot(a, wd_ref[...]) * wcol

    @pl.when(e == n_e - 1)
    def _fin():
        x2 = x1_ref[...] + mod_ref[:, 5 * D_MODEL:6 * D_MODEL] * acc_s[...]
        if final:
            x2 = _rms(x2) * nf_ref[...]
        o_ref[...] = x2


def _moe(h2, wt, wgu_b, wd_b, wsgu_b, wsd_b, x1, mod, norm_f, final):
    t = h2.shape[0]
    tm = min(t, 1024)
    per_row = mod.shape[0] != 1
    mod_spec = (pl.BlockSpec((tm, 6 * D_MODEL), lambda i, e: (i, 0)) if per_row
                else pl.BlockSpec((1, 6 * D_MODEL), lambda i, e: (0, 0)))
    return pl.pallas_call(
        functools.partial(_moe_kernel, final),
        grid=(t // tm, N_EXPERTS),
        in_specs=[pl.BlockSpec((tm, D_MODEL), lambda i, e: (i, 0)),
                  pl.BlockSpec((N_EXPERTS, tm), lambda i, e: (0, i)),
                  pl.BlockSpec((None, D_MODEL, 2 * D_EXPERT), lambda i, e: (e, 0, 0)),
                  pl.BlockSpec((None, D_EXPERT, D_MODEL), lambda i, e: (e, 0, 0)),
                  pl.BlockSpec((D_MODEL, 2 * D_EXPERT), lambda i, e: (0, 0)),
                  pl.BlockSpec((D_EXPERT, D_MODEL), lambda i, e: (0, 0)),
                  pl.BlockSpec((tm, D_MODEL), lambda i, e: (i, 0)),
                  mod_spec,
                  pl.BlockSpec((1, D_MODEL), lambda i, e: (0, 0))],
        out_specs=pl.BlockSpec((tm, D_MODEL), lambda i, e: (i, 0)),
        out_shape=jax.ShapeDtypeStruct((t, D_MODEL), F32),
        scratch_shapes=[pltpu.VMEM((tm, D_MODEL), F32), pltpu.VMEM((tm, D_MODEL), BF16),
                        pltpu.VMEM((tm, N_EXPERTS), F32)],
        compiler_params=_cparams(("arbitrary", "arbitrary")),
        name="moe",
    )(h2, wt, wgu_b, wd_b, wsgu_b, wsd_b, x1, mod, norm_f)


MOE_BM = 512
MOE_SPARSE_MIN_TOKENS = 1024


def _router_sparse_kernel(h_ref, wr_ref, rb_ref, eidx_ref, rank_ref, ew_ref, cnt_ref, carry_s):
    tm = h_ref.shape[0]

    @pl.when(pl.program_id(0) == 0)
    def _():
        carry_s[...] = jnp.zeros_like(carry_s)

    s, firsts, picks = _select_experts(h_ref[...], wr_ref[...], rb_ref[...])
    picked = picks[0]
    for pick in picks[1:]:
        picked = picked | pick
    onehot = picked.astype(F32)
    earlier = (_iota((tm, tm), 0) < _iota((tm, tm), 1)).astype(F32)
    carry = carry_s[...]
    rank = _dot(onehot, earlier) + carry[:, 0:1]
    carry_s[...] = carry + jnp.sum(onehot, axis=1, keepdims=True)
    cnt_ref[...] = carry_s[...]

    w = [jnp.sum(jnp.where(pick, s, 0.0), axis=0, keepdims=True) for pick in picks]
    wsum = w[0]
    for wk in w[1:]:
        wsum = wsum + wk
    row8 = _iota((8, tm), 0)
    eidx8 = jnp.zeros((8, tm), jnp.int32)
    rank8 = jnp.zeros((8, tm), jnp.int32)
    ew8 = jnp.zeros((8, tm), F32)
    for k in range(TOP_K):
        rk = jnp.sum(jnp.where(picks[k], rank, 0.0), axis=0, keepdims=True)
        eidx8 = jnp.where(row8 == k, firsts[k], eidx8)
        rank8 = jnp.where(row8 == k, rk.astype(jnp.int32), rank8)
        ew8 = jnp.where(row8 == k, w[k] / wsum * ROUTED_SCALE, ew8)
    eidx_ref[...] = eidx8
    rank_ref[...] = rank8
    ew_ref[...] = ew8


def _router_sparse(h2, w_router_t, router_bias_col):
    t = h2.shape[0]
    tm = 512
    row = lambda dt: jax.ShapeDtypeStruct((8, t), dt)
    return pl.pallas_call(
        _router_sparse_kernel,
        grid=(t // tm,),
        in_specs=[pl.BlockSpec((tm, D_MODEL), lambda i: (i, 0)),
                  pl.BlockSpec((N_EXPERTS, D_MODEL), lambda i: (0, 0)),
                  pl.BlockSpec((N_EXPERTS, 1), lambda i: (0, 0))],
        out_specs=[pl.BlockSpec((8, tm), lambda i: (0, i)), pl.BlockSpec((8, tm), lambda i: (0, i)),
                   pl.BlockSpec((8, tm), lambda i: (0, i)), pl.BlockSpec((N_EXPERTS, 128), lambda i: (0, 0))],
        out_shape=[row(jnp.int32), row(jnp.int32), row(F32), jax.ShapeDtypeStruct((N_EXPERTS, 128), F32)],
        scratch_shapes=[pltpu.VMEM((N_EXPERTS, 128), F32)],
        compiler_params=_cparams(("arbitrary",)),
        name="router_sparse",
    )(h2, w_router_t, router_bias_col)


def _plan_kernel(nblk_pad, cnt_ref, eidx_ref, rank_ref, dest_ref, blk_ref):
    tm = eidx_ref.shape[1]
    cnt = cnt_ref[...]
    padded = jnp.floor((cnt + (MOE_BM - 1.0)) * (1.0 / MOE_BM)) * MOE_BM
    tril = (_iota((N_EXPERTS, N_EXPERTS), 0) >= _iota((N_EXPERTS, N_EXPERTS), 1)).astype(F32)
    pad_end = _sel(padded, tril, "mx")
    start_col = (pad_end - padded)[:, 0:1]
    end_col = pad_end[:, 0:1]

    e_iota = _iota((N_EXPERTS, tm), 0)
    row8 = _iota((8, tm), 0)
    eidx = eidx_ref[...]
    rank = rank_ref[...]
    dest = jnp.zeros((8, tm), jnp.int32)
    for k in range(TOP_K):
        base = jnp.sum(jnp.where(e_iota == eidx[k:k + 1, :], start_col, 0.0), axis=0, keepdims=True)
        dest = jnp.where(row8 == k, base.astype(jnp.int32) + rank[k:k + 1, :], dest)
    dest_ref[...] = dest

    first_row = (_iota((N_EXPERTS, nblk_pad), 1) * MOE_BM).astype(F32)
    blk_e = jnp.sum((end_col <= first_row).astype(F32), axis=0, keepdims=True)
    blk_e = jnp.minimum(blk_e, N_EXPERTS - 1.0).astype(jnp.int32)
    n_used = (pad_end[N_EXPERTS - 1:N_EXPERTS, 0:1] * (1.0 / MOE_BM)).astype(jnp.int32)
    row8b = _iota((8, nblk_pad), 0)
    blk_ref[...] = jnp.where(row8b == 0, blk_e, jnp.where(row8b == 1, n_used, 0))


def _plan(cnt, eidx, rank, nblk_pad):
    t = eidx.shape[1]
    tm = min(t, 2048)
    return pl.pallas_call(
        functools.partial(_plan_kernel, nblk_pad),
        grid=(t // tm,),
        in_specs=[pl.BlockSpec((N_EXPERTS, 128), lambda i: (0, 0)),
                  pl.BlockSpec((8, tm), lambda i: (0, i)), pl.BlockSpec((8, tm), lambda i: (0, i))],
        out_specs=[pl.BlockSpec((8, tm), lambda i: (0, i)), pl.BlockSpec((8, nblk_pad), lambda i: (0, 0))],
        out_shape=[jax.ShapeDtypeStruct((8, t), jnp.int32), jax.ShapeDtypeStruct((8, nblk_pad), jnp.int32)],
        compiler_params=_cparams(("arbitrary",)),
        name="moe_plan",
    )(cnt, eidx, rank)


def _sc_mesh():
    return plsc.VectorSubcoreMesh(core_axis_name="core", subcore_axis_name="subcore")


SC_CHUNK = 256
SC_WINDOW = 128
N_CHUNK = D_MODEL // SC_CHUNK


def _sc_scatter_rows(x, idx, n_rows):
    t = x.shape[0]
    flat_idx = [i for per_chunk in idx for i in per_chunk]

    @pl.kernel(out_type=jax.ShapeDtypeStruct((N_CHUNK * n_rows, SC_CHUNK), x.dtype), mesh=_sc_mesh(),
               scratch_types=[])
    def scatter_kernel(x_hbm, *rest):
        i_hbm, o_hbm = rest[:-1], rest[-1]

        def body(x_vmem, *i_vmem):
            for iv in i_vmem:
                pltpu.sync_copy(x_vmem, o_hbm.at[iv.at[0]])

        for c in range(N_CHUNK):
            pltpu.emit_pipeline(
                body,
                grid=(t // SC_WINDOW,),
                in_specs=[pl.BlockSpec((SC_WINDOW, SC_CHUNK), lambda i, c=c: (i, c))]
                         + [pl.BlockSpec((1, SC_WINDOW), lambda i: (0, i))] * TOP_K,
                out_specs=[],
                core_axis_name=("core", "subcore"),
                dimension_semantics=(pltpu.PARALLEL,),
            )(x_hbm, *i_hbm[c * TOP_K:(c + 1) * TOP_K])

    return scatter_kernel(x, *flat_idx)


def _sc_gather_rows(y, idx):
    a = idx[0].shape[1]
    n_win = a // SC_WINDOW

    @pl.kernel(out_type=jax.ShapeDtypeStruct((N_CHUNK * a, SC_CHUNK), y.dtype), mesh=_sc_mesh(), scratch_types=[])
    def gather_kernel(y_hbm, *rest):
        i_hbm, o_hbm = rest[:-1], rest[-1]

        def body(i_vmem, o_vmem):
            pltpu.sync_copy(y_hbm.at[i_vmem.at[0]], o_vmem)

        for c in range(N_CHUNK):
            pltpu.emit_pipeline(
                body,
                grid=(n_win,),
                in_specs=[pl.BlockSpec((1, SC_WINDOW), lambda i: (0, i))],
                out_specs=[pl.BlockSpec((SC_WINDOW, SC_CHUNK), lambda i, c=c: (c * n_win + i, 0))],
                core_axis_name=("core", "subcore"),
                dimension_semantics=(pltpu.PARALLEL,),
            )(i_hbm[c], o_hbm)

    return gather_kernel(y, *idx)


def _swiglu(gu):
    return _silu(gu[:, 0:D_EXPERT]) * gu[:, D_EXPERT:2 * D_EXPERT]


def _grouped_kernel(blk_e_ref, n_used_ref, xs_ref, wgu_ref, wd_ref, ys_ref):
    @pl.when(pl.program_id(0) < n_used_ref[0])
    def _():
        gu = _dot(xs_ref[0], wgu_ref[0:SC_CHUNK, :])
        for c in range(1, N_CHUNK):
            gu += _dot(xs_ref[c], wgu_ref[c * SC_CHUNK:(c + 1) * SC_CHUNK, :])
        y = _dot(_swiglu(gu), wd_ref[...])
        for c in range(N_CHUNK):
            ys_ref[c] = y[:, c * SC_CHUNK:(c + 1) * SC_CHUNK]


def _grouped(xs, blk_e, n_used, wgu_b, wd_b):
    n_rows = xs.shape[1]
    clamp = lambda b, nu: jnp.minimum(b, nu[0] - 1)
    rows_spec = pl.BlockSpec((N_CHUNK, MOE_BM, SC_CHUNK), lambda b, be, nu: (0, clamp(b, nu), 0))
    grid_spec = pltpu.PrefetchScalarGridSpec(
        num_scalar_prefetch=2,
        grid=(n_rows // MOE_BM,),
        in_specs=[rows_spec,
                  pl.BlockSpec((None, D_MODEL, 2 * D_EXPERT), lambda b, be, nu: (be[clamp(b, nu)], 0, 0)),
                  pl.BlockSpec((None, D_EXPERT, D_MODEL), lambda b, be, nu: (be[clamp(b, nu)], 0, 0))],
        out_specs=rows_spec,
    )
    return pl.pallas_call(
        _grouped_kernel,
        grid_spec=grid_spec,
        out_shape=jax.ShapeDtypeStruct((N_CHUNK, n_rows, SC_CHUNK), F32),
        compiler_params=_cparams(("arbitrary",)),
        name="moe_grouped",
    )(blk_e, n_used, xs, wgu_b, wd_b)


def _combine_kernel(final, yg_ref, ew_ref, h_ref, wsgu_ref, wsd_ref, x1_ref, mod_ref, nf_ref, o_ref):
    gu = jnp.dot(h_ref[...].astype(BF16), wsgu_ref[...], preferred_element_type=F32)
    acc = _dot(_swiglu(gu), wsd_ref[...])
    wt = _transpose(ew_ref[...])
    routed = []
    for c in range(N_CHUNK):
        part = wt[:, 0:1] * yg_ref[c, 0]
        for k in range(1, TOP_K):
            part = part + wt[:, k:k + 1] * yg_ref[c, k]
        routed.append(part)
    acc = acc + jnp.concatenate(routed, axis=1)
    x2 = x1_ref[...] + mod_ref[:, 5 * D_MODEL:6 * D_MODEL] * acc
    if final:
        x2 = _rms(x2) * nf_ref[...]
    o_ref[...] = x2


def _combine(yg, ew, h2, wsgu_b, wsd_b, x1, mod, norm_f, final):
    t = h2.shape[0]
    tm = 256
    per_row = mod.shape[0] != 1
    mod_spec = (pl.BlockSpec((tm, 6 * D_MODEL), lambda i: (i, 0)) if per_row
                else pl.BlockSpec((1, 6 * D_MODEL), lambda i: (0, 0)))
    return pl.pallas_call(
        functools.partial(_combine_kernel, final),
        grid=(t // tm,),
        in_specs=[pl.BlockSpec((N_CHUNK, TOP_K, tm, SC_CHUNK), lambda i: (0, 0, i, 0)),
                  pl.BlockSpec((8, tm), lambda i: (0, i)),
                  pl.BlockSpec((tm, D_MODEL), lambda i: (i, 0)),
                  pl.BlockSpec((D_MODEL, 2 * D_EXPERT), lambda i: (0, 0)),
                  pl.BlockSpec((D_EXPERT, D_MODEL), lambda i: (0, 0)),
                  pl.BlockSpec((tm, D_MODEL), lambda i: (i, 0)),
                  mod_spec,
                  pl.BlockSpec((1, D_MODEL), lambda i: (0, 0))],
        out_specs=pl.BlockSpec((tm, D_MODEL), lambda i: (i, 0)),
        out_shape=jax.ShapeDtypeStruct((t, D_MODEL), F32),
        compiler_params=_cparams(("arbitrary",)),
        name="moe_combine",
    )(yg, ew, h2, wsgu_b, wsd_b, x1, mod, norm_f)


def _moe_sparse(h2, w_router_t, router_bias_col, wgu_b, wd_b, wsgu_b, wsd_b, x1, mod, norm_f, final):
    t = h2.shape[0]
    n_blk = t * TOP_K // MOE_BM + N_EXPERTS
    n_rows = n_blk * MOE_BM
    nblk_pad = -(-n_blk // 128) * 128
    eidx, rank, ew, cnt = _router_sparse(h2, w_router_t, router_bias_col)
    dest, blk = _plan(cnt, eidx, rank, nblk_pad)
    per_pick = [[dest[k:k + 1] + c * n_rows for k in range(TOP_K)] for c in range(N_CHUNK)]
    flat = dest[0:TOP_K].reshape(1, TOP_K * t)
    xs = _sc_scatter_rows(h2, per_pick, n_rows).reshape(N_CHUNK, n_rows, SC_CHUNK)
    ys = _grouped(xs, blk[0], blk[1, 0:1], wgu_b, wd_b)
    yg = _sc_gather_rows(ys.reshape(N_CHUNK * n_rows, SC_CHUNK), [flat + c * n_rows for c in range(N_CHUNK)])
    return _combine(yg.reshape(N_CHUNK, TOP_K, t, SC_CHUNK), ew, h2, wsgu_b, wsd_b, x1, mod, norm_f, final)


def _permute_in_cols(w):
    ml = w[..., 0:ML_COLS]
    fx = w[..., OFF_FX:OFF_GT]
    small = jnp.concatenate([ml[..., 4 * BR_W:], fx[..., 3 * BR_W:]], axis=-1)
    pad = jnp.zeros(w.shape[:-1] + (P_GT - P_SM - small.shape[-1],), w.dtype)
    return jnp.concatenate([ml[..., :4 * BR_W], w[..., OFF_HG:OFF_RT], w[..., OFF_RT:OFF_FX], fx[..., :3 * BR_W],
                            small, pad, w[..., OFF_GT:]], axis=-1)


def _rope_tables(pos):
    half = HEAD_DIM // 2
    inv = ROPE_BASE ** (-jnp.arange(half, dtype=F32) / half)
    ang = pos.astype(F32)[:, None] * inv[None, :]
    cos = jnp.cos(ang)
    sin = jnp.sin(ang)
    cos_h = jnp.concatenate([cos, cos], axis=-1)
    sin_h = jnp.concatenate([-sin, sin], axis=-1)
    return jnp.tile(cos_h, (1, HEADS)), jnp.tile(sin_h, (1, HEADS))


def kernel(x_prompt, x_sample, cache_fox_k, cache_fox_v, cache_fox_logf, state_mlstm_C, state_mlstm_n, state_mlstm_m, state_hgrn_S, state_ret_S, page_table, c_prompt, c_sample, w_ada, b_ada, norm_mix_w, norm_ffn_w, w_in, b_in, hgrn_lb_logits, mlstm_norm_w, hgrn_norm_w, ret_norm_w, w_branch, w_out, w_router, router_bias, w_exp_gu, w_exp_down, w_shared_gu, w_shared_down, norm_f_w):
    depth = w_in.shape[0]
    bp, seq, _ = x_prompt.shape
    db, t_new, _ = x_sample.shape
    n_pool = cache_fox_k.shape[1]
    n_pages = page_table.shape[1]
    past_len = n_pages * PAGE_SIZE
    assert bp == 1 and seq % 128 == 0 and t_new == 8

    w_in_p = _permute_in_cols(w_in).astype(BF16)
    b_in_p = _permute_in_cols(b_in).reshape(depth, 1, P_TOT)
    w_branch_b = w_branch.astype(BF16)
    w_out_b = w_out.astype(BF16)
    wgu_b = w_exp_gu.astype(BF16)
    wd_b = w_exp_down.astype(BF16)
    wsgu_b = w_shared_gu.astype(BF16)
    wsd_b = w_shared_down.astype(BF16)
    w_router_t = jnp.swapaxes(w_router, 1, 2)
    cache_k = jnp.transpose(cache_fox_k, (0, 1, 3, 4, 2)).reshape(depth, n_pool, BR_W, PAGE_SIZE)
    cache_v = jnp.transpose(cache_fox_v, (0, 1, 3, 4, 2)).reshape(depth, n_pool, BR_W, PAGE_SIZE)
    cache_lf_t = jnp.swapaxes(cache_fox_logf, 2, 3)

    mods = _ada(jnp.concatenate([c_prompt, c_sample], axis=0), w_ada, b_ada)

    cos_p, sin_p = _rope_tables(jnp.arange(seq))
    cos_s, sin_s = _rope_tables(past_len + jnp.arange(t_new))

    def trunk(x, mod_of_layer, cos_t, sin_t, init, fox_fn, lc):
        b, l, _ = x.shape
        xt = x.reshape(b * l, D_MODEL)
        c_in, n_in, m_in, sh_in, sr_in = init
        per_layer = []
        for layer in range(depth):
            mod = mod_of_layer(layer)
            proj = _inproj(xt, mod, norm_mix_w[layer][None], w_in_p[layer], b_in_p[layer])
            proj3 = proj.reshape(b, l, P_TOT)
            br3, lf_rows, cum_t, c_new, n_new, m_new, sh_new, sr_new = _mixers(
                layer, proj3, cos_t, sin_t, c_in[layer], n_in[layer], m_in[layer], sh_in[layer], sr_in[layer],
                hgrn_lb_logits, mlstm_norm_w[layer][None], hgrn_norm_w[layer][None], ret_norm_w[layer][None], lc)
            fox = fox_fn(layer, proj3, cum_t)
            x1, h2 = _merge(br3.reshape(b * l, 3 * BR_W), fox.reshape(b * l, BR_W), proj, xt, mod,
                            w_branch_b[layer], w_out_b[layer], norm_ffn_w[layer][None])
            final = layer == depth - 1
            if b * l >= MOE_SPARSE_MIN_TOKENS:
                xt = _moe_sparse(h2, w_router_t[layer], router_bias[layer][:, None], wgu_b[layer], wd_b[layer],
                                 wsgu_b[layer], wsd_b[layer], x1, mod, norm_f_w[None], final)
            else:
                wt = _router(h2, w_router_t[layer], router_bias[layer][:, None])
                xt = _moe(h2, wt, wgu_b[layer], wd_b[layer], wsgu_b[layer], wsd_b[layer], x1, mod, norm_f_w[None],
                          final)
            per_layer.append((
                proj3[..., P_FX + BR_W:P_FX + 2 * BR_W].reshape(b, l, HEADS, HEAD_DIM),
                proj3[..., P_FX + 2 * BR_W:P_FX + 3 * BR_W].reshape(b, l, HEADS, HEAD_DIM),
                lf_rows[..., SM_FLF:SM_FLF + HEADS],
                c_new.reshape(b, HEADS, HEAD_DIM, HEAD_DIM),
                n_new.reshape(b, HEADS, HEAD_DIM),
                m_new[:, 0, :HEADS],
                sh_new.reshape(b, HEADS, HEAD_DIM, HEAD_DIM),
                sr_new.reshape(b, HEADS, HEAD_DIM, HEAD_DIM)))
        stacked = tuple(jnp.stack([p[i] for p in per_layer]) for i in range(8))
        return (xt.reshape(b, l, D_MODEL),) + stacked

    def head_major(s, b):
        return s.astype(F32).reshape(depth, b, BR_W, HEAD_DIM)

    zero_state = jnp.zeros((depth, bp, BR_W, HEAD_DIM), F32)
    prompt_init = (zero_state, jnp.zeros((depth, bp, 1, BR_W), F32), jnp.zeros((depth, bp, 1, 128), F32),
                   zero_state, zero_state)
    prompt_out = trunk(x_prompt, lambda layer: mods[layer, 0:1], cos_p, sin_p, prompt_init,
                       lambda layer, proj3, cum_t: _fox_prompt(proj3[0], cum_t[0], min(seq, 512)),
                       128)

    m_pad = jnp.pad(state_mlstm_m.astype(F32), ((0, 0), (0, 0), (0, 128 - HEADS))).reshape(depth, db, 1, 128)
    sample_init = (head_major(state_mlstm_C, db), state_mlstm_n.astype(F32).reshape(depth, db, 1, BR_W), m_pad,
                   head_major(state_hgrn_S, db), head_major(state_ret_S, db))
    n_slots = 16 if n_pages % 16 == 0 else 1
    sample_out = trunk(x_sample, lambda layer: jnp.repeat(mods[layer, 1:], t_new, axis=0), cos_s, sin_s, sample_init,
                       lambda layer, proj3, cum_t: _fox_decode(layer, proj3, cache_k, cache_v, cache_lf_t,
                                                               page_table, n_slots),
                       t_new)

    return (prompt_out[0], sample_out[0]) + prompt_out[1:] + sample_out[1:]
```

```python
import functools
import math

import numpy as np
import jax
import jax.numpy as jnp
from jax import lax
from jax.experimental import pallas as pl
from jax.experimental.pallas import tpu as pltpu
from jax.experimental.pallas import tpu_sc as plsc

F32 = jnp.float32
BF16 = jnp.bfloat16
HIGHEST = lax.Precision.HIGHEST

D_MODEL = 1024
N_BRANCH = 4
BR_W = 256
HEAD_DIM = 64
HEADS = 4
ROPE_BASE = 10000.0
RMS_EPS = 1e-6
N_EXPERTS = 64
TOP_K = 6
N_GROUPS = 8
GROUP_SIZE = N_EXPERTS // N_GROUPS
TOPK_GROUPS = 4
D_EXPERT = 256
ROUTED_SCALE = 2.5
PAGE_SIZE = 128
QK_SCALE = HEAD_DIM ** -0.5
LOG2E = math.log2(math.e)

ML_COLS = 4 * BR_W + 2 * HEADS
OFF_HG = ML_COLS
OFF_RT = OFF_HG + 4 * BR_W
OFF_FX = OFF_RT + 4 * BR_W
OFF_GT = OFF_FX + 3 * BR_W + HEADS
N_IN = OFF_GT + N_BRANCH * D_MODEL

P_ML, P_HG, P_RT, P_FX, P_SM, P_GT, P_TOT = 0, 1024, 2048, 3072, 3840, 4096, 8192
SM_IG, SM_MLF, SM_FLF = 0, 4, 8

NEG = -1e30
VMEM_LIMIT = 56 * 1024 * 1024


def _cparams(sem):
    return pltpu.CompilerParams(dimension_semantics=sem, vmem_limit_bytes=VMEM_LIMIT)


def _dot(a, b):
    return jnp.dot(a.astype(BF16), b.astype(BF16), preferred_element_type=F32)


def _dot_nt(a, b):
    return lax.dot_general(a.astype(BF16), b.astype(BF16), (((1,), (1,)), ((), ())), preferred_element_type=F32)


def _dot_tn(a, b):
    return lax.dot_general(a.astype(BF16), b.astype(BF16), (((0,), (0,)), ((), ())), preferred_element_type=F32)


def _split3(x):
    x1 = x.astype(BF16)
    r1 = x - x1.astype(F32)
    x2 = r1.astype(BF16)
    x3 = (r1 - x2.astype(F32)).astype(BF16)
    return x1, x2, x3


def _sel(x, m01, dims):
    m = m01.astype(BF16)
    x1, x2, x3 = _split3(x)
    if dims == "mx":
        f = lambda xi: jnp.dot(m, xi, preferred_element_type=F32)
    elif dims == "xm":
        f = lambda xi: jnp.dot(xi, m, preferred_element_type=F32)
    elif dims == "xmT":
        f = lambda xi: lax.dot_general(xi, m, (((1,), (1,)), ((), ())), preferred_element_type=F32)
    else:
        f = lambda xi: lax.dot_general(m, xi, (((1,), (1,)), ((), ())), preferred_element_type=F32)
    return (f(x1) + f(x2)) + f(x3)


def _iota(shape, dim):
    return lax.broadcasted_iota(jnp.int32, shape, dim)


def _eye(n):
    return (_iota((n, n), 0) == _iota((n, n), 1)).astype(F32)


def _transpose(x):
    if x.shape[0] % 128 == 0 and x.shape[1] % 128 == 0:
        return x.T
    return _sel(x, _eye(x.shape[1]), "mxT")


def _sigmoid(x):
    return jax.nn.sigmoid(x)


def _silu(x):
    return x * jax.nn.sigmoid(x)


def _log_sigmoid(x):
    return jnp.minimum(x, 0.0) - jnp.log1p(jnp.exp(-jnp.abs(x)))


def _head_masks(n=BR_W):
    lane = _iota((1, n), 1) >> 6
    return [lane == h for h in range(HEADS)]


def _block_diag_mask():
    return (_iota((BR_W, BR_W), 0) >> 6) == (_iota((BR_W, BR_W), 1) >> 6)


def _per_head_lanes(vals, masks):
    out = jnp.where(masks[0], vals[0], 0.0)
    for h in range(1, HEADS):
        out = jnp.where(masks[h], vals[h], out)
    return out


def _rms(x, eps=RMS_EPS):
    return x * lax.rsqrt(jnp.mean(x * x, axis=-1, keepdims=True) + eps)


def _head_norm(o, gain, bdf):
    ms = _sel(o * o, bdf, "xm") * (1.0 / HEAD_DIM)
    return o * lax.rsqrt(ms + RMS_EPS) * gain


def _ada_kernel(c_ref, w_ref, b_ref, o_ref):
    o_ref[...] = _dot(_silu(c_ref[...]), w_ref[...]) + b_ref[...]


def _ada(c_all, w_ada, b_ada):
    depth = w_ada.shape[0]
    n_c = c_all.shape[0]
    tn = 1536
    return pl.pallas_call(
        _ada_kernel,
        grid=(depth, 6 * D_MODEL // tn),
        in_specs=[pl.BlockSpec((n_c, D_MODEL), lambda l, j: (0, 0)),
                  pl.BlockSpec((None, D_MODEL, tn), lambda l, j: (l, 0, j)),
                  pl.BlockSpec((None, 1, tn), lambda l, j: (l, 0, j))],
        out_specs=pl.BlockSpec((None, n_c, tn), lambda l, j: (l, 0, j)),
        out_shape=jax.ShapeDtypeStruct((depth, n_c, 6 * D_MODEL), F32),
        compiler_params=_cparams(("arbitrary", "arbitrary")),
        name="ada",
    )(c_all, w_ada, b_ada.reshape(depth, 1, 6 * D_MODEL))


def _inproj_kernel(kv_t, x_ref, mod_ref, nw_ref, w_ref, b_ref, o_ref, *rest):
    h_scr = rest[-1]
    j = pl.program_id(1)

    @pl.when(j == 0)
    def _():
        h = _rms(x_ref[...]) * nw_ref[...]
        h = h * (1.0 + mod_ref[:, D_MODEL:2 * D_MODEL]) + mod_ref[:, 0:D_MODEL]
        h_scr[...] = h.astype(BF16)

    res = jnp.dot(h_scr[...], w_ref[...], preferred_element_type=F32) + b_ref[...]
    o_ref[...] = res

    if kv_t:
        kt_ref, vt_ref = rest[0], rest[1]
        c0 = P_FX % res.shape[1]

        @pl.when(j == P_FX // res.shape[1])
        def _():
            kt_ref[...] = res[:, c0 + BR_W:c0 + 2 * BR_W].T
            vt_ref[...] = res[:, c0 + 2 * BR_W:c0 + 3 * BR_W].T


def _inproj(layer, x, mod, norm_w, w_p, b_p, kv_t):
    t = x.shape[0]
    tm = min(t, 1024)
    tn = 1024
    per_row = mod.shape[0] != 1
    mod_spec = (pl.BlockSpec((tm, 6 * D_MODEL), lambda i, j: (i, 0)) if per_row
                else pl.BlockSpec((1, 6 * D_MODEL), lambda i, j: (0, 0)))
    out_specs = [pl.BlockSpec((tm, tn), lambda i, j: (i, j))]
    out_shape = [jax.ShapeDtypeStruct((t, P_TOT), F32)]
    if kv_t:
        out_specs += [pl.BlockSpec((BR_W, tm), lambda i, j: (0, i))] * 2
        out_shape += [jax.ShapeDtypeStruct((BR_W, t), F32)] * 2
    return pl.pallas_call(
        functools.partial(_inproj_kernel, kv_t),
        grid=(t // tm, P_TOT // tn),
        in_specs=[pl.BlockSpec((tm, D_MODEL), lambda i, j: (i, 0)),
                  mod_spec,
                  pl.BlockSpec((1, D_MODEL), lambda i, j: (0, 0)),
                  pl.BlockSpec((None, D_MODEL, tn), lambda i, j: (layer, 0, j)),
                  pl.BlockSpec((None, 1, tn), lambda i, j: (layer, 0, j))],
        out_specs=out_specs,
        out_shape=out_shape,
        scratch_shapes=[pltpu.VMEM((tm, D_MODEL), BF16)],
        compiler_params=_cparams(("arbitrary", "arbitrary")),
        name="inproj",
    )(x, mod, norm_w, w_p, b_p)


def _mixers_kernel(layer, lc, sc,
                   ml_ref, hg_ref, rt_ref, sm_ref, cos_ref, sin_ref,
                   c0_ref, n0_ref, m0_ref, sh0_ref, sr0_ref, lbl_ref, gml_ref, ghg_ref, grt_ref,
                   br_ref, lf_ref, cumt_ref, cout_ref, nout_ref, mout_ref, shout_ref, srout_ref,
                   cbd, sht, srbd, n_s, m_s, carry_s, ohg_s):
    c = pl.program_id(1)
    n_c = pl.num_programs(1)
    masks = _head_masks()
    bd = _block_diag_mask()
    bdf = bd.astype(F32)
    tile = ((_iota((HEAD_DIM, BR_W), 1) & (HEAD_DIM - 1)) == _iota((HEAD_DIM, BR_W), 0)).astype(F32)

    @pl.when(c == 0)
    def _init():
        def expand(ref):
            return jnp.where(bd, _sel(ref[...], tile, "xm"), 0.0)
        cbd[...] = expand(c0_ref)
        sht[...] = _transpose(expand(sh0_ref))
        srbd[...] = expand(sr0_ref)
        n_s[...] = n0_ref[...]
        m_s[...] = m0_ref[...]
        carry_s[...] = jnp.zeros_like(carry_s)

    row = _iota((lc, lc), 0)
    col = _iota((lc, lc), 1)
    causal = row >= col
    tril = causal.astype(F32)

    sm = sm_ref[:, 0:128]
    lane128 = _iota((1, 128), 1)
    sm2 = jnp.where((lane128 >= SM_MLF) & (lane128 < SM_FLF + HEADS), _log_sigmoid(sm), sm)
    cum = _sel(sm2, tril, "mx")
    sm2_t = _transpose(sm2)
    cum_t = _transpose(cum)
    lf_ref[...] = sm2
    cum_tg = cum_t + carry_s[...]
    carry_s[...] = cum_tg[:, lc - 1:lc]
    cumt_ref[...] = cum_tg[SM_FLF:SM_FLF + 8, :]

    q = ml_ref[:, 0:BR_W]
    k = ml_ref[:, BR_W:2 * BR_W] * QK_SCALE
    v = ml_ref[:, 2 * BR_W:3 * BR_W]
    og = ml_ref[:, 3 * BR_W:4 * BR_W]
    n_row = n_s[...]
    m_row = m_s[...]
    q_c = _dot_nt(q, cbd[...])
    h_all = jnp.zeros((lc, BR_W), F32)
    w_lanes = jnp.zeros((lc, BR_W), F32)
    decay_lanes = jnp.zeros((1, BR_W), F32)
    m_new_row = jnp.zeros((1, 128), F32)
    for h in range(HEADS):
        ig_c = sm2[:, SM_IG + h:SM_IG + h + 1]
        b_c = cum[:, SM_MLF + h:SM_MLF + h + 1]
        ig_r = sm2_t[SM_IG + h:SM_IG + h + 1, :]
        b_r = cum_t[SM_MLF + h:SM_MLF + h + 1, :]
        m_prev = m_row[:, h:h + 1]
        dmat = jnp.where(causal, b_c - b_r + ig_r, NEG)
        m_inter = b_c + m_prev
        m_t = jnp.maximum(m_inter, jnp.max(dmat, axis=1, keepdims=True))
        w_intra = jnp.exp(dmat - m_t)
        w_inter = jnp.exp(m_inter - m_t)
        qh = jnp.where(masks[h], q, 0.0)
        a = _dot_nt(qh, k) * w_intra
        num = _dot(a, v) + w_inter * q_c
        den = jnp.sum(a, axis=1, keepdims=True) + w_inter * jnp.sum(qh * n_row, axis=1, keepdims=True)
        hh = num / jnp.maximum(jnp.abs(den), jnp.exp(-m_t))
        h_all = jnp.where(masks[h], hh, h_all)
        m_new = m_t[lc - 1:lc, :]
        b_last = b_c[lc - 1:lc, :]
        w_s = jnp.exp(b_last - b_c + ig_c - m_new)
        decay = jnp.exp(b_last + m_prev - m_new)
        w_lanes = jnp.where(masks[h], w_s, w_lanes)
        decay_lanes = jnp.where(masks[h], decay, decay_lanes)
        m_new_row = jnp.where(lane128 == h, m_new, m_new_row)
    kw = k * w_lanes
    cbd[...] = cbd[...] * decay_lanes + jnp.where(bd, _dot_tn(v * w_lanes, k), 0.0)
    n_s[...] = n_row * decay_lanes + jnp.sum(kw, axis=0, keepdims=True)
    m_s[...] = m_new_row
    out_ml = _head_norm(h_all, gml_ref[...], bdf) * _sigmoid(og)

    lbl = lbl_ref[...]
    pr = jnp.exp(lbl - jnp.max(lbl, axis=0, keepdims=True))
    pr = pr / jnp.sum(pr, axis=0, keepdims=True)
    lb = jnp.zeros((1, BR_W), F32)
    for i in range(1, layer + 1):
        lb = lb + pr[i:i + 1, :]
    tril_sc = (_iota((sc, sc), 0) >= _iota((sc, sc), 1)).astype(F32)
    s_idx = _iota((sc, BR_W), 0)

    def hg_body(i, carry):
        r0 = pl.multiple_of(i * sc, sc)
        hq = hg_ref[pl.ds(r0, sc), 0:BR_W]
        hf = hg_ref[pl.ds(r0, sc), BR_W:2 * BR_W]
        vi = hg_ref[pl.ds(r0, sc), 2 * BR_W:3 * BR_W]
        qi = _silu(hq)
        f = lb + (1.0 - lb) * _sigmoid(hf)
        ki = 1.0 - f
        bi = _sel(jnp.log(f), tril_sc, "mx")
        st = sht[...]
        o_inter = _dot_nt(qi * jnp.exp(bi), st)
        rows = []
        for t in range(sc):
            e_t = jnp.exp(jnp.where(s_idx <= t, bi[t:t + 1, :] - bi, NEG)) * qi[t:t + 1, :] * ki
            rows.append(e_t)
        e_all = jnp.concatenate(rows, axis=0)
        r_all = _dot(e_all, bdf)
        o_diag = jnp.sum(r_all.reshape(sc, sc, BR_W) * vi[None, :, :], axis=1)
        ohg_s[pl.ds(r0, sc), :] = o_inter + o_diag
        b_last = bi[sc - 1:sc, :]
        sht[...] = st * jnp.exp(b_last) + jnp.where(bd, _dot_tn(vi, ki * jnp.exp(b_last - bi)), 0.0)
        return carry

    lax.fori_loop(0, lc // sc, hg_body, 0, unroll=True)
    out_hg = _head_norm(ohg_s[...], ghg_ref[...], bdf) * _silu(hg_ref[:, 3 * BR_W:4 * BR_W])

    cosv = cos_ref[...]
    sinv = sin_ref[...]
    lane = _iota((1, BR_W), 1)
    first_half = (lane & (HEAD_DIM - 1)) < (HEAD_DIM // 2)

    def rope(x):
        partner = jnp.where(first_half, pltpu.roll(x, BR_W - HEAD_DIM // 2, 1), pltpu.roll(x, HEAD_DIM // 2, 1))
        return x * cosv + partner * sinv

    rq = rope(rt_ref[:, 0:BR_W])
    rk = rope(rt_ref[:, BR_W:2 * BR_W]) * QK_SCALE
    rv = rt_ref[:, 2 * BR_W:3 * BR_W]
    lg = [math.log1p(-(2.0 ** (-5.0 - h))) for h in range(HEADS)]
    lg_lanes = _per_head_lanes([jnp.full((1, 1), g, F32) for g in lg], masks)
    diff = (row - col).astype(F32)
    o_rt = jnp.zeros((lc, BR_W), F32)
    for h in range(HEADS):
        dec = jnp.exp(jnp.where(causal, diff * lg[h], NEG))
        a = _dot_nt(jnp.where(masks[h], rq, 0.0), rk) * dec
        o_rt = jnp.where(masks[h], _dot(a, rv), o_rt)
    t_idx = _iota((lc, BR_W), 0).astype(F32)
    o_rt = o_rt + _dot(rq * jnp.exp((t_idx + 1.0) * lg_lanes), srbd[...])
    w_ret = jnp.exp((lc - 1.0 - t_idx) * lg_lanes)
    srbd[...] = srbd[...] * jnp.exp(lc * lg_lanes) + jnp.where(bd, _dot_tn(rk * w_ret, rv), 0.0)
    out_rt = _head_norm(o_rt, grt_ref[...], bdf) * _silu(rt_ref[:, 3 * BR_W:4 * BR_W])

    br_ref[:, 0:BR_W] = out_ml
    br_ref[:, BR_W:2 * BR_W] = out_hg
    br_ref[:, 2 * BR_W:3 * BR_W] = out_rt

    @pl.when(c == n_c - 1)
    def _fin():
        def compact(x):
            return _sel(x, tile, "xmT")
        cout_ref[...] = compact(cbd[...])
        shout_ref[...] = compact(_transpose(sht[...]))
        srout_ref[...] = compact(srbd[...])
        nout_ref[...] = n_s[...]
        mout_ref[...] = m_s[...]


def _mixers(layer, proj, cos_t, sin_t, c0, n0, m0, sh0, sr0, lb_logits, g_ml, g_hg, g_rt, lc):
    b, l, _ = proj.shape
    sc = min(16, lc)
    n_c = l // lc
    depth = lb_logits.shape[0]
    cb = lambda blk: pl.BlockSpec((None, lc, 1024), lambda bi, ci, blk=blk: (bi, ci, blk))
    st_spec = pl.BlockSpec((None, BR_W, HEAD_DIM), lambda bi, ci: (bi, 0, 0))
    row_spec = lambda n: pl.BlockSpec((None, 1, n), lambda bi, ci: (bi, 0, 0))
    full = lambda r, cc: pl.BlockSpec((r, cc), lambda bi, ci: (0, 0))
    outs = pl.pallas_call(
        functools.partial(_mixers_kernel, layer, lc, sc),
        grid=(b, n_c),
        in_specs=[cb(0), cb(1), cb(2),
                  pl.BlockSpec((None, lc, 256), lambda bi, ci: (bi, ci, P_SM // 256)),
                  pl.BlockSpec((lc, BR_W), lambda bi, ci: (ci, 0)),
                  pl.BlockSpec((lc, BR_W), lambda bi, ci: (ci, 0)),
                  st_spec, row_spec(BR_W), row_spec(128), st_spec, st_spec,
                  full(depth, BR_W), full(1, BR_W), full(1, BR_W), full(1, BR_W)],
        out_specs=[pl.BlockSpec((None, lc, 3 * BR_W), lambda bi, ci: (bi, ci, 0)),
                   pl.BlockSpec((None, lc, 128), lambda bi, ci: (bi, ci, 0)),
                   pl.BlockSpec((None, 8, lc), lambda bi, ci: (bi, 0, ci)),
                   st_spec, row_spec(BR_W), row_spec(128), st_spec, st_spec],
        out_shape=[jax.ShapeDtypeStruct((b, l, 3 * BR_W), F32),
                   jax.ShapeDtypeStruct((b, l, 128), F32),
                   jax.ShapeDtypeStruct((b, 8, l), F32),
                   jax.ShapeDtypeStruct((b, BR_W, HEAD_DIM), F32),
                   jax.ShapeDtypeStruct((b, 1, BR_W), F32),
                   jax.ShapeDtypeStruct((b, 1, 128), F32),
                   jax.ShapeDtypeStruct((b, BR_W, HEAD_DIM), F32),
                   jax.ShapeDtypeStruct((b, BR_W, HEAD_DIM), F32)],
        scratch_shapes=[pltpu.VMEM((BR_W, BR_W), F32), pltpu.VMEM((BR_W, BR_W), F32), pltpu.VMEM((BR_W, BR_W), F32),
                        pltpu.VMEM((1, BR_W), F32), pltpu.VMEM((1, 128), F32), pltpu.VMEM((128, 1), F32),
                        pltpu.VMEM((lc, BR_W), F32)],
        compiler_params=_cparams(("arbitrary", "arbitrary")),
        name="mixers",
    )(proj, proj, proj, proj, cos_t, sin_t, c0, n0, m0, sh0, sr0, lb_logits, g_ml, g_hg, g_rt)
    return outs


def _fox_prompt_kernel(tq, tk, strip, qi_ref, kj_ref, q_ref, k_ref, v_ref, ck_ref, o_ref,
                       m_s, l_s, alpha_s, acc_s, s_scr, p_scr):
    step = pl.program_id(0)
    i = qi_ref[step]
    j = kj_ref[step]
    masks = _head_masks()

    @pl.when(j == 0)
    def _init():
        m_s[...] = jnp.full_like(m_s, NEG)
        l_s[...] = jnp.zeros_like(l_s)
        acc_s[...] = jnp.zeros_like(acc_s)

    q = q_ref[...] * (QK_SCALE * LOG2E)
    kb = k_ref[...].astype(BF16)
    vb = v_ref[...].astype(BF16)
    nck = ck_ref[...] * (-LOG2E)
    n_rep = tk // 128

    def attend(diagonal):
        col = _iota((strip, tk), 1)
        row = _iota((strip, tk), 0)
        for h in range(HEADS):
            s_scr[h] = _dot_nt(jnp.where(masks[h], q, 0.0), kb) + nck[h:h + 1, :]
            parts = []
            for r0 in range(0, tq, strip):
                sc = s_scr[h, pl.ds(r0, strip), :]
                if diagonal:
                    sc = jnp.where(row + (r0 + i * tq - j * tk) >= col, sc, NEG)
                    s_scr[h, pl.ds(r0, strip), :] = sc
                parts.append(jnp.max(sc, axis=1, keepdims=True))
            m_old = m_s[h]
            m_new = jnp.maximum(m_old, jnp.broadcast_to(jnp.concatenate(parts, axis=0), (tq, 128)))
            m_s[h] = m_new
            alpha_s[h] = jnp.exp2(m_old - m_new)
            for r0 in range(0, tq, strip):
                rows = pl.ds(r0, strip)
                m_rep = jnp.concatenate([m_s[h, rows, :]] * n_rep, axis=1)
                p = jnp.exp2(s_scr[h, rows, :] - m_rep)
                p_scr[h, rows, :] = p.astype(BF16)
                psum = p[:, 0:128]
                for c in range(1, n_rep):
                    psum = psum + p[:, c * 128:(c + 1) * 128]
                l_s[h, rows, :] = alpha_s[h, rows, :] * l_s[h, rows, :] + psum
            alpha = alpha_s[h]
            acc_s[h] = jnp.concatenate([alpha, alpha], axis=1) * acc_s[h] + jnp.dot(p_scr[h], vb,
                                                                                   preferred_element_type=F32)

    last = (j + 1) * tk >= (i + 1) * tq

    @pl.when(jnp.logical_not(last))
    def _off_diagonal():
        attend(False)

    @pl.when(last)
    def _diagonal():
        attend(True)
        out = jnp.zeros((tq, BR_W), F32)
        for h in range(HEADS):
            out = jnp.where(masks[h], acc_s[h] / jnp.sum(l_s[h], axis=1, keepdims=True), out)
        o_ref[...] = out


def _fox_prompt(proj, cum_t, tq):
    t = proj.shape[0]
    tk = 2 * tq if t % (2 * tq) == 0 else tq
    nq = t // tq
    n_kj = [((i + 1) * tq - 1) // tk + 1 for i in range(nq)]
    qi = np.concatenate([np.full(n, i, np.int32) for i, n in enumerate(n_kj)])
    kj = np.concatenate([np.arange(n, dtype=np.int32) for n in n_kj])
    cq = P_FX // BR_W
    grid_spec = pltpu.PrefetchScalarGridSpec(
        num_scalar_prefetch=2,
        grid=(len(qi),),
        in_specs=[pl.BlockSpec((tq, BR_W), lambda s, qi, kj: (qi[s], cq)),
                  pl.BlockSpec((tk, BR_W), lambda s, qi, kj: (kj[s], cq + 1)),
                  pl.BlockSpec((tk, BR_W), lambda s, qi, kj: (kj[s], cq + 2)),
                  pl.BlockSpec((8, tk), lambda s, qi, kj: (0, kj[s]))],
        out_specs=pl.BlockSpec((tq, BR_W), lambda s, qi, kj: (qi[s], 0)),
        scratch_shapes=[pltpu.VMEM((HEADS, tq, 128), F32), pltpu.VMEM((HEADS, tq, 128), F32),
                        pltpu.VMEM((HEADS, tq, 128), F32), pltpu.VMEM((HEADS, tq, BR_W), F32),
                        pltpu.VMEM((HEADS, tq, tk), F32), pltpu.VMEM((HEADS, tq, tk), BF16)],
    )
    return pl.pallas_call(
        functools.partial(_fox_prompt_kernel, tq, tk, min(16384 // tk, tq)),
        grid_spec=grid_spec,
        out_shape=jax.ShapeDtypeStruct((t, BR_W), F32),
        compiler_params=_cparams(("arbitrary",)),
        name="fox_prompt",
    )(jnp.asarray(qi), jnp.asarray(kj), proj, proj, proj, cum_t)


def _fox_decode_kernel(t_new, n_slots, pt_ref, q_ref, k_ref, v_ref, sm_ref, *rest):
    kt_pages = rest[0:n_slots]
    vt_pages = rest[n_slots:2 * n_slots]
    lf_pages = rest[2 * n_slots:3 * n_slots]
    o_ref = rest[3 * n_slots]
    m_s, l_s, acc_s, carry_s = rest[3 * n_slots + 1:]
    g = pl.program_id(1)
    n_g = pl.num_programs(1)
    masks = _head_masks()
    rows = HEADS * t_new

    q = q_ref[...] * QK_SCALE
    qbd = jnp.concatenate([jnp.where(masks[h], q, 0.0) for h in range(HEADS)], axis=0).astype(BF16)

    def per_head_rows(x):
        return jnp.concatenate([jnp.broadcast_to(x[h:h + 1, :], (t_new, x.shape[1])) for h in range(HEADS)], axis=0)

    def softmax_step(sc):
        m_old = m_s[...]
        m_new = jnp.maximum(m_old, jnp.max(sc, axis=1, keepdims=True))
        alpha = jnp.exp(m_old - m_new)
        p = jnp.exp(sc - m_new)
        l_s[...] = alpha * l_s[...] + jnp.sum(p, axis=1, keepdims=True)
        m_s[...] = m_new
        return alpha, p.astype(BF16)

    @pl.when(g == 0)
    def _new_rows():
        pad = jnp.zeros((PAGE_SIZE - t_new, BR_W), F32)
        kn = jnp.concatenate([k_ref[...], pad], axis=0).astype(BF16)
        vn = jnp.concatenate([v_ref[...], pad], axis=0).astype(BF16)
        smp = jnp.concatenate([sm_ref[:, 0:128], jnp.zeros((PAGE_SIZE - t_new, 128), F32)], axis=0)
        lane128 = _iota((1, 128), 1)
        lf = jnp.where((lane128 >= SM_FLF) & (lane128 < SM_FLF + HEADS), _log_sigmoid(smp), 0.0)
        tril = (_iota((PAGE_SIZE, PAGE_SIZE), 0) >= _iota((PAGE_SIZE, PAGE_SIZE), 1)).astype(F32)
        cum_t = _transpose(_sel(lf, tril, "mx"))
        t_of_row = _iota((rows, PAGE_SIZE), 0) & (t_new - 1)
        s_of_col = _iota((rows, PAGE_SIZE), 1)
        sc = _dot_nt(qbd, kn) - per_head_rows(cum_t[SM_FLF:SM_FLF + HEADS, :])
        sc = jnp.where(s_of_col <= t_of_row, sc, NEG)
        m_s[...] = jnp.full_like(m_s, NEG)
        l_s[...] = jnp.zeros_like(l_s)
        carry_s[...] = jnp.zeros_like(carry_s)
        _, p = softmax_step(sc)
        acc_s[...] = jnp.dot(p, vn, preferred_element_type=F32)

    strict = (_iota((PAGE_SIZE, PAGE_SIZE), 0) > _iota((PAGE_SIZE, PAGE_SIZE), 1)).astype(F32)
    strict_ones = jnp.concatenate([strict, jnp.ones((PAGE_SIZE, PAGE_SIZE), F32)], axis=1)
    later = carry_s[...]
    sufs = [None] * n_slots
    for slot in reversed(range(n_slots)):
        r = _sel(lf_pages[slot][...], strict_ones, "xm")
        sufs[slot] = r[:, 0:PAGE_SIZE] + later
        later = later + r[:, PAGE_SIZE:2 * PAGE_SIZE]
    carry_s[...] = later
    bias = per_head_rows(jnp.concatenate(sufs, axis=1))
    sc = jnp.concatenate([jnp.dot(qbd, kt_pages[slot][...].astype(BF16), preferred_element_type=F32)
                          for slot in range(n_slots)], axis=1) + bias
    alpha, p = softmax_step(sc)
    pv = _dot_nt(p[:, 0:PAGE_SIZE], vt_pages[0][...])
    for slot in range(1, n_slots):
        pv += _dot_nt(p[:, slot * PAGE_SIZE:(slot + 1) * PAGE_SIZE], vt_pages[slot][...])
    acc_s[...] = alpha * acc_s[...] + pv

    @pl.when(g == n_g - 1)
    def _fin():
        res = acc_s[...] / l_s[...]
        out = jnp.zeros((t_new, BR_W), F32)
        for h in range(HEADS):
            out = jnp.where(masks[h], res[h * t_new:(h + 1) * t_new, :], out)
        o_ref[...] = out


def _fox_decode(layer, proj, cache_kt, cache_vt, cache_lf_t, page_table, n_slots):
    db, t_new, _ = proj.shape
    n_pages = page_table.shape[1]
    n_g = n_pages // n_slots
    cq = P_FX // BR_W

    def page_idx(slot):
        return lambda b, g, pt: (layer, pt[b * n_pages + (n_g - 1 - g) * n_slots + slot], 0, 0)

    kv_specs = [pl.BlockSpec((None, None, BR_W, PAGE_SIZE), page_idx(s)) for s in range(n_slots)]
    lf_specs = [pl.BlockSpec((None, None, HEADS, PAGE_SIZE), page_idx(s)) for s in range(n_slots)]
    rows = HEADS * t_new
    grid_spec = pltpu.PrefetchScalarGridSpec(
        num_scalar_prefetch=1,
        grid=(db, n_g),
        in_specs=[pl.BlockSpec((None, t_new, BR_W), lambda b, g, pt: (b, 0, cq)),
                  pl.BlockSpec((None, t_new, BR_W), lambda b, g, pt: (b, 0, cq + 1)),
                  pl.BlockSpec((None, t_new, BR_W), lambda b, g, pt: (b, 0, cq + 2)),
                  pl.BlockSpec((None, t_new, BR_W), lambda b, g, pt: (b, 0, P_SM // BR_W))]
                 + kv_specs + kv_specs + lf_specs,
        out_specs=pl.BlockSpec((None, t_new, BR_W), lambda b, g, pt: (b, 0, 0)),
        scratch_shapes=[pltpu.VMEM((rows, 1), F32), pltpu.VMEM((rows, 1), F32), pltpu.VMEM((rows, BR_W), F32),
                        pltpu.VMEM((HEADS, PAGE_SIZE), F32)],
    )
    return pl.pallas_call(
        functools.partial(_fox_decode_kernel, t_new, n_slots),
        grid_spec=grid_spec,
        out_shape=jax.ShapeDtypeStruct((db, t_new, BR_W), F32),
        compiler_params=_cparams(("arbitrary", "arbitrary")),
        name="fox_decode",
    )(page_table.reshape(-1), proj, proj, proj, proj,
      *([cache_kt] * n_slots), *([cache_vt] * n_slots), *([cache_lf_t] * n_slots))


def _merge_kernel(br_ref, fx_ref, g0_ref, g1_ref, g2_ref, g3_ref, x_ref, mod_ref, wb_ref, wo_ref, nw_ref,
                  x1_ref, h2_ref):
    merged = _sigmoid(g0_ref[...]) * _dot(br_ref[:, 0:BR_W], wb_ref[0])
    merged += _sigmoid(g1_ref[...]) * _dot(br_ref[:, BR_W:2 * BR_W], wb_ref[1])
    merged += _sigmoid(g2_ref[...]) * _dot(br_ref[:, 2 * BR_W:3 * BR_W], wb_ref[2])
    merged += _sigmoid(g3_ref[...]) * _dot(fx_ref[...], wb_ref[3])
    x1 = x_ref[...] + mod_ref[:, 2 * D_MODEL:3 * D_MODEL] * _dot(merged, wo_ref[...])
    x1_ref[...] = x1
    h2 = _rms(x1) * nw_ref[...]
    h2_ref[...] = h2 * (1.0 + mod_ref[:, 4 * D_MODEL:5 * D_MODEL]) + mod_ref[:, 3 * D_MODEL:4 * D_MODEL]


def _merge(br3, fox, proj, x, mod, w_branch_b, w_out_b, norm_w):
    t = x.shape[0]
    tm = min(t, 256)
    per_row = mod.shape[0] != 1
    mod_spec = (pl.BlockSpec((tm, 6 * D_MODEL), lambda i: (i, 0)) if per_row
                else pl.BlockSpec((1, 6 * D_MODEL), lambda i: (0, 0)))
    gate = lambda b: pl.BlockSpec((tm, D_MODEL), lambda i, b=b: (i, P_GT // D_MODEL + b))
    return pl.pallas_call(
        _merge_kernel,
        grid=(t // tm,),
        in_specs=[pl.BlockSpec((tm, 3 * BR_W), lambda i: (i, 0)),
                  pl.BlockSpec((tm, BR_W), lambda i: (i, 0)),
                  gate(0), gate(1), gate(2), gate(3),
                  pl.BlockSpec((tm, D_MODEL), lambda i: (i, 0)),
                  mod_spec,
                  pl.BlockSpec((N_BRANCH, BR_W, D_MODEL), lambda i: (0, 0, 0)),
                  pl.BlockSpec((D_MODEL, D_MODEL), lambda i: (0, 0)),
                  pl.BlockSpec((1, D_MODEL), lambda i: (0, 0))],
        out_specs=[pl.BlockSpec((tm, D_MODEL), lambda i: (i, 0)),
                   pl.BlockSpec((tm, D_MODEL), lambda i: (i, 0))],
        out_shape=[jax.ShapeDtypeStruct((t, D_MODEL), F32), jax.ShapeDtypeStruct((t, D_MODEL), F32)],
        compiler_params=_cparams(("arbitrary",)),
        name="merge",
    )(br3, fox, proj, proj, proj, proj, x, mod, w_branch_b, w_out_b, norm_w)


def _select_experts(h, wr_t, rb_col):
    tm = h.shape[0]
    logits = _dot_nt(wr_t, h)
    s = _sigmoid(logits)
    sel = s + rb_col
    ninf = -jnp.inf
    sub = _iota((GROUP_SIZE, tm), 0)
    gsc = []
    for g in range(N_GROUPS):
        blk = sel[g * GROUP_SIZE:(g + 1) * GROUP_SIZE, :]
        m1 = jnp.max(blk, axis=0, keepdims=True)
        first = jnp.min(jnp.where(blk == m1, sub, GROUP_SIZE), axis=0, keepdims=True)
        m2 = jnp.max(jnp.where(sub == first, ninf, blk), axis=0, keepdims=True)
        gsc.append(m1 + m2)
    chosen = [jnp.zeros((1, tm), jnp.bool_) for _ in range(N_GROUPS)]
    for _ in range(TOPK_GROUPS):
        mx = gsc[0]
        for g in range(1, N_GROUPS):
            mx = jnp.maximum(mx, gsc[g])
        taken = jnp.zeros((1, tm), jnp.bool_)
        for g in range(N_GROUPS):
            pick = (gsc[g] == mx) & jnp.logical_not(taken)
            taken = taken | pick
            chosen[g] = chosen[g] | pick
            gsc[g] = jnp.where(pick, ninf, gsc[g])
    selm = jnp.concatenate(
        [jnp.where(chosen[g], sel[g * GROUP_SIZE:(g + 1) * GROUP_SIZE, :], ninf) for g in range(N_GROUPS)], axis=0)
    eidx = _iota((N_EXPERTS, tm), 0)
    firsts, picks = [], []
    for _ in range(TOP_K):
        mx = jnp.max(selm, axis=0, keepdims=True)
        first = jnp.min(jnp.where(selm == mx, eidx, N_EXPERTS), axis=0, keepdims=True)
        pick = eidx == first
        selm = jnp.where(pick, ninf, selm)
        firsts.append(first)
        picks.append(pick)
    return s, firsts, picks


def _router_kernel(h_ref, wr_ref, rb_ref, wt_ref):
    s, _, picks = _select_experts(h_ref[...], wr_ref[...], rb_ref[...])
    picked = picks[0]
    for pick in picks[1:]:
        picked = picked | pick
    w = jnp.where(picked, s, 0.0)
    wt_ref[...] = w / jnp.sum(w, axis=0, keepdims=True) * ROUTED_SCALE


def _router(h2, w_router_t, router_bias_col):
    t = h2.shape[0]
    tm = min(t, 512)
    return pl.pallas_call(
        _router_kernel,
        grid=(t // tm,),
        in_specs=[pl.BlockSpec((tm, D_MODEL), lambda i: (i, 0)),
                  pl.BlockSpec((N_EXPERTS, D_MODEL), lambda i: (0, 0)),
                  pl.BlockSpec((N_EXPERTS, 1), lambda i: (0, 0))],
        out_specs=pl.BlockSpec((N_EXPERTS, tm), lambda i: (0, i)),
        out_shape=jax.ShapeDtypeStruct((N_EXPERTS, t), F32),
        compiler_params=_cparams(("arbitrary",)),
        name="router",
    )(h2, w_router_t, router_bias_col)


def _moe_kernel(final, h_ref, wt_ref, wgu_ref, wd_ref, wsgu_ref, wsd_ref, x1_ref, mod_ref, nf_ref, o_ref,
                acc_s, hb_s, wtok_s):
    e = pl.program_id(1)
    n_e = pl.num_programs(1)

    def swiglu(gu):
        return _silu(gu[:, 0:D_EXPERT]) * gu[:, D_EXPERT:2 * D_EXPERT]

    @pl.when(e == 0)
    def _init():
        hb = h_ref[...].astype(BF16)
        hb_s[...] = hb
        wtok_s[...] = _transpose(wt_ref[...])
        acc_s[...] = _dot(swiglu(jnp.dot(hb, wsgu_ref[...], preferred_element_type=F32)), wsd_ref[...])

    wcol = jnp.sum(jnp.where(_iota((1, N_EXPERTS), 1) == e, wtok_s[...], 0.0), axis=1, keepdims=True)
    a = swiglu(_dot(hb_s[...], wgu_ref[...]))
    acc_s[...] += _dot(a, wd_ref[...]) * wcol

    @pl.when(e == n_e - 1)
    def _fin():
        x2 = x1_ref[...] + mod_ref[:, 5 * D_MODEL:6 * D_MODEL] * acc_s[...]
        if final:
            x2 = _rms(x2) * nf_ref[...]
        o_ref[...] = x2


def _moe(layer, h2, wt, wgu, wd, wsgu_b, wsd_b, x1, mod, norm_f, final):
    t = h2.shape[0]
    tm = min(t, 1024)
    per_row = mod.shape[0] != 1
    mod_spec = (pl.BlockSpec((tm, 6 * D_MODEL), lambda i, e: (i, 0)) if per_row
                else pl.BlockSpec((1, 6 * D_MODEL), lambda i, e: (0, 0)))
    return pl.pallas_call(
        functools.partial(_moe_kernel, final),
        grid=(t // tm, N_EXPERTS),
        in_specs=[pl.BlockSpec((tm, D_MODEL), lambda i, e: (i, 0)),
                  pl.BlockSpec((N_EXPERTS, tm), lambda i, e: (0, i)),
                  pl.BlockSpec((None, None, D_MODEL, 2 * D_EXPERT), lambda i, e: (layer, e, 0, 0)),
                  pl.BlockSpec((None, None, D_EXPERT, D_MODEL), lambda i, e: (layer, e, 0, 0)),
                  pl.BlockSpec((D_MODEL, 2 * D_EXPERT), lambda i, e: (0, 0)),
                  pl.BlockSpec((D_EXPERT, D_MODEL), lambda i, e: (0, 0)),
                  pl.BlockSpec((tm, D_MODEL), lambda i, e: (i, 0)),
                  mod_spec,
                  pl.BlockSpec((1, D_MODEL), lambda i, e: (0, 0))],
        out_specs=pl.BlockSpec((tm, D_MODEL), lambda i, e: (i, 0)),
        out_shape=jax.ShapeDtypeStruct((t, D_MODEL), F32),
        scratch_shapes=[pltpu.VMEM((tm, D_MODEL), F32), pltpu.VMEM((tm, D_MODEL), BF16),
                        pltpu.VMEM((tm, N_EXPERTS), F32)],
        compiler_params=_cparams(("arbitrary", "arbitrary")),
        name="moe",
    )(h2, wt, wgu, wd, wsgu_b, wsd_b, x1, mod, norm_f)


MOE_BM = 512
MOE_SPARSE_MIN_TOKENS = 1024


def _router_sparse_kernel(h_ref, wr_ref, rb_ref, eidx_ref, rank_ref, ew_ref, cnt_ref, carry_s):
    tm = h_ref.shape[0]

    @pl.when(pl.program_id(0) == 0)
    def _():
        carry_s[...] = jnp.zeros_like(carry_s)

    s, firsts, picks = _select_experts(h_ref[...], wr_ref[...], rb_ref[...])
    picked = picks[0]
    for pick in picks[1:]:
        picked = picked | pick
    onehot = picked.astype(F32)
    earlier = (_iota((tm, tm), 0) < _iota((tm, tm), 1)).astype(F32)
    carry = carry_s[...]
    rank = _dot(onehot, earlier) + carry[:, 0:1]
    carry_s[...] = carry + jnp.sum(onehot, axis=1, keepdims=True)
    cnt_ref[...] = carry_s[...]

    w = [jnp.sum(jnp.where(pick, s, 0.0), axis=0, keepdims=True) for pick in picks]
    wsum = w[0]
    for wk in w[1:]:
        wsum = wsum + wk
    row8 = _iota((8, tm), 0)
    eidx8 = jnp.zeros((8, tm), jnp.int32)
    rank8 = jnp.zeros((8, tm), jnp.int32)
    ew8 = jnp.zeros((8, tm), F32)
    for k in range(TOP_K):
        rk = jnp.sum(jnp.where(picks[k], rank, 0.0), axis=0, keepdims=True)
        eidx8 = jnp.where(row8 == k, firsts[k], eidx8)
        rank8 = jnp.where(row8 == k, rk.astype(jnp.int32), rank8)
        ew8 = jnp.where(row8 == k, w[k] / wsum * ROUTED_SCALE, ew8)
    eidx_ref[...] = eidx8
    rank_ref[...] = rank8
    ew_ref[...] = ew8


def _router_sparse(h2, w_router_t, router_bias_col):
    t = h2.shape[0]
    tm = 512
    row = lambda dt: jax.ShapeDtypeStruct((8, t), dt)
    return pl.pallas_call(
        _router_sparse_kernel,
        grid=(t // tm,),
        in_specs=[pl.BlockSpec((tm, D_MODEL), lambda i: (i, 0)),
                  pl.BlockSpec((N_EXPERTS, D_MODEL), lambda i: (0, 0)),
                  pl.BlockSpec((N_EXPERTS, 1), lambda i: (0, 0))],
        out_specs=[pl.BlockSpec((8, tm), lambda i: (0, i)), pl.BlockSpec((8, tm), lambda i: (0, i)),
                   pl.BlockSpec((8, tm), lambda i: (0, i)), pl.BlockSpec((N_EXPERTS, 128), lambda i: (0, 0))],
        out_shape=[row(jnp.int32), row(jnp.int32), row(F32), jax.ShapeDtypeStruct((N_EXPERTS, 128), F32)],
        scratch_shapes=[pltpu.VMEM((N_EXPERTS, 128), F32)],
        compiler_params=_cparams(("arbitrary",)),
        name="router_sparse",
    )(h2, w_router_t, router_bias_col)


def _plan_kernel(nblk_pad, cnt_ref, eidx_ref, rank_ref, dest_ref, blk_ref):
    tm = eidx_ref.shape[1]
    cnt = cnt_ref[...]
    padded = jnp.floor((cnt + (MOE_BM - 1.0)) * (1.0 / MOE_BM)) * MOE_BM
    tril = (_iota((N_EXPERTS, N_EXPERTS), 0) >= _iota((N_EXPERTS, N_EXPERTS), 1)).astype(F32)
    pad_end = _sel(padded, tril, "mx")
    start_col = (pad_end - padded)[:, 0:1]
    end_col = pad_end[:, 0:1]

    e_iota = _iota((N_EXPERTS, tm), 0)
    row8 = _iota((8, tm), 0)
    eidx = eidx_ref[...]
    rank = rank_ref[...]
    dest = jnp.zeros((8, tm), jnp.int32)
    for k in range(TOP_K):
        base = jnp.sum(jnp.where(e_iota == eidx[k:k + 1, :], start_col, 0.0), axis=0, keepdims=True)
        dest = jnp.where(row8 == k, base.astype(jnp.int32) + rank[k:k + 1, :], dest)
    dest_ref[...] = dest

    first_row = (_iota((N_EXPERTS, nblk_pad), 1) * MOE_BM).astype(F32)
    blk_e = jnp.sum((end_col <= first_row).astype(F32), axis=0, keepdims=True)
    blk_e = jnp.minimum(blk_e, N_EXPERTS - 1.0).astype(jnp.int32)
    n_used = (pad_end[N_EXPERTS - 1:N_EXPERTS, 0:1] * (1.0 / MOE_BM)).astype(jnp.int32)
    row8b = _iota((8, nblk_pad), 0)
    blk_ref[...] = jnp.where(row8b == 0, blk_e, jnp.where(row8b == 1, n_used, 0))


def _plan(cnt, eidx, rank, nblk_pad):
    t = eidx.shape[1]
    tm = min(t, 2048)
    return pl.pallas_call(
        functools.partial(_plan_kernel, nblk_pad),
        grid=(t // tm,),
        in_specs=[pl.BlockSpec((N_EXPERTS, 128), lambda i: (0, 0)),
                  pl.BlockSpec((8, tm), lambda i: (0, i)), pl.BlockSpec((8, tm), lambda i: (0, i))],
        out_specs=[pl.BlockSpec((8, tm), lambda i: (0, i)), pl.BlockSpec((8, nblk_pad), lambda i: (0, 0))],
        out_shape=[jax.ShapeDtypeStruct((8, t), jnp.int32), jax.ShapeDtypeStruct((8, nblk_pad), jnp.int32)],
        compiler_params=_cparams(("arbitrary",)),
        name="moe_plan",
    )(cnt, eidx, rank)


def _sc_mesh():
    return plsc.VectorSubcoreMesh(core_axis_name="core", subcore_axis_name="subcore")


SC_CHUNK = 256
SC_WINDOW = 128
N_CHUNK = D_MODEL // SC_CHUNK


def _sc_scatter_rows(x, idx, n_rows):
    t = x.shape[0]
    flat_idx = [i for per_chunk in idx for i in per_chunk]

    @pl.kernel(out_type=jax.ShapeDtypeStruct((N_CHUNK * n_rows, SC_CHUNK), x.dtype), mesh=_sc_mesh(),
               scratch_types=[])
    def scatter_kernel(x_hbm, *rest):
        i_hbm, o_hbm = rest[:-1], rest[-1]

        def body(x_vmem, *i_vmem):
            for iv in i_vmem:
                pltpu.sync_copy(x_vmem, o_hbm.at[iv.at[0]])

        for c in range(N_CHUNK):
            pltpu.emit_pipeline(
                body,
                grid=(t // SC_WINDOW,),
                in_specs=[pl.BlockSpec((SC_WINDOW, SC_CHUNK), lambda i, c=c: (i, c))]
                         + [pl.BlockSpec((1, SC_WINDOW), lambda i: (0, i))] * TOP_K,
                out_specs=[],
                core_axis_name=("core", "subcore"),
                dimension_semantics=(pltpu.PARALLEL,),
            )(x_hbm, *i_hbm[c * TOP_K:(c + 1) * TOP_K])

    return scatter_kernel(x, *flat_idx)


def _sc_gather_rows(y, idx):
    a = idx[0].shape[1]
    n_win = a // SC_WINDOW

    @pl.kernel(out_type=jax.ShapeDtypeStruct((N_CHUNK * a, SC_CHUNK), y.dtype), mesh=_sc_mesh(), scratch_types=[])
    def gather_kernel(y_hbm, *rest):
        i_hbm, o_hbm = rest[:-1], rest[-1]

        def body(i_vmem, o_vmem):
            pltpu.sync_copy(y_hbm.at[i_vmem.at[0]], o_vmem)

        for c in range(N_CHUNK):
            pltpu.emit_pipeline(
                body,
                grid=(n_win,),
                in_specs=[pl.BlockSpec((1, SC_WINDOW), lambda i: (0, i))],
                out_specs=[pl.BlockSpec((SC_WINDOW, SC_CHUNK), lambda i, c=c: (c * n_win + i, 0))],
                core_axis_name=("core", "subcore"),
                dimension_semantics=(pltpu.PARALLEL,),
            )(i_hbm[c], o_hbm)

    return gather_kernel(y, *idx)


def _swiglu(gu):
    return _silu(gu[:, 0:D_EXPERT]) * gu[:, D_EXPERT:2 * D_EXPERT]


def _grouped_kernel(blk_e_ref, n_used_ref, xs_ref, wgu_ref, wd_ref, ys_ref):
    @pl.when(pl.program_id(0) < n_used_ref[0])
    def _():
        gu = _dot(xs_ref[0], wgu_ref[0:SC_CHUNK, :])
        for c in range(1, N_CHUNK):
            gu += _dot(xs_ref[c], wgu_ref[c * SC_CHUNK:(c + 1) * SC_CHUNK, :])
        y = _dot(_swiglu(gu), wd_ref[...])
        for c in range(N_CHUNK):
            ys_ref[c] = y[:, c * SC_CHUNK:(c + 1) * SC_CHUNK]


def _grouped(layer, xs, blk_e, n_used, wgu, wd):
    n_rows = xs.shape[1]
    clamp = lambda b, nu: jnp.minimum(b, nu[0] - 1)
    rows_spec = pl.BlockSpec((N_CHUNK, MOE_BM, SC_CHUNK), lambda b, be, nu: (0, clamp(b, nu), 0))
    grid_spec = pltpu.PrefetchScalarGridSpec(
        num_scalar_prefetch=2,
        grid=(n_rows // MOE_BM,),
        in_specs=[rows_spec,
                  pl.BlockSpec((None, None, D_MODEL, 2 * D_EXPERT),
                               lambda b, be, nu: (layer, be[clamp(b, nu)], 0, 0)),
                  pl.BlockSpec((None, None, D_EXPERT, D_MODEL),
                               lambda b, be, nu: (layer, be[clamp(b, nu)], 0, 0))],
        out_specs=rows_spec,
    )
    return pl.pallas_call(
        _grouped_kernel,
        grid_spec=grid_spec,
        out_shape=jax.ShapeDtypeStruct((N_CHUNK, n_rows, SC_CHUNK), F32),
        compiler_params=_cparams(("arbitrary",)),
        name="moe_grouped",
    )(blk_e, n_used, xs, wgu, wd)


def _combine_kernel(final, yg_ref, ew_ref, h_ref, wsgu_ref, wsd_ref, x1_ref, mod_ref, nf_ref, o_ref):
    gu = jnp.dot(h_ref[...].astype(BF16), wsgu_ref[...], preferred_element_type=F32)
    acc = _dot(_swiglu(gu), wsd_ref[...])
    wt = _transpose(ew_ref[...])
    routed = []
    for c in range(N_CHUNK):
        part = wt[:, 0:1] * yg_ref[c, 0]
        for k in range(1, TOP_K):
            part = part + wt[:, k:k + 1] * yg_ref[c, k]
        routed.append(part)
    acc = acc + jnp.concatenate(routed, axis=1)
    x2 = x1_ref[...] + mod_ref[:, 5 * D_MODEL:6 * D_MODEL] * acc
    if final:
        x2 = _rms(x2) * nf_ref[...]
    o_ref[...] = x2


def _combine(yg, ew, h2, wsgu_b, wsd_b, x1, mod, norm_f, final):
    t = h2.shape[0]
    tm = 256
    per_row = mod.shape[0] != 1
    mod_spec = (pl.BlockSpec((tm, 6 * D_MODEL), lambda i: (i, 0)) if per_row
                else pl.BlockSpec((1, 6 * D_MODEL), lambda i: (0, 0)))
    return pl.pallas_call(
        functools.partial(_combine_kernel, final),
        grid=(t // tm,),
        in_specs=[pl.BlockSpec((N_CHUNK, TOP_K, tm, SC_CHUNK), lambda i: (0, 0, i, 0)),
                  pl.BlockSpec((8, tm), lambda i: (0, i)),
                  pl.BlockSpec((tm, D_MODEL), lambda i: (i, 0)),
                  pl.BlockSpec((D_MODEL, 2 * D_EXPERT), lambda i: (0, 0)),
                  pl.BlockSpec((D_EXPERT, D_MODEL), lambda i: (0, 0)),
                  pl.BlockSpec((tm, D_MODEL), lambda i: (i, 0)),
                  mod_spec,
                  pl.BlockSpec((1, D_MODEL), lambda i: (0, 0))],
        out_specs=pl.BlockSpec((tm, D_MODEL), lambda i: (i, 0)),
        out_shape=jax.ShapeDtypeStruct((t, D_MODEL), F32),
        compiler_params=_cparams(("arbitrary",)),
        name="moe_combine",
    )(yg, ew, h2, wsgu_b, wsd_b, x1, mod, norm_f)


def _moe_sparse(layer, h2, w_router_t, router_bias_col, wgu, wd, wsgu_b, wsd_b, x1, mod, norm_f, final):
    t = h2.shape[0]
    n_blk = t * TOP_K // MOE_BM + N_EXPERTS
    n_rows = n_blk * MOE_BM
    nblk_pad = -(-n_blk // 128) * 128
    eidx, rank, ew, cnt = _router_sparse(h2, w_router_t, router_bias_col)
    dest, blk = _plan(cnt, eidx, rank, nblk_pad)
    per_pick = [[dest[k:k + 1] + c * n_rows for k in range(TOP_K)] for c in range(N_CHUNK)]
    flat = dest[0:TOP_K].reshape(1, TOP_K * t)
    xs = _sc_scatter_rows(h2, per_pick, n_rows).reshape(N_CHUNK, n_rows, SC_CHUNK)
    ys = _grouped(layer, xs, blk[0], blk[1, 0:1], wgu, wd)
    yg = _sc_gather_rows(ys.reshape(N_CHUNK * n_rows, SC_CHUNK), [flat + c * n_rows for c in range(N_CHUNK)])
    return _combine(yg.reshape(N_CHUNK, TOP_K, t, SC_CHUNK), ew, h2, wsgu_b, wsd_b, x1, mod, norm_f, final)


def _permute_in_cols(w):
    ml = w[..., 0:ML_COLS]
    fx = w[..., OFF_FX:OFF_GT]
    small = jnp.concatenate([ml[..., 4 * BR_W:], fx[..., 3 * BR_W:]], axis=-1)
    pad = jnp.zeros(w.shape[:-1] + (P_GT - P_SM - small.shape[-1],), w.dtype)
    return jnp.concatenate([ml[..., :4 * BR_W], w[..., OFF_HG:OFF_RT], w[..., OFF_RT:OFF_FX], fx[..., :3 * BR_W],
                            small, pad, w[..., OFF_GT:]], axis=-1)


def _rope_tables(pos):
    half = HEAD_DIM // 2
    inv = ROPE_BASE ** (-jnp.arange(half, dtype=F32) / half)
    ang = pos.astype(F32)[:, None] * inv[None, :]
    cos = jnp.cos(ang)
    sin = jnp.sin(ang)
    cos_h = jnp.concatenate([cos, cos], axis=-1)
    sin_h = jnp.concatenate([-sin, sin], axis=-1)
    return jnp.tile(cos_h, (1, HEADS)), jnp.tile(sin_h, (1, HEADS))


def kernel(x_prompt, x_sample, cache_fox_k, cache_fox_v, cache_fox_logf, state_mlstm_C, state_mlstm_n, state_mlstm_m, state_hgrn_S, state_ret_S, page_table, c_prompt, c_sample, w_ada, b_ada, norm_mix_w, norm_ffn_w, w_in, b_in, hgrn_lb_logits, mlstm_norm_w, hgrn_norm_w, ret_norm_w, w_branch, w_out, w_router, router_bias, w_exp_gu, w_exp_down, w_shared_gu, w_shared_down, norm_f_w):
    depth = w_in.shape[0]
    bp, seq, _ = x_prompt.shape
    db, t_new, _ = x_sample.shape
    n_pool = cache_fox_k.shape[1]
    n_pages = page_table.shape[1]
    past_len = n_pages * PAGE_SIZE
    assert bp == 1 and seq % 128 == 0 and t_new == 8

    w_in_p = _permute_in_cols(w_in).astype(BF16)
    b_in_p = _permute_in_cols(b_in).reshape(depth, 1, P_TOT)
    w_branch_b = w_branch.astype(BF16)
    w_out_b = w_out.astype(BF16)
    wsgu_b = w_shared_gu.astype(BF16)
    wsd_b = w_shared_down.astype(BF16)
    w_router_t = jnp.swapaxes(w_router, 1, 2)
    cache_k = jnp.transpose(cache_fox_k, (0, 1, 3, 4, 2)).reshape(depth, n_pool, BR_W, PAGE_SIZE)
    cache_v = jnp.transpose(cache_fox_v, (0, 1, 3, 4, 2)).reshape(depth, n_pool, BR_W, PAGE_SIZE)
    cache_lf_t = jnp.swapaxes(cache_fox_logf, 2, 3)

    mods = _ada(jnp.concatenate([c_prompt, c_sample], axis=0), w_ada, b_ada)

    cos_p, sin_p = _rope_tables(jnp.arange(seq))
    cos_s, sin_s = _rope_tables(past_len + jnp.arange(t_new))

    def trunk(x, mod_of_layer, cos_t, sin_t, init, fox_fn, lc):
        b, l, _ = x.shape
        xt = x.reshape(b * l, D_MODEL)
        c_in, n_in, m_in, sh_in, sr_in = init
        per_layer = []
        for layer in range(depth):
            mod = mod_of_layer(layer)
            kv_t = b == 1 and (b * l) % 128 == 0
            proj, *kv = _inproj(layer, xt, mod, norm_mix_w[layer][None], w_in_p, b_in_p, kv_t)
            proj3 = proj.reshape(b, l, P_TOT)
            if kv_t:
                fox_k, fox_v = (a.reshape(HEADS, HEAD_DIM, l).transpose(2, 0, 1)[None] for a in kv)
            else:
                fox_k, fox_v = (proj3[..., P_FX + i * BR_W:P_FX + (i + 1) * BR_W].reshape(b, l, HEADS, HEAD_DIM)
                                for i in (1, 2))
            br3, lf_rows, cum_t, c_new, n_new, m_new, sh_new, sr_new = _mixers(
                layer, proj3, cos_t, sin_t, c_in[layer], n_in[layer], m_in[layer], sh_in[layer], sr_in[layer],
                hgrn_lb_logits, mlstm_norm_w[layer][None], hgrn_norm_w[layer][None], ret_norm_w[layer][None], lc)
            fox = fox_fn(layer, proj3, cum_t)
            x1, h2 = _merge(br3.reshape(b * l, 3 * BR_W), fox.reshape(b * l, BR_W), proj, xt, mod,
                            w_branch_b[layer], w_out_b[layer], norm_ffn_w[layer][None])
            final = layer == depth - 1
            if b * l >= MOE_SPARSE_MIN_TOKENS:
                xt = _moe_sparse(layer, h2, w_router_t[layer], router_bias[layer][:, None], w_exp_gu, w_exp_down,
                                 wsgu_b[layer], wsd_b[layer], x1, mod, norm_f_w[None], final)
            else:
                wt = _router(h2, w_router_t[layer], router_bias[layer][:, None])
                xt = _moe(layer, h2, wt, w_exp_gu, w_exp_down, wsgu_b[layer], wsd_b[layer], x1, mod, norm_f_w[None],
                          final)
            per_layer.append((
                fox_k,
                fox_v,
                lf_rows[..., SM_FLF:SM_FLF + HEADS],
                c_new.reshape(b, HEADS, HEAD_DIM, HEAD_DIM),
                n_new.reshape(b, HEADS, HEAD_DIM),
                m_new[:, 0, :HEADS],
                sh_new.reshape(b, HEADS, HEAD_DIM, HEAD_DIM),
                sr_new.reshape(b, HEADS, HEAD_DIM, HEAD_DIM)))
        stacked = tuple(jnp.stack([p[i] for p in per_layer]) for i in range(8))
        return (xt.reshape(b, l, D_MODEL),) + stacked

    def head_major(s, b):
        return s.astype(F32).reshape(depth, b, BR_W, HEAD_DIM)

    zero_state = jnp.zeros((depth, bp, BR_W, HEAD_DIM), F32)
    prompt_init = (zero_state, jnp.zeros((depth, bp, 1, BR_W), F32), jnp.zeros((depth, bp, 1, 128), F32),
                   zero_state, zero_state)
    prompt_out = trunk(x_prompt, lambda layer: mods[layer, 0:1], cos_p, sin_p, prompt_init,
                       lambda layer, proj3, cum_t: _fox_prompt(proj3[0], cum_t[0], min(seq, 512)),
                       128)

    m_pad = jnp.pad(state_mlstm_m.astype(F32), ((0, 0), (0, 0), (0, 128 - HEADS))).reshape(depth, db, 1, 128)
    sample_init = (head_major(state_mlstm_C, db), state_mlstm_n.astype(F32).reshape(depth, db, 1, BR_W), m_pad,
                   head_major(state_hgrn_S, db), head_major(state_ret_S, db))
    n_slots = next(n for n in (32, 16, 8, 4, 2, 1) if n_pages % n == 0)
    sample_out = trunk(x_sample, lambda layer: jnp.repeat(mods[layer, 1:], t_new, axis=0), cos_s, sin_s, sample_init,
                       lambda layer, proj3, cum_t: _fox_decode(layer, proj3, cache_k, cache_v, cache_lf_t,
                                                               page_table, n_slots),
                       t_new)

    return (prompt_out[0], sample_out[0]) + prompt_out[1:] + sample_out[1:]
```

```python
import functools
import math

import numpy as np
import jax
import jax.numpy as jnp
from jax import lax
from jax.experimental import pallas as pl
from jax.experimental.pallas import tpu as pltpu
from jax.experimental.pallas import tpu_sc as plsc

F32 = jnp.float32
BF16 = jnp.bfloat16
HIGHEST = lax.Precision.HIGHEST

D_MODEL = 1024
N_BRANCH = 4
BR_W = 256
HEAD_DIM = 64
HEADS = 4
ROPE_BASE = 10000.0
RMS_EPS = 1e-6
N_EXPERTS = 64
TOP_K = 6
N_GROUPS = 8
GROUP_SIZE = N_EXPERTS // N_GROUPS
TOPK_GROUPS = 4
D_EXPERT = 256
ROUTED_SCALE = 2.5
PAGE_SIZE = 128
QK_SCALE = HEAD_DIM ** -0.5
LOG2E = math.log2(math.e)

ML_COLS = 4 * BR_W + 2 * HEADS
OFF_HG = ML_COLS
OFF_RT = OFF_HG + 4 * BR_W
OFF_FX = OFF_RT + 4 * BR_W
OFF_GT = OFF_FX + 3 * BR_W + HEADS
N_IN = OFF_GT + N_BRANCH * D_MODEL

P_ML, P_HG, P_RT, P_FX, P_SM, P_GT, P_TOT = 0, 1024, 2048, 3072, 3840, 4096, 8192
SM_IG, SM_MLF, SM_FLF = 0, 4, 8

NEG = -1e30
VMEM_LIMIT = 56 * 1024 * 1024


def _cparams(sem):
    return pltpu.CompilerParams(dimension_semantics=sem, vmem_limit_bytes=VMEM_LIMIT)


def _dot(a, b):
    return jnp.dot(a.astype(BF16), b.astype(BF16), preferred_element_type=F32)


def _dot_nt(a, b):
    return lax.dot_general(a.astype(BF16), b.astype(BF16), (((1,), (1,)), ((), ())), preferred_element_type=F32)


def _dot_tn(a, b):
    return lax.dot_general(a.astype(BF16), b.astype(BF16), (((0,), (0,)), ((), ())), preferred_element_type=F32)


def _split3(x):
    x1 = x.astype(BF16)
    r1 = x - x1.astype(F32)
    x2 = r1.astype(BF16)
    x3 = (r1 - x2.astype(F32)).astype(BF16)
    return x1, x2, x3


def _sel(x, m01, dims):
    m = m01.astype(BF16)
    x1, x2, x3 = _split3(x)
    if dims == "mx":
        f = lambda xi: jnp.dot(m, xi, preferred_element_type=F32)
    elif dims == "xm":
        f = lambda xi: jnp.dot(xi, m, preferred_element_type=F32)
    elif dims == "xmT":
        f = lambda xi: lax.dot_general(xi, m, (((1,), (1,)), ((), ())), preferred_element_type=F32)
    else:
        f = lambda xi: lax.dot_general(m, xi, (((1,), (1,)), ((), ())), preferred_element_type=F32)
    return (f(x1) + f(x2)) + f(x3)


def _iota(shape, dim):
    return lax.broadcasted_iota(jnp.int32, shape, dim)


def _eye(n):
    return (_iota((n, n), 0) == _iota((n, n), 1)).astype(F32)


def _transpose(x):
    if x.shape[0] % 128 == 0 and x.shape[1] % 128 == 0:
        return x.T
    return _sel(x, _eye(x.shape[1]), "mxT")


def _sigmoid(x):
    return jax.nn.sigmoid(x)


def _silu(x):
    return x * jax.nn.sigmoid(x)


def _log_sigmoid(x):
    return jnp.minimum(x, 0.0) - jnp.log1p(jnp.exp(-jnp.abs(x)))


def _head_masks(n=BR_W):
    lane = _iota((1, n), 1) >> 6
    return [lane == h for h in range(HEADS)]


def _block_diag_mask():
    return (_iota((BR_W, BR_W), 0) >> 6) == (_iota((BR_W, BR_W), 1) >> 6)


def _per_head_lanes(vals, masks):
    out = jnp.where(masks[0], vals[0], 0.0)
    for h in range(1, HEADS):
        out = jnp.where(masks[h], vals[h], out)
    return out


def _rms(x, eps=RMS_EPS):
    return x * lax.rsqrt(jnp.mean(x * x, axis=-1, keepdims=True) + eps)


def _head_norm(o, gain, bdf):
    ms = _sel(o * o, bdf, "xm") * (1.0 / HEAD_DIM)
    return o * lax.rsqrt(ms + RMS_EPS) * gain


def _ada_kernel(c_ref, w_ref, b_ref, o_ref):
    o_ref[...] = _dot(_silu(c_ref[...]), w_ref[...]) + b_ref[...]


def _ada(c_all, w_ada, b_ada):
    depth = w_ada.shape[0]
    n_c = c_all.shape[0]
    tn = 1536
    return pl.pallas_call(
        _ada_kernel,
        grid=(depth, 6 * D_MODEL // tn),
        in_specs=[pl.BlockSpec((n_c, D_MODEL), lambda l, j: (0, 0)),
                  pl.BlockSpec((None, D_MODEL, tn), lambda l, j: (l, 0, j)),
                  pl.BlockSpec((None, 1, tn), lambda l, j: (l, 0, j))],
        out_specs=pl.BlockSpec((None, n_c, tn), lambda l, j: (l, 0, j)),
        out_shape=jax.ShapeDtypeStruct((depth, n_c, 6 * D_MODEL), F32),
        compiler_params=_cparams(("arbitrary", "arbitrary")),
        name="ada",
    )(c_all, w_ada, b_ada.reshape(depth, 1, 6 * D_MODEL))


def _inproj_kernel(kv_t, x_ref, mod_ref, nw_ref, w_ref, b_ref, o_ref, *rest):
    h_scr = rest[-1]
    j = pl.program_id(1)

    @pl.when(j == 0)
    def _():
        h = _rms(x_ref[...]) * nw_ref[...]
        h = h * (1.0 + mod_ref[:, D_MODEL:2 * D_MODEL]) + mod_ref[:, 0:D_MODEL]
        h_scr[...] = h.astype(BF16)

    res = jnp.dot(h_scr[...], w_ref[...], preferred_element_type=F32) + b_ref[...]
    o_ref[...] = res

    if kv_t:
        kt_ref, vt_ref = rest[0], rest[1]
        c0 = P_FX % res.shape[1]

        @pl.when(j == P_FX // res.shape[1])
        def _():
            kt_ref[...] = res[:, c0 + BR_W:c0 + 2 * BR_W].T
            vt_ref[...] = res[:, c0 + 2 * BR_W:c0 + 3 * BR_W].T


def _inproj(layer, x, mod, norm_w, w_p, b_p, kv_t):
    t = x.shape[0]
    tm = min(t, 1024)
    tn = 1024
    per_row = mod.shape[0] != 1
    mod_spec = (pl.BlockSpec((tm, 6 * D_MODEL), lambda i, j: (i, 0)) if per_row
                else pl.BlockSpec((1, 6 * D_MODEL), lambda i, j: (0, 0)))
    out_specs = [pl.BlockSpec((tm, tn), lambda i, j: (i, j))]
    out_shape = [jax.ShapeDtypeStruct((t, P_TOT), F32)]
    if kv_t:
        out_specs += [pl.BlockSpec((BR_W, tm), lambda i, j: (0, i))] * 2
        out_shape += [jax.ShapeDtypeStruct((BR_W, t), F32)] * 2
    return pl.pallas_call(
        functools.partial(_inproj_kernel, kv_t),
        grid=(t // tm, P_TOT // tn),
        in_specs=[pl.BlockSpec((tm, D_MODEL), lambda i, j: (i, 0)),
                  mod_spec,
                  pl.BlockSpec((1, D_MODEL), lambda i, j: (0, 0)),
                  pl.BlockSpec((None, D_MODEL, tn), lambda i, j: (layer, 0, j)),
                  pl.BlockSpec((None, 1, tn), lambda i, j: (layer, 0, j))],
        out_specs=out_specs,
        out_shape=out_shape,
        scratch_shapes=[pltpu.VMEM((tm, D_MODEL), BF16)],
        compiler_params=_cparams(("arbitrary", "arbitrary")),
        name="inproj",
    )(x, mod, norm_w, w_p, b_p)


def _mixers_kernel(layer, lc, sc,
                   ml_ref, hg_ref, rt_ref, sm_ref, cos_ref, sin_ref,
                   c0_ref, n0_ref, m0_ref, sh0_ref, sr0_ref, lbl_ref, gml_ref, ghg_ref, grt_ref,
                   br_ref, lf_ref, cumt_ref, cout_ref, nout_ref, mout_ref, shout_ref, srout_ref,
                   cbd, sht, srbd, n_s, m_s, carry_s, ohg_s):
    c = pl.program_id(1)
    n_c = pl.num_programs(1)
    masks = _head_masks()
    bd = _block_diag_mask()
    bdf = bd.astype(F32)
    tile = ((_iota((HEAD_DIM, BR_W), 1) & (HEAD_DIM - 1)) == _iota((HEAD_DIM, BR_W), 0)).astype(F32)

    @pl.when(c == 0)
    def _init():
        def expand(ref):
            return jnp.where(bd, _sel(ref[...], tile, "xm"), 0.0)
        cbd[...] = expand(c0_ref)
        sht[...] = _transpose(expand(sh0_ref))
        srbd[...] = expand(sr0_ref)
        n_s[...] = n0_ref[...]
        m_s[...] = m0_ref[...]
        carry_s[...] = jnp.zeros_like(carry_s)

    row = _iota((lc, lc), 0)
    col = _iota((lc, lc), 1)
    causal = row >= col
    tril = causal.astype(F32)

    sm = sm_ref[:, 0:128]
    lane128 = _iota((1, 128), 1)
    sm2 = jnp.where((lane128 >= SM_MLF) & (lane128 < SM_FLF + HEADS), _log_sigmoid(sm), sm)
    cum = _sel(sm2, tril, "mx")
    sm2_t = _transpose(sm2)
    cum_t = _transpose(cum)
    lf_ref[...] = sm2
    cum_tg = cum_t + carry_s[...]
    carry_s[...] = cum_tg[:, lc - 1:lc]
    cumt_ref[...] = cum_tg[SM_FLF:SM_FLF + 8, :]

    q = ml_ref[:, 0:BR_W]
    k = ml_ref[:, BR_W:2 * BR_W] * QK_SCALE
    v = ml_ref[:, 2 * BR_W:3 * BR_W]
    og = ml_ref[:, 3 * BR_W:4 * BR_W]
    n_row = n_s[...]
    m_row = m_s[...]
    q_c = _dot_nt(q, cbd[...])
    h_all = jnp.zeros((lc, BR_W), F32)
    w_lanes = jnp.zeros((lc, BR_W), F32)
    decay_lanes = jnp.zeros((1, BR_W), F32)
    m_new_row = jnp.zeros((1, 128), F32)
    for h in range(HEADS):
        ig_c = sm2[:, SM_IG + h:SM_IG + h + 1]
        b_c = cum[:, SM_MLF + h:SM_MLF + h + 1]
        ig_r = sm2_t[SM_IG + h:SM_IG + h + 1, :]
        b_r = cum_t[SM_MLF + h:SM_MLF + h + 1, :]
        m_prev = m_row[:, h:h + 1]
        dmat = jnp.where(causal, b_c - b_r + ig_r, NEG)
        m_inter = b_c + m_prev
        m_t = jnp.maximum(m_inter, jnp.max(dmat, axis=1, keepdims=True))
        w_intra = jnp.exp(dmat - m_t)
        w_inter = jnp.exp(m_inter - m_t)
        qh = jnp.where(masks[h], q, 0.0)
        a = _dot_nt(qh, k) * w_intra
        num = _dot(a, v) + w_inter * q_c
        den = jnp.sum(a, axis=1, keepdims=True) + w_inter * jnp.sum(qh * n_row, axis=1, keepdims=True)
        hh = num / jnp.maximum(jnp.abs(den), jnp.exp(-m_t))
        h_all = jnp.where(masks[h], hh, h_all)
        m_new = m_t[lc - 1:lc, :]
        b_last = b_c[lc - 1:lc, :]
        w_s = jnp.exp(b_last - b_c + ig_c - m_new)
        decay = jnp.exp(b_last + m_prev - m_new)
        w_lanes = jnp.where(masks[h], w_s, w_lanes)
        decay_lanes = jnp.where(masks[h], decay, decay_lanes)
        m_new_row = jnp.where(lane128 == h, m_new, m_new_row)
    kw = k * w_lanes
    cbd[...] = cbd[...] * decay_lanes + jnp.where(bd, _dot_tn(v * w_lanes, k), 0.0)
    n_s[...] = n_row * decay_lanes + jnp.sum(kw, axis=0, keepdims=True)
    m_s[...] = m_new_row
    out_ml = _head_norm(h_all, gml_ref[...], bdf) * _sigmoid(og)

    lbl = lbl_ref[...]
    pr = jnp.exp(lbl - jnp.max(lbl, axis=0, keepdims=True))
    pr = pr / jnp.sum(pr, axis=0, keepdims=True)
    lb = jnp.zeros((1, BR_W), F32)
    for i in range(1, layer + 1):
        lb = lb + pr[i:i + 1, :]
    tril_sc = (_iota((sc, sc), 0) >= _iota((sc, sc), 1)).astype(F32)
    s_idx = _iota((sc, BR_W), 0)

    def hg_body(i, carry):
        r0 = pl.multiple_of(i * sc, sc)
        hq = hg_ref[pl.ds(r0, sc), 0:BR_W]
        hf = hg_ref[pl.ds(r0, sc), BR_W:2 * BR_W]
        vi = hg_ref[pl.ds(r0, sc), 2 * BR_W:3 * BR_W]
        qi = _silu(hq)
        f = lb + (1.0 - lb) * _sigmoid(hf)
        ki = 1.0 - f
        bi = _sel(jnp.log(f), tril_sc, "mx")
        st = sht[...]
        o_inter = _dot_nt(qi * jnp.exp(bi), st)
        rows = []
        for t in range(sc):
            e_t = jnp.exp(jnp.where(s_idx <= t, bi[t:t + 1, :] - bi, NEG)) * qi[t:t + 1, :] * ki
            rows.append(e_t)
        e_all = jnp.concatenate(rows, axis=0)
        r_all = _dot(e_all, bdf)
        o_diag = jnp.sum(r_all.reshape(sc, sc, BR_W) * vi[None, :, :], axis=1)
        ohg_s[pl.ds(r0, sc), :] = o_inter + o_diag
        b_last = bi[sc - 1:sc, :]
        sht[...] = st * jnp.exp(b_last) + jnp.where(bd, _dot_tn(vi, ki * jnp.exp(b_last - bi)), 0.0)
        return carry

    lax.fori_loop(0, lc // sc, hg_body, 0, unroll=True)
    out_hg = _head_norm(ohg_s[...], ghg_ref[...], bdf) * _silu(hg_ref[:, 3 * BR_W:4 * BR_W])

    cosv = cos_ref[...]
    sinv = sin_ref[...]
    lane = _iota((1, BR_W), 1)
    first_half = (lane & (HEAD_DIM - 1)) < (HEAD_DIM // 2)

    def rope(x):
        partner = jnp.where(first_half, pltpu.roll(x, BR_W - HEAD_DIM // 2, 1), pltpu.roll(x, HEAD_DIM // 2, 1))
        return x * cosv + partner * sinv

    rq = rope(rt_ref[:, 0:BR_W])
    rk = rope(rt_ref[:, BR_W:2 * BR_W]) * QK_SCALE
    rv = rt_ref[:, 2 * BR_W:3 * BR_W]
    lg = [math.log1p(-(2.0 ** (-5.0 - h))) for h in range(HEADS)]
    lg_lanes = _per_head_lanes([jnp.full((1, 1), g, F32) for g in lg], masks)
    diff = (row - col).astype(F32)
    o_rt = jnp.zeros((lc, BR_W), F32)
    for h in range(HEADS):
        dec = jnp.exp(jnp.where(causal, diff * lg[h], NEG))
        a = _dot_nt(jnp.where(masks[h], rq, 0.0), rk) * dec
        o_rt = jnp.where(masks[h], _dot(a, rv), o_rt)
    t_idx = _iota((lc, BR_W), 0).astype(F32)
    o_rt = o_rt + _dot(rq * jnp.exp((t_idx + 1.0) * lg_lanes), srbd[...])
    w_ret = jnp.exp((lc - 1.0 - t_idx) * lg_lanes)
    srbd[...] = srbd[...] * jnp.exp(lc * lg_lanes) + jnp.where(bd, _dot_tn(rk * w_ret, rv), 0.0)
    out_rt = _head_norm(o_rt, grt_ref[...], bdf) * _silu(rt_ref[:, 3 * BR_W:4 * BR_W])

    br_ref[:, 0:BR_W] = out_ml
    br_ref[:, BR_W:2 * BR_W] = out_hg
    br_ref[:, 2 * BR_W:3 * BR_W] = out_rt

    @pl.when(c == n_c - 1)
    def _fin():
        def compact(x):
            return _sel(x, tile, "xmT")
        cout_ref[...] = compact(cbd[...])
        shout_ref[...] = compact(_transpose(sht[...]))
        srout_ref[...] = compact(srbd[...])
        nout_ref[...] = n_s[...]
        mout_ref[...] = m_s[...]


def _mixers(layer, proj, cos_t, sin_t, c0, n0, m0, sh0, sr0, lb_logits, g_ml, g_hg, g_rt, lc):
    b, l, _ = proj.shape
    sc = min(16, lc)
    n_c = l // lc
    depth = lb_logits.shape[0]
    cb = lambda blk: pl.BlockSpec((None, lc, 1024), lambda bi, ci, blk=blk: (bi, ci, blk))
    st_spec = pl.BlockSpec((None, BR_W, HEAD_DIM), lambda bi, ci: (bi, 0, 0))
    row_spec = lambda n: pl.BlockSpec((None, 1, n), lambda bi, ci: (bi, 0, 0))
    full = lambda r, cc: pl.BlockSpec((r, cc), lambda bi, ci: (0, 0))
    outs = pl.pallas_call(
        functools.partial(_mixers_kernel, layer, lc, sc),
        grid=(b, n_c),
        in_specs=[cb(0), cb(1), cb(2),
                  pl.BlockSpec((None, lc, 256), lambda bi, ci: (bi, ci, P_SM // 256)),
                  pl.BlockSpec((lc, BR_W), lambda bi, ci: (ci, 0)),
                  pl.BlockSpec((lc, BR_W), lambda bi, ci: (ci, 0)),
                  st_spec, row_spec(BR_W), row_spec(128), st_spec, st_spec,
                  full(depth, BR_W), full(1, BR_W), full(1, BR_W), full(1, BR_W)],
        out_specs=[pl.BlockSpec((None, lc, 3 * BR_W), lambda bi, ci: (bi, ci, 0)),
                   pl.BlockSpec((None, lc, 128), lambda bi, ci: (bi, ci, 0)),
                   pl.BlockSpec((None, 8, lc), lambda bi, ci: (bi, 0, ci)),
                   st_spec, row_spec(BR_W), row_spec(128), st_spec, st_spec],
        out_shape=[jax.ShapeDtypeStruct((b, l, 3 * BR_W), F32),
                   jax.ShapeDtypeStruct((b, l, 128), F32),
                   jax.ShapeDtypeStruct((b, 8, l), F32),
                   jax.ShapeDtypeStruct((b, BR_W, HEAD_DIM), F32),
                   jax.ShapeDtypeStruct((b, 1, BR_W), F32),
                   jax.ShapeDtypeStruct((b, 1, 128), F32),
                   jax.ShapeDtypeStruct((b, BR_W, HEAD_DIM), F32),
                   jax.ShapeDtypeStruct((b, BR_W, HEAD_DIM), F32)],
        scratch_shapes=[pltpu.VMEM((BR_W, BR_W), F32), pltpu.VMEM((BR_W, BR_W), F32), pltpu.VMEM((BR_W, BR_W), F32),
                        pltpu.VMEM((1, BR_W), F32), pltpu.VMEM((1, 128), F32), pltpu.VMEM((128, 1), F32),
                        pltpu.VMEM((lc, BR_W), F32)],
        compiler_params=_cparams(("arbitrary", "arbitrary")),
        name="mixers",
    )(proj, proj, proj, proj, cos_t, sin_t, c0, n0, m0, sh0, sr0, lb_logits, g_ml, g_hg, g_rt)
    return outs


def _fox_prompt_kernel(tq, tk, strip, qi_ref, kj_ref, q_ref, k_ref, v_ref, ck_ref, o_ref,
                       m_s, l_s, alpha_s, acc_s, s_scr, p_scr):
    step = pl.program_id(0)
    i = qi_ref[step]
    j = kj_ref[step]
    masks = _head_masks()

    @pl.when(j == 0)
    def _init():
        m_s[...] = jnp.full_like(m_s, NEG)
        l_s[...] = jnp.zeros_like(l_s)
        acc_s[...] = jnp.zeros_like(acc_s)

    q = q_ref[...] * (QK_SCALE * LOG2E)
    kb = k_ref[...].astype(BF16)
    vb = v_ref[...].astype(BF16)
    nck = ck_ref[...] * (-LOG2E)
    n_rep = tk // 128

    def attend(diagonal):
        col = _iota((strip, tk), 1)
        row = _iota((strip, tk), 0)
        for h in range(HEADS):
            s_scr[h] = _dot_nt(jnp.where(masks[h], q, 0.0), kb) + nck[h:h + 1, :]
            parts = []
            for r0 in range(0, tq, strip):
                sc = s_scr[h, pl.ds(r0, strip), :]
                if diagonal:
                    sc = jnp.where(row + (r0 + i * tq - j * tk) >= col, sc, NEG)
                    s_scr[h, pl.ds(r0, strip), :] = sc
                parts.append(jnp.max(sc, axis=1, keepdims=True))
            m_old = m_s[h]
            m_new = jnp.maximum(m_old, jnp.broadcast_to(jnp.concatenate(parts, axis=0), (tq, 128)))
            m_s[h] = m_new
            alpha_s[h] = jnp.exp2(m_old - m_new)
            for r0 in range(0, tq, strip):
                rows = pl.ds(r0, strip)
                m_rep = jnp.concatenate([m_s[h, rows, :]] * n_rep, axis=1)
                p = jnp.exp2(s_scr[h, rows, :] - m_rep)
                p_scr[h, rows, :] = p.astype(BF16)
                psum = p[:, 0:128]
                for c in range(1, n_rep):
                    psum = psum + p[:, c * 128:(c + 1) * 128]
                l_s[h, rows, :] = alpha_s[h, rows, :] * l_s[h, rows, :] + psum
            alpha = alpha_s[h]
            acc_s[h] = jnp.concatenate([alpha, alpha], axis=1) * acc_s[h] + jnp.dot(p_scr[h], vb,
                                                                                   preferred_element_type=F32)

    last = (j + 1) * tk >= (i + 1) * tq

    @pl.when(jnp.logical_not(last))
    def _off_diagonal():
        attend(False)

    @pl.when(last)
    def _diagonal():
        attend(True)
        out = jnp.zeros((tq, BR_W), F32)
        for h in range(HEADS):
            out = jnp.where(masks[h], acc_s[h] / jnp.sum(l_s[h], axis=1, keepdims=True), out)
        o_ref[...] = out


def _fox_prompt(proj, cum_t, tq):
    t = proj.shape[0]
    tk = 2 * tq if t % (2 * tq) == 0 else tq
    nq = t // tq
    n_kj = [((i + 1) * tq - 1) // tk + 1 for i in range(nq)]
    qi = np.concatenate([np.full(n, i, np.int32) for i, n in enumerate(n_kj)])
    kj = np.concatenate([np.arange(n, dtype=np.int32) for n in n_kj])
    cq = P_FX // BR_W
    grid_spec = pltpu.PrefetchScalarGridSpec(
        num_scalar_prefetch=2,
        grid=(len(qi),),
        in_specs=[pl.BlockSpec((tq, BR_W), lambda s, qi, kj: (qi[s], cq)),
                  pl.BlockSpec((tk, BR_W), lambda s, qi, kj: (kj[s], cq + 1)),
                  pl.BlockSpec((tk, BR_W), lambda s, qi, kj: (kj[s], cq + 2)),
                  pl.BlockSpec((8, tk), lambda s, qi, kj: (0, kj[s]))],
        out_specs=pl.BlockSpec((tq, BR_W), lambda s, qi, kj: (qi[s], 0)),
        scratch_shapes=[pltpu.VMEM((HEADS, tq, 128), F32), pltpu.VMEM((HEADS, tq, 128), F32),
                        pltpu.VMEM((HEADS, tq, 128), F32), pltpu.VMEM((HEADS, tq, BR_W), F32),
                        pltpu.VMEM((HEADS, tq, tk), F32), pltpu.VMEM((HEADS, tq, tk), BF16)],
    )
    return pl.pallas_call(
        functools.partial(_fox_prompt_kernel, tq, tk, min(16384 // tk, tq)),
        grid_spec=grid_spec,
        out_shape=jax.ShapeDtypeStruct((t, BR_W), F32),
        compiler_params=_cparams(("arbitrary",)),
        name="fox_prompt",
    )(jnp.asarray(qi), jnp.asarray(kj), proj, proj, proj, cum_t)


def _fox_decode_kernel(t_new, n_slots, pt_ref, q_ref, k_ref, v_ref, sm_ref, *rest):
    kt_pages = rest[0:n_slots]
    vt_pages = rest[n_slots:2 * n_slots]
    lf_pages = rest[2 * n_slots:3 * n_slots]
    o_ref = rest[3 * n_slots]
    m_s, l_s, acc_s, carry_s = rest[3 * n_slots + 1:]
    g = pl.program_id(1)
    n_g = pl.num_programs(1)
    masks = _head_masks()
    rows = HEADS * t_new

    q = q_ref[...] * QK_SCALE
    qbd = jnp.concatenate([jnp.where(masks[h], q, 0.0) for h in range(HEADS)], axis=0).astype(BF16)

    def per_head_rows(x):
        return jnp.concatenate([jnp.broadcast_to(x[h:h + 1, :], (t_new, x.shape[1])) for h in range(HEADS)], axis=0)

    def softmax_step(sc):
        m_old = m_s[...]
        m_new = jnp.maximum(m_old, jnp.max(sc, axis=1, keepdims=True))
        alpha = jnp.exp(m_old - m_new)
        p = jnp.exp(sc - m_new)
        l_s[...] = alpha * l_s[...] + jnp.sum(p, axis=1, keepdims=True)
        m_s[...] = m_new
        return alpha, p.astype(BF16)

    @pl.when(g == 0)
    def _new_rows():
        pad = jnp.zeros((PAGE_SIZE - t_new, BR_W), F32)
        kn = jnp.concatenate([k_ref[...], pad], axis=0).astype(BF16)
        vn = jnp.concatenate([v_ref[...], pad], axis=0).astype(BF16)
        smp = jnp.concatenate([sm_ref[:, 0:128], jnp.zeros((PAGE_SIZE - t_new, 128), F32)], axis=0)
        lane128 = _iota((1, 128), 1)
        lf = jnp.where((lane128 >= SM_FLF) & (lane128 < SM_FLF + HEADS), _log_sigmoid(smp), 0.0)
        tril = (_iota((PAGE_SIZE, PAGE_SIZE), 0) >= _iota((PAGE_SIZE, PAGE_SIZE), 1)).astype(F32)
        cum_t = _transpose(_sel(lf, tril, "mx"))
        t_of_row = _iota((rows, PAGE_SIZE), 0) & (t_new - 1)
        s_of_col = _iota((rows, PAGE_SIZE), 1)
        sc = _dot_nt(qbd, kn) - per_head_rows(cum_t[SM_FLF:SM_FLF + HEADS, :])
        sc = jnp.where(s_of_col <= t_of_row, sc, NEG)
        m_s[...] = jnp.full_like(m_s, NEG)
        l_s[...] = jnp.zeros_like(l_s)
        carry_s[...] = jnp.zeros_like(carry_s)
        _, p = softmax_step(sc)
        acc_s[...] = jnp.dot(p, vn, preferred_element_type=F32)

    strict = (_iota((PAGE_SIZE, PAGE_SIZE), 0) > _iota((PAGE_SIZE, PAGE_SIZE), 1)).astype(F32)
    strict_ones = jnp.concatenate([strict, jnp.ones((PAGE_SIZE, PAGE_SIZE), F32)], axis=1)
    later = carry_s[...]
    sufs = [None] * n_slots
    for slot in reversed(range(n_slots)):
        r = _sel(lf_pages[slot][...], strict_ones, "xm")
        sufs[slot] = r[:, 0:PAGE_SIZE] + later
        later = later + r[:, PAGE_SIZE:2 * PAGE_SIZE]
    carry_s[...] = later
    bias = per_head_rows(jnp.concatenate(sufs, axis=1))
    sc = jnp.concatenate([jnp.dot(qbd, kt_pages[slot][...].astype(BF16), preferred_element_type=F32)
                          for slot in range(n_slots)], axis=1) + bias
    alpha, p = softmax_step(sc)
    pv = _dot_nt(p[:, 0:PAGE_SIZE], vt_pages[0][...])
    for slot in range(1, n_slots):
        pv += _dot_nt(p[:, slot * PAGE_SIZE:(slot + 1) * PAGE_SIZE], vt_pages[slot][...])
    acc_s[...] = alpha * acc_s[...] + pv

    @pl.when(g == n_g - 1)
    def _fin():
        res = acc_s[...] / l_s[...]
        out = jnp.zeros((t_new, BR_W), F32)
        for h in range(HEADS):
            out = jnp.where(masks[h], res[h * t_new:(h + 1) * t_new, :], out)
        o_ref[...] = out


def _fox_decode(layer, proj, cache_kt, cache_vt, cache_lf_t, page_table, n_slots):
    db, t_new, _ = proj.shape
    n_pages = page_table.shape[1]
    n_g = n_pages // n_slots
    cq = P_FX // BR_W

    def page_idx(slot):
        return lambda b, g, pt: (layer, pt[b * n_pages + (n_g - 1 - g) * n_slots + slot], 0, 0)

    kv_specs = [pl.BlockSpec((None, None, BR_W, PAGE_SIZE), page_idx(s)) for s in range(n_slots)]
    lf_specs = [pl.BlockSpec((None, None, HEADS, PAGE_SIZE), page_idx(s)) for s in range(n_slots)]
    rows = HEADS * t_new
    grid_spec = pltpu.PrefetchScalarGridSpec(
        num_scalar_prefetch=1,
        grid=(db, n_g),
        in_specs=[pl.BlockSpec((None, t_new, BR_W), lambda b, g, pt: (b, 0, cq)),
                  pl.BlockSpec((None, t_new, BR_W), lambda b, g, pt: (b, 0, cq + 1)),
                  pl.BlockSpec((None, t_new, BR_W), lambda b, g, pt: (b, 0, cq + 2)),
                  pl.BlockSpec((None, t_new, BR_W), lambda b, g, pt: (b, 0, P_SM // BR_W))]
                 + kv_specs + kv_specs + lf_specs,
        out_specs=pl.BlockSpec((None, t_new, BR_W), lambda b, g, pt: (b, 0, 0)),
        scratch_shapes=[pltpu.VMEM((rows, 1), F32), pltpu.VMEM((rows, 1), F32), pltpu.VMEM((rows, BR_W), F32),
                        pltpu.VMEM((HEADS, PAGE_SIZE), F32)],
    )
    return pl.pallas_call(
        functools.partial(_fox_decode_kernel, t_new, n_slots),
        grid_spec=grid_spec,
        out_shape=jax.ShapeDtypeStruct((db, t_new, BR_W), F32),
        compiler_params=_cparams(("arbitrary", "arbitrary")),
        name="fox_decode",
    )(page_table.reshape(-1), proj, proj, proj, proj,
      *([cache_kt] * n_slots), *([cache_vt] * n_slots), *([cache_lf_t] * n_slots))


def _merge_kernel(br_ref, fx_ref, g0_ref, g1_ref, g2_ref, g3_ref, x_ref, mod_ref, wb_ref, wo_ref, nw_ref,
                  x1_ref, h2_ref):
    merged = _sigmoid(g0_ref[...]) * _dot(br_ref[:, 0:BR_W], wb_ref[0])
    merged += _sigmoid(g1_ref[...]) * _dot(br_ref[:, BR_W:2 * BR_W], wb_ref[1])
    merged += _sigmoid(g2_ref[...]) * _dot(br_ref[:, 2 * BR_W:3 * BR_W], wb_ref[2])
    merged += _sigmoid(g3_ref[...]) * _dot(fx_ref[...], wb_ref[3])
    x1 = x_ref[...] + mod_ref[:, 2 * D_MODEL:3 * D_MODEL] * _dot(merged, wo_ref[...])
    x1_ref[...] = x1
    h2 = _rms(x1) * nw_ref[...]
    h2_ref[...] = h2 * (1.0 + mod_ref[:, 4 * D_MODEL:5 * D_MODEL]) + mod_ref[:, 3 * D_MODEL:4 * D_MODEL]


def _merge(br3, fox, proj, x, mod, w_branch_b, w_out_b, norm_w):
    t = x.shape[0]
    tm = min(t, 256)
    per_row = mod.shape[0] != 1
    mod_spec = (pl.BlockSpec((tm, 6 * D_MODEL), lambda i: (i, 0)) if per_row
                else pl.BlockSpec((1, 6 * D_MODEL), lambda i: (0, 0)))
    gate = lambda b: pl.BlockSpec((tm, D_MODEL), lambda i, b=b: (i, P_GT // D_MODEL + b))
    return pl.pallas_call(
        _merge_kernel,
        grid=(t // tm,),
        in_specs=[pl.BlockSpec((tm, 3 * BR_W), lambda i: (i, 0)),
                  pl.BlockSpec((tm, BR_W), lambda i: (i, 0)),
                  gate(0), gate(1), gate(2), gate(3),
                  pl.BlockSpec((tm, D_MODEL), lambda i: (i, 0)),
                  mod_spec,
                  pl.BlockSpec((N_BRANCH, BR_W, D_MODEL), lambda i: (0, 0, 0)),
                  pl.BlockSpec((D_MODEL, D_MODEL), lambda i: (0, 0)),
                  pl.BlockSpec((1, D_MODEL), lambda i: (0, 0))],
        out_specs=[pl.BlockSpec((tm, D_MODEL), lambda i: (i, 0)),
                   pl.BlockSpec((tm, D_MODEL), lambda i: (i, 0))],
        out_shape=[jax.ShapeDtypeStruct((t, D_MODEL), F32), jax.ShapeDtypeStruct((t, D_MODEL), F32)],
        compiler_params=_cparams(("arbitrary",)),
        name="merge",
    )(br3, fox, proj, proj, proj, proj, x, mod, w_branch_b, w_out_b, norm_w)


def _select_experts(h, wr_t, rb_col):
    tm = h.shape[0]
    logits = _dot_nt(wr_t, h)
    s = _sigmoid(logits)
    sel = s + rb_col
    ninf = -jnp.inf
    sub = _iota((GROUP_SIZE, tm), 0)
    gsc = []
    for g in range(N_GROUPS):
        blk = sel[g * GROUP_SIZE:(g + 1) * GROUP_SIZE, :]
        m1 = jnp.max(blk, axis=0, keepdims=True)
        first = jnp.min(jnp.where(blk == m1, sub, GROUP_SIZE), axis=0, keepdims=True)
        m2 = jnp.max(jnp.where(sub == first, ninf, blk), axis=0, keepdims=True)
        gsc.append(m1 + m2)
    chosen = [jnp.zeros((1, tm), jnp.bool_) for _ in range(N_GROUPS)]
    for _ in range(TOPK_GROUPS):
        mx = gsc[0]
        for g in range(1, N_GROUPS):
            mx = jnp.maximum(mx, gsc[g])
        taken = jnp.zeros((1, tm), jnp.bool_)
        for g in range(N_GROUPS):
            pick = (gsc[g] == mx) & jnp.logical_not(taken)
            taken = taken | pick
            chosen[g] = chosen[g] | pick
            gsc[g] = jnp.where(pick, ninf, gsc[g])
    selm = jnp.concatenate(
        [jnp.where(chosen[g], sel[g * GROUP_SIZE:(g + 1) * GROUP_SIZE, :], ninf) for g in range(N_GROUPS)], axis=0)
    eidx = _iota((N_EXPERTS, tm), 0)
    firsts, picks = [], []
    for _ in range(TOP_K):
        mx = jnp.max(selm, axis=0, keepdims=True)
        first = jnp.min(jnp.where(selm == mx, eidx, N_EXPERTS), axis=0, keepdims=True)
        pick = eidx == first
        selm = jnp.where(pick, ninf, selm)
        firsts.append(first)
        picks.append(pick)
    return s, firsts, picks


def _router_kernel(h_ref, wr_ref, rb_ref, wt_ref):
    s, _, picks = _select_experts(h_ref[...], wr_ref[...], rb_ref[...])
    picked = picks[0]
    for pick in picks[1:]:
        picked = picked | pick
    w = jnp.where(picked, s, 0.0)
    wt_ref[...] = w / jnp.sum(w, axis=0, keepdims=True) * ROUTED_SCALE


def _router(h2, w_router_t, router_bias_col):
    t = h2.shape[0]
    tm = min(t, 512)
    return pl.pallas_call(
        _router_kernel,
        grid=(t // tm,),
        in_specs=[pl.BlockSpec((tm, D_MODEL), lambda i: (i, 0)),
                  pl.BlockSpec((N_EXPERTS, D_MODEL), lambda i: (0, 0)),
                  pl.BlockSpec((N_EXPERTS, 1), lambda i: (0, 0))],
        out_specs=pl.BlockSpec((N_EXPERTS, tm), lambda i: (0, i)),
        out_shape=jax.ShapeDtypeStruct((N_EXPERTS, t), F32),
        compiler_params=_cparams(("arbitrary",)),
        name="router",
    )(h2, w_router_t, router_bias_col)


def _moe_kernel(final, h_ref, wt_ref, wgu_ref, wd_ref, wsgu_ref, wsd_ref, x1_ref, mod_ref, nf_ref, o_ref,
                acc_s, hb_s, wtok_s):
    e = pl.program_id(1)
    n_e = pl.num_programs(1)

    def swiglu(gu):
        return _silu(gu[:, 0:D_EXPERT]) * gu[:, D_EXPERT:2 * D_EXPERT]

    @pl.when(e == 0)
    def _init():
        hb = h_ref[...].astype(BF16)
        hb_s[...] = hb
        wtok_s[...] = _transpose(wt_ref[...])
        acc_s[...] = _dot(swiglu(jnp.dot(hb, wsgu_ref[...], preferred_element_type=F32)), wsd_ref[...])

    wcol = jnp.sum(jnp.where(_iota((1, N_EXPERTS), 1) == e, wtok_s[...], 0.0), axis=1, keepdims=True)
    a = swiglu(_dot(hb_s[...], wgu_ref[...]))
    acc_s[...] += _dot(a, wd_ref[...]) * wcol

    @pl.when(e == n_e - 1)
    def _fin():
        x2 = x1_ref[...] + mod_ref[:, 5 * D_MODEL:6 * D_MODEL] * acc_s[...]
        if final:
            x2 = _rms(x2) * nf_ref[...]
        o_ref[...] = x2


def _moe(layer, h2, wt, wgu, wd, wsgu_b, wsd_b, x1, mod, norm_f, final):
    t = h2.shape[0]
    tm = min(t, 1024)
    per_row = mod.shape[0] != 1
    mod_spec = (pl.BlockSpec((tm, 6 * D_MODEL), lambda i, e: (i, 0)) if per_row
                else pl.BlockSpec((1, 6 * D_MODEL), lambda i, e: (0, 0)))
    return pl.pallas_call(
        functools.partial(_moe_kernel, final),
        grid=(t // tm, N_EXPERTS),
        in_specs=[pl.BlockSpec((tm, D_MODEL), lambda i, e: (i, 0)),
                  pl.BlockSpec((N_EXPERTS, tm), lambda i, e: (0, i)),
                  pl.BlockSpec((None, None, D_MODEL, 2 * D_EXPERT), lambda i, e: (layer, e, 0, 0)),
                  pl.BlockSpec((None, None, D_EXPERT, D_MODEL), lambda i, e: (layer, e, 0, 0)),
                  pl.BlockSpec((D_MODEL, 2 * D_EXPERT), lambda i, e: (0, 0)),
                  pl.BlockSpec((D_EXPERT, D_MODEL), lambda i, e: (0, 0)),
                  pl.BlockSpec((tm, D_MODEL), lambda i, e: (i, 0)),
                  mod_spec,
                  pl.BlockSpec((1, D_MODEL), lambda i, e: (0, 0))],
        out_specs=pl.BlockSpec((tm, D_MODEL), lambda i, e: (i, 0)),
        out_shape=jax.ShapeDtypeStruct((t, D_MODEL), F32),
        scratch_shapes=[pltpu.VMEM((tm, D_MODEL), F32), pltpu.VMEM((tm, D_MODEL), BF16),
                        pltpu.VMEM((tm, N_EXPERTS), F32)],
        compiler_params=_cparams(("arbitrary", "arbitrary")),
        name="moe",
    )(h2, wt, wgu, wd, wsgu_b, wsd_b, x1, mod, norm_f)


MOE_BM = 512
MOE_SPARSE_MIN_TOKENS = 1024


def _router_sparse_kernel(h_ref, wr_ref, rb_ref, eidx_ref, rank_ref, ew_ref, cnt_ref, carry_s):
    tm = h_ref.shape[0]

    @pl.when(pl.program_id(0) == 0)
    def _():
        carry_s[...] = jnp.zeros_like(carry_s)

    s, firsts, picks = _select_experts(h_ref[...], wr_ref[...], rb_ref[...])
    picked = picks[0]
    for pick in picks[1:]:
        picked = picked | pick
    onehot = picked.astype(F32)
    earlier = (_iota((tm, tm), 0) < _iota((tm, tm), 1)).astype(F32)
    carry = carry_s[...]
    rank = _dot(onehot, earlier) + carry[:, 0:1]
    carry_s[...] = carry + jnp.sum(onehot, axis=1, keepdims=True)
    cnt_ref[...] = carry_s[...]

    w = [jnp.sum(jnp.where(pick, s, 0.0), axis=0, keepdims=True) for pick in picks]
    wsum = w[0]
    for wk in w[1:]:
        wsum = wsum + wk
    row8 = _iota((8, tm), 0)
    eidx8 = jnp.zeros((8, tm), jnp.int32)
    rank8 = jnp.zeros((8, tm), jnp.int32)
    ew8 = jnp.zeros((8, tm), F32)
    for k in range(TOP_K):
        rk = jnp.sum(jnp.where(picks[k], rank, 0.0), axis=0, keepdims=True)
        eidx8 = jnp.where(row8 == k, firsts[k], eidx8)
        rank8 = jnp.where(row8 == k, rk.astype(jnp.int32), rank8)
        ew8 = jnp.where(row8 == k, w[k] / wsum * ROUTED_SCALE, ew8)
    eidx_ref[...] = eidx8
    rank_ref[...] = rank8
    ew_ref[...] = ew8


def _router_sparse(h2, w_router_t, router_bias_col):
    t = h2.shape[0]
    tm = 512
    row = lambda dt: jax.ShapeDtypeStruct((8, t), dt)
    return pl.pallas_call(
        _router_sparse_kernel,
        grid=(t // tm,),
        in_specs=[pl.BlockSpec((tm, D_MODEL), lambda i: (i, 0)),
                  pl.BlockSpec((N_EXPERTS, D_MODEL), lambda i: (0, 0)),
                  pl.BlockSpec((N_EXPERTS, 1), lambda i: (0, 0))],
        out_specs=[pl.BlockSpec((8, tm), lambda i: (0, i)), pl.BlockSpec((8, tm), lambda i: (0, i)),
                   pl.BlockSpec((8, tm), lambda i: (0, i)), pl.BlockSpec((N_EXPERTS, 128), lambda i: (0, 0))],
        out_shape=[row(jnp.int32), row(jnp.int32), row(F32), jax.ShapeDtypeStruct((N_EXPERTS, 128), F32)],
        scratch_shapes=[pltpu.VMEM((N_EXPERTS, 128), F32)],
        compiler_params=_cparams(("arbitrary",)),
        name="router_sparse",
    )(h2, w_router_t, router_bias_col)


def _plan_kernel(nblk_pad, cnt_ref, eidx_ref, rank_ref, dest_ref, blk_ref):
    tm = eidx_ref.shape[1]
    cnt = cnt_ref[...]
    padded = jnp.floor((cnt + (MOE_BM - 1.0)) * (1.0 / MOE_BM)) * MOE_BM
    tril = (_iota((N_EXPERTS, N_EXPERTS), 0) >= _iota((N_EXPERTS, N_EXPERTS), 1)).astype(F32)
    pad_end = _sel(padded, tril, "mx")
    start_col = (pad_end - padded)[:, 0:1]
    end_col = pad_end[:, 0:1]

    e_iota = _iota((N_EXPERTS, tm), 0)
    row8 = _iota((8, tm), 0)
    eidx = eidx_ref[...]
    rank = rank_ref[...]
    dest = jnp.zeros((8, tm), jnp.int32)
    for k in range(TOP_K):
        base = jnp.sum(jnp.where(e_iota == eidx[k:k + 1, :], start_col, 0.0), axis=0, keepdims=True)
        dest = jnp.where(row8 == k, base.astype(jnp.int32) + rank[k:k + 1, :], dest)
    dest_ref[...] = dest

    first_row = (_iota((N_EXPERTS, nblk_pad), 1) * MOE_BM).astype(F32)
    blk_e = jnp.sum((end_col <= first_row).astype(F32), axis=0, keepdims=True)
    blk_e = jnp.minimum(blk_e, N_EXPERTS - 1.0).astype(jnp.int32)
    n_used = (pad_end[N_EXPERTS - 1:N_EXPERTS, 0:1] * (1.0 / MOE_BM)).astype(jnp.int32)
    row8b = _iota((8, nblk_pad), 0)
    blk_ref[...] = jnp.where(row8b == 0, blk_e, jnp.where(row8b == 1, n_used, 0))


def _plan(cnt, eidx, rank, nblk_pad):
    t = eidx.shape[1]
    tm = min(t, 2048)
    return pl.pallas_call(
        functools.partial(_plan_kernel, nblk_pad),
        grid=(t // tm,),
        in_specs=[pl.BlockSpec((N_EXPERTS, 128), lambda i: (0, 0)),
                  pl.BlockSpec((8, tm), lambda i: (0, i)), pl.BlockSpec((8, tm), lambda i: (0, i))],
        out_specs=[pl.BlockSpec((8, tm), lambda i: (0, i)), pl.BlockSpec((8, nblk_pad), lambda i: (0, 0))],
        out_shape=[jax.ShapeDtypeStruct((8, t), jnp.int32), jax.ShapeDtypeStruct((8, nblk_pad), jnp.int32)],
        compiler_params=_cparams(("arbitrary",)),
        name="moe_plan",
    )(cnt, eidx, rank)


def _sc_mesh():
    return plsc.VectorSubcoreMesh(core_axis_name="core", subcore_axis_name="subcore")


SC_CHUNK = 256
SC_WINDOW = 128
N_CHUNK = D_MODEL // SC_CHUNK


def _sc_scatter_rows(x, idx, n_rows):
    t = x.shape[0]
    flat_idx = [i for per_chunk in idx for i in per_chunk]

    @pl.kernel(out_type=jax.ShapeDtypeStruct((N_CHUNK * n_rows, SC_CHUNK), x.dtype), mesh=_sc_mesh(),
               scratch_types=[])
    def scatter_kernel(x_hbm, *rest):
        i_hbm, o_hbm = rest[:-1], rest[-1]

        def body(x_vmem, *i_vmem):
            for iv in i_vmem:
                pltpu.sync_copy(x_vmem, o_hbm.at[iv.at[0]])

        for c in range(N_CHUNK):
            pltpu.emit_pipeline(
                body,
                grid=(t // SC_WINDOW,),
                in_specs=[pl.BlockSpec((SC_WINDOW, SC_CHUNK), lambda i, c=c: (i, c))]
                         + [pl.BlockSpec((1, SC_WINDOW), lambda i: (0, i))] * TOP_K,
                out_specs=[],
                core_axis_name=("core", "subcore"),
                dimension_semantics=(pltpu.PARALLEL,),
            )(x_hbm, *i_hbm[c * TOP_K:(c + 1) * TOP_K])

    return scatter_kernel(x, *flat_idx)


def _sc_gather_rows(y, idx):
    a = idx[0].shape[1]
    n_win = a // SC_WINDOW

    @pl.kernel(out_type=jax.ShapeDtypeStruct((N_CHUNK * a, SC_CHUNK), y.dtype), mesh=_sc_mesh(), scratch_types=[])
    def gather_kernel(y_hbm, *rest):
        i_hbm, o_hbm = rest[:-1], rest[-1]

        def body(i_vmem, o_vmem):
            pltpu.sync_copy(y_hbm.at[i_vmem.at[0]], o_vmem)

        for c in range(N_CHUNK):
            pltpu.emit_pipeline(
                body,
                grid=(n_win,),
                in_specs=[pl.BlockSpec((1, SC_WINDOW), lambda i: (0, i))],
                out_specs=[pl.BlockSpec((SC_WINDOW, SC_CHUNK), lambda i, c=c: (c * n_win + i, 0))],
                core_axis_name=("core", "subcore"),
                dimension_semantics=(pltpu.PARALLEL,),
            )(i_hbm[c], o_hbm)

    return gather_kernel(y, *idx)


def _swiglu(gu):
    return _silu(gu[:, 0:D_EXPERT]) * gu[:, D_EXPERT:2 * D_EXPERT]


def _grouped_kernel(blk_e_ref, n_used_ref, xs_ref, wgu_ref, wd_ref, ys_ref):
    @pl.when(pl.program_id(0) < n_used_ref[0])
    def _():
        gu = _dot(xs_ref[0], wgu_ref[0:SC_CHUNK, :])
        for c in range(1, N_CHUNK):
            gu += _dot(xs_ref[c], wgu_ref[c * SC_CHUNK:(c + 1) * SC_CHUNK, :])
        y = _dot(_swiglu(gu), wd_ref[...])
        for c in range(N_CHUNK):
            ys_ref[c] = y[:, c * SC_CHUNK:(c + 1) * SC_CHUNK]


def _grouped(layer, xs, blk_e, n_used, wgu, wd):
    n_rows = xs.shape[1]
    clamp = lambda b, nu: jnp.minimum(b, nu[0] - 1)
    rows_spec = pl.BlockSpec((N_CHUNK, MOE_BM, SC_CHUNK), lambda b, be, nu: (0, clamp(b, nu), 0))
    grid_spec = pltpu.PrefetchScalarGridSpec(
        num_scalar_prefetch=2,
        grid=(n_rows // MOE_BM,),
        in_specs=[rows_spec,
                  pl.BlockSpec((None, None, D_MODEL, 2 * D_EXPERT),
                               lambda b, be, nu: (layer, be[clamp(b, nu)], 0, 0)),
                  pl.BlockSpec((None, None, D_EXPERT, D_MODEL),
                               lambda b, be, nu: (layer, be[clamp(b, nu)], 0, 0))],
        out_specs=rows_spec,
    )
    return pl.pallas_call(
        _grouped_kernel,
        grid_spec=grid_spec,
        out_shape=jax.ShapeDtypeStruct((N_CHUNK, n_rows, SC_CHUNK), F32),
        compiler_params=_cparams(("arbitrary",)),
        name="moe_grouped",
    )(blk_e, n_used, xs, wgu, wd)


def _combine_kernel(final, yg_ref, ew_ref, h_ref, wsgu_ref, wsd_ref, x1_ref, mod_ref, nf_ref, o_ref):
    gu = jnp.dot(h_ref[...].astype(BF16), wsgu_ref[...], preferred_element_type=F32)
    acc = _dot(_swiglu(gu), wsd_ref[...])
    wt = _transpose(ew_ref[...])
    routed = []
    for c in range(N_CHUNK):
        part = wt[:, 0:1] * yg_ref[c, 0]
        for k in range(1, TOP_K):
            part = part + wt[:, k:k + 1] * yg_ref[c, k]
        routed.append(part)
    acc = acc + jnp.concatenate(routed, axis=1)
    x2 = x1_ref[...] + mod_ref[:, 5 * D_MODEL:6 * D_MODEL] * acc
    if final:
        x2 = _rms(x2) * nf_ref[...]
    o_ref[...] = x2


def _combine(yg, ew, h2, wsgu_b, wsd_b, x1, mod, norm_f, final):
    t = h2.shape[0]
    tm = 256
    per_row = mod.shape[0] != 1
    mod_spec = (pl.BlockSpec((tm, 6 * D_MODEL), lambda i: (i, 0)) if per_row
                else pl.BlockSpec((1, 6 * D_MODEL), lambda i: (0, 0)))
    return pl.pallas_call(
        functools.partial(_combine_kernel, final),
        grid=(t // tm,),
        in_specs=[pl.BlockSpec((N_CHUNK, TOP_K, tm, SC_CHUNK), lambda i: (0, 0, i, 0)),
                  pl.BlockSpec((8, tm), lambda i: (0, i)),
                  pl.BlockSpec((tm, D_MODEL), lambda i: (i, 0)),
                  pl.BlockSpec((D_MODEL, 2 * D_EXPERT), lambda i: (0, 0)),
                  pl.BlockSpec((D_EXPERT, D_MODEL), lambda i: (0, 0)),
                  pl.BlockSpec((tm, D_MODEL), lambda i: (i, 0)),
                  mod_spec,
                  pl.BlockSpec((1, D_MODEL), lambda i: (0, 0))],
        out_specs=pl.BlockSpec((tm, D_MODEL), lambda i: (i, 0)),
        out_shape=jax.ShapeDtypeStruct((t, D_MODEL), F32),
        compiler_params=_cparams(("arbitrary",)),
        name="moe_combine",
    )(yg, ew, h2, wsgu_b, wsd_b, x1, mod, norm_f)


def _moe_sparse(layer, h2, w_router_t, router_bias_col, wgu, wd, wsgu_b, wsd_b, x1, mod, norm_f, final,
                overlap=None):
    t = h2.shape[0]
    n_blk = t * TOP_K // MOE_BM + N_EXPERTS
    n_rows = n_blk * MOE_BM
    nblk_pad = -(-n_blk // 128) * 128
    eidx, rank, ew, cnt = _router_sparse(h2, w_router_t, router_bias_col)
    dest, blk = _plan(cnt, eidx, rank, nblk_pad)
    per_pick = [[dest[k:k + 1] + c * n_rows for k in range(TOP_K)] for c in range(N_CHUNK)]
    flat = dest[0:TOP_K].reshape(1, TOP_K * t)
    xs = _sc_scatter_rows(h2, per_pick, n_rows).reshape(N_CHUNK, n_rows, SC_CHUNK)
    if overlap is not None:
        xs = overlap(xs)
    ys = _grouped(layer, xs, blk[0], blk[1, 0:1], wgu, wd)
    yg = _sc_gather_rows(ys.reshape(N_CHUNK * n_rows, SC_CHUNK), [flat + c * n_rows for c in range(N_CHUNK)])
    if overlap is not None:
        yg = overlap(yg)
    return _combine(yg.reshape(N_CHUNK, TOP_K, t, SC_CHUNK), ew, h2, wsgu_b, wsd_b, x1, mod, norm_f, final)


def _permute_in_cols(w):
    ml = w[..., 0:ML_COLS]
    fx = w[..., OFF_FX:OFF_GT]
    small = jnp.concatenate([ml[..., 4 * BR_W:], fx[..., 3 * BR_W:]], axis=-1)
    pad = jnp.zeros(w.shape[:-1] + (P_GT - P_SM - small.shape[-1],), w.dtype)
    return jnp.concatenate([ml[..., :4 * BR_W], w[..., OFF_HG:OFF_RT], w[..., OFF_RT:OFF_FX], fx[..., :3 * BR_W],
                            small, pad, w[..., OFF_GT:]], axis=-1)


def _rope_tables(pos):
    half = HEAD_DIM // 2
    inv = ROPE_BASE ** (-jnp.arange(half, dtype=F32) / half)
    ang = pos.astype(F32)[:, None] * inv[None, :]
    cos = jnp.cos(ang)
    sin = jnp.sin(ang)
    cos_h = jnp.concatenate([cos, cos], axis=-1)
    sin_h = jnp.concatenate([-sin, sin], axis=-1)
    return jnp.tile(cos_h, (1, HEADS)), jnp.tile(sin_h, (1, HEADS))


def kernel(x_prompt, x_sample, cache_fox_k, cache_fox_v, cache_fox_logf, state_mlstm_C, state_mlstm_n, state_mlstm_m, state_hgrn_S, state_ret_S, page_table, c_prompt, c_sample, w_ada, b_ada, norm_mix_w, norm_ffn_w, w_in, b_in, hgrn_lb_logits, mlstm_norm_w, hgrn_norm_w, ret_norm_w, w_branch, w_out, w_router, router_bias, w_exp_gu, w_exp_down, w_shared_gu, w_shared_down, norm_f_w):
    depth = w_in.shape[0]
    bp, seq, _ = x_prompt.shape
    db, t_new, _ = x_sample.shape
    n_pool = cache_fox_k.shape[1]
    n_pages = page_table.shape[1]
    past_len = n_pages * PAGE_SIZE
    assert bp == 1 and seq % 128 == 0 and t_new == 8

    w_in_p = _permute_in_cols(w_in).astype(BF16)
    b_in_p = _permute_in_cols(b_in).reshape(depth, 1, P_TOT)
    w_branch_b = w_branch.astype(BF16)
    w_out_b = w_out.astype(BF16)
    wsgu_b = w_shared_gu.astype(BF16)
    wsd_b = w_shared_down.astype(BF16)
    w_router_t = jnp.swapaxes(w_router, 1, 2)
    cache_k = jnp.transpose(cache_fox_k, (0, 1, 3, 4, 2)).reshape(depth, n_pool, BR_W, PAGE_SIZE)
    cache_v = jnp.transpose(cache_fox_v, (0, 1, 3, 4, 2)).reshape(depth, n_pool, BR_W, PAGE_SIZE)
    cache_lf_t = jnp.swapaxes(cache_fox_logf, 2, 3)

    mods = _ada(jnp.concatenate([c_prompt, c_sample], axis=0), w_ada, b_ada)

    cos_p, sin_p = _rope_tables(jnp.arange(seq))
    cos_s, sin_s = _rope_tables(past_len + jnp.arange(t_new))

    def trunk(result, x, mod_of_layer, cos_t, sin_t, init, fox_fn, lc, overlap=None):
        b, l, _ = x.shape
        xt = x.reshape(b * l, D_MODEL)
        c_in, n_in, m_in, sh_in, sr_in = init
        per_layer = []
        for layer in range(depth):
            mod = mod_of_layer(layer)
            kv_t = b == 1 and (b * l) % 128 == 0
            proj, *kv = _inproj(layer, xt, mod, norm_mix_w[layer][None], w_in_p, b_in_p, kv_t)
            proj3 = proj.reshape(b, l, P_TOT)
            if kv_t:
                fox_k, fox_v = (a.reshape(HEADS, HEAD_DIM, l).transpose(2, 0, 1)[None] for a in kv)
            else:
                fox_k, fox_v = (proj3[..., P_FX + i * BR_W:P_FX + (i + 1) * BR_W].reshape(b, l, HEADS, HEAD_DIM)
                                for i in (1, 2))
            br3, lf_rows, cum_t, c_new, n_new, m_new, sh_new, sr_new = _mixers(
                layer, proj3, cos_t, sin_t, c_in[layer], n_in[layer], m_in[layer], sh_in[layer], sr_in[layer],
                hgrn_lb_logits, mlstm_norm_w[layer][None], hgrn_norm_w[layer][None], ret_norm_w[layer][None], lc)
            yield br3
            fox = fox_fn(layer, proj3, cum_t)
            yield fox
            x1, h2 = _merge(br3.reshape(b * l, 3 * BR_W), fox.reshape(b * l, BR_W), proj, xt, mod,
                            w_branch_b[layer], w_out_b[layer], norm_ffn_w[layer][None])
            final = layer == depth - 1
            if b * l >= MOE_SPARSE_MIN_TOKENS:
                xt = _moe_sparse(layer, h2, w_router_t[layer], router_bias[layer][:, None], w_exp_gu, w_exp_down,
                                 wsgu_b[layer], wsd_b[layer], x1, mod, norm_f_w[None], final, overlap)
            else:
                wt = _router(h2, w_router_t[layer], router_bias[layer][:, None])
                xt = _moe(layer, h2, wt, w_exp_gu, w_exp_down, wsgu_b[layer], wsd_b[layer], x1, mod, norm_f_w[None],
                          final)
            yield xt
            per_layer.append((
                fox_k,
                fox_v,
                lf_rows[..., SM_FLF:SM_FLF + HEADS],
                c_new.reshape(b, HEADS, HEAD_DIM, HEAD_DIM),
                n_new.reshape(b, HEADS, HEAD_DIM),
                m_new[:, 0, :HEADS],
                sh_new.reshape(b, HEADS, HEAD_DIM, HEAD_DIM),
                sr_new.reshape(b, HEADS, HEAD_DIM, HEAD_DIM)))
        stacked = tuple(jnp.stack([p[i] for p in per_layer]) for i in range(8))
        result.append((xt.reshape(b, l, D_MODEL),) + stacked)

    def head_major(s, b):
        return s.astype(F32).reshape(depth, b, BR_W, HEAD_DIM)

    zero_state = jnp.zeros((depth, bp, BR_W, HEAD_DIM), F32)
    prompt_init = (zero_state, jnp.zeros((depth, bp, 1, BR_W), F32), jnp.zeros((depth, bp, 1, 128), F32),
                   zero_state, zero_state)
    m_pad = jnp.pad(state_mlstm_m.astype(F32), ((0, 0), (0, 0), (0, 128 - HEADS))).reshape(depth, db, 1, 128)
    sample_init = (head_major(state_mlstm_C, db), state_mlstm_n.astype(F32).reshape(depth, db, 1, BR_W), m_pad,
                   head_major(state_hgrn_S, db), head_major(state_ret_S, db))
    n_slots = next(n for n in (32, 16, 8, 4, 2, 1) if n_pages % n == 0)
    prompt_out, sample_out = [], []
    sample = trunk(sample_out, x_sample, lambda layer: jnp.repeat(mods[layer, 1:], t_new, axis=0), cos_s, sin_s,
                   sample_init,
                   lambda layer, proj3, cum_t: _fox_decode(layer, proj3, cache_k, cache_v, cache_lf_t,
                                                           page_table, n_slots),
                   t_new)
    stages_per_copy = iter((1, 1, 2, 2))

    def overlap(arr):
        token = None
        for _ in range(next(stages_per_copy, 0)):
            token = next(sample, token)
        if token is None:
            return arr
        return lax.optimization_barrier((arr, token))[0]

    prompt = trunk(prompt_out, x_prompt, lambda layer: mods[layer, 0:1], cos_p, sin_p, prompt_init,
                   lambda layer, proj3, cum_t: _fox_prompt(proj3[0], cum_t[0], min(seq, 512)),
                   128, overlap)
    for gen in (prompt, sample):
        for _ in gen:
            pass
    prompt_out, sample_out = prompt_out[0], sample_out[0]

    return (prompt_out[0], sample_out[0]) + prompt_out[1:] + sample_out[1:]
```

```python
import functools
import math

import numpy as np
import jax
import jax.numpy as jnp
from jax import lax
from jax.experimental import pallas as pl
from jax.experimental.pallas import tpu as pltpu
from jax.experimental.pallas import tpu_sc as plsc

F32 = jnp.float32
BF16 = jnp.bfloat16
HIGHEST = lax.Precision.HIGHEST

D_MODEL = 1024
N_BRANCH = 4
BR_W = 256
HEAD_DIM = 64
HEADS = 4
ROPE_BASE = 10000.0
RMS_EPS = 1e-6
N_EXPERTS = 64
TOP_K = 6
N_GROUPS = 8
GROUP_SIZE = N_EXPERTS // N_GROUPS
TOPK_GROUPS = 4
D_EXPERT = 256
ROUTED_SCALE = 2.5
PAGE_SIZE = 128
QK_SCALE = HEAD_DIM ** -0.5
LOG2E = math.log2(math.e)

ML_COLS = 4 * BR_W + 2 * HEADS
OFF_HG = ML_COLS
OFF_RT = OFF_HG + 4 * BR_W
OFF_FX = OFF_RT + 4 * BR_W
OFF_GT = OFF_FX + 3 * BR_W + HEADS
N_IN = OFF_GT + N_BRANCH * D_MODEL

P_ML, P_HG, P_RT, P_FX, P_SM, P_GT, P_TOT = 0, 1024, 2048, 3072, 3840, 4096, 8192
SM_IG, SM_MLF, SM_FLF = 0, 4, 8

NEG = -1e30
VMEM_LIMIT = 56 * 1024 * 1024


def _cparams(sem):
    return pltpu.CompilerParams(dimension_semantics=sem, vmem_limit_bytes=VMEM_LIMIT)


def _dot(a, b):
    return jnp.dot(a.astype(BF16), b.astype(BF16), preferred_element_type=F32)


def _dot_nt(a, b):
    return lax.dot_general(a.astype(BF16), b.astype(BF16), (((1,), (1,)), ((), ())), preferred_element_type=F32)


def _dot_tn(a, b):
    return lax.dot_general(a.astype(BF16), b.astype(BF16), (((0,), (0,)), ((), ())), preferred_element_type=F32)


def _split3(x):
    x1 = x.astype(BF16)
    r1 = x - x1.astype(F32)
    x2 = r1.astype(BF16)
    x3 = (r1 - x2.astype(F32)).astype(BF16)
    return x1, x2, x3


def _sel(x, m01, dims, terms=3):
    m = m01.astype(BF16)
    x1, x2, x3 = _split3(x)
    if dims == "mx":
        f = lambda xi: jnp.dot(m, xi, preferred_element_type=F32)
    elif dims == "xm":
        f = lambda xi: jnp.dot(xi, m, preferred_element_type=F32)
    elif dims == "xmT":
        f = lambda xi: lax.dot_general(xi, m, (((1,), (1,)), ((), ())), preferred_element_type=F32)
    else:
        f = lambda xi: lax.dot_general(m, xi, (((1,), (1,)), ((), ())), preferred_element_type=F32)
    return (f(x1) + f(x2)) + f(x3) if terms == 3 else f(x1) + f(x2)


def _iota(shape, dim):
    return lax.broadcasted_iota(jnp.int32, shape, dim)


def _eye(n):
    return (_iota((n, n), 0) == _iota((n, n), 1)).astype(F32)


def _transpose(x):
    if x.shape[0] % 128 == 0 and x.shape[1] % 128 == 0:
        return x.T
    return _sel(x, _eye(x.shape[1]), "mxT")


def _sigmoid(x):
    return jax.nn.sigmoid(x)


def _silu(x):
    return x * jax.nn.sigmoid(x)


def _log_sigmoid(x):
    return jnp.minimum(x, 0.0) - jnp.log1p(jnp.exp(-jnp.abs(x)))


def _head_masks(n=BR_W):
    lane = _iota((1, n), 1) >> 6
    return [lane == h for h in range(HEADS)]


def _block_diag_mask():
    return (_iota((BR_W, BR_W), 0) >> 6) == (_iota((BR_W, BR_W), 1) >> 6)


def _per_head_lanes(vals, masks):
    out = jnp.where(masks[0], vals[0], 0.0)
    for h in range(1, HEADS):
        out = jnp.where(masks[h], vals[h], out)
    return out


def _rms(x, eps=RMS_EPS):
    return x * lax.rsqrt(jnp.mean(x * x, axis=-1, keepdims=True) + eps)


def _head_norm(o, gain, bdf):
    ms = _sel(o * o, bdf, "xm", terms=2) * (1.0 / HEAD_DIM)
    return o * lax.rsqrt(ms + RMS_EPS) * gain


def _ada_kernel(c_ref, w_ref, b_ref, o_ref):
    o_ref[...] = _dot(_silu(c_ref[...]), w_ref[...]) + b_ref[...]


def _ada(c_all, w_ada, b_ada):
    depth = w_ada.shape[0]
    n_c = c_all.shape[0]
    tn = 1536
    return pl.pallas_call(
        _ada_kernel,
        grid=(depth, 6 * D_MODEL // tn),
        in_specs=[pl.BlockSpec((n_c, D_MODEL), lambda l, j: (0, 0)),
                  pl.BlockSpec((None, D_MODEL, tn), lambda l, j: (l, 0, j)),
                  pl.BlockSpec((None, 1, tn), lambda l, j: (l, 0, j))],
        out_specs=pl.BlockSpec((None, n_c, tn), lambda l, j: (l, 0, j)),
        out_shape=jax.ShapeDtypeStruct((depth, n_c, 6 * D_MODEL), F32),
        compiler_params=_cparams(("arbitrary", "arbitrary")),
        name="ada",
    )(c_all, w_ada, b_ada.reshape(depth, 1, 6 * D_MODEL))


def _inproj_kernel(kv_t, x_ref, mod_ref, nw_ref, w_ref, b_ref, o_ref, *rest):
    h_scr = rest[-1]
    j = pl.program_id(1)

    @pl.when(j == 0)
    def _():
        h = _rms(x_ref[...]) * nw_ref[...]
        h = h * (1.0 + mod_ref[:, D_MODEL:2 * D_MODEL]) + mod_ref[:, 0:D_MODEL]
        h_scr[...] = h.astype(BF16)

    res = jnp.dot(h_scr[...], w_ref[...], preferred_element_type=F32) + b_ref[...]
    o_ref[...] = res

    if kv_t:
        kt_ref, vt_ref = rest[0], rest[1]
        c0 = P_FX % res.shape[1]

        @pl.when(j == P_FX // res.shape[1])
        def _():
            kt_ref[...] = res[:, c0 + BR_W:c0 + 2 * BR_W].T
            vt_ref[...] = res[:, c0 + 2 * BR_W:c0 + 3 * BR_W].T


def _inproj(layer, x, mod, norm_w, w_p, b_p, kv_t):
    t = x.shape[0]
    tm = min(t, 1024)
    tn = 1024
    per_row = mod.shape[0] != 1
    mod_spec = (pl.BlockSpec((tm, 6 * D_MODEL), lambda i, j: (i, 0)) if per_row
                else pl.BlockSpec((1, 6 * D_MODEL), lambda i, j: (0, 0)))
    out_specs = [pl.BlockSpec((tm, tn), lambda i, j: (i, j))]
    out_shape = [jax.ShapeDtypeStruct((t, P_TOT), F32)]
    if kv_t:
        out_specs += [pl.BlockSpec((BR_W, tm), lambda i, j: (0, i))] * 2
        out_shape += [jax.ShapeDtypeStruct((BR_W, t), F32)] * 2
    return pl.pallas_call(
        functools.partial(_inproj_kernel, kv_t),
        grid=(t // tm, P_TOT // tn),
        in_specs=[pl.BlockSpec((tm, D_MODEL), lambda i, j: (i, 0)),
                  mod_spec,
                  pl.BlockSpec((1, D_MODEL), lambda i, j: (0, 0)),
                  pl.BlockSpec((None, D_MODEL, tn), lambda i, j: (layer, 0, j)),
                  pl.BlockSpec((None, 1, tn), lambda i, j: (layer, 0, j))],
        out_specs=out_specs,
        out_shape=out_shape,
        scratch_shapes=[pltpu.VMEM((tm, D_MODEL), BF16)],
        compiler_params=_cparams(("arbitrary", "arbitrary")),
        name="inproj",
    )(x, mod, norm_w, w_p, b_p)


def _mixers_kernel(layer, lc, sc,
                   ml_ref, hg_ref, rt_ref, sm_ref, cos_ref, sin_ref,
                   c0_ref, n0_ref, m0_ref, sh0_ref, sr0_ref, lbl_ref, gml_ref, ghg_ref, grt_ref,
                   br_ref, lf_ref, cumt_ref, cout_ref, nout_ref, mout_ref, shout_ref, srout_ref,
                   cbd, sht, srbd, n_s, m_s, carry_s, ohg_s):
    c = pl.program_id(1)
    n_c = pl.num_programs(1)
    masks = _head_masks()
    bd = _block_diag_mask()
    bdf = bd.astype(F32)
    tile = ((_iota((HEAD_DIM, BR_W), 1) & (HEAD_DIM - 1)) == _iota((HEAD_DIM, BR_W), 0)).astype(F32)

    @pl.when(c == 0)
    def _init():
        def expand(ref):
            return jnp.where(bd, _sel(ref[...], tile, "xm"), 0.0)
        cbd[...] = expand(c0_ref)
        sht[...] = _transpose(expand(sh0_ref))
        srbd[...] = expand(sr0_ref)
        n_s[...] = n0_ref[...]
        m_s[...] = m0_ref[...]
        carry_s[...] = jnp.zeros_like(carry_s)

    row = _iota((lc, lc), 0)
    col = _iota((lc, lc), 1)
    causal = row >= col
    tril = causal.astype(F32)

    sm = sm_ref[:, 0:128]
    lane128 = _iota((1, 128), 1)
    sm2 = jnp.where((lane128 >= SM_MLF) & (lane128 < SM_FLF + HEADS), _log_sigmoid(sm), sm)
    cum = _sel(sm2, tril, "mx")
    sm2_t = _transpose(sm2)
    cum_t = _transpose(cum)
    lf_ref[...] = sm2
    cum_tg = cum_t + carry_s[...]
    carry_s[...] = cum_tg[:, lc - 1:lc]
    cumt_ref[...] = cum_tg[SM_FLF:SM_FLF + 8, :]

    q = ml_ref[:, 0:BR_W]
    k = ml_ref[:, BR_W:2 * BR_W] * QK_SCALE
    v = ml_ref[:, 2 * BR_W:3 * BR_W]
    og = ml_ref[:, 3 * BR_W:4 * BR_W]
    n_row = n_s[...]
    m_row = m_s[...]
    q_c = _dot_nt(q, cbd[...])
    h_all = jnp.zeros((lc, BR_W), F32)
    w_lanes = jnp.zeros((lc, BR_W), F32)
    decay_lanes = jnp.zeros((1, BR_W), F32)
    m_new_row = jnp.zeros((1, 128), F32)
    for h in range(HEADS):
        ig_c = sm2[:, SM_IG + h:SM_IG + h + 1]
        b_c = cum[:, SM_MLF + h:SM_MLF + h + 1]
        ig_r = sm2_t[SM_IG + h:SM_IG + h + 1, :]
        b_r = cum_t[SM_MLF + h:SM_MLF + h + 1, :]
        m_prev = m_row[:, h:h + 1]
        dmat = jnp.where(causal, b_c - b_r + ig_r, NEG)
        m_inter = b_c + m_prev
        m_t = jnp.maximum(m_inter, jnp.max(dmat, axis=1, keepdims=True))
        w_intra = jnp.exp(dmat - m_t)
        w_inter = jnp.exp(m_inter - m_t)
        qh = jnp.where(masks[h], q, 0.0)
        a = _dot_nt(qh, k) * w_intra
        num = _dot(a, v) + w_inter * q_c
        den = jnp.sum(a, axis=1, keepdims=True) + w_inter * jnp.sum(qh * n_row, axis=1, keepdims=True)
        hh = num / jnp.maximum(jnp.abs(den), jnp.exp(-m_t))
        h_all = jnp.where(masks[h], hh, h_all)
        m_new = m_t[lc - 1:lc, :]
        b_last = b_c[lc - 1:lc, :]
        w_s = jnp.exp(b_last - b_c + ig_c - m_new)
        decay = jnp.exp(b_last + m_prev - m_new)
        w_lanes = jnp.where(masks[h], w_s, w_lanes)
        decay_lanes = jnp.where(masks[h], decay, decay_lanes)
        m_new_row = jnp.where(lane128 == h, m_new, m_new_row)
    kw = k * w_lanes
    cbd[...] = cbd[...] * decay_lanes + jnp.where(bd, _dot_tn(v * w_lanes, k), 0.0)
    n_s[...] = n_row * decay_lanes + jnp.sum(kw, axis=0, keepdims=True)
    m_s[...] = m_new_row
    out_ml = _head_norm(h_all, gml_ref[...], bdf) * _sigmoid(og)

    lbl = lbl_ref[...]
    pr = jnp.exp(lbl - jnp.max(lbl, axis=0, keepdims=True))
    pr = pr / jnp.sum(pr, axis=0, keepdims=True)
    lb = jnp.zeros((1, BR_W), F32)
    for i in range(1, layer + 1):
        lb = lb + pr[i:i + 1, :]
    tril_sc = (_iota((sc, sc), 0) >= _iota((sc, sc), 1)).astype(F32)
    s_idx = _iota((sc, BR_W), 0)

    def hg_body(i, carry):
        r0 = pl.multiple_of(i * sc, sc)
        hq = hg_ref[pl.ds(r0, sc), 0:BR_W]
        hf = hg_ref[pl.ds(r0, sc), BR_W:2 * BR_W]
        vi = hg_ref[pl.ds(r0, sc), 2 * BR_W:3 * BR_W]
        qi = _silu(hq)
        f = lb + (1.0 - lb) * _sigmoid(hf)
        ki = 1.0 - f
        bi = _sel(jnp.log(f), tril_sc, "mx")
        st = sht[...]
        o_inter = _dot_nt(qi * jnp.exp(bi), st)
        rows = []
        for t in range(sc):
            e_t = jnp.exp(jnp.where(s_idx <= t, bi[t:t + 1, :] - bi, NEG)) * qi[t:t + 1, :] * ki
            rows.append(e_t)
        e_all = jnp.concatenate(rows, axis=0)
        r_all = _dot(e_all, bdf)
        o_diag = jnp.sum(r_all.reshape(sc, sc, BR_W) * vi[None, :, :], axis=1)
        ohg_s[pl.ds(r0, sc), :] = o_inter + o_diag
        b_last = bi[sc - 1:sc, :]
        sht[...] = st * jnp.exp(b_last) + jnp.where(bd, _dot_tn(vi, ki * jnp.exp(b_last - bi)), 0.0)
        return carry

    lax.fori_loop(0, lc // sc, hg_body, 0, unroll=True)
    out_hg = _head_norm(ohg_s[...], ghg_ref[...], bdf) * _silu(hg_ref[:, 3 * BR_W:4 * BR_W])

    cosv = cos_ref[...]
    sinv = sin_ref[...]
    lane = _iota((1, BR_W), 1)
    first_half = (lane & (HEAD_DIM - 1)) < (HEAD_DIM // 2)

    def rope(x):
        partner = jnp.where(first_half, pltpu.roll(x, BR_W - HEAD_DIM // 2, 1), pltpu.roll(x, HEAD_DIM // 2, 1))
        return x * cosv + partner * sinv

    rq = rope(rt_ref[:, 0:BR_W])
    rk = rope(rt_ref[:, BR_W:2 * BR_W]) * QK_SCALE
    rv = rt_ref[:, 2 * BR_W:3 * BR_W]
    lg = [math.log1p(-(2.0 ** (-5.0 - h))) for h in range(HEADS)]
    lg_lanes = _per_head_lanes([jnp.full((1, 1), g, F32) for g in lg], masks)
    diff = (row - col).astype(F32)
    o_rt = jnp.zeros((lc, BR_W), F32)
    for h in range(HEADS):
        dec = jnp.exp(jnp.where(causal, diff * lg[h], NEG))
        a = _dot_nt(jnp.where(masks[h], rq, 0.0), rk) * dec
        o_rt = jnp.where(masks[h], _dot(a, rv), o_rt)
    t_idx = _iota((lc, BR_W), 0).astype(F32)
    o_rt = o_rt + _dot(rq * jnp.exp((t_idx + 1.0) * lg_lanes), srbd[...])
    w_ret = jnp.exp((lc - 1.0 - t_idx) * lg_lanes)
    srbd[...] = srbd[...] * jnp.exp(lc * lg_lanes) + jnp.where(bd, _dot_tn(rk * w_ret, rv), 0.0)
    out_rt = _head_norm(o_rt, grt_ref[...], bdf) * _silu(rt_ref[:, 3 * BR_W:4 * BR_W])

    br_ref[:, 0:BR_W] = out_ml
    br_ref[:, BR_W:2 * BR_W] = out_hg
    br_ref[:, 2 * BR_W:3 * BR_W] = out_rt

    @pl.when(c == n_c - 1)
    def _fin():
        def compact(x):
            return _sel(x, tile, "xmT")
        cout_ref[...] = compact(cbd[...])
        shout_ref[...] = compact(_transpose(sht[...]))
        srout_ref[...] = compact(srbd[...])
        nout_ref[...] = n_s[...]
        mout_ref[...] = m_s[...]


def _mixers(layer, proj, cos_t, sin_t, c0, n0, m0, sh0, sr0, lb_logits, g_ml, g_hg, g_rt, lc):
    b, l, _ = proj.shape
    sc = min(16, lc)
    n_c = l // lc
    depth = lb_logits.shape[0]
    cb = lambda blk: pl.BlockSpec((None, lc, 1024), lambda bi, ci, blk=blk: (bi, ci, blk))
    st_spec = pl.BlockSpec((None, BR_W, HEAD_DIM), lambda bi, ci: (bi, 0, 0))
    row_spec = lambda n: pl.BlockSpec((None, 1, n), lambda bi, ci: (bi, 0, 0))
    full = lambda r, cc: pl.BlockSpec((r, cc), lambda bi, ci: (0, 0))
    outs = pl.pallas_call(
        functools.partial(_mixers_kernel, layer, lc, sc),
        grid=(b, n_c),
        in_specs=[cb(0), cb(1), cb(2),
                  pl.BlockSpec((None, lc, 256), lambda bi, ci: (bi, ci, P_SM // 256)),
                  pl.BlockSpec((lc, BR_W), lambda bi, ci: (ci, 0)),
                  pl.BlockSpec((lc, BR_W), lambda bi, ci: (ci, 0)),
                  st_spec, row_spec(BR_W), row_spec(128), st_spec, st_spec,
                  full(depth, BR_W), full(1, BR_W), full(1, BR_W), full(1, BR_W)],
        out_specs=[pl.BlockSpec((None, lc, 3 * BR_W), lambda bi, ci: (bi, ci, 0)),
                   pl.BlockSpec((None, lc, 128), lambda bi, ci: (bi, ci, 0)),
                   pl.BlockSpec((None, 8, lc), lambda bi, ci: (bi, 0, ci)),
                   st_spec, row_spec(BR_W), row_spec(128), st_spec, st_spec],
        out_shape=[jax.ShapeDtypeStruct((b, l, 3 * BR_W), F32),
                   jax.ShapeDtypeStruct((b, l, 128), F32),
                   jax.ShapeDtypeStruct((b, 8, l), F32),
                   jax.ShapeDtypeStruct((b, BR_W, HEAD_DIM), F32),
                   jax.ShapeDtypeStruct((b, 1, BR_W), F32),
                   jax.ShapeDtypeStruct((b, 1, 128), F32),
                   jax.ShapeDtypeStruct((b, BR_W, HEAD_DIM), F32),
                   jax.ShapeDtypeStruct((b, BR_W, HEAD_DIM), F32)],
        scratch_shapes=[pltpu.VMEM((BR_W, BR_W), F32), pltpu.VMEM((BR_W, BR_W), F32), pltpu.VMEM((BR_W, BR_W), F32),
                        pltpu.VMEM((1, BR_W), F32), pltpu.VMEM((1, 128), F32), pltpu.VMEM((128, 1), F32),
                        pltpu.VMEM((lc, BR_W), F32)],
        compiler_params=_cparams(("arbitrary", "arbitrary")),
        name="mixers",
    )(proj, proj, proj, proj, cos_t, sin_t, c0, n0, m0, sh0, sr0, lb_logits, g_ml, g_hg, g_rt)
    return outs


def _fox_prompt_kernel(tq, tk, strip, qi_ref, kj_ref, q_ref, k_ref, v_ref, ck_ref, o_ref,
                       m_s, l_s, alpha_s, acc_s, s_scr, p_scr):
    step = pl.program_id(0)
    i = qi_ref[step]
    j = kj_ref[step]
    masks = _head_masks()

    @pl.when(j == 0)
    def _init():
        m_s[...] = jnp.full_like(m_s, NEG)
        l_s[...] = jnp.zeros_like(l_s)
        acc_s[...] = jnp.zeros_like(acc_s)

    q = q_ref[...] * (QK_SCALE * LOG2E)
    kb = k_ref[...].astype(BF16)
    vb = v_ref[...].astype(BF16)
    nck = ck_ref[...] * (-LOG2E)
    n_rep = tk // 128

    def attend(diagonal):
        col = _iota((strip, tk), 1)
        row = _iota((strip, tk), 0)
        for h in range(HEADS):
            s_scr[h] = _dot_nt(jnp.where(masks[h], q, 0.0), kb) + nck[h:h + 1, :]
            parts = []
            for r0 in range(0, tq, strip):
                sc = s_scr[h, pl.ds(r0, strip), :]
                if diagonal:
                    sc = jnp.where(row + (r0 + i * tq - j * tk) >= col, sc, NEG)
                    s_scr[h, pl.ds(r0, strip), :] = sc
                parts.append(jnp.max(sc, axis=1, keepdims=True))
            m_old = m_s[h]
            m_new = jnp.maximum(m_old, jnp.broadcast_to(jnp.concatenate(parts, axis=0), (tq, 128)))
            m_s[h] = m_new
            alpha_s[h] = jnp.exp2(m_old - m_new)
            for r0 in range(0, tq, strip):
                rows = pl.ds(r0, strip)
                m_rep = jnp.concatenate([m_s[h, rows, :]] * n_rep, axis=1)
                p = jnp.exp2(s_scr[h, rows, :] - m_rep)
                p_scr[h, rows, :] = p.astype(BF16)
                psum = p[:, 0:128]
                for c in range(1, n_rep):
                    psum = psum + p[:, c * 128:(c + 1) * 128]
                l_s[h, rows, :] = alpha_s[h, rows, :] * l_s[h, rows, :] + psum
            alpha = alpha_s[h]
            acc_s[h] = jnp.concatenate([alpha, alpha], axis=1) * acc_s[h] + jnp.dot(p_scr[h], vb,
                                                                                   preferred_element_type=F32)

    last = (j + 1) * tk >= (i + 1) * tq

    @pl.when(jnp.logical_not(last))
    def _off_diagonal():
        attend(False)

    @pl.when(last)
    def _diagonal():
        attend(True)
        out = jnp.zeros((tq, BR_W), F32)
        for h in range(HEADS):
            out = jnp.where(masks[h], acc_s[h] / jnp.sum(l_s[h], axis=1, keepdims=True), out)
        o_ref[...] = out


def _fox_prompt(proj, cum_t, tq):
    t = proj.shape[0]
    tk = 2 * tq if t % (2 * tq) == 0 else tq
    nq = t // tq
    n_kj = [((i + 1) * tq - 1) // tk + 1 for i in range(nq)]
    qi = np.concatenate([np.full(n, i, np.int32) for i, n in enumerate(n_kj)])
    kj = np.concatenate([np.arange(n, dtype=np.int32) for n in n_kj])
    cq = P_FX // BR_W
    grid_spec = pltpu.PrefetchScalarGridSpec(
        num_scalar_prefetch=2,
        grid=(len(qi),),
        in_specs=[pl.BlockSpec((tq, BR_W), lambda s, qi, kj: (qi[s], cq)),
                  pl.BlockSpec((tk, BR_W), lambda s, qi, kj: (kj[s], cq + 1)),
                  pl.BlockSpec((tk, BR_W), lambda s, qi, kj: (kj[s], cq + 2)),
                  pl.BlockSpec((8, tk), lambda s, qi, kj: (0, kj[s]))],
        out_specs=pl.BlockSpec((tq, BR_W), lambda s, qi, kj: (qi[s], 0)),
        scratch_shapes=[pltpu.VMEM((HEADS, tq, 128), F32), pltpu.VMEM((HEADS, tq, 128), F32),
                        pltpu.VMEM((HEADS, tq, 128), F32), pltpu.VMEM((HEADS, tq, BR_W), F32),
                        pltpu.VMEM((HEADS, tq, tk), F32), pltpu.VMEM((HEADS, tq, tk), BF16)],
    )
    return pl.pallas_call(
        functools.partial(_fox_prompt_kernel, tq, tk, min(16384 // tk, tq)),
        grid_spec=grid_spec,
        out_shape=jax.ShapeDtypeStruct((t, BR_W), F32),
        compiler_params=_cparams(("arbitrary",)),
        name="fox_prompt",
    )(jnp.asarray(qi), jnp.asarray(kj), proj, proj, proj, cum_t)


def _fox_decode_kernel(t_new, n_slots, pt_ref, q_ref, k_ref, v_ref, sm_ref, *rest):
    kt_pages = rest[0:n_slots]
    vt_pages = rest[n_slots:2 * n_slots]
    lf_pages = rest[2 * n_slots:3 * n_slots]
    o_ref = rest[3 * n_slots]
    m_s, l_s, acc_s, carry_s = rest[3 * n_slots + 1:]
    g = pl.program_id(1)
    n_g = pl.num_programs(1)
    masks = _head_masks()
    rows = HEADS * t_new

    q = q_ref[...] * QK_SCALE
    qbd = jnp.concatenate([jnp.where(masks[h], q, 0.0) for h in range(HEADS)], axis=0).astype(BF16)

    def per_head_rows(x):
        return jnp.concatenate([jnp.broadcast_to(x[h:h + 1, :], (t_new, x.shape[1])) for h in range(HEADS)], axis=0)

    def softmax_step(sc):
        m_old = m_s[...]
        m_new = jnp.maximum(m_old, jnp.max(sc, axis=1, keepdims=True))
        alpha = jnp.exp(m_old - m_new)
        p = jnp.exp(sc - m_new)
        l_s[...] = alpha * l_s[...] + jnp.sum(p, axis=1, keepdims=True)
        m_s[...] = m_new
        return alpha, p.astype(BF16)

    @pl.when(g == 0)
    def _new_rows():
        pad = jnp.zeros((PAGE_SIZE - t_new, BR_W), F32)
        kn = jnp.concatenate([k_ref[...], pad], axis=0).astype(BF16)
        vn = jnp.concatenate([v_ref[...], pad], axis=0).astype(BF16)
        smp = jnp.concatenate([sm_ref[:, 0:128], jnp.zeros((PAGE_SIZE - t_new, 128), F32)], axis=0)
        lane128 = _iota((1, 128), 1)
        lf = jnp.where((lane128 >= SM_FLF) & (lane128 < SM_FLF + HEADS), _log_sigmoid(smp), 0.0)
        tril = (_iota((PAGE_SIZE, PAGE_SIZE), 0) >= _iota((PAGE_SIZE, PAGE_SIZE), 1)).astype(F32)
        cum_t = _transpose(_sel(lf, tril, "mx"))
        t_of_row = _iota((rows, PAGE_SIZE), 0) & (t_new - 1)
        s_of_col = _iota((rows, PAGE_SIZE), 1)
        sc = _dot_nt(qbd, kn) - per_head_rows(cum_t[SM_FLF:SM_FLF + HEADS, :])
        sc = jnp.where(s_of_col <= t_of_row, sc, NEG)
        m_s[...] = jnp.full_like(m_s, NEG)
        l_s[...] = jnp.zeros_like(l_s)
        carry_s[...] = jnp.zeros_like(carry_s)
        _, p = softmax_step(sc)
        acc_s[...] = jnp.dot(p, vn, preferred_element_type=F32)

    strict = (_iota((PAGE_SIZE, PAGE_SIZE), 0) > _iota((PAGE_SIZE, PAGE_SIZE), 1)).astype(F32)
    strict_ones = jnp.concatenate([strict, jnp.ones((PAGE_SIZE, PAGE_SIZE), F32)], axis=1)
    later = carry_s[...]
    sufs = [None] * n_slots
    for slot in reversed(range(n_slots)):
        r = _sel(lf_pages[slot][...], strict_ones, "xm")
        sufs[slot] = r[:, 0:PAGE_SIZE] + later
        later = later + r[:, PAGE_SIZE:2 * PAGE_SIZE]
    carry_s[...] = later
    bias = per_head_rows(jnp.concatenate(sufs, axis=1))
    sc = jnp.concatenate([jnp.dot(qbd, kt_pages[slot][...].astype(BF16), preferred_element_type=F32)
                          for slot in range(n_slots)], axis=1) + bias
    alpha, p = softmax_step(sc)
    pv = _dot_nt(p[:, 0:PAGE_SIZE], vt_pages[0][...])
    for slot in range(1, n_slots):
        pv += _dot_nt(p[:, slot * PAGE_SIZE:(slot + 1) * PAGE_SIZE], vt_pages[slot][...])
    acc_s[...] = alpha * acc_s[...] + pv

    @pl.when(g == n_g - 1)
    def _fin():
        res = acc_s[...] / l_s[...]
        out = jnp.zeros((t_new, BR_W), F32)
        for h in range(HEADS):
            out = jnp.where(masks[h], res[h * t_new:(h + 1) * t_new, :], out)
        o_ref[...] = out


def _fox_decode(layer, proj, cache_kt, cache_vt, cache_lf_t, page_table, n_slots):
    db, t_new, _ = proj.shape
    n_pages = page_table.shape[1]
    n_g = n_pages // n_slots
    cq = P_FX // BR_W

    def page_idx(slot):
        return lambda b, g, pt: (layer, pt[b * n_pages + (n_g - 1 - g) * n_slots + slot], 0, 0)

    kv_specs = [pl.BlockSpec((None, None, BR_W, PAGE_SIZE), page_idx(s)) for s in range(n_slots)]
    lf_specs = [pl.BlockSpec((None, None, HEADS, PAGE_SIZE), page_idx(s)) for s in range(n_slots)]
    rows = HEADS * t_new
    grid_spec = pltpu.PrefetchScalarGridSpec(
        num_scalar_prefetch=1,
        grid=(db, n_g),
        in_specs=[pl.BlockSpec((None, t_new, BR_W), lambda b, g, pt: (b, 0, cq)),
                  pl.BlockSpec((None, t_new, BR_W), lambda b, g, pt: (b, 0, cq + 1)),
                  pl.BlockSpec((None, t_new, BR_W), lambda b, g, pt: (b, 0, cq + 2)),
                  pl.BlockSpec((None, t_new, BR_W), lambda b, g, pt: (b, 0, P_SM // BR_W))]
                 + kv_specs + kv_specs + lf_specs,
        out_specs=pl.BlockSpec((None, t_new, BR_W), lambda b, g, pt: (b, 0, 0)),
        scratch_shapes=[pltpu.VMEM((rows, 1), F32), pltpu.VMEM((rows, 1), F32), pltpu.VMEM((rows, BR_W), F32),
                        pltpu.VMEM((HEADS, PAGE_SIZE), F32)],
    )
    return pl.pallas_call(
        functools.partial(_fox_decode_kernel, t_new, n_slots),
        grid_spec=grid_spec,
        out_shape=jax.ShapeDtypeStruct((db, t_new, BR_W), F32),
        compiler_params=_cparams(("arbitrary", "arbitrary")),
        name="fox_decode",
    )(page_table.reshape(-1), proj, proj, proj, proj,
      *([cache_kt] * n_slots), *([cache_vt] * n_slots), *([cache_lf_t] * n_slots))


def _merge_kernel(br_ref, fx_ref, g0_ref, g1_ref, g2_ref, g3_ref, x_ref, mod_ref, wb_ref, wo_ref, nw_ref,
                  x1_ref, h2_ref):
    merged = _sigmoid(g0_ref[...]) * _dot(br_ref[:, 0:BR_W], wb_ref[0])
    merged += _sigmoid(g1_ref[...]) * _dot(br_ref[:, BR_W:2 * BR_W], wb_ref[1])
    merged += _sigmoid(g2_ref[...]) * _dot(br_ref[:, 2 * BR_W:3 * BR_W], wb_ref[2])
    merged += _sigmoid(g3_ref[...]) * _dot(fx_ref[...], wb_ref[3])
    x1 = x_ref[...] + mod_ref[:, 2 * D_MODEL:3 * D_MODEL] * _dot(merged, wo_ref[...])
    x1_ref[...] = x1
    h2 = _rms(x1) * nw_ref[...]
    h2_ref[...] = h2 * (1.0 + mod_ref[:, 4 * D_MODEL:5 * D_MODEL]) + mod_ref[:, 3 * D_MODEL:4 * D_MODEL]


def _merge(br3, fox, proj, x, mod, w_branch_b, w_out_b, norm_w):
    t = x.shape[0]
    tm = min(t, 512) if mod.shape[0] == 1 else min(t, 256)
    per_row = mod.shape[0] != 1
    mod_spec = (pl.BlockSpec((tm, 6 * D_MODEL), lambda i: (i, 0)) if per_row
                else pl.BlockSpec((1, 6 * D_MODEL), lambda i: (0, 0)))
    gate = lambda b: pl.BlockSpec((tm, D_MODEL), lambda i, b=b: (i, P_GT // D_MODEL + b))
    return pl.pallas_call(
        _merge_kernel,
        grid=(t // tm,),
        in_specs=[pl.BlockSpec((tm, 3 * BR_W), lambda i: (i, 0)),
                  pl.BlockSpec((tm, BR_W), lambda i: (i, 0)),
                  gate(0), gate(1), gate(2), gate(3),
                  pl.BlockSpec((tm, D_MODEL), lambda i: (i, 0)),
                  mod_spec,
                  pl.BlockSpec((N_BRANCH, BR_W, D_MODEL), lambda i: (0, 0, 0)),
                  pl.BlockSpec((D_MODEL, D_MODEL), lambda i: (0, 0)),
                  pl.BlockSpec((1, D_MODEL), lambda i: (0, 0))],
        out_specs=[pl.BlockSpec((tm, D_MODEL), lambda i: (i, 0)),
                   pl.BlockSpec((tm, D_MODEL), lambda i: (i, 0))],
        out_shape=[jax.ShapeDtypeStruct((t, D_MODEL), F32), jax.ShapeDtypeStruct((t, D_MODEL), F32)],
        compiler_params=_cparams(("arbitrary",)),
        name="merge",
    )(br3, fox, proj, proj, proj, proj, x, mod, w_branch_b, w_out_b, norm_w)


def _select_experts(h, wr_t, rb_col):
    tm = h.shape[0]
    logits = _dot_nt(wr_t, h)
    s = _sigmoid(logits)
    sel = s + rb_col
    ninf = -jnp.inf
    sub = _iota((GROUP_SIZE, tm), 0)
    gsc = []
    for g in range(N_GROUPS):
        blk = sel[g * GROUP_SIZE:(g + 1) * GROUP_SIZE, :]
        m1 = jnp.max(blk, axis=0, keepdims=True)
        first = jnp.min(jnp.where(blk == m1, sub, GROUP_SIZE), axis=0, keepdims=True)
        m2 = jnp.max(jnp.where(sub == first, ninf, blk), axis=0, keepdims=True)
        gsc.append(m1 + m2)
    chosen = [jnp.zeros((1, tm), jnp.bool_) for _ in range(N_GROUPS)]
    for _ in range(TOPK_GROUPS):
        mx = gsc[0]
        for g in range(1, N_GROUPS):
            mx = jnp.maximum(mx, gsc[g])
        taken = jnp.zeros((1, tm), jnp.bool_)
        for g in range(N_GROUPS):
            pick = (gsc[g] == mx) & jnp.logical_not(taken)
            taken = taken | pick
            chosen[g] = chosen[g] | pick
            gsc[g] = jnp.where(pick, ninf, gsc[g])
    selm = jnp.concatenate(
        [jnp.where(chosen[g], sel[g * GROUP_SIZE:(g + 1) * GROUP_SIZE, :], ninf) for g in range(N_GROUPS)], axis=0)
    eidx = _iota((N_EXPERTS, tm), 0)
    firsts, picks = [], []
    for _ in range(TOP_K):
        mx = jnp.max(selm, axis=0, keepdims=True)
        first = jnp.min(jnp.where(selm == mx, eidx, N_EXPERTS), axis=0, keepdims=True)
        pick = eidx == first
        selm = jnp.where(pick, ninf, selm)
        firsts.append(first)
        picks.append(pick)
    return s, firsts, picks


def _router_kernel(h_ref, wr_ref, rb_ref, wt_ref):
    s, _, picks = _select_experts(h_ref[...], wr_ref[...], rb_ref[...])
    picked = picks[0]
    for pick in picks[1:]:
        picked = picked | pick
    w = jnp.where(picked, s, 0.0)
    wt_ref[...] = w / jnp.sum(w, axis=0, keepdims=True) * ROUTED_SCALE


def _router(h2, w_router_t, router_bias_col):
    t = h2.shape[0]
    tm = min(t, 512)
    return pl.pallas_call(
        _router_kernel,
        grid=(t // tm,),
        in_specs=[pl.BlockSpec((tm, D_MODEL), lambda i: (i, 0)),
                  pl.BlockSpec((N_EXPERTS, D_MODEL), lambda i: (0, 0)),
                  pl.BlockSpec((N_EXPERTS, 1), lambda i: (0, 0))],
        out_specs=pl.BlockSpec((N_EXPERTS, tm), lambda i: (0, i)),
        out_shape=jax.ShapeDtypeStruct((N_EXPERTS, t), F32),
        compiler_params=_cparams(("arbitrary",)),
        name="router",
    )(h2, w_router_t, router_bias_col)


def _moe_kernel(final, h_ref, wt_ref, wgu_ref, wd_ref, wsgu_ref, wsd_ref, x1_ref, mod_ref, nf_ref, o_ref,
                acc_s, hb_s, wtok_s):
    e = pl.program_id(1)
    n_e = pl.num_programs(1)

    def swiglu(gu):
        return _silu(gu[:, 0:D_EXPERT]) * gu[:, D_EXPERT:2 * D_EXPERT]

    @pl.when(e == 0)
    def _init():
        hb = h_ref[...].astype(BF16)
        hb_s[...] = hb
        wtok_s[...] = _transpose(wt_ref[...])
        acc_s[...] = _dot(swiglu(jnp.dot(hb, wsgu_ref[...], preferred_element_type=F32)), wsd_ref[...])

    wcol = jnp.sum(jnp.where(_iota((1, N_EXPERTS), 1) == e, wtok_s[...], 0.0), axis=1, keepdims=True)
    a = swiglu(_dot(hb_s[...], wgu_ref[...]))
    acc_s[...] += _dot(a, wd_ref[...]) * wcol

    @pl.when(e == n_e - 1)
    def _fin():
        x2 = x1_ref[...] + mod_ref[:, 5 * D_MODEL:6 * D_MODEL] * acc_s[...]
        if final:
            x2 = _rms(x2) * nf_ref[...]
        o_ref[...] = x2


def _moe(layer, h2, wt, wgu, wd, wsgu_b, wsd_b, x1, mod, norm_f, final):
    t = h2.shape[0]
    tm = min(t, 1024)
    per_row = mod.shape[0] != 1
    mod_spec = (pl.BlockSpec((tm, 6 * D_MODEL), lambda i, e: (i, 0)) if per_row
                else pl.BlockSpec((1, 6 * D_MODEL), lambda i, e: (0, 0)))
    return pl.pallas_call(
        functools.partial(_moe_kernel, final),
        grid=(t // tm, N_EXPERTS),
        in_specs=[pl.BlockSpec((tm, D_MODEL), lambda i, e: (i, 0)),
                  pl.BlockSpec((N_EXPERTS, tm), lambda i, e: (0, i)),
                  pl.BlockSpec((None, None, D_MODEL, 2 * D_EXPERT), lambda i, e: (layer, e, 0, 0)),
                  pl.BlockSpec((None, None, D_EXPERT, D_MODEL), lambda i, e: (layer, e, 0, 0)),
                  pl.BlockSpec((D_MODEL, 2 * D_EXPERT), lambda i, e: (0, 0)),
                  pl.BlockSpec((D_EXPERT, D_MODEL), lambda i, e: (0, 0)),
                  pl.BlockSpec((tm, D_MODEL), lambda i, e: (i, 0)),
                  mod_spec,
                  pl.BlockSpec((1, D_MODEL), lambda i, e: (0, 0))],
        out_specs=pl.BlockSpec((tm, D_MODEL), lambda i, e: (i, 0)),
        out_shape=jax.ShapeDtypeStruct((t, D_MODEL), F32),
        scratch_shapes=[pltpu.VMEM((tm, D_MODEL), F32), pltpu.VMEM((tm, D_MODEL), BF16),
                        pltpu.VMEM((tm, N_EXPERTS), F32)],
        compiler_params=_cparams(("arbitrary", "arbitrary")),
        name="moe",
    )(h2, wt, wgu, wd, wsgu_b, wsd_b, x1, mod, norm_f)


MOE_BM = 512
MOE_SPARSE_MIN_TOKENS = 1024


def _pack_bf16_pairs(x):
    n = x.shape[1] // 2
    bits = lax.bitcast_convert_type(x.astype(BF16).astype(F32), jnp.uint32)
    return lax.bitcast_convert_type(bits[:, n:] | (bits[:, :n] >> 16), F32)


def _unpack_bf16_pairs(w):
    u = lax.bitcast_convert_type(w, jnp.uint32)
    return (lax.bitcast_convert_type(u << 16, F32),
            lax.bitcast_convert_type(u & jnp.uint32(0xFFFF0000), F32))


def _router_sparse_kernel(h_ref, wr_ref, rb_ref, eidx_ref, rank_ref, ew_ref, cnt_ref, hp_ref, carry_s):
    tm = h_ref.shape[0]
    hp_ref[...] = _pack_bf16_pairs(h_ref[...])

    @pl.when(pl.program_id(0) == 0)
    def _():
        carry_s[...] = jnp.zeros_like(carry_s)

    s, firsts, picks = _select_experts(h_ref[...], wr_ref[...], rb_ref[...])
    picked = picks[0]
    for pick in picks[1:]:
        picked = picked | pick
    onehot = picked.astype(F32)
    earlier = (_iota((tm, tm), 0) < _iota((tm, tm), 1)).astype(F32)
    carry = carry_s[...]
    rank = _dot(onehot, earlier) + carry[:, 0:1]
    carry_s[...] = carry + jnp.sum(onehot, axis=1, keepdims=True)
    cnt_ref[...] = carry_s[...]

    w = [jnp.sum(jnp.where(pick, s, 0.0), axis=0, keepdims=True) for pick in picks]
    wsum = w[0]
    for wk in w[1:]:
        wsum = wsum + wk
    row8 = _iota((8, tm), 0)
    eidx8 = jnp.zeros((8, tm), jnp.int32)
    rank8 = jnp.zeros((8, tm), jnp.int32)
    ew8 = jnp.zeros((8, tm), F32)
    for k in range(TOP_K):
        rk = jnp.sum(jnp.where(picks[k], rank, 0.0), axis=0, keepdims=True)
        eidx8 = jnp.where(row8 == k, firsts[k], eidx8)
        rank8 = jnp.where(row8 == k, rk.astype(jnp.int32), rank8)
        ew8 = jnp.where(row8 == k, w[k] / wsum * ROUTED_SCALE, ew8)
    eidx_ref[...] = eidx8
    rank_ref[...] = rank8
    ew_ref[...] = ew8


def _router_sparse(h2, w_router_t, router_bias_col):
    t = h2.shape[0]
    tm = 512
    row = lambda dt: jax.ShapeDtypeStruct((8, t), dt)
    return pl.pallas_call(
        _router_sparse_kernel,
        grid=(t // tm,),
        in_specs=[pl.BlockSpec((tm, D_MODEL), lambda i: (i, 0)),
                  pl.BlockSpec((N_EXPERTS, D_MODEL), lambda i: (0, 0)),
                  pl.BlockSpec((N_EXPERTS, 1), lambda i: (0, 0))],
        out_specs=[pl.BlockSpec((8, tm), lambda i: (0, i)), pl.BlockSpec((8, tm), lambda i: (0, i)),
                   pl.BlockSpec((8, tm), lambda i: (0, i)), pl.BlockSpec((N_EXPERTS, 128), lambda i: (0, 0)),
                   pl.BlockSpec((tm, D_MODEL // 2), lambda i: (i, 0))],
        out_shape=[row(jnp.int32), row(jnp.int32), row(F32), jax.ShapeDtypeStruct((N_EXPERTS, 128), F32),
                   jax.ShapeDtypeStruct((t, D_MODEL // 2), F32)],
        scratch_shapes=[pltpu.VMEM((N_EXPERTS, 128), F32)],
        compiler_params=_cparams(("arbitrary",)),
        name="router_sparse",
    )(h2, w_router_t, router_bias_col)


def _plan_kernel(nblk_pad, cnt_ref, eidx_ref, rank_ref, dest_ref, blk_ref):
    tm = eidx_ref.shape[1]
    cnt = cnt_ref[...]
    padded = jnp.floor((cnt + (MOE_BM - 1.0)) * (1.0 / MOE_BM)) * MOE_BM
    tril = (_iota((N_EXPERTS, N_EXPERTS), 0) >= _iota((N_EXPERTS, N_EXPERTS), 1)).astype(F32)
    pad_end = _sel(padded, tril, "mx")
    start_col = (pad_end - padded)[:, 0:1]
    end_col = pad_end[:, 0:1]

    e_iota = _iota((N_EXPERTS, tm), 0)
    row8 = _iota((8, tm), 0)
    eidx = eidx_ref[...]
    rank = rank_ref[...]
    dest = jnp.zeros((8, tm), jnp.int32)
    for k in range(TOP_K):
        base = jnp.sum(jnp.where(e_iota == eidx[k:k + 1, :], start_col, 0.0), axis=0, keepdims=True)
        dest = jnp.where(row8 == k, base.astype(jnp.int32) + rank[k:k + 1, :], dest)
    dest_ref[...] = dest

    first_row = (_iota((N_EXPERTS, nblk_pad), 1) * MOE_BM).astype(F32)
    blk_e = jnp.sum((end_col <= first_row).astype(F32), axis=0, keepdims=True)
    blk_e = jnp.minimum(blk_e, N_EXPERTS - 1.0).astype(jnp.int32)
    n_used = (pad_end[N_EXPERTS - 1:N_EXPERTS, 0:1] * (1.0 / MOE_BM)).astype(jnp.int32)
    row8b = _iota((8, nblk_pad), 0)
    blk_ref[...] = jnp.where(row8b == 0, blk_e, jnp.where(row8b == 1, n_used, 0))


def _plan(cnt, eidx, rank, nblk_pad):
    t = eidx.shape[1]
    tm = min(t, 2048)
    return pl.pallas_call(
        functools.partial(_plan_kernel, nblk_pad),
        grid=(t // tm,),
        in_specs=[pl.BlockSpec((N_EXPERTS, 128), lambda i: (0, 0)),
                  pl.BlockSpec((8, tm), lambda i: (0, i)), pl.BlockSpec((8, tm), lambda i: (0, i))],
        out_specs=[pl.BlockSpec((8, tm), lambda i: (0, i)), pl.BlockSpec((8, nblk_pad), lambda i: (0, 0))],
        out_shape=[jax.ShapeDtypeStruct((8, t), jnp.int32), jax.ShapeDtypeStruct((8, nblk_pad), jnp.int32)],
        compiler_params=_cparams(("arbitrary",)),
        name="moe_plan",
    )(cnt, eidx, rank)


def _sc_mesh():
    return plsc.VectorSubcoreMesh(core_axis_name="core", subcore_axis_name="subcore")


SC_CHUNK = 256
SC_WINDOW = 128
N_CHUNK = D_MODEL // SC_CHUNK


def _sc_scatter_rows(x, idx, n_rows):
    t = x.shape[0]
    n_chunk = x.shape[1] // SC_CHUNK
    flat_idx = [i for per_chunk in idx for i in per_chunk]

    @pl.kernel(out_type=jax.ShapeDtypeStruct((n_chunk * n_rows, SC_CHUNK), x.dtype), mesh=_sc_mesh(),
               scratch_types=[])
    def scatter_kernel(x_hbm, *rest):
        i_hbm, o_hbm = rest[:-1], rest[-1]

        def body(x_vmem, *i_vmem):
            for iv in i_vmem:
                pltpu.sync_copy(x_vmem, o_hbm.at[iv.at[0]])

        for c in range(n_chunk):
            pltpu.emit_pipeline(
                body,
                grid=(t // SC_WINDOW,),
                in_specs=[pl.BlockSpec((SC_WINDOW, SC_CHUNK), lambda i, c=c: (i, c))]
                         + [pl.BlockSpec((1, SC_WINDOW), lambda i: (0, i))] * TOP_K,
                out_specs=[],
                core_axis_name=("core", "subcore"),
                dimension_semantics=(pltpu.PARALLEL,),
            )(x_hbm, *i_hbm[c * TOP_K:(c + 1) * TOP_K])

    return scatter_kernel(x, *flat_idx)


def _sc_gather_rows(y, idx):
    a = idx[0].shape[1]
    n_win = a // SC_WINDOW

    @pl.kernel(out_type=jax.ShapeDtypeStruct((N_CHUNK * a, SC_CHUNK), y.dtype), mesh=_sc_mesh(), scratch_types=[])
    def gather_kernel(y_hbm, *rest):
        i_hbm, o_hbm = rest[:-1], rest[-1]

        def body(i_vmem, o_vmem):
            pltpu.sync_copy(y_hbm.at[i_vmem.at[0]], o_vmem)

        for c in range(N_CHUNK):
            pltpu.emit_pipeline(
                body,
                grid=(n_win,),
                in_specs=[pl.BlockSpec((1, SC_WINDOW), lambda i: (0, i))],
                out_specs=[pl.BlockSpec((SC_WINDOW, SC_CHUNK), lambda i, c=c: (c * n_win + i, 0))],
                core_axis_name=("core", "subcore"),
                dimension_semantics=(pltpu.PARALLEL,),
            )(i_hbm[c], o_hbm)

    return gather_kernel(y, *idx)


def _swiglu(gu):
    return _silu(gu[:, 0:D_EXPERT]) * gu[:, D_EXPERT:2 * D_EXPERT]


def _grouped_kernel(blk_e_ref, n_used_ref, xs_ref, wgu_ref, wd_ref, ys_ref):
    @pl.when(pl.program_id(0) < n_used_ref[0])
    def _():
        gu = None
        for c in range(N_CHUNK // 2):
            lo, hi = _unpack_bf16_pairs(xs_ref[c])
            part = (_dot(lo, wgu_ref[c * SC_CHUNK:(c + 1) * SC_CHUNK, :])
                    + _dot(hi, wgu_ref[D_MODEL // 2 + c * SC_CHUNK:D_MODEL // 2 + (c + 1) * SC_CHUNK, :]))
            gu = part if gu is None else gu + part
        y = _dot(_swiglu(gu), wd_ref[...])
        for c in range(N_CHUNK):
            ys_ref[c] = y[:, c * SC_CHUNK:(c + 1) * SC_CHUNK]


def _grouped(layer, xs, blk_e, n_used, wgu, wd):
    n_rows = xs.shape[1]
    clamp = lambda b, nu: jnp.minimum(b, nu[0] - 1)
    rows_spec = pl.BlockSpec((N_CHUNK, MOE_BM, SC_CHUNK), lambda b, be, nu: (0, clamp(b, nu), 0))
    in_rows_spec = pl.BlockSpec((N_CHUNK // 2, MOE_BM, SC_CHUNK), lambda b, be, nu: (0, clamp(b, nu), 0))
    grid_spec = pltpu.PrefetchScalarGridSpec(
        num_scalar_prefetch=2,
        grid=(n_rows // MOE_BM,),
        in_specs=[in_rows_spec,
                  pl.BlockSpec((None, None, D_MODEL, 2 * D_EXPERT),
                               lambda b, be, nu: (layer, be[clamp(b, nu)], 0, 0)),
                  pl.BlockSpec((None, None, D_EXPERT, D_MODEL),
                               lambda b, be, nu: (layer, be[clamp(b, nu)], 0, 0))],
        out_specs=rows_spec,
    )
    return pl.pallas_call(
        _grouped_kernel,
        grid_spec=grid_spec,
        out_shape=jax.ShapeDtypeStruct((N_CHUNK, n_rows, SC_CHUNK), F32),
        compiler_params=_cparams(("arbitrary",)),
        name="moe_grouped",
    )(blk_e, n_used, xs, wgu, wd)


def _combine_kernel(final, yg_ref, ew_ref, h_ref, wsgu_ref, wsd_ref, x1_ref, mod_ref, nf_ref, o_ref):
    gu = jnp.dot(h_ref[...].astype(BF16), wsgu_ref[...], preferred_element_type=F32)
    acc = _dot(_swiglu(gu), wsd_ref[...])
    wt = _transpose(ew_ref[...])
    routed = []
    for c in range(N_CHUNK):
        part = wt[:, 0:1] * yg_ref[c, 0]
        for k in range(1, TOP_K):
            part = part + wt[:, k:k + 1] * yg_ref[c, k]
        routed.append(part)
    acc = acc + jnp.concatenate(routed, axis=1)
    x2 = x1_ref[...] + mod_ref[:, 5 * D_MODEL:6 * D_MODEL] * acc
    if final:
        x2 = _rms(x2) * nf_ref[...]
    o_ref[...] = x2


def _combine(yg, ew, h2, wsgu_b, wsd_b, x1, mod, norm_f, final):
    t = h2.shape[0]
    tm = 512
    per_row = mod.shape[0] != 1
    mod_spec = (pl.BlockSpec((tm, 6 * D_MODEL), lambda i: (i, 0)) if per_row
                else pl.BlockSpec((1, 6 * D_MODEL), lambda i: (0, 0)))
    return pl.pallas_call(
        functools.partial(_combine_kernel, final),
        grid=(t // tm,),
        in_specs=[pl.BlockSpec((N_CHUNK, TOP_K, tm, SC_CHUNK), lambda i: (0, 0, i, 0)),
                  pl.BlockSpec((8, tm), lambda i: (0, i)),
                  pl.BlockSpec((tm, D_MODEL), lambda i: (i, 0)),
                  pl.BlockSpec((D_MODEL, 2 * D_EXPERT), lambda i: (0, 0)),
                  pl.BlockSpec((D_EXPERT, D_MODEL), lambda i: (0, 0)),
                  pl.BlockSpec((tm, D_MODEL), lambda i: (i, 0)),
                  mod_spec,
                  pl.BlockSpec((1, D_MODEL), lambda i: (0, 0))],
        out_specs=pl.BlockSpec((tm, D_MODEL), lambda i: (i, 0)),
        out_shape=jax.ShapeDtypeStruct((t, D_MODEL), F32),
        compiler_params=_cparams(("arbitrary",)),
        name="moe_combine",
    )(yg, ew, h2, wsgu_b, wsd_b, x1, mod, norm_f)


def _moe_sparse(layer, h2, w_router_t, router_bias_col, wgu, wd, wsgu_b, wsd_b, x1, mod, norm_f, final,
                overlap=None):
    t = h2.shape[0]
    n_blk = t * TOP_K // MOE_BM + N_EXPERTS
    n_rows = n_blk * MOE_BM
    nblk_pad = -(-n_blk // 128) * 128
    eidx, rank, ew, cnt, h2_packed = _router_sparse(h2, w_router_t, router_bias_col)
    dest, blk = _plan(cnt, eidx, rank, nblk_pad)
    per_pick = [[dest[k:k + 1] + c * n_rows for k in range(TOP_K)] for c in range(N_CHUNK)]
    flat = dest[0:TOP_K].reshape(1, TOP_K * t)
    xs = _sc_scatter_rows(h2_packed, per_pick[:N_CHUNK // 2], n_rows).reshape(N_CHUNK // 2, n_rows, SC_CHUNK)
    if overlap is not None:
        xs = overlap(xs)
    ys = _grouped(layer, xs, blk[0], blk[1, 0:1], wgu, wd)
    yg = _sc_gather_rows(ys.reshape(N_CHUNK * n_rows, SC_CHUNK), [flat + c * n_rows for c in range(N_CHUNK)])
    if overlap is not None:
        yg = overlap(yg)
    return _combine(yg.reshape(N_CHUNK, TOP_K, t, SC_CHUNK), ew, h2, wsgu_b, wsd_b, x1, mod, norm_f, final)


def _permute_in_cols(w):
    ml = w[..., 0:ML_COLS]
    fx = w[..., OFF_FX:OFF_GT]
    small = jnp.concatenate([ml[..., 4 * BR_W:], fx[..., 3 * BR_W:]], axis=-1)
    pad = jnp.zeros(w.shape[:-1] + (P_GT - P_SM - small.shape[-1],), w.dtype)
    return jnp.concatenate([ml[..., :4 * BR_W], w[..., OFF_HG:OFF_RT], w[..., OFF_RT:OFF_FX], fx[..., :3 * BR_W],
                            small, pad, w[..., OFF_GT:]], axis=-1)


def _rope_tables(pos):
    half = HEAD_DIM // 2
    inv = ROPE_BASE ** (-jnp.arange(half, dtype=F32) / half)
    ang = pos.astype(F32)[:, None] * inv[None, :]
    cos = jnp.cos(ang)
    sin = jnp.sin(ang)
    cos_h = jnp.concatenate([cos, cos], axis=-1)
    sin_h = jnp.concatenate([-sin, sin], axis=-1)
    return jnp.tile(cos_h, (1, HEADS)), jnp.tile(sin_h, (1, HEADS))


def kernel(x_prompt, x_sample, cache_fox_k, cache_fox_v, cache_fox_logf, state_mlstm_C, state_mlstm_n, state_mlstm_m, state_hgrn_S, state_ret_S, page_table, c_prompt, c_sample, w_ada, b_ada, norm_mix_w, norm_ffn_w, w_in, b_in, hgrn_lb_logits, mlstm_norm_w, hgrn_norm_w, ret_norm_w, w_branch, w_out, w_router, router_bias, w_exp_gu, w_exp_down, w_shared_gu, w_shared_down, norm_f_w):
    depth = w_in.shape[0]
    bp, seq, _ = x_prompt.shape
    db, t_new, _ = x_sample.shape
    n_pool = cache_fox_k.shape[1]
    n_pages = page_table.shape[1]
    past_len = n_pages * PAGE_SIZE
    assert bp == 1 and seq % 128 == 0 and t_new == 8

    w_in_p = _permute_in_cols(w_in).astype(BF16)
    b_in_p = _permute_in_cols(b_in).reshape(depth, 1, P_TOT)
    w_branch_b = w_branch.astype(BF16)
    w_out_b = w_out.astype(BF16)
    wsgu_b = w_shared_gu.astype(BF16)
    wsd_b = w_shared_down.astype(BF16)
    w_router_t = jnp.swapaxes(w_router, 1, 2)
    cache_k = jnp.transpose(cache_fox_k, (0, 1, 3, 4, 2)).reshape(depth, n_pool, BR_W, PAGE_SIZE)
    cache_v = jnp.transpose(cache_fox_v, (0, 1, 3, 4, 2)).reshape(depth, n_pool, BR_W, PAGE_SIZE)
    cache_lf_t = jnp.swapaxes(cache_fox_logf, 2, 3)

    mods = _ada(jnp.concatenate([c_prompt, c_sample], axis=0), w_ada, b_ada)

    cos_p, sin_p = _rope_tables(jnp.arange(seq))
    cos_s, sin_s = _rope_tables(past_len + jnp.arange(t_new))

    def trunk(result, x, mod_of_layer, cos_t, sin_t, init, fox_fn, lc, overlap=None):
        b, l, _ = x.shape
        xt = x.reshape(b * l, D_MODEL)
        c_in, n_in, m_in, sh_in, sr_in = init
        per_layer = []
        for layer in range(depth):
            mod = mod_of_layer(layer)
            kv_t = b == 1 and (b * l) % 128 == 0
            proj, *kv = _inproj(layer, xt, mod, norm_mix_w[layer][None], w_in_p, b_in_p, kv_t)
            proj3 = proj.reshape(b, l, P_TOT)
            if kv_t:
                fox_k, fox_v = (a.reshape(HEADS, HEAD_DIM, l).transpose(2, 0, 1)[None] for a in kv)
            else:
                fox_k, fox_v = (proj3[..., P_FX + i * BR_W:P_FX + (i + 1) * BR_W].reshape(b, l, HEADS, HEAD_DIM)
                                for i in (1, 2))
            br3, lf_rows, cum_t, c_new, n_new, m_new, sh_new, sr_new = _mixers(
                layer, proj3, cos_t, sin_t, c_in[layer], n_in[layer], m_in[layer], sh_in[layer], sr_in[layer],
                hgrn_lb_logits, mlstm_norm_w[layer][None], hgrn_norm_w[layer][None], ret_norm_w[layer][None], lc)
            yield br3
            fox = fox_fn(layer, proj3, cum_t)
            yield fox
            x1, h2 = _merge(br3.reshape(b * l, 3 * BR_W), fox.reshape(b * l, BR_W), proj, xt, mod,
                            w_branch_b[layer], w_out_b[layer], norm_ffn_w[layer][None])
            final = layer == depth - 1
            if b * l >= MOE_SPARSE_MIN_TOKENS:
                xt = _moe_sparse(layer, h2, w_router_t[layer], router_bias[layer][:, None], w_exp_gu, w_exp_down,
                                 wsgu_b[layer], wsd_b[layer], x1, mod, norm_f_w[None], final, overlap)
            else:
                wt = _router(h2, w_router_t[layer], router_bias[layer][:, None])
                xt = _moe(layer, h2, wt, w_exp_gu, w_exp_down, wsgu_b[layer], wsd_b[layer], x1, mod, norm_f_w[None],
                          final)
            yield xt
            per_layer.append((
                fox_k,
                fox_v,
                lf_rows[..., SM_FLF:SM_FLF + HEADS],
                c_new.reshape(b, HEADS, HEAD_DIM, HEAD_DIM),
                n_new.reshape(b, HEADS, HEAD_DIM),
                m_new[:, 0, :HEADS],
                sh_new.reshape(b, HEADS, HEAD_DIM, HEAD_DIM),
                sr_new.reshape(b, HEADS, HEAD_DIM, HEAD_DIM)))
        stacked = tuple(jnp.stack([p[i] for p in per_layer]) for i in range(8))
        result.append((xt.reshape(b, l, D_MODEL),) + stacked)

    def head_major(s, b):
        return s.astype(F32).reshape(depth, b, BR_W, HEAD_DIM)

    zero_state = jnp.zeros((depth, bp, BR_W, HEAD_DIM), F32)
    prompt_init = (zero_state, jnp.zeros((depth, bp, 1, BR_W), F32), jnp.zeros((depth, bp, 1, 128), F32),
                   zero_state, zero_state)
    m_pad = jnp.pad(state_mlstm_m.astype(F32), ((0, 0), (0, 0), (0, 128 - HEADS))).reshape(depth, db, 1, 128)
    sample_init = (head_major(state_mlstm_C, db), state_mlstm_n.astype(F32).reshape(depth, db, 1, BR_W), m_pad,
                   head_major(state_hgrn_S, db), head_major(state_ret_S, db))
    n_slots = next(n for n in (32, 16, 8, 4, 2, 1) if n_pages % n == 0)
    prompt_out, sample_out = [], []
    sample = trunk(sample_out, x_sample, lambda layer: jnp.repeat(mods[layer, 1:], t_new, axis=0), cos_s, sin_s,
                   sample_init,
                   lambda layer, proj3, cum_t: _fox_decode(layer, proj3, cache_k, cache_v, cache_lf_t,
                                                           page_table, n_slots),
                   t_new)
    stages_per_copy = iter((1, 1, 2, 2))

    def overlap(arr):
        token = None
        for _ in range(next(stages_per_copy, 0)):
            token = next(sample, token)
        if token is None:
            return arr
        return lax.optimization_barrier((arr, token))[0]

    prompt = trunk(prompt_out, x_prompt, lambda layer: mods[layer, 0:1], cos_p, sin_p, prompt_init,
                   lambda layer, proj3, cum_t: _fox_prompt(proj3[0], cum_t[0], min(seq, 512)),
                   128, overlap)
    for gen in (prompt, sample):
        for _ in gen:
            pass
    prompt_out, sample_out = prompt_out[0], sample_out[0]

    return (prompt_out[0], sample_out[0]) + prompt_out[1:] + sample_out[1:]
```

```python
import functools
import math

import numpy as np
import jax
import jax.numpy as jnp
from jax import lax
from jax.experimental import pallas as pl
from jax.experimental.pallas import tpu as pltpu
from jax.experimental.pallas import tpu_sc as plsc

F32 = jnp.float32
BF16 = jnp.bfloat16
HIGHEST = lax.Precision.HIGHEST

D_MODEL = 1024
N_BRANCH = 4
BR_W = 256
HEAD_DIM = 64
HEADS = 4
ROPE_BASE = 10000.0
RMS_EPS = 1e-6
N_EXPERTS = 64
TOP_K = 6
N_GROUPS = 8
GROUP_SIZE = N_EXPERTS // N_GROUPS
TOPK_GROUPS = 4
D_EXPERT = 256
ROUTED_SCALE = 2.5
PAGE_SIZE = 128
QK_SCALE = HEAD_DIM ** -0.5
LOG2E = math.log2(math.e)

ML_COLS = 4 * BR_W + 2 * HEADS
OFF_HG = ML_COLS
OFF_RT = OFF_HG + 4 * BR_W
OFF_FX = OFF_RT + 4 * BR_W
OFF_GT = OFF_FX + 3 * BR_W + HEADS
N_IN = OFF_GT + N_BRANCH * D_MODEL

P_ML, P_HG, P_RT, P_FX, P_SM, P_GT, P_TOT = 0, 1024, 2048, 3072, 3840, 4096, 8192
SM_IG, SM_MLF, SM_FLF = 0, 4, 8

NEG = -1e30
VMEM_LIMIT = 56 * 1024 * 1024


def _cparams(sem):
    return pltpu.CompilerParams(dimension_semantics=sem, vmem_limit_bytes=VMEM_LIMIT)


def _dot(a, b):
    return jnp.dot(a.astype(BF16), b.astype(BF16), preferred_element_type=F32)


def _dot_nt(a, b):
    return lax.dot_general(a.astype(BF16), b.astype(BF16), (((1,), (1,)), ((), ())), preferred_element_type=F32)


def _dot_tn(a, b):
    return lax.dot_general(a.astype(BF16), b.astype(BF16), (((0,), (0,)), ((), ())), preferred_element_type=F32)


def _split3(x):
    x1 = x.astype(BF16)
    r1 = x - x1.astype(F32)
    x2 = r1.astype(BF16)
    x3 = (r1 - x2.astype(F32)).astype(BF16)
    return x1, x2, x3


def _sel(x, m01, dims, terms=3):
    m = m01.astype(BF16)
    x1, x2, x3 = _split3(x)
    if dims == "mx":
        f = lambda xi: jnp.dot(m, xi, preferred_element_type=F32)
    elif dims == "xm":
        f = lambda xi: jnp.dot(xi, m, preferred_element_type=F32)
    elif dims == "xmT":
        f = lambda xi: lax.dot_general(xi, m, (((1,), (1,)), ((), ())), preferred_element_type=F32)
    else:
        f = lambda xi: lax.dot_general(m, xi, (((1,), (1,)), ((), ())), preferred_element_type=F32)
    return (f(x1) + f(x2)) + f(x3) if terms == 3 else f(x1) + f(x2)


def _iota(shape, dim):
    return lax.broadcasted_iota(jnp.int32, shape, dim)


def _eye(n):
    return (_iota((n, n), 0) == _iota((n, n), 1)).astype(F32)


def _transpose(x):
    if x.shape[0] % 128 == 0 and x.shape[1] % 128 == 0:
        return x.T
    return _sel(x, _eye(x.shape[1]), "mxT")


def _sigmoid(x):
    return jax.nn.sigmoid(x)


def _silu(x):
    return x * jax.nn.sigmoid(x)


def _log_sigmoid(x):
    return jnp.minimum(x, 0.0) - jnp.log1p(jnp.exp(-jnp.abs(x)))


def _head_masks(n=BR_W):
    lane = _iota((1, n), 1) >> 6
    return [lane == h for h in range(HEADS)]


def _block_diag_mask():
    return (_iota((BR_W, BR_W), 0) >> 6) == (_iota((BR_W, BR_W), 1) >> 6)


def _per_head_lanes(vals, masks):
    out = jnp.where(masks[0], vals[0], 0.0)
    for h in range(1, HEADS):
        out = jnp.where(masks[h], vals[h], out)
    return out


def _rms(x, eps=RMS_EPS):
    return x * lax.rsqrt(jnp.mean(x * x, axis=-1, keepdims=True) + eps)


def _head_norm(o, gain, bdf):
    ms = _sel(o * o, bdf, "xm", terms=2) * (1.0 / HEAD_DIM)
    return o * lax.rsqrt(ms + RMS_EPS) * gain


def _ada_kernel(c_ref, w_ref, b_ref, o_ref):
    o_ref[...] = _dot(_silu(c_ref[...]), w_ref[...]) + b_ref[...]


def _ada(c_all, w_ada, b_ada):
    depth = w_ada.shape[0]
    n_c = c_all.shape[0]
    tn = 1536
    return pl.pallas_call(
        _ada_kernel,
        grid=(depth, 6 * D_MODEL // tn),
        in_specs=[pl.BlockSpec((n_c, D_MODEL), lambda l, j: (0, 0)),
                  pl.BlockSpec((None, D_MODEL, tn), lambda l, j: (l, 0, j)),
                  pl.BlockSpec((None, 1, tn), lambda l, j: (l, 0, j))],
        out_specs=pl.BlockSpec((None, n_c, tn), lambda l, j: (l, 0, j)),
        out_shape=jax.ShapeDtypeStruct((depth, n_c, 6 * D_MODEL), F32),
        compiler_params=_cparams(("arbitrary", "arbitrary")),
        name="ada",
    )(c_all, w_ada, b_ada.reshape(depth, 1, 6 * D_MODEL))


def _inproj_kernel(kv_t, x_ref, mod_ref, nw_ref, w_ref, b_ref, o_ref, *rest):
    h_scr = rest[-1]
    j = pl.program_id(1)

    @pl.when(j == 0)
    def _():
        h = _rms(x_ref[...]) * nw_ref[...]
        h = h * (1.0 + mod_ref[:, D_MODEL:2 * D_MODEL]) + mod_ref[:, 0:D_MODEL]
        h_scr[...] = h.astype(BF16)

    res = jnp.dot(h_scr[...], w_ref[...], preferred_element_type=F32) + b_ref[...]
    o_ref[...] = res

    if kv_t:
        kt_ref, vt_ref = rest[0], rest[1]
        c0 = P_FX % res.shape[1]

        @pl.when(j == P_FX // res.shape[1])
        def _():
            kt_ref[...] = res[:, c0 + BR_W:c0 + 2 * BR_W].T
            vt_ref[...] = res[:, c0 + 2 * BR_W:c0 + 3 * BR_W].T


def _inproj(layer, x, mod, norm_w, w_p, b_p, kv_t):
    t = x.shape[0]
    tm = min(t, 1024)
    tn = 1024
    per_row = mod.shape[0] != 1
    mod_spec = (pl.BlockSpec((tm, 6 * D_MODEL), lambda i, j: (i, 0)) if per_row
                else pl.BlockSpec((1, 6 * D_MODEL), lambda i, j: (0, 0)))
    out_specs = [pl.BlockSpec((tm, tn), lambda i, j: (i, j))]
    out_shape = [jax.ShapeDtypeStruct((t, P_TOT), F32)]
    if kv_t:
        out_specs += [pl.BlockSpec((BR_W, tm), lambda i, j: (0, i))] * 2
        out_shape += [jax.ShapeDtypeStruct((BR_W, t), F32)] * 2
    return pl.pallas_call(
        functools.partial(_inproj_kernel, kv_t),
        grid=(t // tm, P_TOT // tn),
        in_specs=[pl.BlockSpec((tm, D_MODEL), lambda i, j: (i, 0)),
                  mod_spec,
                  pl.BlockSpec((1, D_MODEL), lambda i, j: (0, 0)),
                  pl.BlockSpec((None, D_MODEL, tn), lambda i, j: (layer, 0, j)),
                  pl.BlockSpec((None, 1, tn), lambda i, j: (layer, 0, j))],
        out_specs=out_specs,
        out_shape=out_shape,
        scratch_shapes=[pltpu.VMEM((tm, D_MODEL), BF16)],
        compiler_params=_cparams(("arbitrary", "arbitrary")),
        name="inproj",
    )(x, mod, norm_w, w_p, b_p)


def _mixers_kernel(layer, lc, sc,
                   ml_ref, hg_ref, rt_ref, sm_ref, cos_ref, sin_ref,
                   c0_ref, n0_ref, m0_ref, sh0_ref, sr0_ref, lbl_ref, gml_ref, ghg_ref, grt_ref,
                   br_ref, lf_ref, cumt_ref, cout_ref, nout_ref, mout_ref, shout_ref, srout_ref,
                   cbd, sht, srbd, n_s, m_s, carry_s, ohg_s):
    c = pl.program_id(1)
    n_c = pl.num_programs(1)
    masks = _head_masks()
    bd = _block_diag_mask()
    bdf = bd.astype(F32)
    tile = ((_iota((HEAD_DIM, BR_W), 1) & (HEAD_DIM - 1)) == _iota((HEAD_DIM, BR_W), 0)).astype(F32)

    @pl.when(c == 0)
    def _init():
        def expand(ref):
            return jnp.where(bd, _sel(ref[...], tile, "xm"), 0.0)
        cbd[...] = expand(c0_ref)
        sht[...] = _transpose(expand(sh0_ref))
        srbd[...] = expand(sr0_ref)
        n_s[...] = n0_ref[...]
        m_s[...] = m0_ref[...]
        carry_s[...] = jnp.zeros_like(carry_s)

    row = _iota((lc, lc), 0)
    col = _iota((lc, lc), 1)
    causal = row >= col
    tril = causal.astype(F32)

    sm = sm_ref[:, 0:128]
    lane128 = _iota((1, 128), 1)
    sm2 = jnp.where((lane128 >= SM_MLF) & (lane128 < SM_FLF + HEADS), _log_sigmoid(sm), sm)
    cum = _sel(sm2, tril, "mx")
    sm2_t = _transpose(sm2)
    cum_t = _transpose(cum)
    lf_ref[...] = sm2
    cum_tg = cum_t + carry_s[...]
    carry_s[...] = cum_tg[:, lc - 1:lc]
    cumt_ref[...] = cum_tg[SM_FLF:SM_FLF + 8, :]

    q = ml_ref[:, 0:BR_W]
    k = ml_ref[:, BR_W:2 * BR_W] * QK_SCALE
    v = ml_ref[:, 2 * BR_W:3 * BR_W]
    og = ml_ref[:, 3 * BR_W:4 * BR_W]
    n_row = n_s[...]
    m_row = m_s[...]
    q_c = _dot_nt(q, cbd[...])
    h_all = jnp.zeros((lc, BR_W), F32)
    w_lanes = jnp.zeros((lc, BR_W), F32)
    decay_lanes = jnp.zeros((1, BR_W), F32)
    m_new_row = jnp.zeros((1, 128), F32)
    for h in range(HEADS):
        ig_c = sm2[:, SM_IG + h:SM_IG + h + 1]
        b_c = cum[:, SM_MLF + h:SM_MLF + h + 1]
        ig_r = sm2_t[SM_IG + h:SM_IG + h + 1, :]
        b_r = cum_t[SM_MLF + h:SM_MLF + h + 1, :]
        m_prev = m_row[:, h:h + 1]
        dmat = jnp.where(causal, b_c - b_r + ig_r, NEG)
        m_inter = b_c + m_prev
        m_t = jnp.maximum(m_inter, jnp.max(dmat, axis=1, keepdims=True))
        w_intra = jnp.exp(dmat - m_t)
        w_inter = jnp.exp(m_inter - m_t)
        qh = jnp.where(masks[h], q, 0.0)
        a = _dot_nt(qh, k) * w_intra
        num = _dot(a, v) + w_inter * q_c
        den = jnp.sum(a, axis=1, keepdims=True) + w_inter * jnp.sum(qh * n_row, axis=1, keepdims=True)
        hh = num / jnp.maximum(jnp.abs(den), jnp.exp(-m_t))
        h_all = jnp.where(masks[h], hh, h_all)
        m_new = m_t[lc - 1:lc, :]
        b_last = b_c[lc - 1:lc, :]
        w_s = jnp.exp(b_last - b_c + ig_c - m_new)
        decay = jnp.exp(b_last + m_prev - m_new)
        w_lanes = jnp.where(masks[h], w_s, w_lanes)
        decay_lanes = jnp.where(masks[h], decay, decay_lanes)
        m_new_row = jnp.where(lane128 == h, m_new, m_new_row)
    kw = k * w_lanes
    cbd[...] = cbd[...] * decay_lanes + jnp.where(bd, _dot_tn(v * w_lanes, k), 0.0)
    n_s[...] = n_row * decay_lanes + jnp.sum(kw, axis=0, keepdims=True)
    m_s[...] = m_new_row
    out_ml = _head_norm(h_all, gml_ref[...], bdf) * _sigmoid(og)

    lbl = lbl_ref[...]
    pr = jnp.exp(lbl - jnp.max(lbl, axis=0, keepdims=True))
    pr = pr / jnp.sum(pr, axis=0, keepdims=True)
    lb = jnp.zeros((1, BR_W), F32)
    for i in range(1, layer + 1):
        lb = lb + pr[i:i + 1, :]
    tril_sc = (_iota((sc, sc), 0) >= _iota((sc, sc), 1)).astype(F32)
    s_idx = _iota((sc, BR_W), 0)

    def hg_body(i, carry):
        r0 = pl.multiple_of(i * sc, sc)
        hq = hg_ref[pl.ds(r0, sc), 0:BR_W]
        hf = hg_ref[pl.ds(r0, sc), BR_W:2 * BR_W]
        vi = hg_ref[pl.ds(r0, sc), 2 * BR_W:3 * BR_W]
        qi = _silu(hq)
        f = lb + (1.0 - lb) * _sigmoid(hf)
        ki = 1.0 - f
        bi = _sel(jnp.log(f), tril_sc, "mx")
        st = sht[...]
        o_inter = _dot_nt(qi * jnp.exp(bi), st)
        rows = []
        for t in range(sc):
            e_t = jnp.exp(jnp.where(s_idx <= t, bi[t:t + 1, :] - bi, NEG)) * qi[t:t + 1, :] * ki
            rows.append(e_t)
        e_all = jnp.concatenate(rows, axis=0)
        r_all = _dot(e_all, bdf)
        o_diag = jnp.sum(r_all.reshape(sc, sc, BR_W) * vi[None, :, :], axis=1)
        ohg_s[pl.ds(r0, sc), :] = o_inter + o_diag
        b_last = bi[sc - 1:sc, :]
        sht[...] = st * jnp.exp(b_last) + jnp.where(bd, _dot_tn(vi, ki * jnp.exp(b_last - bi)), 0.0)
        return carry

    lax.fori_loop(0, lc // sc, hg_body, 0, unroll=True)
    out_hg = _head_norm(ohg_s[...], ghg_ref[...], bdf) * _silu(hg_ref[:, 3 * BR_W:4 * BR_W])

    cosv = cos_ref[...]
    sinv = sin_ref[...]
    lane = _iota((1, BR_W), 1)
    first_half = (lane & (HEAD_DIM - 1)) < (HEAD_DIM // 2)

    def rope(x):
        partner = jnp.where(first_half, pltpu.roll(x, BR_W - HEAD_DIM // 2, 1), pltpu.roll(x, HEAD_DIM // 2, 1))
        return x * cosv + partner * sinv

    rq = rope(rt_ref[:, 0:BR_W])
    rk = rope(rt_ref[:, BR_W:2 * BR_W]) * QK_SCALE
    rv = rt_ref[:, 2 * BR_W:3 * BR_W]
    lg = [math.log1p(-(2.0 ** (-5.0 - h))) for h in range(HEADS)]
    lg_lanes = _per_head_lanes([jnp.full((1, 1), g, F32) for g in lg], masks)
    diff = (row - col).astype(F32)
    o_rt = jnp.zeros((lc, BR_W), F32)
    for h in range(HEADS):
        dec = jnp.exp(jnp.where(causal, diff * lg[h], NEG))
        a = _dot_nt(jnp.where(masks[h], rq, 0.0), rk) * dec
        o_rt = jnp.where(masks[h], _dot(a, rv), o_rt)
    t_idx = _iota((lc, BR_W), 0).astype(F32)
    o_rt = o_rt + _dot(rq * jnp.exp((t_idx + 1.0) * lg_lanes), srbd[...])
    w_ret = jnp.exp((lc - 1.0 - t_idx) * lg_lanes)
    srbd[...] = srbd[...] * jnp.exp(lc * lg_lanes) + jnp.where(bd, _dot_tn(rk * w_ret, rv), 0.0)
    out_rt = _head_norm(o_rt, grt_ref[...], bdf) * _silu(rt_ref[:, 3 * BR_W:4 * BR_W])

    br_ref[:, 0:BR_W] = out_ml
    br_ref[:, BR_W:2 * BR_W] = out_hg
    br_ref[:, 2 * BR_W:3 * BR_W] = out_rt

    @pl.when(c == n_c - 1)
    def _fin():
        def compact(x):
            return _sel(x, tile, "xmT")
        cout_ref[...] = compact(cbd[...])
        shout_ref[...] = compact(_transpose(sht[...]))
        srout_ref[...] = compact(srbd[...])
        nout_ref[...] = n_s[...]
        mout_ref[...] = m_s[...]


def _mixers(layer, proj, cos_t, sin_t, c0, n0, m0, sh0, sr0, lb_logits, g_ml, g_hg, g_rt, lc):
    b, l, _ = proj.shape
    sc = min(16, lc)
    n_c = l // lc
    depth = lb_logits.shape[0]
    cb = lambda blk: pl.BlockSpec((None, lc, 1024), lambda bi, ci, blk=blk: (bi, ci, blk))
    st_spec = pl.BlockSpec((None, BR_W, HEAD_DIM), lambda bi, ci: (bi, 0, 0))
    row_spec = lambda n: pl.BlockSpec((None, 1, n), lambda bi, ci: (bi, 0, 0))
    full = lambda r, cc: pl.BlockSpec((r, cc), lambda bi, ci: (0, 0))
    outs = pl.pallas_call(
        functools.partial(_mixers_kernel, layer, lc, sc),
        grid=(b, n_c),
        in_specs=[cb(0), cb(1), cb(2),
                  pl.BlockSpec((None, lc, 256), lambda bi, ci: (bi, ci, P_SM // 256)),
                  pl.BlockSpec((lc, BR_W), lambda bi, ci: (ci, 0)),
                  pl.BlockSpec((lc, BR_W), lambda bi, ci: (ci, 0)),
                  st_spec, row_spec(BR_W), row_spec(128), st_spec, st_spec,
                  full(depth, BR_W), full(1, BR_W), full(1, BR_W), full(1, BR_W)],
        out_specs=[pl.BlockSpec((None, lc, 3 * BR_W), lambda bi, ci: (bi, ci, 0)),
                   pl.BlockSpec((None, lc, 128), lambda bi, ci: (bi, ci, 0)),
                   pl.BlockSpec((None, 8, lc), lambda bi, ci: (bi, 0, ci)),
                   st_spec, row_spec(BR_W), row_spec(128), st_spec, st_spec],
        out_shape=[jax.ShapeDtypeStruct((b, l, 3 * BR_W), F32),
                   jax.ShapeDtypeStruct((b, l, 128), F32),
                   jax.ShapeDtypeStruct((b, 8, l), F32),
                   jax.ShapeDtypeStruct((b, BR_W, HEAD_DIM), F32),
                   jax.ShapeDtypeStruct((b, 1, BR_W), F32),
                   jax.ShapeDtypeStruct((b, 1, 128), F32),
                   jax.ShapeDtypeStruct((b, BR_W, HEAD_DIM), F32),
                   jax.ShapeDtypeStruct((b, BR_W, HEAD_DIM), F32)],
        scratch_shapes=[pltpu.VMEM((BR_W, BR_W), F32), pltpu.VMEM((BR_W, BR_W), F32), pltpu.VMEM((BR_W, BR_W), F32),
                        pltpu.VMEM((1, BR_W), F32), pltpu.VMEM((1, 128), F32), pltpu.VMEM((128, 1), F32),
                        pltpu.VMEM((lc, BR_W), F32)],
        compiler_params=_cparams(("arbitrary", "arbitrary")),
        name="mixers",
    )(proj, proj, proj, proj, cos_t, sin_t, c0, n0, m0, sh0, sr0, lb_logits, g_ml, g_hg, g_rt)
    return outs


def _fox_prompt_kernel(tq, tk, strip, qi_ref, kj_ref, q_ref, k_ref, v_ref, ck_ref, o_ref,
                       m_s, l_s, alpha_s, acc_s, s_scr, p_scr):
    step = pl.program_id(0)
    i = qi_ref[step]
    j = kj_ref[step]
    masks = _head_masks()

    @pl.when(j == 0)
    def _init():
        m_s[...] = jnp.full_like(m_s, NEG)
        l_s[...] = jnp.zeros_like(l_s)
        acc_s[...] = jnp.zeros_like(acc_s)

    q = q_ref[...] * (QK_SCALE * LOG2E)
    kb = k_ref[...].astype(BF16)
    vb = v_ref[...].astype(BF16)
    nck = ck_ref[...] * (-LOG2E)
    n_rep = tk // 128

    def attend(diagonal):
        col = _iota((strip, tk), 1)
        row = _iota((strip, tk), 0)
        for h in range(HEADS):
            s_scr[h] = _dot_nt(jnp.where(masks[h], q, 0.0), kb) + nck[h:h + 1, :]
            parts = []
            for r0 in range(0, tq, strip):
                sc = s_scr[h, pl.ds(r0, strip), :]
                if diagonal:
                    sc = jnp.where(row + (r0 + i * tq - j * tk) >= col, sc, NEG)
                    s_scr[h, pl.ds(r0, strip), :] = sc
                parts.append(jnp.max(sc, axis=1, keepdims=True))
            m_old = m_s[h]
            m_new = jnp.maximum(m_old, jnp.broadcast_to(jnp.concatenate(parts, axis=0), (tq, 128)))
            m_s[h] = m_new
            alpha_s[h] = jnp.exp2(m_old - m_new)
            for r0 in range(0, tq, strip):
                rows = pl.ds(r0, strip)
                m_rep = jnp.concatenate([m_s[h, rows, :]] * n_rep, axis=1)
                p = jnp.exp2(s_scr[h, rows, :] - m_rep)
                p_scr[h, rows, :] = p.astype(BF16)
                psum = p[:, 0:128]
                for c in range(1, n_rep):
                    psum = psum + p[:, c * 128:(c + 1) * 128]
                l_s[h, rows, :] = alpha_s[h, rows, :] * l_s[h, rows, :] + psum
            alpha = alpha_s[h]
            acc_s[h] = jnp.concatenate([alpha, alpha], axis=1) * acc_s[h] + jnp.dot(p_scr[h], vb,
                                                                                   preferred_element_type=F32)

    last = (j + 1) * tk >= (i + 1) * tq

    @pl.when(jnp.logical_not(last))
    def _off_diagonal():
        attend(False)

    @pl.when(last)
    def _diagonal():
        attend(True)
        out = jnp.zeros((tq, BR_W), F32)
        for h in range(HEADS):
            out = jnp.where(masks[h], acc_s[h] / jnp.sum(l_s[h], axis=1, keepdims=True), out)
        o_ref[...] = out


def _fox_prompt(proj, cum_t, tq):
    t = proj.shape[0]
    tk = 2 * tq if t % (2 * tq) == 0 else tq
    nq = t // tq
    n_kj = [((i + 1) * tq - 1) // tk + 1 for i in range(nq)]
    qi = np.concatenate([np.full(n, i, np.int32) for i, n in enumerate(n_kj)])
    kj = np.concatenate([np.arange(n, dtype=np.int32) for n in n_kj])
    cq = P_FX // BR_W
    grid_spec = pltpu.PrefetchScalarGridSpec(
        num_scalar_prefetch=2,
        grid=(len(qi),),
        in_specs=[pl.BlockSpec((tq, BR_W), lambda s, qi, kj: (qi[s], cq)),
                  pl.BlockSpec((tk, BR_W), lambda s, qi, kj: (kj[s], cq + 1)),
                  pl.BlockSpec((tk, BR_W), lambda s, qi, kj: (kj[s], cq + 2)),
                  pl.BlockSpec((8, tk), lambda s, qi, kj: (0, kj[s]))],
        out_specs=pl.BlockSpec((tq, BR_W), lambda s, qi, kj: (qi[s], 0)),
        scratch_shapes=[pltpu.VMEM((HEADS, tq, 128), F32), pltpu.VMEM((HEADS, tq, 128), F32),
                        pltpu.VMEM((HEADS, tq, 128), F32), pltpu.VMEM((HEADS, tq, BR_W), F32),
                        pltpu.VMEM((HEADS, tq, tk), F32), pltpu.VMEM((HEADS, tq, tk), BF16)],
    )
    return pl.pallas_call(
        functools.partial(_fox_prompt_kernel, tq, tk, min(16384 // tk, tq)),
        grid_spec=grid_spec,
        out_shape=jax.ShapeDtypeStruct((t, BR_W), F32),
        compiler_params=_cparams(("arbitrary",)),
        name="fox_prompt",
    )(jnp.asarray(qi), jnp.asarray(kj), proj, proj, proj, cum_t)


def _fox_decode_kernel(t_new, n_slots, pt_ref, q_ref, k_ref, v_ref, sm_ref, *rest):
    kt_pages = rest[0:n_slots]
    vt_pages = rest[n_slots:2 * n_slots]
    lf_pages = rest[2 * n_slots:3 * n_slots]
    o_ref = rest[3 * n_slots]
    m_s, l_s, acc_s, carry_s = rest[3 * n_slots + 1:]
    g = pl.program_id(1)
    n_g = pl.num_programs(1)
    masks = _head_masks()
    rows = HEADS * t_new

    q = q_ref[...] * QK_SCALE
    qbd = jnp.concatenate([jnp.where(masks[h], q, 0.0) for h in range(HEADS)], axis=0).astype(BF16)

    def per_head_rows(x):
        return jnp.concatenate([jnp.broadcast_to(x[h:h + 1, :], (t_new, x.shape[1])) for h in range(HEADS)], axis=0)

    def softmax_step(sc):
        m_old = m_s[...]
        m_new = jnp.maximum(m_old, jnp.max(sc, axis=1, keepdims=True))
        alpha = jnp.exp(m_old - m_new)
        p = jnp.exp(sc - m_new)
        l_s[...] = alpha * l_s[...] + jnp.sum(p, axis=1, keepdims=True)
        m_s[...] = m_new
        return alpha, p.astype(BF16)

    @pl.when(g == 0)
    def _new_rows():
        pad = jnp.zeros((PAGE_SIZE - t_new, BR_W), F32)
        kn = jnp.concatenate([k_ref[...], pad], axis=0).astype(BF16)
        vn = jnp.concatenate([v_ref[...], pad], axis=0).astype(BF16)
        smp = jnp.concatenate([sm_ref[:, 0:128], jnp.zeros((PAGE_SIZE - t_new, 128), F32)], axis=0)
        lane128 = _iota((1, 128), 1)
        lf = jnp.where((lane128 >= SM_FLF) & (lane128 < SM_FLF + HEADS), _log_sigmoid(smp), 0.0)
        tril = (_iota((PAGE_SIZE, PAGE_SIZE), 0) >= _iota((PAGE_SIZE, PAGE_SIZE), 1)).astype(F32)
        cum_t = _transpose(_sel(lf, tril, "mx"))
        t_of_row = _iota((rows, PAGE_SIZE), 0) & (t_new - 1)
        s_of_col = _iota((rows, PAGE_SIZE), 1)
        sc = _dot_nt(qbd, kn) - per_head_rows(cum_t[SM_FLF:SM_FLF + HEADS, :])
        sc = jnp.where(s_of_col <= t_of_row, sc, NEG)
        m_s[...] = jnp.full_like(m_s, NEG)
        l_s[...] = jnp.zeros_like(l_s)
        carry_s[...] = jnp.zeros_like(carry_s)
        _, p = softmax_step(sc)
        acc_s[...] = jnp.dot(p, vn, preferred_element_type=F32)

    strict = (_iota((PAGE_SIZE, PAGE_SIZE), 0) > _iota((PAGE_SIZE, PAGE_SIZE), 1)).astype(F32)
    strict_ones = jnp.concatenate([strict, jnp.ones((PAGE_SIZE, PAGE_SIZE), F32)], axis=1)
    later = carry_s[...]
    sufs = [None] * n_slots
    for slot in reversed(range(n_slots)):
        r = _sel(lf_pages[slot][...], strict_ones, "xm")
        sufs[slot] = r[:, 0:PAGE_SIZE] + later
        later = later + r[:, PAGE_SIZE:2 * PAGE_SIZE]
    carry_s[...] = later
    bias = per_head_rows(jnp.concatenate(sufs, axis=1))
    sc = jnp.concatenate([jnp.dot(qbd, kt_pages[slot][...].astype(BF16), preferred_element_type=F32)
                          for slot in range(n_slots)], axis=1) + bias
    alpha, p = softmax_step(sc)
    pv = _dot_nt(p[:, 0:PAGE_SIZE], vt_pages[0][...])
    for slot in range(1, n_slots):
        pv += _dot_nt(p[:, slot * PAGE_SIZE:(slot + 1) * PAGE_SIZE], vt_pages[slot][...])
    acc_s[...] = alpha * acc_s[...] + pv

    @pl.when(g == n_g - 1)
    def _fin():
        res = acc_s[...] / l_s[...]
        out = jnp.zeros((t_new, BR_W), F32)
        for h in range(HEADS):
            out = jnp.where(masks[h], res[h * t_new:(h + 1) * t_new, :], out)
        o_ref[...] = out


def _fox_decode(layer, proj, cache_kt, cache_vt, cache_lf_t, page_table, n_slots):
    db, t_new, _ = proj.shape
    n_pages = page_table.shape[1]
    n_g = n_pages // n_slots
    cq = P_FX // BR_W

    def page_idx(slot):
        return lambda b, g, pt: (layer, pt[b * n_pages + (n_g - 1 - g) * n_slots + slot], 0, 0)

    kv_specs = [pl.BlockSpec((None, None, BR_W, PAGE_SIZE), page_idx(s)) for s in range(n_slots)]
    lf_specs = [pl.BlockSpec((None, None, HEADS, PAGE_SIZE), page_idx(s)) for s in range(n_slots)]
    rows = HEADS * t_new
    grid_spec = pltpu.PrefetchScalarGridSpec(
        num_scalar_prefetch=1,
        grid=(db, n_g),
        in_specs=[pl.BlockSpec((None, t_new, BR_W), lambda b, g, pt: (b, 0, cq)),
                  pl.BlockSpec((None, t_new, BR_W), lambda b, g, pt: (b, 0, cq + 1)),
                  pl.BlockSpec((None, t_new, BR_W), lambda b, g, pt: (b, 0, cq + 2)),
                  pl.BlockSpec((None, t_new, BR_W), lambda b, g, pt: (b, 0, P_SM // BR_W))]
                 + kv_specs + kv_specs + lf_specs,
        out_specs=pl.BlockSpec((None, t_new, BR_W), lambda b, g, pt: (b, 0, 0)),
        scratch_shapes=[pltpu.VMEM((rows, 1), F32), pltpu.VMEM((rows, 1), F32), pltpu.VMEM((rows, BR_W), F32),
                        pltpu.VMEM((HEADS, PAGE_SIZE), F32)],
    )
    return pl.pallas_call(
        functools.partial(_fox_decode_kernel, t_new, n_slots),
        grid_spec=grid_spec,
        out_shape=jax.ShapeDtypeStruct((db, t_new, BR_W), F32),
        compiler_params=_cparams(("arbitrary", "arbitrary")),
        name="fox_decode",
    )(page_table.reshape(-1), proj, proj, proj, proj,
      *([cache_kt] * n_slots), *([cache_vt] * n_slots), *([cache_lf_t] * n_slots))


def _merge_kernel(br_ref, fx_ref, g0_ref, g1_ref, g2_ref, g3_ref, x_ref, mod_ref, wb_ref, wo_ref, nw_ref,
                  x1_ref, h2_ref):
    merged = _sigmoid(g0_ref[...]) * _dot(br_ref[:, 0:BR_W], wb_ref[0])
    merged += _sigmoid(g1_ref[...]) * _dot(br_ref[:, BR_W:2 * BR_W], wb_ref[1])
    merged += _sigmoid(g2_ref[...]) * _dot(br_ref[:, 2 * BR_W:3 * BR_W], wb_ref[2])
    merged += _sigmoid(g3_ref[...]) * _dot(fx_ref[...], wb_ref[3])
    x1 = x_ref[...] + mod_ref[:, 2 * D_MODEL:3 * D_MODEL] * _dot(merged, wo_ref[...])
    x1_ref[...] = x1
    h2 = _rms(x1) * nw_ref[...]
    h2_ref[...] = h2 * (1.0 + mod_ref[:, 4 * D_MODEL:5 * D_MODEL]) + mod_ref[:, 3 * D_MODEL:4 * D_MODEL]


def _merge(br3, fox, proj, x, mod, w_branch_b, w_out_b, norm_w):
    t = x.shape[0]
    tm = min(t, 512) if mod.shape[0] == 1 else min(t, 256)
    per_row = mod.shape[0] != 1
    mod_spec = (pl.BlockSpec((tm, 6 * D_MODEL), lambda i: (i, 0)) if per_row
                else pl.BlockSpec((1, 6 * D_MODEL), lambda i: (0, 0)))
    gate = lambda b: pl.BlockSpec((tm, D_MODEL), lambda i, b=b: (i, P_GT // D_MODEL + b))
    return pl.pallas_call(
        _merge_kernel,
        grid=(t // tm,),
        in_specs=[pl.BlockSpec((tm, 3 * BR_W), lambda i: (i, 0)),
                  pl.BlockSpec((tm, BR_W), lambda i: (i, 0)),
                  gate(0), gate(1), gate(2), gate(3),
                  pl.BlockSpec((tm, D_MODEL), lambda i: (i, 0)),
                  mod_spec,
                  pl.BlockSpec((N_BRANCH, BR_W, D_MODEL), lambda i: (0, 0, 0)),
                  pl.BlockSpec((D_MODEL, D_MODEL), lambda i: (0, 0)),
                  pl.BlockSpec((1, D_MODEL), lambda i: (0, 0))],
        out_specs=[pl.BlockSpec((tm, D_MODEL), lambda i: (i, 0)),
                   pl.BlockSpec((tm, D_MODEL), lambda i: (i, 0))],
        out_shape=[jax.ShapeDtypeStruct((t, D_MODEL), F32), jax.ShapeDtypeStruct((t, D_MODEL), F32)],
        compiler_params=_cparams(("arbitrary",)),
        name="merge",
    )(br3, fox, proj, proj, proj, proj, x, mod, w_branch_b, w_out_b, norm_w)


def _select_experts(h, wr_t, rb_col):
    tm = h.shape[0]
    logits = _dot_nt(wr_t, h)
    s = _sigmoid(logits)
    sel = s + rb_col
    ninf = -jnp.inf
    sub = _iota((GROUP_SIZE, tm), 0)
    gsc = []
    for g in range(N_GROUPS):
        blk = sel[g * GROUP_SIZE:(g + 1) * GROUP_SIZE, :]
        m1 = jnp.max(blk, axis=0, keepdims=True)
        first = jnp.min(jnp.where(blk == m1, sub, GROUP_SIZE), axis=0, keepdims=True)
        m2 = jnp.max(jnp.where(sub == first, ninf, blk), axis=0, keepdims=True)
        gsc.append(m1 + m2)
    chosen = [jnp.zeros((1, tm), jnp.bool_) for _ in range(N_GROUPS)]
    for _ in range(TOPK_GROUPS):
        mx = gsc[0]
        for g in range(1, N_GROUPS):
            mx = jnp.maximum(mx, gsc[g])
        taken = jnp.zeros((1, tm), jnp.bool_)
        for g in range(N_GROUPS):
            pick = (gsc[g] == mx) & jnp.logical_not(taken)
            taken = taken | pick
            chosen[g] = chosen[g] | pick
            gsc[g] = jnp.where(pick, ninf, gsc[g])
    selm = jnp.concatenate(
        [jnp.where(chosen[g], sel[g * GROUP_SIZE:(g + 1) * GROUP_SIZE, :], ninf) for g in range(N_GROUPS)], axis=0)
    eidx = _iota((N_EXPERTS, tm), 0)
    firsts, picks = [], []
    for _ in range(TOP_K):
        mx = jnp.max(selm, axis=0, keepdims=True)
        first = jnp.min(jnp.where(selm == mx, eidx, N_EXPERTS), axis=0, keepdims=True)
        pick = eidx == first
        selm = jnp.where(pick, ninf, selm)
        firsts.append(first)
        picks.append(pick)
    return s, firsts, picks


def _router_kernel(h_ref, wr_ref, rb_ref, wt_ref):
    s, _, picks = _select_experts(h_ref[...], wr_ref[...], rb_ref[...])
    picked = picks[0]
    for pick in picks[1:]:
        picked = picked | pick
    w = jnp.where(picked, s, 0.0)
    wt_ref[...] = w / jnp.sum(w, axis=0, keepdims=True) * ROUTED_SCALE


def _router(h2, w_router_t, router_bias_col):
    t = h2.shape[0]
    tm = min(t, 512)
    return pl.pallas_call(
        _router_kernel,
        grid=(t // tm,),
        in_specs=[pl.BlockSpec((tm, D_MODEL), lambda i: (i, 0)),
                  pl.BlockSpec((N_EXPERTS, D_MODEL), lambda i: (0, 0)),
                  pl.BlockSpec((N_EXPERTS, 1), lambda i: (0, 0))],
        out_specs=pl.BlockSpec((N_EXPERTS, tm), lambda i: (0, i)),
        out_shape=jax.ShapeDtypeStruct((N_EXPERTS, t), F32),
        compiler_params=_cparams(("arbitrary",)),
        name="router",
    )(h2, w_router_t, router_bias_col)


def _moe_kernel(final, h_ref, wt_ref, wgu_ref, wd_ref, wsgu_ref, wsd_ref, x1_ref, mod_ref, nf_ref, o_ref,
                acc_s, hb_s, wtok_s):
    e = pl.program_id(1)
    n_e = pl.num_programs(1)

    def swiglu(gu):
        return _silu(gu[:, 0:D_EXPERT]) * gu[:, D_EXPERT:2 * D_EXPERT]

    @pl.when(e == 0)
    def _init():
        hb = h_ref[...].astype(BF16)
        hb_s[...] = hb
        wtok_s[...] = _transpose(wt_ref[...])
        acc_s[...] = _dot(swiglu(jnp.dot(hb, wsgu_ref[...], preferred_element_type=F32)), wsd_ref[...])

    wcol = jnp.sum(jnp.where(_iota((1, N_EXPERTS), 1) == e, wtok_s[...], 0.0), axis=1, keepdims=True)
    a = swiglu(_dot(hb_s[...], wgu_ref[...]))
    acc_s[...] += _dot(a, wd_ref[...]) * wcol

    @pl.when(e == n_e - 1)
    def _fin():
        x2 = x1_ref[...] + mod_ref[:, 5 * D_MODEL:6 * D_MODEL] * acc_s[...]
        if final:
            x2 = _rms(x2) * nf_ref[...]
        o_ref[...] = x2


def _moe(layer, h2, wt, wgu, wd, wsgu_b, wsd_b, x1, mod, norm_f, final):
    t = h2.shape[0]
    tm = min(t, 1024)
    per_row = mod.shape[0] != 1
    mod_spec = (pl.BlockSpec((tm, 6 * D_MODEL), lambda i, e: (i, 0)) if per_row
                else pl.BlockSpec((1, 6 * D_MODEL), lambda i, e: (0, 0)))
    return pl.pallas_call(
        functools.partial(_moe_kernel, final),
        grid=(t // tm, N_EXPERTS),
        in_specs=[pl.BlockSpec((tm, D_MODEL), lambda i, e: (i, 0)),
                  pl.BlockSpec((N_EXPERTS, tm), lambda i, e: (0, i)),
                  pl.BlockSpec((None, None, D_MODEL, 2 * D_EXPERT), lambda i, e: (layer, e, 0, 0)),
                  pl.BlockSpec((None, None, D_EXPERT, D_MODEL), lambda i, e: (layer, e, 0, 0)),
                  pl.BlockSpec((D_MODEL, 2 * D_EXPERT), lambda i, e: (0, 0)),
                  pl.BlockSpec((D_EXPERT, D_MODEL), lambda i, e: (0, 0)),
                  pl.BlockSpec((tm, D_MODEL), lambda i, e: (i, 0)),
                  mod_spec,
                  pl.BlockSpec((1, D_MODEL), lambda i, e: (0, 0))],
        out_specs=pl.BlockSpec((tm, D_MODEL), lambda i, e: (i, 0)),
        out_shape=jax.ShapeDtypeStruct((t, D_MODEL), F32),
        scratch_shapes=[pltpu.VMEM((tm, D_MODEL), F32), pltpu.VMEM((tm, D_MODEL), BF16),
                        pltpu.VMEM((tm, N_EXPERTS), F32)],
        compiler_params=_cparams(("arbitrary", "arbitrary")),
        name="moe",
    )(h2, wt, wgu, wd, wsgu_b, wsd_b, x1, mod, norm_f)


MOE_BM = 512
MOE_SPARSE_MIN_TOKENS = 1024


def _pack_bf16_pairs(x):
    n = x.shape[1] // 2
    bits = lax.bitcast_convert_type(x.astype(BF16).astype(F32), jnp.uint32)
    return lax.bitcast_convert_type(bits[:, n:] | (bits[:, :n] >> 16), F32)


def _unpack_bf16_pairs(w):
    u = lax.bitcast_convert_type(w, jnp.uint32)
    return (lax.bitcast_convert_type(u << 16, F32),
            lax.bitcast_convert_type(u & jnp.uint32(0xFFFF0000), F32))


def _router_sparse_kernel(h_ref, wr_ref, rb_ref, eidx_ref, rank_ref, ew_ref, cnt_ref, hp_ref, carry_s):
    tm = h_ref.shape[0]
    hp_ref[...] = _pack_bf16_pairs(h_ref[...])

    @pl.when(pl.program_id(0) == 0)
    def _():
        carry_s[...] = jnp.zeros_like(carry_s)

    s, firsts, picks = _select_experts(h_ref[...], wr_ref[...], rb_ref[...])
    picked = picks[0]
    for pick in picks[1:]:
        picked = picked | pick
    onehot = picked.astype(F32)
    earlier = (_iota((tm, tm), 0) < _iota((tm, tm), 1)).astype(F32)
    carry = carry_s[...]
    rank = _dot(onehot, earlier) + carry[:, 0:1]
    carry_s[...] = carry + jnp.sum(onehot, axis=1, keepdims=True)
    cnt_ref[...] = carry_s[...]

    w = [jnp.sum(jnp.where(pick, s, 0.0), axis=0, keepdims=True) for pick in picks]
    wsum = w[0]
    for wk in w[1:]:
        wsum = wsum + wk
    row8 = _iota((8, tm), 0)
    eidx8 = jnp.zeros((8, tm), jnp.int32)
    rank8 = jnp.zeros((8, tm), jnp.int32)
    ew8 = jnp.zeros((8, tm), F32)
    for k in range(TOP_K):
        rk = jnp.sum(jnp.where(picks[k], rank, 0.0), axis=0, keepdims=True)
        eidx8 = jnp.where(row8 == k, firsts[k], eidx8)
        rank8 = jnp.where(row8 == k, rk.astype(jnp.int32), rank8)
        ew8 = jnp.where(row8 == k, w[k] / wsum * ROUTED_SCALE, ew8)
    eidx_ref[...] = eidx8
    rank_ref[...] = rank8
    ew_ref[...] = ew8


def _router_sparse(h2, w_router_t, router_bias_col):
    t = h2.shape[0]
    tm = 512
    row = lambda dt: jax.ShapeDtypeStruct((8, t), dt)
    return pl.pallas_call(
        _router_sparse_kernel,
        grid=(t // tm,),
        in_specs=[pl.BlockSpec((tm, D_MODEL), lambda i: (i, 0)),
                  pl.BlockSpec((N_EXPERTS, D_MODEL), lambda i: (0, 0)),
                  pl.BlockSpec((N_EXPERTS, 1), lambda i: (0, 0))],
        out_specs=[pl.BlockSpec((8, tm), lambda i: (0, i)), pl.BlockSpec((8, tm), lambda i: (0, i)),
                   pl.BlockSpec((8, tm), lambda i: (0, i)), pl.BlockSpec((N_EXPERTS, 128), lambda i: (0, 0)),
                   pl.BlockSpec((tm, D_MODEL // 2), lambda i: (i, 0))],
        out_shape=[row(jnp.int32), row(jnp.int32), row(F32), jax.ShapeDtypeStruct((N_EXPERTS, 128), F32),
                   jax.ShapeDtypeStruct((t, D_MODEL // 2), F32)],
        scratch_shapes=[pltpu.VMEM((N_EXPERTS, 128), F32)],
        compiler_params=_cparams(("arbitrary",)),
        name="router_sparse",
    )(h2, w_router_t, router_bias_col)


def _plan_kernel(nblk_pad, cnt_ref, eidx_ref, rank_ref, dest_ref, blk_ref):
    tm = eidx_ref.shape[1]
    cnt = cnt_ref[...]
    padded = jnp.floor((cnt + (MOE_BM - 1.0)) * (1.0 / MOE_BM)) * MOE_BM
    tril = (_iota((N_EXPERTS, N_EXPERTS), 0) >= _iota((N_EXPERTS, N_EXPERTS), 1)).astype(F32)
    pad_end = _sel(padded, tril, "mx")
    start_col = (pad_end - padded)[:, 0:1]
    end_col = pad_end[:, 0:1]

    e_iota = _iota((N_EXPERTS, tm), 0)
    row8 = _iota((8, tm), 0)
    eidx = eidx_ref[...]
    rank = rank_ref[...]
    dest = jnp.zeros((8, tm), jnp.int32)
    for k in range(TOP_K):
        base = jnp.sum(jnp.where(e_iota == eidx[k:k + 1, :], start_col, 0.0), axis=0, keepdims=True)
        dest = jnp.where(row8 == k, base.astype(jnp.int32) + rank[k:k + 1, :], dest)
    dest_ref[...] = dest

    first_row = (_iota((N_EXPERTS, nblk_pad), 1) * MOE_BM).astype(F32)
    blk_e = jnp.sum((end_col <= first_row).astype(F32), axis=0, keepdims=True)
    blk_e = jnp.minimum(blk_e, N_EXPERTS - 1.0).astype(jnp.int32)
    n_used = (pad_end[N_EXPERTS - 1:N_EXPERTS, 0:1] * (1.0 / MOE_BM)).astype(jnp.int32)
    row8b = _iota((8, nblk_pad), 0)
    blk_ref[...] = jnp.where(row8b == 0, blk_e, jnp.where(row8b == 1, n_used, 0))


def _plan(cnt, eidx, rank, nblk_pad):
    t = eidx.shape[1]
    tm = min(t, 2048)
    return pl.pallas_call(
        functools.partial(_plan_kernel, nblk_pad),
        grid=(t // tm,),
        in_specs=[pl.BlockSpec((N_EXPERTS, 128), lambda i: (0, 0)),
                  pl.BlockSpec((8, tm), lambda i: (0, i)), pl.BlockSpec((8, tm), lambda i: (0, i))],
        out_specs=[pl.BlockSpec((8, tm), lambda i: (0, i)), pl.BlockSpec((8, nblk_pad), lambda i: (0, 0))],
        out_shape=[jax.ShapeDtypeStruct((8, t), jnp.int32), jax.ShapeDtypeStruct((8, nblk_pad), jnp.int32)],
        compiler_params=_cparams(("arbitrary",)),
        name="moe_plan",
    )(cnt, eidx, rank)


def _sc_mesh():
    return plsc.VectorSubcoreMesh(core_axis_name="core", subcore_axis_name="subcore")


SC_CHUNK = 256
SC_WINDOW = 128
N_CHUNK = D_MODEL // SC_CHUNK


def _sc_scatter_rows(x, idx, n_rows):
    t = x.shape[0]
    n_chunk = x.shape[1] // SC_CHUNK
    flat_idx = [i for per_chunk in idx for i in per_chunk]

    @pl.kernel(out_type=jax.ShapeDtypeStruct((n_chunk * n_rows, SC_CHUNK), x.dtype), mesh=_sc_mesh(),
               scratch_types=[])
    def scatter_kernel(x_hbm, *rest):
        i_hbm, o_hbm = rest[:-1], rest[-1]

        def body(x_vmem, *i_vmem):
            for iv in i_vmem:
                pltpu.sync_copy(x_vmem, o_hbm.at[iv.at[0]])

        for c in range(n_chunk):
            pltpu.emit_pipeline(
                body,
                grid=(t // SC_WINDOW,),
                in_specs=[pl.BlockSpec((SC_WINDOW, SC_CHUNK), lambda i, c=c: (i, c))]
                         + [pl.BlockSpec((1, SC_WINDOW), lambda i: (0, i))] * TOP_K,
                out_specs=[],
                core_axis_name=("core", "subcore"),
                dimension_semantics=(pltpu.PARALLEL,),
            )(x_hbm, *i_hbm[c * TOP_K:(c + 1) * TOP_K])

    return scatter_kernel(x, *flat_idx)


def _sc_gather_rows(y, idx):
    a = idx[0].shape[1]
    n_win = a // SC_WINDOW

    @pl.kernel(out_type=jax.ShapeDtypeStruct((N_CHUNK * a, SC_CHUNK), y.dtype), mesh=_sc_mesh(), scratch_types=[])
    def gather_kernel(y_hbm, *rest):
        i_hbm, o_hbm = rest[:-1], rest[-1]

        def body(i_vmem, o_vmem):
            pltpu.sync_copy(y_hbm.at[i_vmem.at[0]], o_vmem)

        for c in range(N_CHUNK):
            pltpu.emit_pipeline(
                body,
                grid=(n_win,),
                in_specs=[pl.BlockSpec((1, SC_WINDOW), lambda i: (0, i))],
                out_specs=[pl.BlockSpec((SC_WINDOW, SC_CHUNK), lambda i, c=c: (c * n_win + i, 0))],
                core_axis_name=("core", "subcore"),
                dimension_semantics=(pltpu.PARALLEL,),
            )(i_hbm[c], o_hbm)

    return gather_kernel(y, *idx)


def _swiglu(gu):
    return _silu(gu[:, 0:D_EXPERT]) * gu[:, D_EXPERT:2 * D_EXPERT]


def _grouped_kernel(blk_e_ref, n_used_ref, xs_ref, wgu_ref, wd_ref, ys_ref):
    @pl.when(pl.program_id(0) < n_used_ref[0])
    def _():
        gu = None
        for c in range(N_CHUNK // 2):
            lo, hi = _unpack_bf16_pairs(xs_ref[c])
            part = (_dot(lo, wgu_ref[c * SC_CHUNK:(c + 1) * SC_CHUNK, :])
                    + _dot(hi, wgu_ref[D_MODEL // 2 + c * SC_CHUNK:D_MODEL // 2 + (c + 1) * SC_CHUNK, :]))
            gu = part if gu is None else gu + part
        y = _dot(_swiglu(gu), wd_ref[...])
        for c in range(N_CHUNK):
            ys_ref[c] = y[:, c * SC_CHUNK:(c + 1) * SC_CHUNK]


def _grouped(layer, xs, blk_e, n_used, wgu, wd):
    n_rows = xs.shape[1]
    clamp = lambda b, nu: jnp.minimum(b, nu[0] - 1)
    rows_spec = pl.BlockSpec((N_CHUNK, MOE_BM, SC_CHUNK), lambda b, be, nu: (0, clamp(b, nu), 0))
    in_rows_spec = pl.BlockSpec((N_CHUNK // 2, MOE_BM, SC_CHUNK), lambda b, be, nu: (0, clamp(b, nu), 0))
    grid_spec = pltpu.PrefetchScalarGridSpec(
        num_scalar_prefetch=2,
        grid=(n_rows // MOE_BM,),
        in_specs=[in_rows_spec,
                  pl.BlockSpec((None, None, D_MODEL, 2 * D_EXPERT),
                               lambda b, be, nu: (layer, be[clamp(b, nu)], 0, 0)),
                  pl.BlockSpec((None, None, D_EXPERT, D_MODEL),
                               lambda b, be, nu: (layer, be[clamp(b, nu)], 0, 0))],
        out_specs=rows_spec,
    )
    return pl.pallas_call(
        _grouped_kernel,
        grid_spec=grid_spec,
        out_shape=jax.ShapeDtypeStruct((N_CHUNK, n_rows, SC_CHUNK), F32),
        compiler_params=_cparams(("arbitrary",)),
        name="moe_grouped",
    )(blk_e, n_used, xs, wgu, wd)


def _combine_kernel(final, yg_ref, ew_ref, h_ref, wsgu_ref, wsd_ref, x1_ref, mod_ref, nf_ref, o_ref):
    gu = jnp.dot(h_ref[...].astype(BF16), wsgu_ref[...], preferred_element_type=F32)
    acc = _dot(_swiglu(gu), wsd_ref[...])
    wt = _transpose(ew_ref[...])
    routed = []
    for c in range(N_CHUNK):
        part = wt[:, 0:1] * yg_ref[c, 0]
        for k in range(1, TOP_K):
            part = part + wt[:, k:k + 1] * yg_ref[c, k]
        routed.append(part)
    acc = acc + jnp.concatenate(routed, axis=1)
    x2 = x1_ref[...] + mod_ref[:, 5 * D_MODEL:6 * D_MODEL] * acc
    if final:
        x2 = _rms(x2) * nf_ref[...]
    o_ref[...] = x2


def _combine(yg, ew, h2, wsgu_b, wsd_b, x1, mod, norm_f, final):
    t = h2.shape[0]
    tm = 512
    per_row = mod.shape[0] != 1
    mod_spec = (pl.BlockSpec((tm, 6 * D_MODEL), lambda i: (i, 0)) if per_row
                else pl.BlockSpec((1, 6 * D_MODEL), lambda i: (0, 0)))
    return pl.pallas_call(
        functools.partial(_combine_kernel, final),
        grid=(t // tm,),
        in_specs=[pl.BlockSpec((N_CHUNK, TOP_K, tm, SC_CHUNK), lambda i: (0, 0, i, 0)),
                  pl.BlockSpec((8, tm), lambda i: (0, i)),
                  pl.BlockSpec((tm, D_MODEL), lambda i: (i, 0)),
                  pl.BlockSpec((D_MODEL, 2 * D_EXPERT), lambda i: (0, 0)),
                  pl.BlockSpec((D_EXPERT, D_MODEL), lambda i: (0, 0)),
                  pl.BlockSpec((tm, D_MODEL), lambda i: (i, 0)),
                  mod_spec,
                  pl.BlockSpec((1, D_MODEL), lambda i: (0, 0))],
        out_specs=pl.BlockSpec((tm, D_MODEL), lambda i: (i, 0)),
        out_shape=jax.ShapeDtypeStruct((t, D_MODEL), F32),
        compiler_params=_cparams(("arbitrary",)),
        name="moe_combine",
    )(yg, ew, h2, wsgu_b, wsd_b, x1, mod, norm_f)


def _moe_sparse(layer, h2, w_router_t, router_bias_col, wgu, wd, wsgu_b, wsd_b, x1, mod, norm_f, final,
                overlap=None):
    t = h2.shape[0]
    n_blk = t * TOP_K // MOE_BM + N_EXPERTS
    n_rows = n_blk * MOE_BM
    nblk_pad = -(-n_blk // 128) * 128
    eidx, rank, ew, cnt, h2_packed = _router_sparse(h2, w_router_t, router_bias_col)
    dest, blk = _plan(cnt, eidx, rank, nblk_pad)
    per_pick = [[dest[k:k + 1] + c * n_rows for k in range(TOP_K)] for c in range(N_CHUNK)]
    flat = dest[0:TOP_K].reshape(1, TOP_K * t)
    xs = _sc_scatter_rows(h2_packed, per_pick[:N_CHUNK // 2], n_rows).reshape(N_CHUNK // 2, n_rows, SC_CHUNK)
    if overlap is not None:
        xs = overlap(xs)
    ys = _grouped(layer, xs, blk[0], blk[1, 0:1], wgu, wd)
    yg = _sc_gather_rows(ys.reshape(N_CHUNK * n_rows, SC_CHUNK), [flat + c * n_rows for c in range(N_CHUNK)])
    if overlap is not None:
        yg = overlap(yg)
    return _combine(yg.reshape(N_CHUNK, TOP_K, t, SC_CHUNK), ew, h2, wsgu_b, wsd_b, x1, mod, norm_f, final)


def _permute_in_cols(w):
    ml = w[..., 0:ML_COLS]
    fx = w[..., OFF_FX:OFF_GT]
    small = jnp.concatenate([ml[..., 4 * BR_W:], fx[..., 3 * BR_W:]], axis=-1)
    pad = jnp.zeros(w.shape[:-1] + (P_GT - P_SM - small.shape[-1],), w.dtype)
    return jnp.concatenate([ml[..., :4 * BR_W], w[..., OFF_HG:OFF_RT], w[..., OFF_RT:OFF_FX], fx[..., :3 * BR_W],
                            small, pad, w[..., OFF_GT:]], axis=-1)


def _rope_tables(pos):
    half = HEAD_DIM // 2
    inv = ROPE_BASE ** (-jnp.arange(half, dtype=F32) / half)
    ang = pos.astype(F32)[:, None] * inv[None, :]
    cos = jnp.cos(ang)
    sin = jnp.sin(ang)
    cos_h = jnp.concatenate([cos, cos], axis=-1)
    sin_h = jnp.concatenate([-sin, sin], axis=-1)
    return jnp.tile(cos_h, (1, HEADS)), jnp.tile(sin_h, (1, HEADS))


def kernel(x_prompt, x_sample, cache_fox_k, cache_fox_v, cache_fox_logf, state_mlstm_C, state_mlstm_n, state_mlstm_m, state_hgrn_S, state_ret_S, page_table, c_prompt, c_sample, w_ada, b_ada, norm_mix_w, norm_ffn_w, w_in, b_in, hgrn_lb_logits, mlstm_norm_w, hgrn_norm_w, ret_norm_w, w_branch, w_out, w_router, router_bias, w_exp_gu, w_exp_down, w_shared_gu, w_shared_down, norm_f_w):
    depth = w_in.shape[0]
    bp, seq, _ = x_prompt.shape
    db, t_new, _ = x_sample.shape
    n_pool = cache_fox_k.shape[1]
    n_pages = page_table.shape[1]
    past_len = n_pages * PAGE_SIZE
    assert bp == 1 and seq % 128 == 0 and t_new == 8

    w_in_p = _permute_in_cols(w_in).astype(BF16)
    b_in_p = _permute_in_cols(b_in).reshape(depth, 1, P_TOT)
    w_branch_b = w_branch.astype(BF16)
    w_out_b = w_out.astype(BF16)
    wsgu_b = w_shared_gu.astype(BF16)
    wsd_b = w_shared_down.astype(BF16)
    w_router_t = jnp.swapaxes(w_router, 1, 2)
    cache_k = jnp.transpose(cache_fox_k, (0, 1, 3, 4, 2)).reshape(depth, n_pool, BR_W, PAGE_SIZE)
    cache_v = jnp.transpose(cache_fox_v, (0, 1, 3, 4, 2)).reshape(depth, n_pool, BR_W, PAGE_SIZE)
    cache_lf_t = jnp.swapaxes(cache_fox_logf, 2, 3)

    mods = _ada(jnp.concatenate([c_prompt, c_sample], axis=0), w_ada, b_ada)

    cos_p, sin_p = _rope_tables(jnp.arange(seq))
    cos_s, sin_s = _rope_tables(past_len + jnp.arange(t_new))

    def trunk(result, x, mod_of_layer, cos_t, sin_t, init, fox_fn, lc, overlap=None):
        b, l, _ = x.shape
        xt = x.reshape(b * l, D_MODEL)
        c_in, n_in, m_in, sh_in, sr_in = init
        per_layer = []
        for layer in range(depth):
            mod = mod_of_layer(layer)
            kv_t = b == 1 and (b * l) % 128 == 0
            proj, *kv = _inproj(layer, xt, mod, norm_mix_w[layer][None], w_in_p, b_in_p, kv_t)
            proj3 = proj.reshape(b, l, P_TOT)
            if kv_t:
                fox_k, fox_v = (a.reshape(HEADS, HEAD_DIM, l).transpose(2, 0, 1)[None] for a in kv)
            else:
                fox_k, fox_v = (proj3[..., P_FX + i * BR_W:P_FX + (i + 1) * BR_W].reshape(b, l, HEADS, HEAD_DIM)
                                for i in (1, 2))
            br3, lf_rows, cum_t, c_new, n_new, m_new, sh_new, sr_new = _mixers(
                layer, proj3, cos_t, sin_t, c_in[layer], n_in[layer], m_in[layer], sh_in[layer], sr_in[layer],
                hgrn_lb_logits, mlstm_norm_w[layer][None], hgrn_norm_w[layer][None], ret_norm_w[layer][None], lc)
            yield br3
            fox = fox_fn(layer, proj3, cum_t)
            yield fox
            x1, h2 = _merge(br3.reshape(b * l, 3 * BR_W), fox.reshape(b * l, BR_W), proj, xt, mod,
                            w_branch_b[layer], w_out_b[layer], norm_ffn_w[layer][None])
            final = layer == depth - 1
            if b * l >= MOE_SPARSE_MIN_TOKENS:
                xt = _moe_sparse(layer, h2, w_router_t[layer], router_bias[layer][:, None], w_exp_gu, w_exp_down,
                                 wsgu_b[layer], wsd_b[layer], x1, mod, norm_f_w[None], final, overlap)
            else:
                wt = _router(h2, w_router_t[layer], router_bias[layer][:, None])
                xt = _moe(layer, h2, wt, w_exp_gu, w_exp_down, wsgu_b[layer], wsd_b[layer], x1, mod, norm_f_w[None],
                          final)
            yield xt
            per_layer.append((
                fox_k,
                fox_v,
                lf_rows[..., SM_FLF:SM_FLF + HEADS],
                c_new.reshape(b, HEADS, HEAD_DIM, HEAD_DIM),
                n_new.reshape(b, HEADS, HEAD_DIM),
                m_new[:, 0, :HEADS],
                sh_new.reshape(b, HEADS, HEAD_DIM, HEAD_DIM),
                sr_new.reshape(b, HEADS, HEAD_DIM, HEAD_DIM)))
        stacked = tuple(jnp.stack([p[i] for p in per_layer]) for i in range(8))
        result.append((xt.reshape(b, l, D_MODEL),) + stacked)

    def head_major(s, b):
        return s.astype(F32).reshape(depth, b, BR_W, HEAD_DIM)

    zero_state = jnp.zeros((depth, bp, BR_W, HEAD_DIM), F32)
    prompt_init = (zero_state, jnp.zeros((depth, bp, 1, BR_W), F32), jnp.zeros((depth, bp, 1, 128), F32),
                   zero_state, zero_state)
    m_pad = jnp.pad(state_mlstm_m.astype(F32), ((0, 0), (0, 0), (0, 128 - HEADS))).reshape(depth, db, 1, 128)
    sample_init = (head_major(state_mlstm_C, db), state_mlstm_n.astype(F32).reshape(depth, db, 1, BR_W), m_pad,
                   head_major(state_hgrn_S, db), head_major(state_ret_S, db))
    n_slots = next(n for n in (32, 16, 8, 4, 2, 1) if n_pages % n == 0)
    prompt_out, sample_out = [], []
    sample = trunk(sample_out, x_sample, lambda layer: jnp.repeat(mods[layer, 1:], t_new, axis=0), cos_s, sin_s,
                   sample_init,
                   lambda layer, proj3, cum_t: _fox_decode(layer, proj3, cache_k, cache_v, cache_lf_t,
                                                           page_table, n_slots),
                   t_new)
    stages_per_copy = iter((2, 2, 1, 1))

    def overlap(arr):
        token = None
        for _ in range(next(stages_per_copy, 0)):
            token = next(sample, token)
        if token is None:
            return arr
        return lax.optimization_barrier((arr, token))[0]

    prompt = trunk(prompt_out, x_prompt, lambda layer: mods[layer, 0:1], cos_p, sin_p, prompt_init,
                   lambda layer, proj3, cum_t: _fox_prompt(proj3[0], cum_t[0], min(seq, 512)),
                   128, overlap)
    for gen in (prompt, sample):
        for _ in gen:
            pass
    prompt_out, sample_out = prompt_out[0], sample_out[0]

    return (prompt_out[0], sample_out[0]) + prompt_out[1:] + sample_out[1:]
```

```python
import functools
import math

import numpy as np
import jax
import jax.numpy as jnp
from jax import lax
from jax.experimental import pallas as pl
from jax.experimental.pallas import tpu as pltpu
from jax.experimental.pallas import tpu_sc as plsc

F32 = jnp.float32
BF16 = jnp.bfloat16

D_MODEL = 1024
N_BRANCH = 4
BR_W = 256
HEAD_DIM = 64
HEADS = 4
ROPE_BASE = 10000.0
RMS_EPS = 1e-6
N_EXPERTS = 64
TOP_K = 6
N_GROUPS = 8
GROUP_SIZE = N_EXPERTS // N_GROUPS
TOPK_GROUPS = 4
D_EXPERT = 256
ROUTED_SCALE = 2.5
PAGE_SIZE = 128
QK_SCALE = HEAD_DIM ** -0.5
LOG2E = math.log2(math.e)

ML_COLS = 4 * BR_W + 2 * HEADS
OFF_HG = ML_COLS
OFF_RT = OFF_HG + 4 * BR_W
OFF_FX = OFF_RT + 4 * BR_W
OFF_GT = OFF_FX + 3 * BR_W + HEADS
N_IN = OFF_GT + N_BRANCH * D_MODEL

P_ML, P_HG, P_RT, P_FX, P_SM, P_GT, P_TOT = 0, 1024, 2048, 3072, 3840, 4096, 8192
SM_IG, SM_MLF, SM_FLF = 0, 4, 8

NEG = -1e30
VMEM_LIMIT = 56 * 1024 * 1024


def _cparams(sem):
    return pltpu.CompilerParams(dimension_semantics=sem, vmem_limit_bytes=VMEM_LIMIT)


def _dot(a, b):
    return jnp.dot(a.astype(BF16), b.astype(BF16), preferred_element_type=F32)


def _dot_nt(a, b):
    return lax.dot_general(a.astype(BF16), b.astype(BF16), (((1,), (1,)), ((), ())), preferred_element_type=F32)


def _dot_tn(a, b):
    return lax.dot_general(a.astype(BF16), b.astype(BF16), (((0,), (0,)), ((), ())), preferred_element_type=F32)


def _split3(x):
    x1 = x.astype(BF16)
    r1 = x - x1.astype(F32)
    x2 = r1.astype(BF16)
    x3 = (r1 - x2.astype(F32)).astype(BF16)
    return x1, x2, x3


def _sel(x, m01, dims, terms=3):
    m = m01.astype(BF16)
    x1, x2, x3 = _split3(x)
    if dims == "mx":
        f = lambda xi: jnp.dot(m, xi, preferred_element_type=F32)
    elif dims == "xm":
        f = lambda xi: jnp.dot(xi, m, preferred_element_type=F32)
    elif dims == "xmT":
        f = lambda xi: lax.dot_general(xi, m, (((1,), (1,)), ((), ())), preferred_element_type=F32)
    else:
        f = lambda xi: lax.dot_general(m, xi, (((1,), (1,)), ((), ())), preferred_element_type=F32)
    return (f(x1) + f(x2)) + f(x3) if terms == 3 else f(x1) + f(x2)


def _iota(shape, dim):
    return lax.broadcasted_iota(jnp.int32, shape, dim)


def _eye(n):
    return (_iota((n, n), 0) == _iota((n, n), 1)).astype(F32)


def _transpose(x):
    if x.shape[0] % 128 == 0 and x.shape[1] % 128 == 0:
        return x.T
    return _sel(x, _eye(x.shape[1]), "mxT")


def _sigmoid(x):
    return jax.nn.sigmoid(x)


def _silu(x):
    return x * jax.nn.sigmoid(x)


def _log_sigmoid(x):
    return jnp.minimum(x, 0.0) - jnp.log1p(jnp.exp(-jnp.abs(x)))


def _head_masks(n=BR_W):
    lane = _iota((1, n), 1) >> 6
    return [lane == h for h in range(HEADS)]


def _block_diag_mask():
    return (_iota((BR_W, BR_W), 0) >> 6) == (_iota((BR_W, BR_W), 1) >> 6)


def _per_head_lanes(vals, masks):
    out = jnp.where(masks[0], vals[0], 0.0)
    for h in range(1, HEADS):
        out = jnp.where(masks[h], vals[h], out)
    return out


def _rms(x, eps=RMS_EPS):
    return x * lax.rsqrt(jnp.mean(x * x, axis=-1, keepdims=True) + eps)


def _head_norm(o, gain, bdf):
    ms = _sel(o * o, bdf, "xm", terms=2) * (1.0 / HEAD_DIM)
    return o * lax.rsqrt(ms + RMS_EPS) * gain


def _ada_kernel(c_ref, w_ref, b_ref, o_ref):
    o_ref[...] = _dot(_silu(c_ref[...]), w_ref[...]) + b_ref[...]


def _ada(c_all, w_ada, b_ada):
    depth = w_ada.shape[0]
    n_c = c_all.shape[0]
    tn = 1536
    return pl.pallas_call(
        _ada_kernel,
        grid=(depth, 6 * D_MODEL // tn),
        in_specs=[pl.BlockSpec((n_c, D_MODEL), lambda l, j: (0, 0)),
                  pl.BlockSpec((None, D_MODEL, tn), lambda l, j: (l, 0, j)),
                  pl.BlockSpec((None, 1, tn), lambda l, j: (l, 0, j))],
        out_specs=pl.BlockSpec((None, n_c, tn), lambda l, j: (l, 0, j)),
        out_shape=jax.ShapeDtypeStruct((depth, n_c, 6 * D_MODEL), F32),
        compiler_params=_cparams(("arbitrary", "arbitrary")),
        name="ada",
    )(c_all, w_ada, b_ada.reshape(depth, 1, 6 * D_MODEL))


def _inproj_kernel(kv_t, x_ref, mod_ref, nw_ref, w_ref, b_ref, o_ref, *rest):
    h_scr = rest[-1]
    j = pl.program_id(1)

    @pl.when(j == 0)
    def _():
        h = _rms(x_ref[...]) * nw_ref[...]
        h = h * (1.0 + mod_ref[:, D_MODEL:2 * D_MODEL]) + mod_ref[:, 0:D_MODEL]
        h_scr[...] = h.astype(BF16)

    res = jnp.dot(h_scr[...], w_ref[...], preferred_element_type=F32) + b_ref[...]
    o_ref[...] = res

    if kv_t:
        kt_ref, vt_ref = rest[0], rest[1]
        c0 = P_FX % res.shape[1]

        @pl.when(j == P_FX // res.shape[1])
        def _():
            kt_ref[...] = res[:, c0 + BR_W:c0 + 2 * BR_W].T
            vt_ref[...] = res[:, c0 + 2 * BR_W:c0 + 3 * BR_W].T


def _inproj(layer, x, mod, norm_w, w_p, b_p, kv_t):
    t = x.shape[0]
    tm = min(t, 1024)
    tn = 2048
    per_row = mod.shape[0] != 1
    mod_spec = (pl.BlockSpec((tm, 6 * D_MODEL), lambda i, j: (i, 0)) if per_row
                else pl.BlockSpec((1, 6 * D_MODEL), lambda i, j: (0, 0)))
    out_specs = [pl.BlockSpec((tm, tn), lambda i, j: (i, j))]
    out_shape = [jax.ShapeDtypeStruct((t, P_TOT), F32)]
    if kv_t:
        out_specs += [pl.BlockSpec((BR_W, tm), lambda i, j: (0, i))] * 2
        out_shape += [jax.ShapeDtypeStruct((BR_W, t), F32)] * 2
    return pl.pallas_call(
        functools.partial(_inproj_kernel, kv_t),
        grid=(t // tm, P_TOT // tn),
        in_specs=[pl.BlockSpec((tm, D_MODEL), lambda i, j: (i, 0)),
                  mod_spec,
                  pl.BlockSpec((1, D_MODEL), lambda i, j: (0, 0)),
                  pl.BlockSpec((None, D_MODEL, tn), lambda i, j: (layer, 0, j)),
                  pl.BlockSpec((None, 1, tn), lambda i, j: (layer, 0, j))],
        out_specs=out_specs,
        out_shape=out_shape,
        scratch_shapes=[pltpu.VMEM((tm, D_MODEL), BF16)],
        compiler_params=_cparams(("arbitrary", "arbitrary")),
        name="inproj",
    )(x, mod, norm_w, w_p, b_p)


def _mixers_kernel(layer, lc, sc,
                   ml_ref, hg_ref, rt_ref, sm_ref, cos_ref, sin_ref,
                   c0_ref, n0_ref, m0_ref, sh0_ref, sr0_ref, lbl_ref, gml_ref, ghg_ref, grt_ref,
                   br_ref, lf_ref, cumt_ref, cout_ref, nout_ref, mout_ref, shout_ref, srout_ref,
                   cbd, sht, srbd, n_s, m_s, carry_s, ohg_s):
    c = pl.program_id(1)
    n_c = pl.num_programs(1)
    masks = _head_masks()
    bd = _block_diag_mask()
    bdf = bd.astype(F32)
    tile = ((_iota((HEAD_DIM, BR_W), 1) & (HEAD_DIM - 1)) == _iota((HEAD_DIM, BR_W), 0)).astype(F32)

    @pl.when(c == 0)
    def _init():
        def expand(ref):
            return jnp.where(bd, _sel(ref[...], tile, "xm"), 0.0)
        cbd[...] = expand(c0_ref)
        sht[...] = _transpose(expand(sh0_ref))
        srbd[...] = expand(sr0_ref)
        n_s[...] = n0_ref[...]
        m_s[...] = m0_ref[...]
        carry_s[...] = jnp.zeros_like(carry_s)

    row = _iota((lc, lc), 0)
    col = _iota((lc, lc), 1)
    causal = row >= col
    tril = causal.astype(F32)

    sm = sm_ref[:, 0:128]
    lane128 = _iota((1, 128), 1)
    sm2 = jnp.where((lane128 >= SM_MLF) & (lane128 < SM_FLF + HEADS), _log_sigmoid(sm), sm)
    cum = _sel(sm2, tril, "mx")
    sm2_t = _transpose(sm2)
    cum_t = _transpose(cum)
    lf_ref[...] = sm2
    cum_tg = cum_t + carry_s[...]
    carry_s[...] = cum_tg[:, lc - 1:lc]
    cumt_ref[...] = cum_tg[SM_FLF:SM_FLF + 8, :]

    q = ml_ref[:, 0:BR_W]
    k = ml_ref[:, BR_W:2 * BR_W] * QK_SCALE
    v = ml_ref[:, 2 * BR_W:3 * BR_W]
    og = ml_ref[:, 3 * BR_W:4 * BR_W]
    n_row = n_s[...]
    m_row = m_s[...]
    q_c = _dot_nt(q, cbd[...])
    h_all = jnp.zeros((lc, BR_W), F32)
    w_lanes = jnp.zeros((lc, BR_W), F32)
    decay_lanes = jnp.zeros((1, BR_W), F32)
    m_new_row = jnp.zeros((1, 128), F32)
    for h in range(HEADS):
        ig_c = sm2[:, SM_IG + h:SM_IG + h + 1]
        b_c = cum[:, SM_MLF + h:SM_MLF + h + 1]
        ig_r = sm2_t[SM_IG + h:SM_IG + h + 1, :]
        b_r = cum_t[SM_MLF + h:SM_MLF + h + 1, :]
        m_prev = m_row[:, h:h + 1]
        dmat = jnp.where(causal, b_c - b_r + ig_r, NEG)
        m_inter = b_c + m_prev
        m_t = jnp.maximum(m_inter, jnp.max(dmat, axis=1, keepdims=True))
        w_intra = jnp.exp(dmat - m_t)
        w_inter = jnp.exp(m_inter - m_t)
        qh = jnp.where(masks[h], q, 0.0)
        a = _dot_nt(qh, k) * w_intra
        num = _dot(a, v) + w_inter * q_c
        den = jnp.sum(a, axis=1, keepdims=True) + w_inter * jnp.sum(qh * n_row, axis=1, keepdims=True)
        hh = num / jnp.maximum(jnp.abs(den), jnp.exp(-m_t))
        h_all = jnp.where(masks[h], hh, h_all)
        m_new = m_t[lc - 1:lc, :]
        b_last = b_c[lc - 1:lc, :]
        w_s = jnp.exp(b_last - b_c + ig_c - m_new)
        decay = jnp.exp(b_last + m_prev - m_new)
        w_lanes = jnp.where(masks[h], w_s, w_lanes)
        decay_lanes = jnp.where(masks[h], decay, decay_lanes)
        m_new_row = jnp.where(lane128 == h, m_new, m_new_row)
    kw = k * w_lanes
    cbd[...] = cbd[...] * decay_lanes + jnp.where(bd, _dot_tn(v * w_lanes, k), 0.0)
    n_s[...] = n_row * decay_lanes + jnp.sum(kw, axis=0, keepdims=True)
    m_s[...] = m_new_row
    out_ml = _head_norm(h_all, gml_ref[...], bdf) * _sigmoid(og)

    lbl = lbl_ref[...]
    pr = jnp.exp(lbl - jnp.max(lbl, axis=0, keepdims=True))
    pr = pr / jnp.sum(pr, axis=0, keepdims=True)
    lb = jnp.zeros((1, BR_W), F32)
    for i in range(1, layer + 1):
        lb = lb + pr[i:i + 1, :]
    tril_sc = (_iota((sc, sc), 0) >= _iota((sc, sc), 1)).astype(F32)
    s_idx = _iota((sc, BR_W), 0)

    def hg_body(i, carry):
        r0 = pl.multiple_of(i * sc, sc)
        hq = hg_ref[pl.ds(r0, sc), 0:BR_W]
        hf = hg_ref[pl.ds(r0, sc), BR_W:2 * BR_W]
        vi = hg_ref[pl.ds(r0, sc), 2 * BR_W:3 * BR_W]
        qi = _silu(hq)
        f = lb + (1.0 - lb) * _sigmoid(hf)
        ki = 1.0 - f
        bi = _sel(jnp.log(f), tril_sc, "mx")
        st = sht[...]
        o_inter = _dot_nt(qi * jnp.exp(bi), st)
        rows = []
        for t in range(sc):
            e_t = jnp.exp(jnp.where(s_idx <= t, bi[t:t + 1, :] - bi, NEG)) * qi[t:t + 1, :] * ki
            rows.append(e_t)
        e_all = jnp.concatenate(rows, axis=0)
        r_all = _dot(e_all, bdf)
        o_diag = jnp.sum(r_all.reshape(sc, sc, BR_W) * vi[None, :, :], axis=1)
        ohg_s[pl.ds(r0, sc), :] = o_inter + o_diag
        b_last = bi[sc - 1:sc, :]
        sht[...] = st * jnp.exp(b_last) + jnp.where(bd, _dot_tn(vi, ki * jnp.exp(b_last - bi)), 0.0)
        return carry

    lax.fori_loop(0, lc // sc, hg_body, 0, unroll=True)
    out_hg = _head_norm(ohg_s[...], ghg_ref[...], bdf) * _silu(hg_ref[:, 3 * BR_W:4 * BR_W])

    cosv = cos_ref[...]
    sinv = sin_ref[...]
    lane = _iota((1, BR_W), 1)
    first_half = (lane & (HEAD_DIM - 1)) < (HEAD_DIM // 2)

    def rope(x):
        partner = jnp.where(first_half, pltpu.roll(x, BR_W - HEAD_DIM // 2, 1), pltpu.roll(x, HEAD_DIM // 2, 1))
        return x * cosv + partner * sinv

    rq = rope(rt_ref[:, 0:BR_W])
    rk = rope(rt_ref[:, BR_W:2 * BR_W]) * QK_SCALE
    rv = rt_ref[:, 2 * BR_W:3 * BR_W]
    lg = [math.log1p(-(2.0 ** (-5.0 - h))) for h in range(HEADS)]
    lg_lanes = _per_head_lanes([jnp.full((1, 1), g, F32) for g in lg], masks)
    diff = (row - col).astype(F32)
    o_rt = jnp.zeros((lc, BR_W), F32)
    for h in range(HEADS):
        dec = jnp.exp(jnp.where(causal, diff * lg[h], NEG))
        a = _dot_nt(jnp.where(masks[h], rq, 0.0), rk) * dec
        o_rt = jnp.where(masks[h], _dot(a, rv), o_rt)
    t_idx = _iota((lc, BR_W), 0).astype(F32)
    o_rt = o_rt + _dot(rq * jnp.exp((t_idx + 1.0) * lg_lanes), srbd[...])
    w_ret = jnp.exp((lc - 1.0 - t_idx) * lg_lanes)
    srbd[...] = srbd[...] * jnp.exp(lc * lg_lanes) + jnp.where(bd, _dot_tn(rk * w_ret, rv), 0.0)
    out_rt = _head_norm(o_rt, grt_ref[...], bdf) * _silu(rt_ref[:, 3 * BR_W:4 * BR_W])

    br_ref[:, 0:BR_W] = out_ml
    br_ref[:, BR_W:2 * BR_W] = out_hg
    br_ref[:, 2 * BR_W:3 * BR_W] = out_rt

    @pl.when(c == n_c - 1)
    def _fin():
        def compact(x):
            return _sel(x, tile, "xmT")
        cout_ref[...] = compact(cbd[...])
        shout_ref[...] = compact(_transpose(sht[...]))
        srout_ref[...] = compact(srbd[...])
        nout_ref[...] = n_s[...]
        mout_ref[...] = m_s[...]


def _mixers(layer, proj, cos_t, sin_t, c0, n0, m0, sh0, sr0, lb_logits, g_ml, g_hg, g_rt, lc):
    b, l, _ = proj.shape
    sc = min(16, lc)
    n_c = l // lc
    depth = lb_logits.shape[0]
    cb = lambda blk: pl.BlockSpec((None, lc, 1024), lambda bi, ci, blk=blk: (bi, ci, blk))
    st_spec = pl.BlockSpec((None, BR_W, HEAD_DIM), lambda bi, ci: (bi, 0, 0))
    row_spec = lambda n: pl.BlockSpec((None, 1, n), lambda bi, ci: (bi, 0, 0))
    full = lambda r, cc: pl.BlockSpec((r, cc), lambda bi, ci: (0, 0))
    outs = pl.pallas_call(
        functools.partial(_mixers_kernel, layer, lc, sc),
        grid=(b, n_c),
        in_specs=[cb(0), cb(1), cb(2),
                  pl.BlockSpec((None, lc, 256), lambda bi, ci: (bi, ci, P_SM // 256)),
                  pl.BlockSpec((lc, BR_W), lambda bi, ci: (ci, 0)),
                  pl.BlockSpec((lc, BR_W), lambda bi, ci: (ci, 0)),
                  st_spec, row_spec(BR_W), row_spec(128), st_spec, st_spec,
                  full(depth, BR_W), full(1, BR_W), full(1, BR_W), full(1, BR_W)],
        out_specs=[pl.BlockSpec((None, lc, 3 * BR_W), lambda bi, ci: (bi, ci, 0)),
                   pl.BlockSpec((None, lc, 128), lambda bi, ci: (bi, ci, 0)),
                   pl.BlockSpec((None, 8, lc), lambda bi, ci: (bi, 0, ci)),
                   st_spec, row_spec(BR_W), row_spec(128), st_spec, st_spec],
        out_shape=[jax.ShapeDtypeStruct((b, l, 3 * BR_W), F32),
                   jax.ShapeDtypeStruct((b, l, 128), F32),
                   jax.ShapeDtypeStruct((b, 8, l), F32),
                   jax.ShapeDtypeStruct((b, BR_W, HEAD_DIM), F32),
                   jax.ShapeDtypeStruct((b, 1, BR_W), F32),
                   jax.ShapeDtypeStruct((b, 1, 128), F32),
                   jax.ShapeDtypeStruct((b, BR_W, HEAD_DIM), F32),
                   jax.ShapeDtypeStruct((b, BR_W, HEAD_DIM), F32)],
        scratch_shapes=[pltpu.VMEM((BR_W, BR_W), F32), pltpu.VMEM((BR_W, BR_W), F32), pltpu.VMEM((BR_W, BR_W), F32),
                        pltpu.VMEM((1, BR_W), F32), pltpu.VMEM((1, 128), F32), pltpu.VMEM((128, 1), F32),
                        pltpu.VMEM((lc, BR_W), F32)],
        compiler_params=_cparams(("arbitrary", "arbitrary")),
        name="mixers",
    )(proj, proj, proj, proj, cos_t, sin_t, c0, n0, m0, sh0, sr0, lb_logits, g_ml, g_hg, g_rt)
    return outs


def _fox_prompt_kernel(tq, tk, strip, qi_ref, kj_ref, q_ref, k_ref, v_ref, ck_ref, o_ref,
                       m_s, l_s, alpha_s, acc_s, s_scr, p_scr):
    step = pl.program_id(0)
    i = qi_ref[step]
    j = kj_ref[step]
    masks = _head_masks()

    @pl.when(j == 0)
    def _init():
        m_s[...] = jnp.full_like(m_s, NEG)
        l_s[...] = jnp.zeros_like(l_s)
        acc_s[...] = jnp.zeros_like(acc_s)

    q = q_ref[...] * (QK_SCALE * LOG2E)
    kb = k_ref[...].astype(BF16)
    vb = v_ref[...].astype(BF16)
    nck = ck_ref[...] * (-LOG2E)
    n_rep = tk // 128

    def attend(diagonal):
        col = _iota((strip, tk), 1)
        row = _iota((strip, tk), 0)
        for h in range(HEADS):
            s_scr[h] = _dot_nt(jnp.where(masks[h], q, 0.0), kb) + nck[h:h + 1, :]
            parts = []
            for r0 in range(0, tq, strip):
                sc = s_scr[h, pl.ds(r0, strip), :]
                if diagonal:
                    sc = jnp.where(row + (r0 + i * tq - j * tk) >= col, sc, NEG)
                    s_scr[h, pl.ds(r0, strip), :] = sc
                parts.append(jnp.max(sc, axis=1, keepdims=True))
            m_old = m_s[h]
            m_new = jnp.maximum(m_old, jnp.broadcast_to(jnp.concatenate(parts, axis=0), (tq, 128)))
            m_s[h] = m_new
            alpha_s[h] = jnp.exp2(m_old - m_new)
            for r0 in range(0, tq, strip):
                rows = pl.ds(r0, strip)
                m_rep = jnp.concatenate([m_s[h, rows, :]] * n_rep, axis=1)
                p = jnp.exp2(s_scr[h, rows, :] - m_rep)
                p_scr[h, rows, :] = p.astype(BF16)
                psum = p[:, 0:128]
                for c in range(1, n_rep):
                    psum = psum + p[:, c * 128:(c + 1) * 128]
                l_s[h, rows, :] = alpha_s[h, rows, :] * l_s[h, rows, :] + psum
            alpha = alpha_s[h]
            acc_s[h] = jnp.concatenate([alpha, alpha], axis=1) * acc_s[h] + jnp.dot(p_scr[h], vb,
                                                                                   preferred_element_type=F32)

    last = (j + 1) * tk >= (i + 1) * tq

    @pl.when(jnp.logical_not(last))
    def _off_diagonal():
        attend(False)

    @pl.when(last)
    def _diagonal():
        attend(True)
        out = jnp.zeros((tq, BR_W), F32)
        for h in range(HEADS):
            out = jnp.where(masks[h], acc_s[h] / jnp.sum(l_s[h], axis=1, keepdims=True), out)
        o_ref[...] = out


def _fox_prompt(proj, cum_t, tq):
    t = proj.shape[0]
    tk = 2 * tq if t % (2 * tq) == 0 else tq
    nq = t // tq
    n_kj = [((i + 1) * tq - 1) // tk + 1 for i in range(nq)]
    qi = np.concatenate([np.full(n, i, np.int32) for i, n in enumerate(n_kj)])
    kj = np.concatenate([np.arange(n, dtype=np.int32) for n in n_kj])
    cq = P_FX // BR_W
    grid_spec = pltpu.PrefetchScalarGridSpec(
        num_scalar_prefetch=2,
        grid=(len(qi),),
        in_specs=[pl.BlockSpec((tq, BR_W), lambda s, qi, kj: (qi[s], cq)),
                  pl.BlockSpec((tk, BR_W), lambda s, qi, kj: (kj[s], cq + 1)),
                  pl.BlockSpec((tk, BR_W), lambda s, qi, kj: (kj[s], cq + 2)),
                  pl.BlockSpec((8, tk), lambda s, qi, kj: (0, kj[s]))],
        out_specs=pl.BlockSpec((tq, BR_W), lambda s, qi, kj: (qi[s], 0)),
        scratch_shapes=[pltpu.VMEM((HEADS, tq, 128), F32), pltpu.VMEM((HEADS, tq, 128), F32),
                        pltpu.VMEM((HEADS, tq, 128), F32), pltpu.VMEM((HEADS, tq, BR_W), F32),
                        pltpu.VMEM((HEADS, tq, tk), F32), pltpu.VMEM((HEADS, tq, tk), BF16)],
    )
    return pl.pallas_call(
        functools.partial(_fox_prompt_kernel, tq, tk, min(16384 // tk, tq)),
        grid_spec=grid_spec,
        out_shape=jax.ShapeDtypeStruct((t, BR_W), F32),
        compiler_params=_cparams(("arbitrary",)),
        name="fox_prompt",
    )(jnp.asarray(qi), jnp.asarray(kj), proj, proj, proj, cum_t)


def _fox_decode_kernel(t_new, n_slots, pt_ref, q_ref, k_ref, v_ref, sm_ref, *rest):
    kt_pages = rest[0:n_slots]
    vt_pages = rest[n_slots:2 * n_slots]
    lf_pages = rest[2 * n_slots:3 * n_slots]
    o_ref = rest[3 * n_slots]
    m_s, l_s, acc_s, carry_s = rest[3 * n_slots + 1:]
    g = pl.program_id(1)
    n_g = pl.num_programs(1)
    masks = _head_masks()
    rows = HEADS * t_new

    q = q_ref[...] * QK_SCALE
    qbd = jnp.concatenate([jnp.where(masks[h], q, 0.0) for h in range(HEADS)], axis=0).astype(BF16)

    def per_head_rows(x):
        return jnp.concatenate([jnp.broadcast_to(x[h:h + 1, :], (t_new, x.shape[1])) for h in range(HEADS)], axis=0)

    def softmax_step(sc):
        m_old = m_s[...]
        m_new = jnp.maximum(m_old, jnp.max(sc, axis=1, keepdims=True))
        alpha = jnp.exp(m_old - m_new)
        p = jnp.exp(sc - m_new)
        l_s[...] = alpha * l_s[...] + jnp.sum(p, axis=1, keepdims=True)
        m_s[...] = m_new
        return alpha, p.astype(BF16)

    @pl.when(g == 0)
    def _new_rows():
        pad = jnp.zeros((PAGE_SIZE - t_new, BR_W), F32)
        kn = jnp.concatenate([k_ref[...], pad], axis=0).astype(BF16)
        vn = jnp.concatenate([v_ref[...], pad], axis=0).astype(BF16)
        smp = jnp.concatenate([sm_ref[:, 0:128], jnp.zeros((PAGE_SIZE - t_new, 128), F32)], axis=0)
        lane128 = _iota((1, 128), 1)
        lf = jnp.where((lane128 >= SM_FLF) & (lane128 < SM_FLF + HEADS), _log_sigmoid(smp), 0.0)
        tril = (_iota((PAGE_SIZE, PAGE_SIZE), 0) >= _iota((PAGE_SIZE, PAGE_SIZE), 1)).astype(F32)
        cum_t = _transpose(_sel(lf, tril, "mx"))
        t_of_row = _iota((rows, PAGE_SIZE), 0) & (t_new - 1)
        s_of_col = _iota((rows, PAGE_SIZE), 1)
        sc = _dot_nt(qbd, kn) - per_head_rows(cum_t[SM_FLF:SM_FLF + HEADS, :])
        sc = jnp.where(s_of_col <= t_of_row, sc, NEG)
        m_s[...] = jnp.full_like(m_s, NEG)
        l_s[...] = jnp.zeros_like(l_s)
        carry_s[...] = jnp.zeros_like(carry_s)
        _, p = softmax_step(sc)
        acc_s[...] = jnp.dot(p, vn, preferred_element_type=F32)

    strict = (_iota((PAGE_SIZE, PAGE_SIZE), 0) > _iota((PAGE_SIZE, PAGE_SIZE), 1)).astype(F32)
    strict_ones = jnp.concatenate([strict, jnp.ones((PAGE_SIZE, PAGE_SIZE), F32)], axis=1)
    later = carry_s[...]
    sufs = [None] * n_slots
    for slot in reversed(range(n_slots)):
        r = _sel(lf_pages[slot][...], strict_ones, "xm")
        sufs[slot] = r[:, 0:PAGE_SIZE] + later
        later = later + r[:, PAGE_SIZE:2 * PAGE_SIZE]
    carry_s[...] = later
    bias = per_head_rows(jnp.concatenate(sufs, axis=1))
    sc = jnp.concatenate([jnp.dot(qbd, kt_pages[slot][...].astype(BF16), preferred_element_type=F32)
                          for slot in range(n_slots)], axis=1) + bias
    alpha, p = softmax_step(sc)
    pv = _dot_nt(p[:, 0:PAGE_SIZE], vt_pages[0][...])
    for slot in range(1, n_slots):
        pv += _dot_nt(p[:, slot * PAGE_SIZE:(slot + 1) * PAGE_SIZE], vt_pages[slot][...])
    acc_s[...] = alpha * acc_s[...] + pv

    @pl.when(g == n_g - 1)
    def _fin():
        res = acc_s[...] / l_s[...]
        out = jnp.zeros((t_new, BR_W), F32)
        for h in range(HEADS):
            out = jnp.where(masks[h], res[h * t_new:(h + 1) * t_new, :], out)
        o_ref[...] = out


def _fox_decode(layer, proj, cache_kt, cache_vt, cache_lf_t, page_table, n_slots):
    db, t_new, _ = proj.shape
    n_pages = page_table.shape[1]
    n_g = n_pages // n_slots
    cq = P_FX // BR_W

    def page_idx(slot):
        return lambda b, g, pt: (layer, pt[b * n_pages + (n_g - 1 - g) * n_slots + slot], 0, 0)

    kv_specs = [pl.BlockSpec((None, None, BR_W, PAGE_SIZE), page_idx(s)) for s in range(n_slots)]
    lf_specs = [pl.BlockSpec((None, None, HEADS, PAGE_SIZE), page_idx(s)) for s in range(n_slots)]
    rows = HEADS * t_new
    grid_spec = pltpu.PrefetchScalarGridSpec(
        num_scalar_prefetch=1,
        grid=(db, n_g),
        in_specs=[pl.BlockSpec((None, t_new, BR_W), lambda b, g, pt: (b, 0, cq)),
                  pl.BlockSpec((None, t_new, BR_W), lambda b, g, pt: (b, 0, cq + 1)),
                  pl.BlockSpec((None, t_new, BR_W), lambda b, g, pt: (b, 0, cq + 2)),
                  pl.BlockSpec((None, t_new, BR_W), lambda b, g, pt: (b, 0, P_SM // BR_W))]
                 + kv_specs + kv_specs + lf_specs,
        out_specs=pl.BlockSpec((None, t_new, BR_W), lambda b, g, pt: (b, 0, 0)),
        scratch_shapes=[pltpu.VMEM((rows, 1), F32), pltpu.VMEM((rows, 1), F32), pltpu.VMEM((rows, BR_W), F32),
                        pltpu.VMEM((HEADS, PAGE_SIZE), F32)],
    )
    return pl.pallas_call(
        functools.partial(_fox_decode_kernel, t_new, n_slots),
        grid_spec=grid_spec,
        out_shape=jax.ShapeDtypeStruct((db, t_new, BR_W), F32),
        compiler_params=_cparams(("arbitrary", "arbitrary")),
        name="fox_decode",
    )(page_table.reshape(-1), proj, proj, proj, proj,
      *([cache_kt] * n_slots), *([cache_vt] * n_slots), *([cache_lf_t] * n_slots))


def _merge_kernel(br_ref, fx_ref, g0_ref, g1_ref, g2_ref, g3_ref, x_ref, mod_ref, wb_ref, wo_ref, nw_ref,
                  x1_ref, h2_ref):
    merged = _sigmoid(g0_ref[...]) * _dot(br_ref[:, 0:BR_W], wb_ref[0])
    merged += _sigmoid(g1_ref[...]) * _dot(br_ref[:, BR_W:2 * BR_W], wb_ref[1])
    merged += _sigmoid(g2_ref[...]) * _dot(br_ref[:, 2 * BR_W:3 * BR_W], wb_ref[2])
    merged += _sigmoid(g3_ref[...]) * _dot(fx_ref[...], wb_ref[3])
    x1 = x_ref[...] + mod_ref[:, 2 * D_MODEL:3 * D_MODEL] * _dot(merged, wo_ref[...])
    x1_ref[...] = x1
    h2 = _rms(x1) * nw_ref[...]
    h2_ref[...] = h2 * (1.0 + mod_ref[:, 4 * D_MODEL:5 * D_MODEL]) + mod_ref[:, 3 * D_MODEL:4 * D_MODEL]


def _merge(br3, fox, proj, x, mod, w_branch_b, w_out_b, norm_w):
    t = x.shape[0]
    tm = min(t, 512) if mod.shape[0] == 1 else min(t, 256)
    per_row = mod.shape[0] != 1
    mod_spec = (pl.BlockSpec((tm, 6 * D_MODEL), lambda i: (i, 0)) if per_row
                else pl.BlockSpec((1, 6 * D_MODEL), lambda i: (0, 0)))
    gate = lambda b: pl.BlockSpec((tm, D_MODEL), lambda i, b=b: (i, P_GT // D_MODEL + b))
    return pl.pallas_call(
        _merge_kernel,
        grid=(t // tm,),
        in_specs=[pl.BlockSpec((tm, 3 * BR_W), lambda i: (i, 0)),
                  pl.BlockSpec((tm, BR_W), lambda i: (i, 0)),
                  gate(0), gate(1), gate(2), gate(3),
                  pl.BlockSpec((tm, D_MODEL), lambda i: (i, 0)),
                  mod_spec,
                  pl.BlockSpec((N_BRANCH, BR_W, D_MODEL), lambda i: (0, 0, 0)),
                  pl.BlockSpec((D_MODEL, D_MODEL), lambda i: (0, 0)),
                  pl.BlockSpec((1, D_MODEL), lambda i: (0, 0))],
        out_specs=[pl.BlockSpec((tm, D_MODEL), lambda i: (i, 0)),
                   pl.BlockSpec((tm, D_MODEL), lambda i: (i, 0))],
        out_shape=[jax.ShapeDtypeStruct((t, D_MODEL), F32), jax.ShapeDtypeStruct((t, D_MODEL), F32)],
        compiler_params=_cparams(("arbitrary",)),
        name="merge",
    )(br3, fox, proj, proj, proj, proj, x, mod, w_branch_b, w_out_b, norm_w)


def _select_experts(h, wr_t, rb_col):
    tm = h.shape[0]
    logits = _dot_nt(wr_t, h)
    s = _sigmoid(logits)
    sel = s + rb_col
    ninf = -jnp.inf
    sub = _iota((GROUP_SIZE, tm), 0)
    gsc = []
    for g in range(N_GROUPS):
        blk = sel[g * GROUP_SIZE:(g + 1) * GROUP_SIZE, :]
        m1 = jnp.max(blk, axis=0, keepdims=True)
        first = jnp.min(jnp.where(blk == m1, sub, GROUP_SIZE), axis=0, keepdims=True)
        m2 = jnp.max(jnp.where(sub == first, ninf, blk), axis=0, keepdims=True)
        gsc.append(m1 + m2)
    chosen = [jnp.zeros((1, tm), jnp.bool_) for _ in range(N_GROUPS)]
    for _ in range(TOPK_GROUPS):
        mx = gsc[0]
        for g in range(1, N_GROUPS):
            mx = jnp.maximum(mx, gsc[g])
        taken = jnp.zeros((1, tm), jnp.bool_)
        for g in range(N_GROUPS):
            pick = (gsc[g] == mx) & jnp.logical_not(taken)
            taken = taken | pick
            chosen[g] = chosen[g] | pick
            gsc[g] = jnp.where(pick, ninf, gsc[g])
    selm = jnp.concatenate(
        [jnp.where(chosen[g], sel[g * GROUP_SIZE:(g + 1) * GROUP_SIZE, :], ninf) for g in range(N_GROUPS)], axis=0)
    eidx = _iota((N_EXPERTS, tm), 0)
    firsts, picks = [], []
    for _ in range(TOP_K):
        mx = jnp.max(selm, axis=0, keepdims=True)
        first = jnp.min(jnp.where(selm == mx, eidx, N_EXPERTS), axis=0, keepdims=True)
        pick = eidx == first
        selm = jnp.where(pick, ninf, selm)
        firsts.append(first)
        picks.append(pick)
    return s, firsts, picks


def _router_kernel(h_ref, wr_ref, rb_ref, wt_ref):
    s, _, picks = _select_experts(h_ref[...], wr_ref[...], rb_ref[...])
    picked = picks[0]
    for pick in picks[1:]:
        picked = picked | pick
    w = jnp.where(picked, s, 0.0)
    wt_ref[...] = w / jnp.sum(w, axis=0, keepdims=True) * ROUTED_SCALE


def _router(h2, w_router_t, router_bias_col):
    t = h2.shape[0]
    tm = min(t, 512)
    return pl.pallas_call(
        _router_kernel,
        grid=(t // tm,),
        in_specs=[pl.BlockSpec((tm, D_MODEL), lambda i: (i, 0)),
                  pl.BlockSpec((N_EXPERTS, D_MODEL), lambda i: (0, 0)),
                  pl.BlockSpec((N_EXPERTS, 1), lambda i: (0, 0))],
        out_specs=pl.BlockSpec((N_EXPERTS, tm), lambda i: (0, i)),
        out_shape=jax.ShapeDtypeStruct((N_EXPERTS, t), F32),
        compiler_params=_cparams(("arbitrary",)),
        name="router",
    )(h2, w_router_t, router_bias_col)


def _moe_kernel(final, h_ref, wt_ref, wgu_ref, wd_ref, wsgu_ref, wsd_ref, x1_ref, mod_ref, nf_ref, o_ref,
                acc_s, hb_s, wtok_s):
    e = pl.program_id(1)
    n_e = pl.num_programs(1)

    def swiglu(gu):
        return _silu(gu[:, 0:D_EXPERT]) * gu[:, D_EXPERT:2 * D_EXPERT]

    @pl.when(e == 0)
    def _init():
        hb = h_ref[...].astype(BF16)
        hb_s[...] = hb
        wtok_s[...] = _transpose(wt_ref[...])
        acc_s[...] = _dot(swiglu(jnp.dot(hb, wsgu_ref[...], preferred_element_type=F32)), wsd_ref[...])

    wcol = jnp.sum(jnp.where(_iota((1, N_EXPERTS), 1) == e, wtok_s[...], 0.0), axis=1, keepdims=True)
    a = swiglu(_dot(hb_s[...], wgu_ref[...]))
    acc_s[...] += _dot(a, wd_ref[...]) * wcol

    @pl.when(e == n_e - 1)
    def _fin():
        x2 = x1_ref[...] + mod_ref[:, 5 * D_MODEL:6 * D_MODEL] * acc_s[...]
        if final:
            x2 = _rms(x2) * nf_ref[...]
        o_ref[...] = x2


def _moe(layer, h2, wt, wgu, wd, wsgu_b, wsd_b, x1, mod, norm_f, final):
    t = h2.shape[0]
    tm = min(t, 1024)
    per_row = mod.shape[0] != 1
    mod_spec = (pl.BlockSpec((tm, 6 * D_MODEL), lambda i, e: (i, 0)) if per_row
                else pl.BlockSpec((1, 6 * D_MODEL), lambda i, e: (0, 0)))
    return pl.pallas_call(
        functools.partial(_moe_kernel, final),
        grid=(t // tm, N_EXPERTS),
        in_specs=[pl.BlockSpec((tm, D_MODEL), lambda i, e: (i, 0)),
                  pl.BlockSpec((N_EXPERTS, tm), lambda i, e: (0, i)),
                  pl.BlockSpec((None, None, D_MODEL, 2 * D_EXPERT), lambda i, e: (layer, e, 0, 0)),
                  pl.BlockSpec((None, None, D_EXPERT, D_MODEL), lambda i, e: (layer, e, 0, 0)),
                  pl.BlockSpec((D_MODEL, 2 * D_EXPERT), lambda i, e: (0, 0)),
                  pl.BlockSpec((D_EXPERT, D_MODEL), lambda i, e: (0, 0)),
                  pl.BlockSpec((tm, D_MODEL), lambda i, e: (i, 0)),
                  mod_spec,
                  pl.BlockSpec((1, D_MODEL), lambda i, e: (0, 0))],
        out_specs=pl.BlockSpec((tm, D_MODEL), lambda i, e: (i, 0)),
        out_shape=jax.ShapeDtypeStruct((t, D_MODEL), F32),
        scratch_shapes=[pltpu.VMEM((tm, D_MODEL), F32), pltpu.VMEM((tm, D_MODEL), BF16),
                        pltpu.VMEM((tm, N_EXPERTS), F32)],
        compiler_params=_cparams(("arbitrary", "arbitrary")),
        name="moe",
    )(h2, wt, wgu, wd, wsgu_b, wsd_b, x1, mod, norm_f)


MOE_BM = 512
MOE_SPARSE_MIN_TOKENS = 1024


def _pack_bf16_pairs(x):
    n = x.shape[1] // 2
    bits = lax.bitcast_convert_type(x.astype(BF16).astype(F32), jnp.uint32)
    return lax.bitcast_convert_type(bits[:, n:] | (bits[:, :n] >> 16), F32)


def _unpack_bf16_pairs(w):
    u = lax.bitcast_convert_type(w, jnp.uint32)
    return (lax.bitcast_convert_type(u << 16, F32),
            lax.bitcast_convert_type(u & jnp.uint32(0xFFFF0000), F32))


def _router_sparse_kernel(h_ref, wr_ref, rb_ref, eidx_ref, rank_ref, ew_ref, cnt_ref, hp_ref, carry_s):
    tm = h_ref.shape[0]
    hp_ref[...] = _pack_bf16_pairs(h_ref[...])

    @pl.when(pl.program_id(0) == 0)
    def _():
        carry_s[...] = jnp.zeros_like(carry_s)

    s, firsts, picks = _select_experts(h_ref[...], wr_ref[...], rb_ref[...])
    picked = picks[0]
    for pick in picks[1:]:
        picked = picked | pick
    onehot = picked.astype(F32)
    earlier = (_iota((tm, tm), 0) < _iota((tm, tm), 1)).astype(F32)
    carry = carry_s[...]
    rank = _dot(onehot, earlier) + carry[:, 0:1]
    carry_s[...] = carry + jnp.sum(onehot, axis=1, keepdims=True)
    cnt_ref[...] = carry_s[...]

    w = [jnp.sum(jnp.where(pick, s, 0.0), axis=0, keepdims=True) for pick in picks]
    wsum = w[0]
    for wk in w[1:]:
        wsum = wsum + wk
    row8 = _iota((8, tm), 0)
    eidx8 = jnp.zeros((8, tm), jnp.int32)
    rank8 = jnp.zeros((8, tm), jnp.int32)
    ew8 = jnp.zeros((8, tm), F32)
    for k in range(TOP_K):
        rk = jnp.sum(jnp.where(picks[k], rank, 0.0), axis=0, keepdims=True)
        eidx8 = jnp.where(row8 == k, firsts[k], eidx8)
        rank8 = jnp.where(row8 == k, rk.astype(jnp.int32), rank8)
        ew8 = jnp.where(row8 == k, w[k] / wsum * ROUTED_SCALE, ew8)
    eidx_ref[...] = eidx8
    rank_ref[...] = rank8
    ew_ref[...] = ew8


def _router_sparse(h2, w_router_t, router_bias_col):
    t = h2.shape[0]
    tm = 512
    row = lambda dt: jax.ShapeDtypeStruct((8, t), dt)
    return pl.pallas_call(
        _router_sparse_kernel,
        grid=(t // tm,),
        in_specs=[pl.BlockSpec((tm, D_MODEL), lambda i: (i, 0)),
                  pl.BlockSpec((N_EXPERTS, D_MODEL), lambda i: (0, 0)),
                  pl.BlockSpec((N_EXPERTS, 1), lambda i: (0, 0))],
        out_specs=[pl.BlockSpec((8, tm), lambda i: (0, i)), pl.BlockSpec((8, tm), lambda i: (0, i)),
                   pl.BlockSpec((8, tm), lambda i: (0, i)), pl.BlockSpec((N_EXPERTS, 128), lambda i: (0, 0)),
                   pl.BlockSpec((tm, D_MODEL // 2), lambda i: (i, 0))],
        out_shape=[row(jnp.int32), row(jnp.int32), row(F32), jax.ShapeDtypeStruct((N_EXPERTS, 128), F32),
                   jax.ShapeDtypeStruct((t, D_MODEL // 2), F32)],
        scratch_shapes=[pltpu.VMEM((N_EXPERTS, 128), F32)],
        compiler_params=_cparams(("arbitrary",)),
        name="router_sparse",
    )(h2, w_router_t, router_bias_col)


def _plan_kernel(nblk_pad, cnt_ref, eidx_ref, rank_ref, dest_ref, blk_ref):
    tm = eidx_ref.shape[1]
    cnt = cnt_ref[...]
    padded = jnp.floor((cnt + (MOE_BM - 1.0)) * (1.0 / MOE_BM)) * MOE_BM
    tril = (_iota((N_EXPERTS, N_EXPERTS), 0) >= _iota((N_EXPERTS, N_EXPERTS), 1)).astype(F32)
    pad_end = _sel(padded, tril, "mx")
    start_col = (pad_end - padded)[:, 0:1]
    end_col = pad_end[:, 0:1]

    e_iota = _iota((N_EXPERTS, tm), 0)
    row8 = _iota((8, tm), 0)
    eidx = eidx_ref[...]
    rank = rank_ref[...]
    dest = jnp.zeros((8, tm), jnp.int32)
    for k in range(TOP_K):
        base = jnp.sum(jnp.where(e_iota == eidx[k:k + 1, :], start_col, 0.0), axis=0, keepdims=True)
        dest = jnp.where(row8 == k, base.astype(jnp.int32) + rank[k:k + 1, :], dest)
    dest_ref[...] = dest

    first_row = (_iota((N_EXPERTS, nblk_pad), 1) * MOE_BM).astype(F32)
    blk_e = jnp.sum((end_col <= first_row).astype(F32), axis=0, keepdims=True)
    blk_e = jnp.minimum(blk_e, N_EXPERTS - 1.0).astype(jnp.int32)
    n_used = (pad_end[N_EXPERTS - 1:N_EXPERTS, 0:1] * (1.0 / MOE_BM)).astype(jnp.int32)
    row8b = _iota((8, nblk_pad), 0)
    blk_ref[...] = jnp.where(row8b == 0, blk_e, jnp.where(row8b == 1, n_used, 0))


def _plan(cnt, eidx, rank, nblk_pad):
    t = eidx.shape[1]
    tm = min(t, 2048)
    return pl.pallas_call(
        functools.partial(_plan_kernel, nblk_pad),
        grid=(t // tm,),
        in_specs=[pl.BlockSpec((N_EXPERTS, 128), lambda i: (0, 0)),
                  pl.BlockSpec((8, tm), lambda i: (0, i)), pl.BlockSpec((8, tm), lambda i: (0, i))],
        out_specs=[pl.BlockSpec((8, tm), lambda i: (0, i)), pl.BlockSpec((8, nblk_pad), lambda i: (0, 0))],
        out_shape=[jax.ShapeDtypeStruct((8, t), jnp.int32), jax.ShapeDtypeStruct((8, nblk_pad), jnp.int32)],
        compiler_params=_cparams(("arbitrary",)),
        name="moe_plan",
    )(cnt, eidx, rank)


def _sc_mesh():
    return plsc.VectorSubcoreMesh(core_axis_name="core", subcore_axis_name="subcore")


SC_CHUNK = 256
SC_WINDOW = 128
N_CHUNK = D_MODEL // SC_CHUNK


def _sc_scatter_rows(x, idx, n_rows):
    t = x.shape[0]
    n_chunk = x.shape[1] // SC_CHUNK
    flat_idx = [i for per_chunk in idx for i in per_chunk]

    @pl.kernel(out_type=jax.ShapeDtypeStruct((n_chunk * n_rows, SC_CHUNK), x.dtype), mesh=_sc_mesh(),
               scratch_types=[])
    def scatter_kernel(x_hbm, *rest):
        i_hbm, o_hbm = rest[:-1], rest[-1]

        def body(x_vmem, *i_vmem):
            for iv in i_vmem:
                pltpu.sync_copy(x_vmem, o_hbm.at[iv.at[0]])

        for c in range(n_chunk):
            pltpu.emit_pipeline(
                body,
                grid=(t // SC_WINDOW,),
                in_specs=[pl.BlockSpec((SC_WINDOW, SC_CHUNK), lambda i, c=c: (i, c))]
                         + [pl.BlockSpec((1, SC_WINDOW), lambda i: (0, i))] * TOP_K,
                out_specs=[],
                core_axis_name=("core", "subcore"),
                dimension_semantics=(pltpu.PARALLEL,),
            )(x_hbm, *i_hbm[c * TOP_K:(c + 1) * TOP_K])

    return scatter_kernel(x, *flat_idx)


def _sc_gather_rows(y, idx):
    a = idx[0].shape[1]
    n_win = a // SC_WINDOW

    @pl.kernel(out_type=jax.ShapeDtypeStruct((N_CHUNK * a, SC_CHUNK), y.dtype), mesh=_sc_mesh(), scratch_types=[])
    def gather_kernel(y_hbm, *rest):
        i_hbm, o_hbm = rest[:-1], rest[-1]

        def body(i_vmem, o_vmem):
            pltpu.sync_copy(y_hbm.at[i_vmem.at[0]], o_vmem)

        for c in range(N_CHUNK):
            pltpu.emit_pipeline(
                body,
                grid=(n_win,),
                in_specs=[pl.BlockSpec((1, SC_WINDOW), lambda i: (0, i))],
                out_specs=[pl.BlockSpec((SC_WINDOW, SC_CHUNK), lambda i, c=c: (c * n_win + i, 0))],
                core_axis_name=("core", "subcore"),
                dimension_semantics=(pltpu.PARALLEL,),
            )(i_hbm[c], o_hbm)

    return gather_kernel(y, *idx)


def _swiglu(gu):
    return _silu(gu[:, 0:D_EXPERT]) * gu[:, D_EXPERT:2 * D_EXPERT]


def _grouped_kernel(blk_e_ref, n_used_ref, xs_ref, wgu_ref, wd_ref, ys_ref):
    @pl.when(pl.program_id(0) < n_used_ref[0])
    def _():
        gu = None
        for c in range(N_CHUNK // 2):
            lo, hi = _unpack_bf16_pairs(xs_ref[c])
            part = (_dot(lo, wgu_ref[c * SC_CHUNK:(c + 1) * SC_CHUNK, :])
                    + _dot(hi, wgu_ref[D_MODEL // 2 + c * SC_CHUNK:D_MODEL // 2 + (c + 1) * SC_CHUNK, :]))
            gu = part if gu is None else gu + part
        y = _dot(_swiglu(gu), wd_ref[...])
        for c in range(N_CHUNK):
            ys_ref[c] = y[:, c * SC_CHUNK:(c + 1) * SC_CHUNK]


def _grouped(layer, xs, blk_e, n_used, wgu, wd):
    n_rows = xs.shape[1]
    clamp = lambda b, nu: jnp.minimum(b, nu[0] - 1)
    rows_spec = pl.BlockSpec((N_CHUNK, MOE_BM, SC_CHUNK), lambda b, be, nu: (0, clamp(b, nu), 0))
    in_rows_spec = pl.BlockSpec((N_CHUNK // 2, MOE_BM, SC_CHUNK), lambda b, be, nu: (0, clamp(b, nu), 0))
    grid_spec = pltpu.PrefetchScalarGridSpec(
        num_scalar_prefetch=2,
        grid=(n_rows // MOE_BM,),
        in_specs=[in_rows_spec,
                  pl.BlockSpec((None, None, D_MODEL, 2 * D_EXPERT),
                               lambda b, be, nu: (layer, be[clamp(b, nu)], 0, 0)),
                  pl.BlockSpec((None, None, D_EXPERT, D_MODEL),
                               lambda b, be, nu: (layer, be[clamp(b, nu)], 0, 0))],
        out_specs=rows_spec,
    )
    return pl.pallas_call(
        _grouped_kernel,
        grid_spec=grid_spec,
        out_shape=jax.ShapeDtypeStruct((N_CHUNK, n_rows, SC_CHUNK), F32),
        compiler_params=_cparams(("arbitrary",)),
        name="moe_grouped",
    )(blk_e, n_used, xs, wgu, wd)


def _combine_kernel(final, yg_ref, ew_ref, h_ref, wsgu_ref, wsd_ref, x1_ref, mod_ref, nf_ref, o_ref):
    gu = jnp.dot(h_ref[...].astype(BF16), wsgu_ref[...], preferred_element_type=F32)
    acc = _dot(_swiglu(gu), wsd_ref[...])
    wt = _transpose(ew_ref[...])
    routed = []
    for c in range(N_CHUNK):
        part = wt[:, 0:1] * yg_ref[c, 0]
        for k in range(1, TOP_K):
            part = part + wt[:, k:k + 1] * yg_ref[c, k]
        routed.append(part)
    acc = acc + jnp.concatenate(routed, axis=1)
    x2 = x1_ref[...] + mod_ref[:, 5 * D_MODEL:6 * D_MODEL] * acc
    if final:
        x2 = _rms(x2) * nf_ref[...]
    o_ref[...] = x2


def _combine(yg, ew, h2, wsgu_b, wsd_b, x1, mod, norm_f, final):
    t = h2.shape[0]
    tm = 512
    per_row = mod.shape[0] != 1
    mod_spec = (pl.BlockSpec((tm, 6 * D_MODEL), lambda i: (i, 0)) if per_row
                else pl.BlockSpec((1, 6 * D_MODEL), lambda i: (0, 0)))
    return pl.pallas_call(
        functools.partial(_combine_kernel, final),
        grid=(t // tm,),
        in_specs=[pl.BlockSpec((N_CHUNK, TOP_K, tm, SC_CHUNK), lambda i: (0, 0, i, 0)),
                  pl.BlockSpec((8, tm), lambda i: (0, i)),
                  pl.BlockSpec((tm, D_MODEL), lambda i: (i, 0)),
                  pl.BlockSpec((D_MODEL, 2 * D_EXPERT), lambda i: (0, 0)),
                  pl.BlockSpec((D_EXPERT, D_MODEL), lambda i: (0, 0)),
                  pl.BlockSpec((tm, D_MODEL), lambda i: (i, 0)),
                  mod_spec,
                  pl.BlockSpec((1, D_MODEL), lambda i: (0, 0))],
        out_specs=pl.BlockSpec((tm, D_MODEL), lambda i: (i, 0)),
        out_shape=jax.ShapeDtypeStruct((t, D_MODEL), F32),
        compiler_params=_cparams(("arbitrary",)),
        name="moe_combine",
    )(yg, ew, h2, wsgu_b, wsd_b, x1, mod, norm_f)


def _moe_sparse(layer, h2, w_router_t, router_bias_col, wgu, wd, wsgu_b, wsd_b, x1, mod, norm_f, final,
                overlap=None):
    t = h2.shape[0]
    n_blk = t * TOP_K // MOE_BM + N_EXPERTS
    n_rows = n_blk * MOE_BM
    nblk_pad = -(-n_blk // 128) * 128
    eidx, rank, ew, cnt, h2_packed = _router_sparse(h2, w_router_t, router_bias_col)
    dest, blk = _plan(cnt, eidx, rank, nblk_pad)
    per_pick = [[dest[k:k + 1] + c * n_rows for k in range(TOP_K)] for c in range(N_CHUNK)]
    flat = dest[0:TOP_K].reshape(1, TOP_K * t)
    xs = _sc_scatter_rows(h2_packed, per_pick[:N_CHUNK // 2], n_rows).reshape(N_CHUNK // 2, n_rows, SC_CHUNK)
    if overlap is not None:
        xs = overlap(xs)
    ys = _grouped(layer, xs, blk[0], blk[1, 0:1], wgu, wd)
    yg = _sc_gather_rows(ys.reshape(N_CHUNK * n_rows, SC_CHUNK), [flat + c * n_rows for c in range(N_CHUNK)])
    if overlap is not None:
        yg = overlap(yg)
    return _combine(yg.reshape(N_CHUNK, TOP_K, t, SC_CHUNK), ew, h2, wsgu_b, wsd_b, x1, mod, norm_f, final)


def _permute_in_cols(w):
    ml = w[..., 0:ML_COLS]
    fx = w[..., OFF_FX:OFF_GT]
    small = jnp.concatenate([ml[..., 4 * BR_W:], fx[..., 3 * BR_W:]], axis=-1)
    pad = jnp.zeros(w.shape[:-1] + (P_GT - P_SM - small.shape[-1],), w.dtype)
    return jnp.concatenate([ml[..., :4 * BR_W], w[..., OFF_HG:OFF_RT], w[..., OFF_RT:OFF_FX], fx[..., :3 * BR_W],
                            small, pad, w[..., OFF_GT:]], axis=-1)


def _rope_tables(pos):
    half = HEAD_DIM // 2
    inv = ROPE_BASE ** (-jnp.arange(half, dtype=F32) / half)
    ang = pos.astype(F32)[:, None] * inv[None, :]
    cos = jnp.cos(ang)
    sin = jnp.sin(ang)
    cos_h = jnp.concatenate([cos, cos], axis=-1)
    sin_h = jnp.concatenate([-sin, sin], axis=-1)
    return jnp.tile(cos_h, (1, HEADS)), jnp.tile(sin_h, (1, HEADS))


def kernel(x_prompt, x_sample, cache_fox_k, cache_fox_v, cache_fox_logf, state_mlstm_C, state_mlstm_n, state_mlstm_m, state_hgrn_S, state_ret_S, page_table, c_prompt, c_sample, w_ada, b_ada, norm_mix_w, norm_ffn_w, w_in, b_in, hgrn_lb_logits, mlstm_norm_w, hgrn_norm_w, ret_norm_w, w_branch, w_out, w_router, router_bias, w_exp_gu, w_exp_down, w_shared_gu, w_shared_down, norm_f_w):
    depth = w_in.shape[0]
    bp, seq, _ = x_prompt.shape
    db, t_new, _ = x_sample.shape
    n_pool = cache_fox_k.shape[1]
    n_pages = page_table.shape[1]
    past_len = n_pages * PAGE_SIZE
    assert bp == 1 and seq % 128 == 0 and t_new == 8

    w_in_p = _permute_in_cols(w_in).astype(BF16)
    b_in_p = _permute_in_cols(b_in).reshape(depth, 1, P_TOT)
    w_branch_b = w_branch.astype(BF16)
    w_out_b = w_out.astype(BF16)
    wsgu_b = w_shared_gu.astype(BF16)
    wsd_b = w_shared_down.astype(BF16)
    w_router_t = jnp.swapaxes(w_router, 1, 2)
    cache_k = jnp.transpose(cache_fox_k, (0, 1, 3, 4, 2)).reshape(depth, n_pool, BR_W, PAGE_SIZE)
    cache_v = jnp.transpose(cache_fox_v, (0, 1, 3, 4, 2)).reshape(depth, n_pool, BR_W, PAGE_SIZE)
    cache_lf_t = jnp.swapaxes(cache_fox_logf, 2, 3)

    mods = _ada(jnp.concatenate([c_prompt, c_sample], axis=0), w_ada, b_ada)

    cos_p, sin_p = _rope_tables(jnp.arange(seq))
    cos_s, sin_s = _rope_tables(past_len + jnp.arange(t_new))

    def trunk(result, x, mod_of_layer, cos_t, sin_t, init, fox_fn, lc, overlap=None):
        b, l, _ = x.shape
        xt = x.reshape(b * l, D_MODEL)
        c_in, n_in, m_in, sh_in, sr_in = init
        per_layer = []
        for layer in range(depth):
            mod = mod_of_layer(layer)
            kv_t = b == 1 and (b * l) % 128 == 0
            proj, *kv = _inproj(layer, xt, mod, norm_mix_w[layer][None], w_in_p, b_in_p, kv_t)
            proj3 = proj.reshape(b, l, P_TOT)
            if kv_t:
                fox_k, fox_v = (a.reshape(HEADS, HEAD_DIM, l).transpose(2, 0, 1)[None] for a in kv)
            else:
                fox_k, fox_v = (proj3[..., P_FX + i * BR_W:P_FX + (i + 1) * BR_W].reshape(b, l, HEADS, HEAD_DIM)
                                for i in (1, 2))
            br3, lf_rows, cum_t, c_new, n_new, m_new, sh_new, sr_new = _mixers(
                layer, proj3, cos_t, sin_t, c_in[layer], n_in[layer], m_in[layer], sh_in[layer], sr_in[layer],
                hgrn_lb_logits, mlstm_norm_w[layer][None], hgrn_norm_w[layer][None], ret_norm_w[layer][None], lc)
            yield br3
            fox = fox_fn(layer, proj3, cum_t)
            yield fox
            x1, h2 = _merge(br3.reshape(b * l, 3 * BR_W), fox.reshape(b * l, BR_W), proj, xt, mod,
                            w_branch_b[layer], w_out_b[layer], norm_ffn_w[layer][None])
            final = layer == depth - 1
            if b * l >= MOE_SPARSE_MIN_TOKENS:
                xt = _moe_sparse(layer, h2, w_router_t[layer], router_bias[layer][:, None], w_exp_gu, w_exp_down,
                                 wsgu_b[layer], wsd_b[layer], x1, mod, norm_f_w[None], final, overlap)
            else:
                wt = _router(h2, w_router_t[layer], router_bias[layer][:, None])
                xt = _moe(layer, h2, wt, w_exp_gu, w_exp_down, wsgu_b[layer], wsd_b[layer], x1, mod, norm_f_w[None],
                          final)
            yield xt
            per_layer.append((
                fox_k,
                fox_v,
                lf_rows[..., SM_FLF:SM_FLF + HEADS],
                c_new.reshape(b, HEADS, HEAD_DIM, HEAD_DIM),
                n_new.reshape(b, HEADS, HEAD_DIM),
                m_new[:, 0, :HEADS],
                sh_new.reshape(b, HEADS, HEAD_DIM, HEAD_DIM),
                sr_new.reshape(b, HEADS, HEAD_DIM, HEAD_DIM)))
        stacked = tuple(jnp.stack([p[i] for p in per_layer]) for i in range(8))
        result.append((xt.reshape(b, l, D_MODEL),) + stacked)

    def head_major(s, b):
        return s.astype(F32).reshape(depth, b, BR_W, HEAD_DIM)

    zero_state = jnp.zeros((depth, bp, BR_W, HEAD_DIM), F32)
    prompt_init = (zero_state, jnp.zeros((depth, bp, 1, BR_W), F32), jnp.zeros((depth, bp, 1, 128), F32),
                   zero_state, zero_state)
    m_pad = jnp.pad(state_mlstm_m.astype(F32), ((0, 0), (0, 0), (0, 128 - HEADS))).reshape(depth, db, 1, 128)
    sample_init = (head_major(state_mlstm_C, db), state_mlstm_n.astype(F32).reshape(depth, db, 1, BR_W), m_pad,
                   head_major(state_hgrn_S, db), head_major(state_ret_S, db))
    n_slots = next(n for n in (32, 16, 8, 4, 2, 1) if n_pages % n == 0)
    prompt_out, sample_out = [], []
    sample = trunk(sample_out, x_sample, lambda layer: jnp.repeat(mods[layer, 1:], t_new, axis=0), cos_s, sin_s,
                   sample_init,
                   lambda layer, proj3, cum_t: _fox_decode(layer, proj3, cache_k, cache_v, cache_lf_t,
                                                           page_table, n_slots),
                   t_new)
    stages_per_copy = iter((2, 2, 1, 1))

    def overlap(arr):
        token = None
        for _ in range(next(stages_per_copy, 0)):
            token = next(sample, token)
        if token is None:
            return arr
        return lax.optimization_barrier((arr, token))[0]

    prompt = trunk(prompt_out, x_prompt, lambda layer: mods[layer, 0:1], cos_p, sin_p, prompt_init,
                   lambda layer, proj3, cum_t: _fox_prompt(proj3[0], cum_t[0], min(seq, 512)),
                   128, overlap)
    for gen in (prompt, sample):
        for _ in gen:
            pass
    prompt_out, sample_out = prompt_out[0], sample_out[0]

    return (prompt_out[0], sample_out[0]) + prompt_out[1:] + sample_out[1:]
```

```python
import functools
import math

import numpy as np
import jax
import jax.numpy as jnp
from jax import lax
from jax.experimental import pallas as pl
from jax.experimental.pallas import tpu as pltpu
from jax.experimental.pallas import tpu_sc as plsc

F32 = jnp.float32
BF16 = jnp.bfloat16

D_MODEL = 1024
N_BRANCH = 4
BR_W = 256
HEAD_DIM = 64
HEADS = 4
ROPE_BASE = 10000.0
RMS_EPS = 1e-6
N_EXPERTS = 64
TOP_K = 6
N_GROUPS = 8
GROUP_SIZE = N_EXPERTS // N_GROUPS
TOPK_GROUPS = 4
D_EXPERT = 256
ROUTED_SCALE = 2.5
PAGE_SIZE = 128
QK_SCALE = HEAD_DIM ** -0.5
LOG2E = math.log2(math.e)

ML_COLS = 4 * BR_W + 2 * HEADS
OFF_HG = ML_COLS
OFF_RT = OFF_HG + 4 * BR_W
OFF_FX = OFF_RT + 4 * BR_W
OFF_GT = OFF_FX + 3 * BR_W + HEADS
N_IN = OFF_GT + N_BRANCH * D_MODEL

P_ML, P_HG, P_RT, P_FX, P_SM, P_GT, P_TOT = 0, 1024, 2048, 3072, 3840, 4096, 8192
SM_IG, SM_MLF, SM_FLF = 0, 4, 8

NEG = -1e30
VMEM_LIMIT = 56 * 1024 * 1024


def _cparams(sem):
    return pltpu.CompilerParams(dimension_semantics=sem, vmem_limit_bytes=VMEM_LIMIT)


def _dot(a, b):
    return jnp.dot(a.astype(BF16), b.astype(BF16), preferred_element_type=F32)


def _dot_nt(a, b):
    return lax.dot_general(a.astype(BF16), b.astype(BF16), (((1,), (1,)), ((), ())), preferred_element_type=F32)


def _dot_tn(a, b):
    return lax.dot_general(a.astype(BF16), b.astype(BF16), (((0,), (0,)), ((), ())), preferred_element_type=F32)


def _split3(x):
    x1 = x.astype(BF16)
    r1 = x - x1.astype(F32)
    x2 = r1.astype(BF16)
    x3 = (r1 - x2.astype(F32)).astype(BF16)
    return x1, x2, x3


def _sel(x, m01, dims, terms=3):
    m = m01.astype(BF16)
    x1, x2, x3 = _split3(x)
    if dims == "mx":
        f = lambda xi: jnp.dot(m, xi, preferred_element_type=F32)
    elif dims == "xm":
        f = lambda xi: jnp.dot(xi, m, preferred_element_type=F32)
    elif dims == "xmT":
        f = lambda xi: lax.dot_general(xi, m, (((1,), (1,)), ((), ())), preferred_element_type=F32)
    else:
        f = lambda xi: lax.dot_general(m, xi, (((1,), (1,)), ((), ())), preferred_element_type=F32)
    return (f(x1) + f(x2)) + f(x3) if terms == 3 else f(x1) + f(x2)


def _iota(shape, dim):
    return lax.broadcasted_iota(jnp.int32, shape, dim)


def _eye(n):
    return (_iota((n, n), 0) == _iota((n, n), 1)).astype(F32)


def _transpose(x):
    if x.shape[0] % 128 == 0 and x.shape[1] % 128 == 0:
        return x.T
    return _sel(x, _eye(x.shape[1]), "mxT")


def _sigmoid(x):
    return jax.nn.sigmoid(x)


def _silu(x):
    return x * jax.nn.sigmoid(x)


def _log_sigmoid(x):
    return jnp.minimum(x, 0.0) - jnp.log1p(jnp.exp(-jnp.abs(x)))


def _head_masks(n=BR_W):
    lane = _iota((1, n), 1) >> 6
    return [lane == h for h in range(HEADS)]


def _block_diag_mask():
    return (_iota((BR_W, BR_W), 0) >> 6) == (_iota((BR_W, BR_W), 1) >> 6)


def _per_head_lanes(vals, masks):
    out = jnp.where(masks[0], vals[0], 0.0)
    for h in range(1, HEADS):
        out = jnp.where(masks[h], vals[h], out)
    return out


def _rms(x, eps=RMS_EPS):
    return x * lax.rsqrt(jnp.mean(x * x, axis=-1, keepdims=True) + eps)


def _head_norm(o, gain, bdf):
    ms = _sel(o * o, bdf, "xm", terms=2) * (1.0 / HEAD_DIM)
    return o * lax.rsqrt(ms + RMS_EPS) * gain


def _ada_kernel(c_ref, w_ref, b_ref, o_ref):
    o_ref[...] = _dot(_silu(c_ref[...]), w_ref[...]) + b_ref[...]


def _ada(c_all, w_ada, b_ada):
    depth = w_ada.shape[0]
    n_c = c_all.shape[0]
    tn = 1536
    return pl.pallas_call(
        _ada_kernel,
        grid=(depth, 6 * D_MODEL // tn),
        in_specs=[pl.BlockSpec((n_c, D_MODEL), lambda l, j: (0, 0)),
                  pl.BlockSpec((None, D_MODEL, tn), lambda l, j: (l, 0, j)),
                  pl.BlockSpec((None, 1, tn), lambda l, j: (l, 0, j))],
        out_specs=pl.BlockSpec((None, n_c, tn), lambda l, j: (l, 0, j)),
        out_shape=jax.ShapeDtypeStruct((depth, n_c, 6 * D_MODEL), F32),
        compiler_params=_cparams(("arbitrary", "arbitrary")),
        name="ada",
    )(c_all, w_ada, b_ada.reshape(depth, 1, 6 * D_MODEL))


def _inproj_kernel(kv_t, x_ref, mod_ref, nw_ref, w_ref, b_ref, o_ref, *rest):
    h_scr = rest[-1]
    j = pl.program_id(1)

    @pl.when(j == 0)
    def _():
        h = _rms(x_ref[...]) * nw_ref[...]
        h = h * (1.0 + mod_ref[:, D_MODEL:2 * D_MODEL]) + mod_ref[:, 0:D_MODEL]
        h_scr[...] = h.astype(BF16)

    res = jnp.dot(h_scr[...], w_ref[...], preferred_element_type=F32) + b_ref[...]
    o_ref[...] = res

    if kv_t:
        kt_ref, vt_ref = rest[0], rest[1]
        c0 = P_FX % res.shape[1]

        @pl.when(j == P_FX // res.shape[1])
        def _():
            kt_ref[...] = res[:, c0 + BR_W:c0 + 2 * BR_W].T
            vt_ref[...] = res[:, c0 + 2 * BR_W:c0 + 3 * BR_W].T


def _inproj(layer, x, mod, norm_w, w_p, b_p, kv_t):
    t = x.shape[0]
    tm = min(t, 1024)
    tn = 2048
    per_row = mod.shape[0] != 1
    mod_spec = (pl.BlockSpec((tm, 6 * D_MODEL), lambda i, j: (i, 0)) if per_row
                else pl.BlockSpec((1, 6 * D_MODEL), lambda i, j: (0, 0)))
    out_specs = [pl.BlockSpec((tm, tn), lambda i, j: (i, j))]
    out_shape = [jax.ShapeDtypeStruct((t, P_TOT), F32)]
    if kv_t:
        out_specs += [pl.BlockSpec((BR_W, tm), lambda i, j: (0, i))] * 2
        out_shape += [jax.ShapeDtypeStruct((BR_W, t), F32)] * 2
    return pl.pallas_call(
        functools.partial(_inproj_kernel, kv_t),
        grid=(t // tm, P_TOT // tn),
        in_specs=[pl.BlockSpec((tm, D_MODEL), lambda i, j: (i, 0)),
                  mod_spec,
                  pl.BlockSpec((1, D_MODEL), lambda i, j: (0, 0)),
                  pl.BlockSpec((None, D_MODEL, tn), lambda i, j: (layer, 0, j)),
                  pl.BlockSpec((None, 1, tn), lambda i, j: (layer, 0, j))],
        out_specs=out_specs,
        out_shape=out_shape,
        scratch_shapes=[pltpu.VMEM((tm, D_MODEL), BF16)],
        compiler_params=_cparams(("arbitrary", "arbitrary")),
        name="inproj",
    )(x, mod, norm_w, w_p, b_p)


def _mixers_kernel(layer, lc, sc,
                   ml_ref, hg_ref, rt_ref, sm_ref, cos_ref, sin_ref,
                   c0_ref, n0_ref, m0_ref, sh0_ref, sr0_ref, lbl_ref, gml_ref, ghg_ref, grt_ref,
                   br_ref, lf_ref, cumt_ref, cout_ref, nout_ref, mout_ref, shout_ref, srout_ref,
                   cbd, sht, srbd, n_s, m_s, carry_s, ohg_s):
    c = pl.program_id(1)
    n_c = pl.num_programs(1)
    masks = _head_masks()
    bd = _block_diag_mask()
    bdf = bd.astype(F32)
    tile = ((_iota((HEAD_DIM, BR_W), 1) & (HEAD_DIM - 1)) == _iota((HEAD_DIM, BR_W), 0)).astype(F32)

    @pl.when(c == 0)
    def _init():
        def expand(ref):
            return jnp.where(bd, _sel(ref[...], tile, "xm"), 0.0)
        cbd[...] = expand(c0_ref)
        sht[...] = _transpose(expand(sh0_ref))
        srbd[...] = expand(sr0_ref)
        n_s[...] = n0_ref[...]
        m_s[...] = m0_ref[...]
        carry_s[...] = jnp.zeros_like(carry_s)

    row = _iota((lc, lc), 0)
    col = _iota((lc, lc), 1)
    causal = row >= col
    tril = causal.astype(F32)

    sm = sm_ref[:, 0:128]
    lane128 = _iota((1, 128), 1)
    sm2 = jnp.where((lane128 >= SM_MLF) & (lane128 < SM_FLF + HEADS), _log_sigmoid(sm), sm)
    cum = _sel(sm2, tril, "mx")
    sm2_t = _transpose(sm2)
    cum_t = _transpose(cum)
    lf_ref[...] = sm2
    cum_tg = cum_t + carry_s[...]
    carry_s[...] = cum_tg[:, lc - 1:lc]
    cumt_ref[...] = cum_tg[SM_FLF:SM_FLF + 8, :]

    q = ml_ref[:, 0:BR_W]
    k = ml_ref[:, BR_W:2 * BR_W] * QK_SCALE
    v = ml_ref[:, 2 * BR_W:3 * BR_W]
    og = ml_ref[:, 3 * BR_W:4 * BR_W]
    n_row = n_s[...]
    m_row = m_s[...]
    q_c = _dot_nt(q, cbd[...])
    h_all = jnp.zeros((lc, BR_W), F32)
    w_lanes = jnp.zeros((lc, BR_W), F32)
    decay_lanes = jnp.zeros((1, BR_W), F32)
    m_new_row = jnp.zeros((1, 128), F32)
    for h in range(HEADS):
        ig_c = sm2[:, SM_IG + h:SM_IG + h + 1]
        b_c = cum[:, SM_MLF + h:SM_MLF + h + 1]
        ig_r = sm2_t[SM_IG + h:SM_IG + h + 1, :]
        b_r = cum_t[SM_MLF + h:SM_MLF + h + 1, :]
        m_prev = m_row[:, h:h + 1]
        dmat = jnp.where(causal, b_c - b_r + ig_r, NEG)
        m_inter = b_c + m_prev
        m_t = jnp.maximum(m_inter, jnp.max(dmat, axis=1, keepdims=True))
        w_intra = jnp.exp(dmat - m_t)
        w_inter = jnp.exp(m_inter - m_t)
        qh = jnp.where(masks[h], q, 0.0)
        a = _dot_nt(qh, k) * w_intra
        num = _dot(a, v) + w_inter * q_c
        den = jnp.sum(a, axis=1, keepdims=True) + w_inter * jnp.sum(qh * n_row, axis=1, keepdims=True)
        hh = num / jnp.maximum(jnp.abs(den), jnp.exp(-m_t))
        h_all = jnp.where(masks[h], hh, h_all)
        m_new = m_t[lc - 1:lc, :]
        b_last = b_c[lc - 1:lc, :]
        w_s = jnp.exp(b_last - b_c + ig_c - m_new)
        decay = jnp.exp(b_last + m_prev - m_new)
        w_lanes = jnp.where(masks[h], w_s, w_lanes)
        decay_lanes = jnp.where(masks[h], decay, decay_lanes)
        m_new_row = jnp.where(lane128 == h, m_new, m_new_row)
    kw = k * w_lanes
    cbd[...] = cbd[...] * decay_lanes + jnp.where(bd, _dot_tn(v * w_lanes, k), 0.0)
    n_s[...] = n_row * decay_lanes + jnp.sum(kw, axis=0, keepdims=True)
    m_s[...] = m_new_row
    out_ml = _head_norm(h_all, gml_ref[...], bdf) * _sigmoid(og)

    lbl = lbl_ref[...]
    pr = jnp.exp(lbl - jnp.max(lbl, axis=0, keepdims=True))
    pr = pr / jnp.sum(pr, axis=0, keepdims=True)
    lb = jnp.zeros((1, BR_W), F32)
    for i in range(1, layer + 1):
        lb = lb + pr[i:i + 1, :]
    tril_sc = (_iota((sc, sc), 0) >= _iota((sc, sc), 1)).astype(F32)
    s_idx = _iota((sc, BR_W), 0)

    def hg_body(i, carry):
        r0 = pl.multiple_of(i * sc, sc)
        hq = hg_ref[pl.ds(r0, sc), 0:BR_W]
        hf = hg_ref[pl.ds(r0, sc), BR_W:2 * BR_W]
        vi = hg_ref[pl.ds(r0, sc), 2 * BR_W:3 * BR_W]
        qi = _silu(hq)
        f = lb + (1.0 - lb) * _sigmoid(hf)
        ki = 1.0 - f
        bi = _sel(jnp.log(f), tril_sc, "mx")
        st = sht[...]
        o_inter = _dot_nt(qi * jnp.exp(bi), st)
        rows = []
        for t in range(sc):
            e_t = jnp.exp(jnp.where(s_idx <= t, bi[t:t + 1, :] - bi, NEG)) * qi[t:t + 1, :] * ki
            rows.append(e_t)
        e_all = jnp.concatenate(rows, axis=0)
        r_all = _dot(e_all, bdf)
        o_diag = jnp.sum(r_all.reshape(sc, sc, BR_W) * vi[None, :, :], axis=1)
        ohg_s[pl.ds(r0, sc), :] = o_inter + o_diag
        b_last = bi[sc - 1:sc, :]
        sht[...] = st * jnp.exp(b_last) + jnp.where(bd, _dot_tn(vi, ki * jnp.exp(b_last - bi)), 0.0)
        return carry

    lax.fori_loop(0, lc // sc, hg_body, 0, unroll=True)
    out_hg = _head_norm(ohg_s[...], ghg_ref[...], bdf) * _silu(hg_ref[:, 3 * BR_W:4 * BR_W])

    cosv = cos_ref[...]
    sinv = sin_ref[...]
    lane = _iota((1, BR_W), 1)
    first_half = (lane & (HEAD_DIM - 1)) < (HEAD_DIM // 2)

    def rope(x):
        partner = jnp.where(first_half, pltpu.roll(x, BR_W - HEAD_DIM // 2, 1), pltpu.roll(x, HEAD_DIM // 2, 1))
        return x * cosv + partner * sinv

    rq = rope(rt_ref[:, 0:BR_W])
    rk = rope(rt_ref[:, BR_W:2 * BR_W]) * QK_SCALE
    rv = rt_ref[:, 2 * BR_W:3 * BR_W]
    lg = [math.log1p(-(2.0 ** (-5.0 - h))) for h in range(HEADS)]
    lg_lanes = _per_head_lanes([jnp.full((1, 1), g, F32) for g in lg], masks)
    diff = (row - col).astype(F32)
    o_rt = jnp.zeros((lc, BR_W), F32)
    for h in range(HEADS):
        dec = jnp.exp(jnp.where(causal, diff * lg[h], NEG))
        a = _dot_nt(jnp.where(masks[h], rq, 0.0), rk) * dec
        o_rt = jnp.where(masks[h], _dot(a, rv), o_rt)
    t_idx = _iota((lc, BR_W), 0).astype(F32)
    o_rt = o_rt + _dot(rq * jnp.exp((t_idx + 1.0) * lg_lanes), srbd[...])
    w_ret = jnp.exp((lc - 1.0 - t_idx) * lg_lanes)
    srbd[...] = srbd[...] * jnp.exp(lc * lg_lanes) + jnp.where(bd, _dot_tn(rk * w_ret, rv), 0.0)
    out_rt = _head_norm(o_rt, grt_ref[...], bdf) * _silu(rt_ref[:, 3 * BR_W:4 * BR_W])

    br_ref[:, 0:BR_W] = out_ml
    br_ref[:, BR_W:2 * BR_W] = out_hg
    br_ref[:, 2 * BR_W:3 * BR_W] = out_rt

    @pl.when(c == n_c - 1)
    def _fin():
        def compact(x):
            return _sel(x, tile, "xmT")
        cout_ref[...] = compact(cbd[...])
        shout_ref[...] = compact(_transpose(sht[...]))
        srout_ref[...] = compact(srbd[...])
        nout_ref[...] = n_s[...]
        mout_ref[...] = m_s[...]


def _mixers(layer, proj, cos_t, sin_t, c0, n0, m0, sh0, sr0, lb_logits, g_ml, g_hg, g_rt, lc):
    b, l, _ = proj.shape
    sc = min(16, lc)
    n_c = l // lc
    depth = lb_logits.shape[0]
    cb = lambda blk: pl.BlockSpec((None, lc, 1024), lambda bi, ci, blk=blk: (bi, ci, blk))
    st_spec = pl.BlockSpec((None, BR_W, HEAD_DIM), lambda bi, ci: (bi, 0, 0))
    row_spec = lambda n: pl.BlockSpec((None, 1, n), lambda bi, ci: (bi, 0, 0))
    full = lambda r, cc: pl.BlockSpec((r, cc), lambda bi, ci: (0, 0))
    outs = pl.pallas_call(
        functools.partial(_mixers_kernel, layer, lc, sc),
        grid=(b, n_c),
        in_specs=[cb(0), cb(1), cb(2),
                  pl.BlockSpec((None, lc, 256), lambda bi, ci: (bi, ci, P_SM // 256)),
                  pl.BlockSpec((lc, BR_W), lambda bi, ci: (ci, 0)),
                  pl.BlockSpec((lc, BR_W), lambda bi, ci: (ci, 0)),
                  st_spec, row_spec(BR_W), row_spec(128), st_spec, st_spec,
                  full(depth, BR_W), full(1, BR_W), full(1, BR_W), full(1, BR_W)],
        out_specs=[pl.BlockSpec((None, lc, 3 * BR_W), lambda bi, ci: (bi, ci, 0)),
                   pl.BlockSpec((None, lc, 128), lambda bi, ci: (bi, ci, 0)),
                   pl.BlockSpec((None, 8, lc), lambda bi, ci: (bi, 0, ci)),
                   st_spec, row_spec(BR_W), row_spec(128), st_spec, st_spec],
        out_shape=[jax.ShapeDtypeStruct((b, l, 3 * BR_W), F32),
                   jax.ShapeDtypeStruct((b, l, 128), F32),
                   jax.ShapeDtypeStruct((b, 8, l), F32),
                   jax.ShapeDtypeStruct((b, BR_W, HEAD_DIM), F32),
                   jax.ShapeDtypeStruct((b, 1, BR_W), F32),
                   jax.ShapeDtypeStruct((b, 1, 128), F32),
                   jax.ShapeDtypeStruct((b, BR_W, HEAD_DIM), F32),
                   jax.ShapeDtypeStruct((b, BR_W, HEAD_DIM), F32)],
        scratch_shapes=[pltpu.VMEM((BR_W, BR_W), F32), pltpu.VMEM((BR_W, BR_W), F32), pltpu.VMEM((BR_W, BR_W), F32),
                        pltpu.VMEM((1, BR_W), F32), pltpu.VMEM((1, 128), F32), pltpu.VMEM((128, 1), F32),
                        pltpu.VMEM((lc, BR_W), F32)],
        compiler_params=_cparams(("arbitrary", "arbitrary")),
        name="mixers",
    )(proj, proj, proj, proj, cos_t, sin_t, c0, n0, m0, sh0, sr0, lb_logits, g_ml, g_hg, g_rt)
    return outs


def _fox_prompt_kernel(tq, tk, strip, qi_ref, kj_ref, q_ref, k_ref, v_ref, ck_ref, o_ref,
                       m_s, l_s, alpha_s, acc_s, s_scr, p_scr):
    step = pl.program_id(0)
    i = qi_ref[step]
    j = kj_ref[step]
    masks = _head_masks()

    @pl.when(j == 0)
    def _init():
        m_s[...] = jnp.full_like(m_s, NEG)
        l_s[...] = jnp.zeros_like(l_s)
        acc_s[...] = jnp.zeros_like(acc_s)

    q = q_ref[...] * (QK_SCALE * LOG2E)
    kb = k_ref[...].astype(BF16)
    vb = v_ref[...].astype(BF16)
    nck = ck_ref[...] * (-LOG2E)
    n_rep = tk // 128

    def attend(diagonal):
        col = _iota((strip, tk), 1)
        row = _iota((strip, tk), 0)
        for h in range(HEADS):
            s_scr[h] = _dot_nt(jnp.where(masks[h], q, 0.0), kb) + nck[h:h + 1, :]
            parts = []
            for r0 in range(0, tq, strip):
                sc = s_scr[h, pl.ds(r0, strip), :]
                if diagonal:
                    sc = jnp.where(row + (r0 + i * tq - j * tk) >= col, sc, NEG)
                    s_scr[h, pl.ds(r0, strip), :] = sc
                parts.append(jnp.max(sc, axis=1, keepdims=True))
            m_old = m_s[h]
            m_new = jnp.maximum(m_old, jnp.broadcast_to(jnp.concatenate(parts, axis=0), (tq, 128)))
            m_s[h] = m_new
            alpha_s[h] = jnp.exp2(m_old - m_new)
            for r0 in range(0, tq, strip):
                rows = pl.ds(r0, strip)
                m_rep = jnp.concatenate([m_s[h, rows, :]] * n_rep, axis=1)
                p = jnp.exp2(s_scr[h, rows, :] - m_rep)
                p_scr[h, rows, :] = p.astype(BF16)
                psum = p[:, 0:128]
                for c in range(1, n_rep):
                    psum = psum + p[:, c * 128:(c + 1) * 128]
                l_s[h, rows, :] = alpha_s[h, rows, :] * l_s[h, rows, :] + psum
            alpha = alpha_s[h]
            acc_s[h] = jnp.concatenate([alpha, alpha], axis=1) * acc_s[h] + jnp.dot(p_scr[h], vb,
                                                                                   preferred_element_type=F32)

    last = (j + 1) * tk >= (i + 1) * tq

    @pl.when(jnp.logical_not(last))
    def _off_diagonal():
        attend(False)

    @pl.when(last)
    def _diagonal():
        attend(True)
        out = jnp.zeros((tq, BR_W), F32)
        for h in range(HEADS):
            out = jnp.where(masks[h], acc_s[h] / jnp.sum(l_s[h], axis=1, keepdims=True), out)
        o_ref[...] = out


def _fox_prompt(proj, cum_t, tq):
    t = proj.shape[0]
    tk = 2 * tq if t % (2 * tq) == 0 else tq
    nq = t // tq
    n_kj = [((i + 1) * tq - 1) // tk + 1 for i in range(nq)]
    qi = np.concatenate([np.full(n, i, np.int32) for i, n in enumerate(n_kj)])
    kj = np.concatenate([np.arange(n, dtype=np.int32) for n in n_kj])
    cq = P_FX // BR_W
    grid_spec = pltpu.PrefetchScalarGridSpec(
        num_scalar_prefetch=2,
        grid=(len(qi),),
        in_specs=[pl.BlockSpec((tq, BR_W), lambda s, qi, kj: (qi[s], cq)),
                  pl.BlockSpec((tk, BR_W), lambda s, qi, kj: (kj[s], cq + 1)),
                  pl.BlockSpec((tk, BR_W), lambda s, qi, kj: (kj[s], cq + 2)),
                  pl.BlockSpec((8, tk), lambda s, qi, kj: (0, kj[s]))],
        out_specs=pl.BlockSpec((tq, BR_W), lambda s, qi, kj: (qi[s], 0)),
        scratch_shapes=[pltpu.VMEM((HEADS, tq, 128), F32), pltpu.VMEM((HEADS, tq, 128), F32),
                        pltpu.VMEM((HEADS, tq, 128), F32), pltpu.VMEM((HEADS, tq, BR_W), F32),
                        pltpu.VMEM((HEADS, tq, tk), F32), pltpu.VMEM((HEADS, tq, tk), BF16)],
    )
    return pl.pallas_call(
        functools.partial(_fox_prompt_kernel, tq, tk, min(16384 // tk, tq)),
        grid_spec=grid_spec,
        out_shape=jax.ShapeDtypeStruct((t, BR_W), F32),
        compiler_params=_cparams(("arbitrary",)),
        name="fox_prompt",
    )(jnp.asarray(qi), jnp.asarray(kj), proj, proj, proj, cum_t)


def _fox_decode_kernel(t_new, n_slots, pt_ref, q_ref, k_ref, v_ref, sm_ref, *rest):
    kt_pages = rest[0:n_slots]
    vt_pages = rest[n_slots:2 * n_slots]
    lf_pages = rest[2 * n_slots:3 * n_slots]
    o_ref = rest[3 * n_slots]
    m_s, l_s, acc_s, carry_s = rest[3 * n_slots + 1:]
    g = pl.program_id(1)
    n_g = pl.num_programs(1)
    masks = _head_masks()
    rows = HEADS * t_new

    q = q_ref[...] * QK_SCALE
    qbd = jnp.concatenate([jnp.where(masks[h], q, 0.0) for h in range(HEADS)], axis=0).astype(BF16)

    def per_head_rows(x):
        return jnp.concatenate([jnp.broadcast_to(x[h:h + 1, :], (t_new, x.shape[1])) for h in range(HEADS)], axis=0)

    def softmax_step(sc):
        m_old = m_s[...]
        m_new = jnp.maximum(m_old, jnp.max(sc, axis=1, keepdims=True))
        alpha = jnp.exp(m_old - m_new)
        p = jnp.exp(sc - m_new)
        l_s[...] = alpha * l_s[...] + jnp.sum(p, axis=1, keepdims=True)
        m_s[...] = m_new
        return alpha, p.astype(BF16)

    @pl.when(g == 0)
    def _new_rows():
        pad = jnp.zeros((PAGE_SIZE - t_new, BR_W), F32)
        kn = jnp.concatenate([k_ref[...], pad], axis=0).astype(BF16)
        vn = jnp.concatenate([v_ref[...], pad], axis=0).astype(BF16)
        smp = jnp.concatenate([sm_ref[:, 0:128], jnp.zeros((PAGE_SIZE - t_new, 128), F32)], axis=0)
        lane128 = _iota((1, 128), 1)
        lf = jnp.where((lane128 >= SM_FLF) & (lane128 < SM_FLF + HEADS), _log_sigmoid(smp), 0.0)
        tril = (_iota((PAGE_SIZE, PAGE_SIZE), 0) >= _iota((PAGE_SIZE, PAGE_SIZE), 1)).astype(F32)
        cum_t = _transpose(_sel(lf, tril, "mx"))
        t_of_row = _iota((rows, PAGE_SIZE), 0) & (t_new - 1)
        s_of_col = _iota((rows, PAGE_SIZE), 1)
        sc = _dot_nt(qbd, kn) - per_head_rows(cum_t[SM_FLF:SM_FLF + HEADS, :])
        sc = jnp.where(s_of_col <= t_of_row, sc, NEG)
        m_s[...] = jnp.full_like(m_s, NEG)
        l_s[...] = jnp.zeros_like(l_s)
        carry_s[...] = jnp.zeros_like(carry_s)
        _, p = softmax_step(sc)
        acc_s[...] = jnp.dot(p, vn, preferred_element_type=F32)

    strict = (_iota((PAGE_SIZE, PAGE_SIZE), 0) > _iota((PAGE_SIZE, PAGE_SIZE), 1)).astype(F32)
    strict_ones = jnp.concatenate([strict, jnp.ones((PAGE_SIZE, PAGE_SIZE), F32)], axis=1)
    later = carry_s[...]
    sufs = [None] * n_slots
    for slot in reversed(range(n_slots)):
        r = _sel(lf_pages[slot][...], strict_ones, "xm")
        sufs[slot] = r[:, 0:PAGE_SIZE] + later
        later = later + r[:, PAGE_SIZE:2 * PAGE_SIZE]
    carry_s[...] = later
    bias = per_head_rows(jnp.concatenate(sufs, axis=1))
    sc = jnp.concatenate([jnp.dot(qbd, kt_pages[slot][...].astype(BF16), preferred_element_type=F32)
                          for slot in range(n_slots)], axis=1) + bias
    alpha, p = softmax_step(sc)
    pv = _dot_nt(p[:, 0:PAGE_SIZE], vt_pages[0][...])
    for slot in range(1, n_slots):
        pv += _dot_nt(p[:, slot * PAGE_SIZE:(slot + 1) * PAGE_SIZE], vt_pages[slot][...])
    acc_s[...] = alpha * acc_s[...] + pv

    @pl.when(g == n_g - 1)
    def _fin():
        res = acc_s[...] / l_s[...]
        out = jnp.zeros((t_new, BR_W), F32)
        for h in range(HEADS):
            out = jnp.where(masks[h], res[h * t_new:(h + 1) * t_new, :], out)
        o_ref[...] = out


def _fox_decode(layer, proj, cache_kt, cache_vt, cache_lf_t, page_table, n_slots):
    db, t_new, _ = proj.shape
    n_pages = page_table.shape[1]
    n_g = n_pages // n_slots
    cq = P_FX // BR_W

    def page_idx(slot):
        return lambda b, g, pt: (layer, pt[b * n_pages + (n_g - 1 - g) * n_slots + slot], 0, 0)

    kv_specs = [pl.BlockSpec((None, None, BR_W, PAGE_SIZE), page_idx(s)) for s in range(n_slots)]
    lf_specs = [pl.BlockSpec((None, None, HEADS, PAGE_SIZE), page_idx(s)) for s in range(n_slots)]
    rows = HEADS * t_new
    grid_spec = pltpu.PrefetchScalarGridSpec(
        num_scalar_prefetch=1,
        grid=(db, n_g),
        in_specs=[pl.BlockSpec((None, t_new, BR_W), lambda b, g, pt: (b, 0, cq)),
                  pl.BlockSpec((None, t_new, BR_W), lambda b, g, pt: (b, 0, cq + 1)),
                  pl.BlockSpec((None, t_new, BR_W), lambda b, g, pt: (b, 0, cq + 2)),
                  pl.BlockSpec((None, t_new, BR_W), lambda b, g, pt: (b, 0, P_SM // BR_W))]
                 + kv_specs + kv_specs + lf_specs,
        out_specs=pl.BlockSpec((None, t_new, BR_W), lambda b, g, pt: (b, 0, 0)),
        scratch_shapes=[pltpu.VMEM((rows, 1), F32), pltpu.VMEM((rows, 1), F32), pltpu.VMEM((rows, BR_W), F32),
                        pltpu.VMEM((HEADS, PAGE_SIZE), F32)],
    )
    return pl.pallas_call(
        functools.partial(_fox_decode_kernel, t_new, n_slots),
        grid_spec=grid_spec,
        out_shape=jax.ShapeDtypeStruct((db, t_new, BR_W), F32),
        compiler_params=_cparams(("arbitrary", "arbitrary")),
        name="fox_decode",
    )(page_table.reshape(-1), proj, proj, proj, proj,
      *([cache_kt] * n_slots), *([cache_vt] * n_slots), *([cache_lf_t] * n_slots))


def _merge_kernel(br_ref, fx_ref, g0_ref, g1_ref, g2_ref, g3_ref, x_ref, mod_ref, wb_ref, wo_ref, nw_ref,
                  x1_ref, h2_ref):
    merged = _sigmoid(g0_ref[...]) * _dot(br_ref[:, 0:BR_W], wb_ref[0])
    merged += _sigmoid(g1_ref[...]) * _dot(br_ref[:, BR_W:2 * BR_W], wb_ref[1])
    merged += _sigmoid(g2_ref[...]) * _dot(br_ref[:, 2 * BR_W:3 * BR_W], wb_ref[2])
    merged += _sigmoid(g3_ref[...]) * _dot(fx_ref[...], wb_ref[3])
    x1 = x_ref[...] + mod_ref[:, 2 * D_MODEL:3 * D_MODEL] * _dot(merged, wo_ref[...])
    x1_ref[...] = x1
    h2 = _rms(x1) * nw_ref[...]
    h2_ref[...] = h2 * (1.0 + mod_ref[:, 4 * D_MODEL:5 * D_MODEL]) + mod_ref[:, 3 * D_MODEL:4 * D_MODEL]


def _merge(br3, fox, proj, x, mod, w_branch_b, w_out_b, norm_w):
    t = x.shape[0]
    tm = min(t, 512) if mod.shape[0] == 1 else min(t, 256)
    per_row = mod.shape[0] != 1
    mod_spec = (pl.BlockSpec((tm, 6 * D_MODEL), lambda i: (i, 0)) if per_row
                else pl.BlockSpec((1, 6 * D_MODEL), lambda i: (0, 0)))
    gate = lambda b: pl.BlockSpec((tm, D_MODEL), lambda i, b=b: (i, P_GT // D_MODEL + b))
    return pl.pallas_call(
        _merge_kernel,
        grid=(t // tm,),
        in_specs=[pl.BlockSpec((tm, 3 * BR_W), lambda i: (i, 0)),
                  pl.BlockSpec((tm, BR_W), lambda i: (i, 0)),
                  gate(0), gate(1), gate(2), gate(3),
                  pl.BlockSpec((tm, D_MODEL), lambda i: (i, 0)),
                  mod_spec,
                  pl.BlockSpec((N_BRANCH, BR_W, D_MODEL), lambda i: (0, 0, 0)),
                  pl.BlockSpec((D_MODEL, D_MODEL), lambda i: (0, 0)),
                  pl.BlockSpec((1, D_MODEL), lambda i: (0, 0))],
        out_specs=[pl.BlockSpec((tm, D_MODEL), lambda i: (i, 0)),
                   pl.BlockSpec((tm, D_MODEL), lambda i: (i, 0))],
        out_shape=[jax.ShapeDtypeStruct((t, D_MODEL), F32), jax.ShapeDtypeStruct((t, D_MODEL), F32)],
        compiler_params=_cparams(("arbitrary",)),
        name="merge",
    )(br3, fox, proj, proj, proj, proj, x, mod, w_branch_b, w_out_b, norm_w)


def _select_experts(h, wr_t, rb_col):
    tm = h.shape[0]
    logits = _dot_nt(wr_t, h)
    s = _sigmoid(logits)
    sel = s + rb_col
    ninf = -jnp.inf
    sub = _iota((GROUP_SIZE, tm), 0)
    gsc = []
    for g in range(N_GROUPS):
        blk = sel[g * GROUP_SIZE:(g + 1) * GROUP_SIZE, :]
        m1 = jnp.max(blk, axis=0, keepdims=True)
        first = jnp.min(jnp.where(blk == m1, sub, GROUP_SIZE), axis=0, keepdims=True)
        m2 = jnp.max(jnp.where(sub == first, ninf, blk), axis=0, keepdims=True)
        gsc.append(m1 + m2)
    chosen = [jnp.zeros((1, tm), jnp.bool_) for _ in range(N_GROUPS)]
    for _ in range(TOPK_GROUPS):
        mx = gsc[0]
        for g in range(1, N_GROUPS):
            mx = jnp.maximum(mx, gsc[g])
        taken = jnp.zeros((1, tm), jnp.bool_)
        for g in range(N_GROUPS):
            pick = (gsc[g] == mx) & jnp.logical_not(taken)
            taken = taken | pick
            chosen[g] = chosen[g] | pick
            gsc[g] = jnp.where(pick, ninf, gsc[g])
    selm = jnp.concatenate(
        [jnp.where(chosen[g], sel[g * GROUP_SIZE:(g + 1) * GROUP_SIZE, :], ninf) for g in range(N_GROUPS)], axis=0)
    eidx = _iota((N_EXPERTS, tm), 0)
    firsts, picks = [], []
    for _ in range(TOP_K):
        mx = jnp.max(selm, axis=0, keepdims=True)
        first = jnp.min(jnp.where(selm == mx, eidx, N_EXPERTS), axis=0, keepdims=True)
        pick = eidx == first
        selm = jnp.where(pick, ninf, selm)
        firsts.append(first)
        picks.append(pick)
    return s, firsts, picks


def _router_kernel(h_ref, wr_ref, rb_ref, wt_ref):
    s, _, picks = _select_experts(h_ref[...], wr_ref[...], rb_ref[...])
    picked = picks[0]
    for pick in picks[1:]:
        picked = picked | pick
    w = jnp.where(picked, s, 0.0)
    wt_ref[...] = w / jnp.sum(w, axis=0, keepdims=True) * ROUTED_SCALE


def _router(h2, w_router_t, router_bias_col):
    t = h2.shape[0]
    tm = min(t, 512)
    return pl.pallas_call(
        _router_kernel,
        grid=(t // tm,),
        in_specs=[pl.BlockSpec((tm, D_MODEL), lambda i: (i, 0)),
                  pl.BlockSpec((N_EXPERTS, D_MODEL), lambda i: (0, 0)),
                  pl.BlockSpec((N_EXPERTS, 1), lambda i: (0, 0))],
        out_specs=pl.BlockSpec((N_EXPERTS, tm), lambda i: (0, i)),
        out_shape=jax.ShapeDtypeStruct((N_EXPERTS, t), F32),
        compiler_params=_cparams(("arbitrary",)),
        name="router",
    )(h2, w_router_t, router_bias_col)


def _moe_kernel(final, h_ref, wt_ref, wgu_ref, wd_ref, wsgu_ref, wsd_ref, x1_ref, mod_ref, nf_ref, o_ref,
                acc_s, hb_s, wtok_s):
    e = pl.program_id(1)
    n_e = pl.num_programs(1)

    def swiglu(gu):
        return _silu(gu[:, 0:D_EXPERT]) * gu[:, D_EXPERT:2 * D_EXPERT]

    @pl.when(e == 0)
    def _init():
        hb = h_ref[...].astype(BF16)
        hb_s[...] = hb
        wtok_s[...] = _transpose(wt_ref[...])
        acc_s[...] = _dot(swiglu(jnp.dot(hb, wsgu_ref[...], preferred_element_type=F32)), wsd_ref[...])

    wcol = jnp.sum(jnp.where(_iota((1, N_EXPERTS), 1) == e, wtok_s[...], 0.0), axis=1, keepdims=True)
    a = swiglu(_dot(hb_s[...], wgu_ref[...]))
    acc_s[...] += _dot(a, wd_ref[...]) * wcol

    @pl.when(e == n_e - 1)
    def _fin():
        x2 = x1_ref[...] + mod_ref[:, 5 * D_MODEL:6 * D_MODEL] * acc_s[...]
        if final:
            x2 = _rms(x2) * nf_ref[...]
        o_ref[...] = x2


def _moe(layer, h2, wt, wgu, wd, wsgu_b, wsd_b, x1, mod, norm_f, final):
    t = h2.shape[0]
    tm = min(t, 1024)
    per_row = mod.shape[0] != 1
    mod_spec = (pl.BlockSpec((tm, 6 * D_MODEL), lambda i, e: (i, 0)) if per_row
                else pl.BlockSpec((1, 6 * D_MODEL), lambda i, e: (0, 0)))
    return pl.pallas_call(
        functools.partial(_moe_kernel, final),
        grid=(t // tm, N_EXPERTS),
        in_specs=[pl.BlockSpec((tm, D_MODEL), lambda i, e: (i, 0)),
                  pl.BlockSpec((N_EXPERTS, tm), lambda i, e: (0, i)),
                  pl.BlockSpec((None, None, D_MODEL, 2 * D_EXPERT), lambda i, e: (layer, e, 0, 0)),
                  pl.BlockSpec((None, None, D_EXPERT, D_MODEL), lambda i, e: (layer, e, 0, 0)),
                  pl.BlockSpec((D_MODEL, 2 * D_EXPERT), lambda i, e: (0, 0)),
                  pl.BlockSpec((D_EXPERT, D_MODEL), lambda i, e: (0, 0)),
                  pl.BlockSpec((tm, D_MODEL), lambda i, e: (i, 0)),
                  mod_spec,
                  pl.BlockSpec((1, D_MODEL), lambda i, e: (0, 0))],
        out_specs=pl.BlockSpec((tm, D_MODEL), lambda i, e: (i, 0)),
        out_shape=jax.ShapeDtypeStruct((t, D_MODEL), F32),
        scratch_shapes=[pltpu.VMEM((tm, D_MODEL), F32), pltpu.VMEM((tm, D_MODEL), BF16),
                        pltpu.VMEM((tm, N_EXPERTS), F32)],
        compiler_params=_cparams(("arbitrary", "arbitrary")),
        name="moe",
    )(h2, wt, wgu, wd, wsgu_b, wsd_b, x1, mod, norm_f)


MOE_BM = 512
MOE_SPARSE_MIN_TOKENS = 1024


def _pack_bf16_pairs(x):
    n = x.shape[1] // 2
    bits = lax.bitcast_convert_type(x.astype(BF16).astype(F32), jnp.uint32)
    return lax.bitcast_convert_type(bits[:, n:] | (bits[:, :n] >> 16), F32)


def _unpack_bf16_pairs(w):
    u = lax.bitcast_convert_type(w, jnp.uint32)
    return (lax.bitcast_convert_type(u << 16, F32),
            lax.bitcast_convert_type(u & jnp.uint32(0xFFFF0000), F32))


def _router_sparse_kernel(h_ref, wr_ref, rb_ref, eidx_ref, rank_ref, ew_ref, cnt_ref, hp_ref, carry_s):
    tm = h_ref.shape[0]
    hp_ref[...] = _pack_bf16_pairs(h_ref[...])

    @pl.when(pl.program_id(0) == 0)
    def _():
        carry_s[...] = jnp.zeros_like(carry_s)

    s, firsts, picks = _select_experts(h_ref[...], wr_ref[...], rb_ref[...])
    picked = picks[0]
    for pick in picks[1:]:
        picked = picked | pick
    onehot = picked.astype(F32)
    earlier = (_iota((tm, tm), 0) < _iota((tm, tm), 1)).astype(F32)
    carry = carry_s[...]
    rank = _dot(onehot, earlier) + carry[:, 0:1]
    carry_s[...] = carry + jnp.sum(onehot, axis=1, keepdims=True)
    cnt_ref[...] = carry_s[...]

    w = [jnp.sum(jnp.where(pick, s, 0.0), axis=0, keepdims=True) for pick in picks]
    wsum = w[0]
    for wk in w[1:]:
        wsum = wsum + wk
    row8 = _iota((8, tm), 0)
    eidx8 = jnp.zeros((8, tm), jnp.int32)
    rank8 = jnp.zeros((8, tm), jnp.int32)
    ew8 = jnp.zeros((8, tm), F32)
    for k in range(TOP_K):
        rk = jnp.sum(jnp.where(picks[k], rank, 0.0), axis=0, keepdims=True)
        eidx8 = jnp.where(row8 == k, firsts[k], eidx8)
        rank8 = jnp.where(row8 == k, rk.astype(jnp.int32), rank8)
        ew8 = jnp.where(row8 == k, w[k] / wsum * ROUTED_SCALE, ew8)
    eidx_ref[...] = eidx8
    rank_ref[...] = rank8
    ew_ref[...] = ew8


def _router_sparse(h2, w_router_t, router_bias_col):
    t = h2.shape[0]
    tm = 512
    row = lambda dt: jax.ShapeDtypeStruct((8, t), dt)
    return pl.pallas_call(
        _router_sparse_kernel,
        grid=(t // tm,),
        in_specs=[pl.BlockSpec((tm, D_MODEL), lambda i: (i, 0)),
                  pl.BlockSpec((N_EXPERTS, D_MODEL), lambda i: (0, 0)),
                  pl.BlockSpec((N_EXPERTS, 1), lambda i: (0, 0))],
        out_specs=[pl.BlockSpec((8, tm), lambda i: (0, i)), pl.BlockSpec((8, tm), lambda i: (0, i)),
                   pl.BlockSpec((8, tm), lambda i: (0, i)), pl.BlockSpec((N_EXPERTS, 128), lambda i: (0, 0)),
                   pl.BlockSpec((tm, D_MODEL // 2), lambda i: (i, 0))],
        out_shape=[row(jnp.int32), row(jnp.int32), row(F32), jax.ShapeDtypeStruct((N_EXPERTS, 128), F32),
                   jax.ShapeDtypeStruct((t, D_MODEL // 2), F32)],
        scratch_shapes=[pltpu.VMEM((N_EXPERTS, 128), F32)],
        compiler_params=_cparams(("arbitrary",)),
        name="router_sparse",
    )(h2, w_router_t, router_bias_col)


def _plan_kernel(nblk_pad, cnt_ref, eidx_ref, rank_ref, dest_ref, blk_ref):
    tm = eidx_ref.shape[1]
    cnt = cnt_ref[...]
    padded = jnp.floor((cnt + (MOE_BM - 1.0)) * (1.0 / MOE_BM)) * MOE_BM
    tril = (_iota((N_EXPERTS, N_EXPERTS), 0) >= _iota((N_EXPERTS, N_EXPERTS), 1)).astype(F32)
    pad_end = _sel(padded, tril, "mx")
    start_col = (pad_end - padded)[:, 0:1]
    end_col = pad_end[:, 0:1]

    e_iota = _iota((N_EXPERTS, tm), 0)
    row8 = _iota((8, tm), 0)
    eidx = eidx_ref[...]
    rank = rank_ref[...]
    dest = jnp.zeros((8, tm), jnp.int32)
    for k in range(TOP_K):
        base = jnp.sum(jnp.where(e_iota == eidx[k:k + 1, :], start_col, 0.0), axis=0, keepdims=True)
        dest = jnp.where(row8 == k, base.astype(jnp.int32) + rank[k:k + 1, :], dest)
    dest_ref[...] = dest

    first_row = (_iota((N_EXPERTS, nblk_pad), 1) * MOE_BM).astype(F32)
    blk_e = jnp.sum((end_col <= first_row).astype(F32), axis=0, keepdims=True)
    blk_e = jnp.minimum(blk_e, N_EXPERTS - 1.0).astype(jnp.int32)
    n_used = (pad_end[N_EXPERTS - 1:N_EXPERTS, 0:1] * (1.0 / MOE_BM)).astype(jnp.int32)
    row8b = _iota((8, nblk_pad), 0)
    blk_ref[...] = jnp.where(row8b == 0, blk_e, jnp.where(row8b == 1, n_used, 0))


def _plan(cnt, eidx, rank, nblk_pad):
    t = eidx.shape[1]
    tm = min(t, 2048)
    return pl.pallas_call(
        functools.partial(_plan_kernel, nblk_pad),
        grid=(t // tm,),
        in_specs=[pl.BlockSpec((N_EXPERTS, 128), lambda i: (0, 0)),
                  pl.BlockSpec((8, tm), lambda i: (0, i)), pl.BlockSpec((8, tm), lambda i: (0, i))],
        out_specs=[pl.BlockSpec((8, tm), lambda i: (0, i)), pl.BlockSpec((8, nblk_pad), lambda i: (0, 0))],
        out_shape=[jax.ShapeDtypeStruct((8, t), jnp.int32), jax.ShapeDtypeStruct((8, nblk_pad), jnp.int32)],
        compiler_params=_cparams(("arbitrary",)),
        name="moe_plan",
    )(cnt, eidx, rank)


def _sc_mesh():
    return plsc.VectorSubcoreMesh(core_axis_name="core", subcore_axis_name="subcore")


SC_CHUNK = 256
SC_WINDOW = 128
N_CHUNK = D_MODEL // SC_CHUNK


def _sc_scatter_rows(x, idx, n_rows):
    t = x.shape[0]
    n_chunk = x.shape[1] // SC_CHUNK
    flat_idx = [i for per_chunk in idx for i in per_chunk]

    @pl.kernel(out_type=jax.ShapeDtypeStruct((n_chunk * n_rows, SC_CHUNK), x.dtype), mesh=_sc_mesh(),
               scratch_types=[])
    def scatter_kernel(x_hbm, *rest):
        i_hbm, o_hbm = rest[:-1], rest[-1]

        def body(x_vmem, *i_vmem):
            for iv in i_vmem:
                pltpu.sync_copy(x_vmem, o_hbm.at[iv.at[0]])

        for c in range(n_chunk):
            pltpu.emit_pipeline(
                body,
                grid=(t // SC_WINDOW,),
                in_specs=[pl.BlockSpec((SC_WINDOW, SC_CHUNK), lambda i, c=c: (i, c))]
                         + [pl.BlockSpec((1, SC_WINDOW), lambda i: (0, i))] * TOP_K,
                out_specs=[],
                core_axis_name=("core", "subcore"),
                dimension_semantics=(pltpu.PARALLEL,),
            )(x_hbm, *i_hbm[c * TOP_K:(c + 1) * TOP_K])

    return scatter_kernel(x, *flat_idx)


def _sc_gather_rows(y, idx):
    a = idx[0].shape[1]
    n_win = a // SC_WINDOW

    @pl.kernel(out_type=jax.ShapeDtypeStruct((N_CHUNK * a, SC_CHUNK), y.dtype), mesh=_sc_mesh(), scratch_types=[])
    def gather_kernel(y_hbm, *rest):
        i_hbm, o_hbm = rest[:-1], rest[-1]

        def body(i_vmem, o_vmem):
            pltpu.sync_copy(y_hbm.at[i_vmem.at[0]], o_vmem)

        for c in range(N_CHUNK):
            pltpu.emit_pipeline(
                body,
                grid=(n_win,),
                in_specs=[pl.BlockSpec((1, SC_WINDOW), lambda i: (0, i))],
                out_specs=[pl.BlockSpec((SC_WINDOW, SC_CHUNK), lambda i, c=c: (c * n_win + i, 0))],
                core_axis_name=("core", "subcore"),
                dimension_semantics=(pltpu.PARALLEL,),
            )(i_hbm[c], o_hbm)

    return gather_kernel(y, *idx)


def _swiglu(gu):
    return _silu(gu[:, 0:D_EXPERT]) * gu[:, D_EXPERT:2 * D_EXPERT]


def _grouped_kernel(blk_e_ref, n_used_ref, xs_ref, wgu_ref, wd_ref, ys_ref):
    @pl.when(pl.program_id(0) < n_used_ref[0])
    def _():
        gu = None
        for c in range(N_CHUNK // 2):
            lo, hi = _unpack_bf16_pairs(xs_ref[c])
            part = (_dot(lo, wgu_ref[c * SC_CHUNK:(c + 1) * SC_CHUNK, :])
                    + _dot(hi, wgu_ref[D_MODEL // 2 + c * SC_CHUNK:D_MODEL // 2 + (c + 1) * SC_CHUNK, :]))
            gu = part if gu is None else gu + part
        y = _dot(_swiglu(gu), wd_ref[...])
        for c in range(N_CHUNK):
            ys_ref[c] = y[:, c * SC_CHUNK:(c + 1) * SC_CHUNK]


def _grouped(layer, xs, blk_e, n_used, wgu, wd):
    n_rows = xs.shape[1]
    clamp = lambda b, nu: jnp.minimum(b, nu[0] - 1)
    rows_spec = pl.BlockSpec((N_CHUNK, MOE_BM, SC_CHUNK), lambda b, be, nu: (0, clamp(b, nu), 0))
    in_rows_spec = pl.BlockSpec((N_CHUNK // 2, MOE_BM, SC_CHUNK), lambda b, be, nu: (0, clamp(b, nu), 0))
    grid_spec = pltpu.PrefetchScalarGridSpec(
        num_scalar_prefetch=2,
        grid=(n_rows // MOE_BM,),
        in_specs=[in_rows_spec,
                  pl.BlockSpec((None, None, D_MODEL, 2 * D_EXPERT),
                               lambda b, be, nu: (layer, be[clamp(b, nu)], 0, 0)),
                  pl.BlockSpec((None, None, D_EXPERT, D_MODEL),
                               lambda b, be, nu: (layer, be[clamp(b, nu)], 0, 0))],
        out_specs=rows_spec,
    )
    return pl.pallas_call(
        _grouped_kernel,
        grid_spec=grid_spec,
        out_shape=jax.ShapeDtypeStruct((N_CHUNK, n_rows, SC_CHUNK), F32),
        compiler_params=_cparams(("arbitrary",)),
        name="moe_grouped",
    )(blk_e, n_used, xs, wgu, wd)


def _combine_kernel(final, yg_ref, ew_ref, h_ref, wsgu_ref, wsd_ref, x1_ref, mod_ref, nf_ref, o_ref):
    gu = jnp.dot(h_ref[...].astype(BF16), wsgu_ref[...], preferred_element_type=F32)
    acc = _dot(_swiglu(gu), wsd_ref[...])
    wt = _transpose(ew_ref[...])
    routed = []
    for c in range(N_CHUNK):
        part = wt[:, 0:1] * yg_ref[c, 0]
        for k in range(1, TOP_K):
            part = part + wt[:, k:k + 1] * yg_ref[c, k]
        routed.append(part)
    acc = acc + jnp.concatenate(routed, axis=1)
    x2 = x1_ref[...] + mod_ref[:, 5 * D_MODEL:6 * D_MODEL] * acc
    if final:
        x2 = _rms(x2) * nf_ref[...]
    o_ref[...] = x2


def _combine(yg, ew, h2, wsgu_b, wsd_b, x1, mod, norm_f, final):
    t = h2.shape[0]
    tm = 512
    per_row = mod.shape[0] != 1
    mod_spec = (pl.BlockSpec((tm, 6 * D_MODEL), lambda i: (i, 0)) if per_row
                else pl.BlockSpec((1, 6 * D_MODEL), lambda i: (0, 0)))
    return pl.pallas_call(
        functools.partial(_combine_kernel, final),
        grid=(t // tm,),
        in_specs=[pl.BlockSpec((N_CHUNK, TOP_K, tm, SC_CHUNK), lambda i: (0, 0, i, 0)),
                  pl.BlockSpec((8, tm), lambda i: (0, i)),
                  pl.BlockSpec((tm, D_MODEL), lambda i: (i, 0)),
                  pl.BlockSpec((D_MODEL, 2 * D_EXPERT), lambda i: (0, 0)),
                  pl.BlockSpec((D_EXPERT, D_MODEL), lambda i: (0, 0)),
                  pl.BlockSpec((tm, D_MODEL), lambda i: (i, 0)),
                  mod_spec,
                  pl.BlockSpec((1, D_MODEL), lambda i: (0, 0))],
        out_specs=pl.BlockSpec((tm, D_MODEL), lambda i: (i, 0)),
        out_shape=jax.ShapeDtypeStruct((t, D_MODEL), F32),
        compiler_params=_cparams(("arbitrary",)),
        name="moe_combine",
    )(yg, ew, h2, wsgu_b, wsd_b, x1, mod, norm_f)


def _moe_sparse(layer, h2, w_router_t, router_bias_col, wgu, wd, wsgu_b, wsd_b, x1, mod, norm_f, final,
                overlap=None):
    t = h2.shape[0]
    n_blk = t * TOP_K // MOE_BM + N_EXPERTS
    n_rows = n_blk * MOE_BM
    nblk_pad = -(-n_blk // 128) * 128
    eidx, rank, ew, cnt, h2_packed = _router_sparse(h2, w_router_t, router_bias_col)
    dest, blk = _plan(cnt, eidx, rank, nblk_pad)
    per_pick = [[dest[k:k + 1] + c * n_rows for k in range(TOP_K)] for c in range(N_CHUNK)]
    flat = dest[0:TOP_K].reshape(1, TOP_K * t)
    xs = _sc_scatter_rows(h2_packed, per_pick[:N_CHUNK // 2], n_rows).reshape(N_CHUNK // 2, n_rows, SC_CHUNK)
    if overlap is not None:
        xs = overlap(xs)
    ys = _grouped(layer, xs, blk[0], blk[1, 0:1], wgu, wd)
    yg = _sc_gather_rows(ys.reshape(N_CHUNK * n_rows, SC_CHUNK), [flat + c * n_rows for c in range(N_CHUNK)])
    if overlap is not None:
        yg = overlap(yg)
    return _combine(yg.reshape(N_CHUNK, TOP_K, t, SC_CHUNK), ew, h2, wsgu_b, wsd_b, x1, mod, norm_f, final)


def _permute_in_cols(w):
    ml = w[..., 0:ML_COLS]
    fx = w[..., OFF_FX:OFF_GT]
    small = jnp.concatenate([ml[..., 4 * BR_W:], fx[..., 3 * BR_W:]], axis=-1)
    pad = jnp.zeros(w.shape[:-1] + (P_GT - P_SM - small.shape[-1],), w.dtype)
    return jnp.concatenate([ml[..., :4 * BR_W], w[..., OFF_HG:OFF_RT], w[..., OFF_RT:OFF_FX], fx[..., :3 * BR_W],
                            small, pad, w[..., OFF_GT:]], axis=-1)


def _rope_tables(pos):
    half = HEAD_DIM // 2
    inv = ROPE_BASE ** (-jnp.arange(half, dtype=F32) / half)
    ang = pos.astype(F32)[:, None] * inv[None, :]
    cos = jnp.cos(ang)
    sin = jnp.sin(ang)
    cos_h = jnp.concatenate([cos, cos], axis=-1)
    sin_h = jnp.concatenate([-sin, sin], axis=-1)
    return jnp.tile(cos_h, (1, HEADS)), jnp.tile(sin_h, (1, HEADS))


def kernel(x_prompt, x_sample, cache_fox_k, cache_fox_v, cache_fox_logf, state_mlstm_C, state_mlstm_n, state_mlstm_m, state_hgrn_S, state_ret_S, page_table, c_prompt, c_sample, w_ada, b_ada, norm_mix_w, norm_ffn_w, w_in, b_in, hgrn_lb_logits, mlstm_norm_w, hgrn_norm_w, ret_norm_w, w_branch, w_out, w_router, router_bias, w_exp_gu, w_exp_down, w_shared_gu, w_shared_down, norm_f_w):
    depth = w_in.shape[0]
    bp, seq, _ = x_prompt.shape
    db, t_new, _ = x_sample.shape
    n_pool = cache_fox_k.shape[1]
    n_pages = page_table.shape[1]
    past_len = n_pages * PAGE_SIZE
    assert bp == 1 and seq % 128 == 0 and t_new == 8

    w_in_p = _permute_in_cols(w_in).astype(BF16)
    b_in_p = _permute_in_cols(b_in).reshape(depth, 1, P_TOT)
    w_branch_b = w_branch.astype(BF16)
    w_out_b = w_out.astype(BF16)
    wsgu_b = w_shared_gu.astype(BF16)
    wsd_b = w_shared_down.astype(BF16)
    w_router_t = jnp.swapaxes(w_router, 1, 2)
    cache_k = jnp.transpose(cache_fox_k, (0, 1, 3, 4, 2)).reshape(depth, n_pool, BR_W, PAGE_SIZE)
    cache_v = jnp.transpose(cache_fox_v, (0, 1, 3, 4, 2)).reshape(depth, n_pool, BR_W, PAGE_SIZE)
    cache_lf_t = jnp.swapaxes(cache_fox_logf, 2, 3)

    mods = _ada(jnp.concatenate([c_prompt, c_sample], axis=0), w_ada, b_ada)

    cos_p, sin_p = _rope_tables(jnp.arange(seq))
    cos_s, sin_s = _rope_tables(past_len + jnp.arange(t_new))

    def trunk(result, x, mod_of_layer, cos_t, sin_t, init, fox_fn, lc, overlap=None):
        b, l, _ = x.shape
        xt = x.reshape(b * l, D_MODEL)
        c_in, n_in, m_in, sh_in, sr_in = init
        per_layer = []
        for layer in range(depth):
            mod = mod_of_layer(layer)
            kv_t = b == 1 and (b * l) % 128 == 0
            proj, *kv = _inproj(layer, xt, mod, norm_mix_w[layer][None], w_in_p, b_in_p, kv_t)
            proj3 = proj.reshape(b, l, P_TOT)
            if kv_t:
                fox_k, fox_v = (a.reshape(HEADS, HEAD_DIM, l).transpose(2, 0, 1)[None] for a in kv)
            else:
                fox_k, fox_v = (proj3[..., P_FX + i * BR_W:P_FX + (i + 1) * BR_W].reshape(b, l, HEADS, HEAD_DIM)
                                for i in (1, 2))
            br3, lf_rows, cum_t, c_new, n_new, m_new, sh_new, sr_new = _mixers(
                layer, proj3, cos_t, sin_t, c_in[layer], n_in[layer], m_in[layer], sh_in[layer], sr_in[layer],
                hgrn_lb_logits, mlstm_norm_w[layer][None], hgrn_norm_w[layer][None], ret_norm_w[layer][None], lc)
            yield br3
            fox = fox_fn(layer, proj3, cum_t)
            yield fox
            x1, h2 = _merge(br3.reshape(b * l, 3 * BR_W), fox.reshape(b * l, BR_W), proj, xt, mod,
                            w_branch_b[layer], w_out_b[layer], norm_ffn_w[layer][None])
            final = layer == depth - 1
            if b * l >= MOE_SPARSE_MIN_TOKENS:
                xt = _moe_sparse(layer, h2, w_router_t[layer], router_bias[layer][:, None], w_exp_gu, w_exp_down,
                                 wsgu_b[layer], wsd_b[layer], x1, mod, norm_f_w[None], final, overlap)
            else:
                wt = _router(h2, w_router_t[layer], router_bias[layer][:, None])
                xt = _moe(layer, h2, wt, w_exp_gu, w_exp_down, wsgu_b[layer], wsd_b[layer], x1, mod, norm_f_w[None],
                          final)
            yield xt
            per_layer.append((
                fox_k,
                fox_v,
                lf_rows[..., SM_FLF:SM_FLF + HEADS],
                c_new.reshape(b, HEADS, HEAD_DIM, HEAD_DIM),
                n_new.reshape(b, HEADS, HEAD_DIM),
                m_new[:, 0, :HEADS],
                sh_new.reshape(b, HEADS, HEAD_DIM, HEAD_DIM),
                sr_new.reshape(b, HEADS, HEAD_DIM, HEAD_DIM)))
        stacked = tuple(jnp.stack([p[i] for p in per_layer]) for i in range(8))
        result.append((xt.reshape(b, l, D_MODEL),) + stacked)

    def head_major(s, b):
        return s.astype(F32).reshape(depth, b, BR_W, HEAD_DIM)

    zero_state = jnp.zeros((depth, bp, BR_W, HEAD_DIM), F32)
    prompt_init = (zero_state, jnp.zeros((depth, bp, 1, BR_W), F32), jnp.zeros((depth, bp, 1, 128), F32),
                   zero_state, zero_state)
    m_pad = jnp.pad(state_mlstm_m.astype(F32), ((0, 0), (0, 0), (0, 128 - HEADS))).reshape(depth, db, 1, 128)
    sample_init = (head_major(state_mlstm_C, db), state_mlstm_n.astype(F32).reshape(depth, db, 1, BR_W), m_pad,
                   head_major(state_hgrn_S, db), head_major(state_ret_S, db))
    n_slots = next(n for n in (64, 32, 16, 8, 4, 2, 1) if n_pages % n == 0)
    prompt_out, sample_out = [], []
    sample = trunk(sample_out, x_sample, lambda layer: jnp.repeat(mods[layer, 1:], t_new, axis=0), cos_s, sin_s,
                   sample_init,
                   lambda layer, proj3, cum_t: _fox_decode(layer, proj3, cache_k, cache_v, cache_lf_t,
                                                           page_table, n_slots),
                   t_new)
    stages_per_copy = iter((2, 2, 1, 1))

    def overlap(arr):
        token = None
        for _ in range(next(stages_per_copy, 0)):
            token = next(sample, token)
        if token is None:
            return arr
        return lax.optimization_barrier((arr, token))[0]

    prompt = trunk(prompt_out, x_prompt, lambda layer: mods[layer, 0:1], cos_p, sin_p, prompt_init,
                   lambda layer, proj3, cum_t: _fox_prompt(proj3[0], cum_t[0], min(seq, 512)),
                   128, overlap)
    for gen in (prompt, sample):
        for _ in gen:
            pass
    prompt_out, sample_out = prompt_out[0], sample_out[0]

    return (prompt_out[0], sample_out[0]) + prompt_out[1:] + sample_out[1:]
```
